```python
import jax, jax.numpy as jnp
from jax import lax
import numpy as np

D_MODEL = 2048
BATCH = 8
SEQ = 2048
DEPTH = 2
DEC_BATCH = 128
DEC_SEQ = 4
PAST_LEN = 8192
PAGE_SIZE = 128

N_A_LAYERS = DEPTH // 2
N_B_LAYERS = DEPTH - N_A_LAYERS
N_META = 16
A_HEADS = 4
A_DV = D_MODEL // A_HEADS
A_DK = A_DV // 2
A_CHUNK = 128
A_GATE_CAP = 15.0
A_IN_COLS = 2 * A_HEADS * A_DK + 2 * A_HEADS * A_DV + 2 * A_HEADS
B_HEADS = 32
B_DH = D_MODEL // B_HEADS
B_KV_HEADS = 4
B_GROUP = B_HEADS // B_KV_HEADS
WINDOW = 128
B_BLOCK = 128
D_FF = ((8 * D_MODEL // 3 + 255) // 256) * 256
EPS = 1e-6

kernel_name = 'yoco_mlstm_swa_sink_macaron_step'


def rms_norm(x, g):
    xf = x.astype(jnp.float32)
    y = xf * lax.rsqrt(jnp.mean(xf * xf, axis=-1, keepdims=True) + EPS)
    return (y * g.astype(jnp.float32)).astype(x.dtype)


def swiglu_ffn(x, w_in, w_out):
    g, u = jnp.split(x @ w_in, 2, axis=-1)
    return (jax.nn.silu(g) * u) @ w_out


def alibi_slopes():
    h = jnp.arange(1, B_HEADS + 1, dtype=jnp.float32)
    return jnp.exp2(-8.0 * h / B_HEADS).reshape(B_KV_HEADS, B_GROUP)


def mlstm_project(xn, w_in, b_gate):
    bsz, t = xn.shape[0], xn.shape[1]
    p = xn @ w_in
    qk = A_HEADS * A_DK
    hv = A_HEADS * A_DV
    cuts = [qk, 2 * qk, 2 * qk + hv, 2 * qk + 2 * hv, 2 * qk + 2 * hv + A_HEADS]
    q, k, v, o, ig, fg = jnp.split(p, cuts, axis=-1)
    q = q.reshape(bsz, t, A_HEADS, A_DK)
    k = k.reshape(bsz, t, A_HEADS, A_DK) * (A_DK ** -0.5)
    v = v.reshape(bsz, t, A_HEADS, A_DV)
    gates = jnp.concatenate([ig, fg], axis=-1).astype(jnp.float32) + b_gate.astype(jnp.float32)
    gates = A_GATE_CAP * jnp.tanh(gates / A_GATE_CAP)
    ig, fg = jnp.split(gates, 2, axis=-1)
    return q, k, v, o, ig, jax.nn.log_sigmoid(fg)


def mlstm_chunk(carry, inp):
    c_prev, n_prev, m_prev = carry
    q, k, v, ig, log_f = inp
    q = q.astype(jnp.float32)
    k = k.astype(jnp.float32)
    v = v.astype(jnp.float32)
    L = q.shape[1]
    b = jnp.cumsum(log_f, axis=1).transpose(0, 2, 1)
    igt = ig.transpose(0, 2, 1)
    causal = jnp.tril(jnp.ones((L, L), dtype=bool))
    d_log = jnp.where(causal, b[..., :, None] - b[..., None, :] + igt[..., None, :], -jnp.inf)
    inter_log = b + m_prev[..., None]
    m_t = jnp.maximum(inter_log, jnp.max(d_log, axis=-1))
    w_intra = jnp.exp(d_log - m_t[..., None])
    w_inter = jnp.exp(inter_log - m_t)
    s = jnp.einsum('blhd,bshd->bhls', q, k) * w_intra
    num = jnp.einsum('bhls,bshv->bhlv', s, v) + w_inter[..., None] * jnp.einsum('blhd,bhdv->bhlv', q, c_prev)
    den = jnp.sum(s, axis=-1) + w_inter * jnp.einsum('blhd,bhd->bhl', q, n_prev)
    den = jnp.maximum(jnp.abs(den), jnp.exp(-m_t))
    h = (num / den[..., None]).transpose(0, 2, 1, 3)
    b_last = b[..., -1]
    w_log = b_last[..., None] - b + igt
    m_new = jnp.maximum(b_last + m_prev, jnp.max(w_log, axis=-1))
    w_state = jnp.exp(w_log - m_new[..., None])
    decay = jnp.exp(b_last + m_prev - m_new)
    c_new = decay[..., None, None] * c_prev + jnp.einsum('bhs,bshd,bshv->bhdv', w_state, k, v)
    n_new = decay[..., None] * n_prev + jnp.einsum('bhs,bshd->bhd', w_state, k)
    return (c_new, n_new, m_new), h


def mlstm_prompt(q, k, v, ig, log_f):
    bsz = q.shape[0]
    zero = (jnp.zeros((bsz, A_HEADS, A_DK, A_DV), jnp.float32),
            jnp.zeros((bsz, A_HEADS, A_DK), jnp.float32),
            jnp.zeros((bsz, A_HEADS), jnp.float32))
    arrs = (q, k, v, ig, log_f)
    carry, h_meta = mlstm_chunk(zero, tuple(a[:, :N_META] for a in arrs))

    def to_chunks(a):
        r = a[:, N_META:]
        nc = r.shape[1] // A_CHUNK
        return jnp.moveaxis(r.reshape((bsz, nc, A_CHUNK) + r.shape[2:]), 1, 0)

    carry, h_seq = lax.scan(mlstm_chunk, carry, tuple(to_chunks(a) for a in arrs))
    h_seq = jnp.moveaxis(h_seq, 0, 1).reshape(bsz, -1, A_HEADS, A_DV)
    return jnp.concatenate([h_meta, h_seq], axis=1), carry


def mlstm_out(h, o, head_gain, w_out, dtype):
    hn = h * lax.rsqrt(jnp.mean(h * h, axis=-1, keepdims=True) + EPS)
    hn = hn.reshape(h.shape[0], h.shape[1], A_HEADS * A_DV) * head_gain.astype(jnp.float32)
    return (hn * jax.nn.sigmoid(o.astype(jnp.float32))).astype(dtype) @ w_out


def shared_kv(h, kv_norm, w_kv, k_norm):
    xn = rms_norm(h, kv_norm)
    kv = (xn @ w_kv).reshape(h.shape[0], h.shape[1], 2, B_KV_HEADS, B_DH)
    return rms_norm(kv[:, :, 0], k_norm), kv[:, :, 1]


def b_queries(xn, w_q, q_norm):
    q = (xn @ w_q).reshape(xn.shape[0], xn.shape[1], B_KV_HEADS, B_GROUP, B_DH)
    return rms_norm(q, q_norm)


def sink_attention(q, k, v, valid, dist, sinks):
    slopes = alibi_slopes()
    s = jnp.einsum('bnqgrd,bnkgd->bngrqk', q, k).astype(jnp.float32) * (B_DH ** -0.5)
    s = s - slopes[None, None, :, :, None, None] * dist[None, :, None, None]
    s = jnp.where(valid[None, :, None, None], s, -jnp.inf)
    sink = sinks.astype(jnp.float32)[None, None, :, :, None]
    mx = jnp.maximum(jnp.max(s, axis=-1), sink)
    p = jnp.exp(s - mx[..., None])
    den = jnp.sum(p, axis=-1) + jnp.exp(sink - mx)
    return jnp.einsum('bngrqk,bnkgd->bnqgrd', p / den[..., None], v.astype(jnp.float32))


def window_attn_prompt(q, k, v, sinks):
    bsz, t = q.shape[0], q.shape[1]
    nb = -(-t // B_BLOCK)
    pad = nb * B_BLOCK - t
    qb = jnp.pad(q, ((0, 0), (0, pad), (0, 0), (0, 0), (0, 0))).reshape(bsz, nb, B_BLOCK, B_KV_HEADS, B_GROUP, B_DH)
    kp = jnp.pad(k, ((0, 0), (B_BLOCK, pad), (0, 0), (0, 0))).reshape(bsz, nb + 1, B_BLOCK, B_KV_HEADS, B_DH)
    vp = jnp.pad(v, ((0, 0), (B_BLOCK, pad), (0, 0), (0, 0))).reshape(bsz, nb + 1, B_BLOCK, B_KV_HEADS, B_DH)
    meta_shape = (bsz, nb, N_META, B_KV_HEADS, B_DH)
    keys = jnp.concatenate([jnp.broadcast_to(k[:, None, :N_META], meta_shape), kp[:, :-1], kp[:, 1:]], axis=2)
    vals = jnp.concatenate([jnp.broadcast_to(v[:, None, :N_META], meta_shape), vp[:, :-1], vp[:, 1:]], axis=2)
    blk = jnp.arange(nb)[:, None]
    t_pos = blk * B_BLOCK + jnp.arange(B_BLOCK)[None, :]
    s_band = (blk - 1) * B_BLOCK + jnp.arange(2 * B_BLOCK)[None, :]
    s_meta = jnp.broadcast_to(jnp.arange(N_META)[None, :], (nb, N_META))
    s_all = jnp.concatenate([s_meta, s_band], axis=1)
    is_meta = jnp.concatenate([jnp.ones((N_META,), bool), jnp.zeros((2 * B_BLOCK,), bool)])
    rel = t_pos[:, :, None] - s_all[:, None, :]
    band_ok = (rel >= 0) & (rel < WINDOW) & (s_all[:, None, :] >= 0) & (s_all[:, None, :] < t)
    valid = jnp.where(is_meta[None, None, :], rel >= WINDOW, band_ok)
    dist = jnp.minimum(rel, WINDOW).astype(jnp.float32)
    o = sink_attention(qb, keys, vals, valid, dist, sinks)
    return o.reshape(bsz, nb * B_BLOCK, B_HEADS * B_DH)[:, :t]


def window_attn_sample(q, k_new, v_new, k_meta, v_meta, k_win, v_win, sinks):
    bsz, s_len = q.shape[0], q.shape[1]
    keys = jnp.concatenate([k_meta, k_win, k_new.astype(k_win.dtype)], axis=1)[:, None]
    vals = jnp.concatenate([v_meta, v_win, v_new.astype(v_win.dtype)], axis=1)[:, None]
    t_pos = PAST_LEN + jnp.arange(s_len)
    s_all = jnp.concatenate([jnp.arange(N_META), PAST_LEN - WINDOW + jnp.arange(WINDOW), PAST_LEN + jnp.arange(s_len)])
    is_meta = jnp.concatenate([jnp.ones((N_META,), bool), jnp.zeros((WINDOW + s_len,), bool)])
    rel = t_pos[:, None] - s_all[None, :]
    valid = jnp.where(is_meta[None, :], rel >= WINDOW, (rel >= 0) & (rel < WINDOW))
    dist = jnp.minimum(rel, WINDOW).astype(jnp.float32)
    o = sink_attention(q[:, None], keys, vals, valid[None], dist[None], sinks)
    return o.reshape(bsz, s_len, B_HEADS * B_DH)


def trunk(h, a_mixer, make_kv, b_mixer, ffn_norm, w_ffn_in, w_ffn_out, mix_norm):
    a_states = []
    kv = None
    for layer in range(DEPTH):
        if layer == N_A_LAYERS:
            kv = make_kv(h)
        h = h + 0.5 * swiglu_ffn(rms_norm(h, ffn_norm[layer, 0]), w_ffn_in[layer, 0], w_ffn_out[layer, 0])
        xn = rms_norm(h, mix_norm[layer])
        if layer < N_A_LAYERS:
            out, st = a_mixer(layer, xn)
            a_states.append(st)
        else:
            out = b_mixer(layer - N_A_LAYERS, xn, kv)
        h = h + out
        h = h + 0.5 * swiglu_ffn(rms_norm(h, ffn_norm[layer, 1]), w_ffn_in[layer, 1], w_ffn_out[layer, 1])
    return h, a_states, kv


def setup_inputs(seed: int = 0) -> dict:
    key = jax.random.key(seed)
    ks = jax.random.split(key, 32)

    def nrm(k, shape, scale):
        return jax.random.normal(k, shape, jnp.float32) * scale

    f_bias = jnp.linspace(3.0, 6.0, A_HEADS, dtype=jnp.float32)[None, :] + nrm(ks[20], (N_A_LAYERS, A_HEADS), 0.1)
    i_bias = nrm(ks[21], (N_A_LAYERS, A_HEADS), 0.1)
    return {
        'x_prompt': nrm(ks[0], (BATCH, SEQ, D_MODEL), 1.0),
        'x_sample': nrm(ks[1], (DEC_BATCH, DEC_SEQ, D_MODEL), 1.0),
        'state_C': nrm(ks[2], (N_A_LAYERS, DEC_BATCH, A_HEADS, A_DK, A_DV), 0.1),
        'state_n': nrm(ks[3], (N_A_LAYERS, DEC_BATCH, A_HEADS, A_DK), 0.1),
        'state_m': nrm(ks[4], (N_A_LAYERS, DEC_BATCH, A_HEADS), 1.0),
        'cache_k_meta': nrm(ks[5], (DEC_BATCH, N_META, B_KV_HEADS, B_DH), 1.0),
        'cache_v_meta': nrm(ks[6], (DEC_BATCH, N_META, B_KV_HEADS, B_DH), 1.0),
        'cache_k_win': nrm(ks[7], (DEC_BATCH, WINDOW, B_KV_HEADS, B_DH), 1.0),
        'cache_v_win': nrm(ks[8], (DEC_BATCH, WINDOW, B_KV_HEADS, B_DH), 1.0),
        'meta_tokens': nrm(ks[9], (N_META, D_MODEL), 1.0),
        'ffn_norm': 1.0 + nrm(ks[10], (DEPTH, 2, D_MODEL), 0.02),
        'w_ffn_in': nrm(ks[11], (DEPTH, 2, D_MODEL, 2 * D_FF), D_MODEL ** -0.5),
        'w_ffn_out': nrm(ks[12], (DEPTH, 2, D_FF, D_MODEL), D_FF ** -0.5),
        'mix_norm': 1.0 + nrm(ks[13], (DEPTH, D_MODEL), 0.02),
        'w_a_in': nrm(ks[14], (N_A_LAYERS, D_MODEL, A_IN_COLS), D_MODEL ** -0.5),
        'b_a_gate': jnp.concatenate([i_bias, f_bias], axis=-1),
        'a_head_norm': 1.0 + nrm(ks[15], (N_A_LAYERS, A_HEADS * A_DV), 0.02),
        'w_a_out': nrm(ks[16], (N_A_LAYERS, A_HEADS * A_DV, D_MODEL), (A_HEADS * A_DV) ** -0.5),
        'kv_norm': 1.0 + nrm(ks[17], (D_MODEL,), 0.02),
        'w_kv': nrm(ks[18], (D_MODEL, 2 * B_KV_HEADS * B_DH), D_MODEL ** -0.5),
        'k_norm': 1.0 + nrm(ks[19], (B_DH,), 0.02),
        'w_q': nrm(ks[22], (N_B_LAYERS, D_MODEL, B_HEADS * B_DH), D_MODEL ** -0.5),
        'q_norm': 1.0 + nrm(ks[23], (N_B_LAYERS, B_DH), 0.02),
        'sinks': nrm(ks[24], (N_B_LAYERS, B_HEADS), 0.5),
        'w_b_out': nrm(ks[25], (N_B_LAYERS, B_HEADS * B_DH, D_MODEL), (B_HEADS * B_DH) ** -0.5),
    }


def reference(x_prompt, x_sample, state_C, state_n, state_m, cache_k_meta, cache_v_meta, cache_k_win, cache_v_win,
              meta_tokens, ffn_norm, w_ffn_in, w_ffn_out, mix_norm, w_a_in, b_a_gate, a_head_norm, w_a_out,
              kv_norm, w_kv, k_norm, w_q, q_norm, sinks, w_b_out):
    def make_kv(h):
        return shared_kv(h, kv_norm, w_kv, k_norm)

    def a_prompt(la, xn):
        q, k, v, o, ig, lf = mlstm_project(xn, w_a_in[la], b_a_gate[la])
        h, st = mlstm_prompt(q, k, v, ig, lf)
        return mlstm_out(h, o, a_head_norm[la], w_a_out[la], xn.dtype), st

    def a_sample(la, xn):
        q, k, v, o, ig, lf = mlstm_project(xn, w_a_in[la], b_a_gate[la])
        carry = (state_C[la].astype(jnp.float32), state_n[la].astype(jnp.float32), state_m[la].astype(jnp.float32))
        st, h = mlstm_chunk(carry, (q, k, v, ig, lf))
        return mlstm_out(h, o, a_head_norm[la], w_a_out[la], xn.dtype), st

    def b_prompt(lb, xn, kv):
        q = b_queries(xn, w_q[lb], q_norm[lb])
        o = window_attn_prompt(q, kv[0], kv[1], sinks[lb].reshape(B_KV_HEADS, B_GROUP))
        return o.astype(xn.dtype) @ w_b_out[lb]

    def b_sample(lb, xn, kv):
        q = b_queries(xn, w_q[lb], q_norm[lb])
        o = window_attn_sample(q, kv[0], kv[1], cache_k_meta, cache_v_meta, cache_k_win, cache_v_win,
                               sinks[lb].reshape(B_KV_HEADS, B_GROUP))
        return o.astype(xn.dtype) @ w_b_out[lb]

    bsz = x_prompt.shape[0]
    meta = jnp.broadcast_to(meta_tokens.astype(x_prompt.dtype)[None], (bsz, N_META, D_MODEL))
    h_p, st_p, kv_p = trunk(jnp.concatenate([meta, x_prompt], axis=1), a_prompt, make_kv, b_prompt,
                            ffn_norm, w_ffn_in, w_ffn_out, mix_norm)
    h_s, st_s, kv_s = trunk(x_sample, a_sample, make_kv, b_sample, ffn_norm, w_ffn_in, w_ffn_out, mix_norm)

    y_prompt = h_p[:, N_META:]
    k_p, v_p = kv_p
    k_s, v_s = kv_s
    c_p = jnp.stack([st[0] for st in st_p]).astype(state_C.dtype)
    n_p = jnp.stack([st[1] for st in st_p]).astype(state_n.dtype)
    m_p = jnp.stack([st[2] for st in st_p]).astype(state_m.dtype)
    c_s = jnp.stack([st[0] for st in st_s]).astype(state_C.dtype)
    n_s = jnp.stack([st[1] for st in st_s]).astype(state_n.dtype)
    m_s = jnp.stack([st[2] for st in st_s]).astype(state_m.dtype)
    k_win_s = jnp.concatenate([cache_k_win, k_s.astype(cache_k_win.dtype)], axis=1)[:, -WINDOW:]
    v_win_s = jnp.concatenate([cache_v_win, v_s.astype(cache_v_win.dtype)], axis=1)[:, -WINDOW:]
    return (y_prompt, h_s, c_p, n_p, m_p, k_p[:, :N_META], v_p[:, :N_META], k_p[:, -WINDOW:], v_p[:, -WINDOW:], c_s, n_s, m_s, k_win_s, v_win_s)
```

```python
import functools

import jax
import jax.numpy as jnp
from jax import lax
from jax.experimental import pallas as pl
from jax.experimental.pallas import tpu as pltpu

D_MODEL = 2048
BATCH = 8
SEQ = 2048
DEC_BATCH = 128
DEC_SEQ = 4
PAST_LEN = 8192
N_META = 16
A_HEADS = 4
A_DV = D_MODEL // A_HEADS
A_DK = A_DV // 2
A_CHUNK = 128
A_GATE_CAP = 15.0
A_QKVO = 2 * A_HEADS * A_DK + 2 * A_HEADS * A_DV
B_HEADS = 32
B_DH = D_MODEL // B_HEADS
B_KV_HEADS = 4
B_GROUP = B_HEADS // B_KV_HEADS
B_KV = B_KV_HEADS * B_DH
WINDOW = 128
D_FF = ((8 * D_MODEL // 3 + 255) // 256) * 256
EPS = 1e-6

LANES = 128
SUBLANES = 8
VMEM_LIMIT = 52 * 1024 * 1024

P_ROWS = BATCH * SEQ
S_ROWS = DEC_BATCH * DEC_SEQ
Q_ROWS = S_ROWS + A_CHUNK
META_BLOCK = S_ROWS // A_CHUNK

F32 = jnp.float32
BF16 = jnp.bfloat16
NEG_INF = float("-inf")


def _params(n_axes):
    return pltpu.CompilerParams(dimension_semantics=("arbitrary",) * n_axes,
                                vmem_limit_bytes=VMEM_LIMIT)


def _rms(x, g):
    return x * lax.rsqrt(jnp.mean(x * x, axis=-1, keepdims=True) + EPS) * g


def _dot(a, b):
    return jnp.dot(a, b, preferred_element_type=F32)


def _dot_nt(a, b):
    return lax.dot_general(a, b, (((1,), (1,)), ((), ())), preferred_element_type=F32)


def _dot_tn(a, b):
    return lax.dot_general(a, b, (((0,), (0,)), ((), ())), preferred_element_type=F32)


def _log_sigmoid(x):
    return -(jnp.maximum(-x, 0.0) + jnp.log1p(jnp.exp(-jnp.abs(x))))


def _head_norm64(y, gain):
    lo = lax.broadcasted_iota(jnp.int32, (1, LANES), 1) < B_DH
    cols = []
    for c in range(y.shape[1] // LANES):
        x = y[:, c * LANES:(c + 1) * LANES]
        xx = x * x
        s_lo = jnp.sum(jnp.where(lo, xx, 0.0), axis=-1, keepdims=True)
        s_hi = jnp.sum(jnp.where(lo, 0.0, xx), axis=-1, keepdims=True)
        scale = jnp.where(lo, lax.rsqrt(s_lo / B_DH + EPS), lax.rsqrt(s_hi / B_DH + EPS))
        cols.append(x * scale * gain[:, c * LANES:(c + 1) * LANES])
    return jnp.concatenate(cols, axis=1)


def _ffn_kernel(h_ref, g_ref, wg_ref, wu_ref, wo_ref, o_ref, xn_ref, *, n_ff):
    j = pl.program_id(1)

    @pl.when(j == 0)
    def _():
        xn_ref[...] = _rms(h_ref[...], g_ref[...]).astype(BF16)
        o_ref[...] = jnp.zeros_like(o_ref)

    xn = xn_ref[...]
    g = _dot(xn, wg_ref[...])
    u = _dot(xn, wu_ref[...])
    a = (g / (1.0 + jnp.exp(-g))) * u
    o_ref[...] += _dot(a.astype(BF16), wo_ref[...])

    @pl.when(j == n_ff - 1)
    def _():
        o_ref[...] = h_ref[...] + 0.5 * o_ref[...]


def _ffn(h, gain, w_in, w_out, layer, which, tm, tf=512):
    rows = h.shape[0]
    n_ff = D_FF // tf
    return pl.pallas_call(
        functools.partial(_ffn_kernel, n_ff=n_ff),
        grid=(rows // tm, n_ff),
        in_specs=[
            pl.BlockSpec((tm, D_MODEL), lambda i, j: (i, 0)),
            pl.BlockSpec((1, D_MODEL), lambda i, j: (0, 0)),
            pl.BlockSpec((None, None, D_MODEL, tf), lambda i, j: (layer, which, 0, j)),
            pl.BlockSpec((None, None, D_MODEL, tf), lambda i, j: (layer, which, 0, j + n_ff)),
            pl.BlockSpec((None, None, tf, D_MODEL), lambda i, j: (layer, which, j, 0)),
        ],
        out_specs=pl.BlockSpec((tm, D_MODEL), lambda i, j: (i, 0)),
        out_shape=jax.ShapeDtypeStruct((rows, D_MODEL), F32),
        scratch_shapes=[pltpu.VMEM((tm, D_MODEL), BF16)],
        compiler_params=_params(2),
        name="ffn",
    )(h, gain, w_in, w_in, w_out)


def _inproj_kernel(h_ref, g_ref, w_ref, wgate_ref, bgate_ref, p_ref, gates_ref, xn_ref):
    j = pl.program_id(1)

    @pl.when(j == 0)
    def _():
        xn = _rms(h_ref[...], g_ref[...]).astype(BF16)
        xn_ref[...] = xn
        pre = _dot(xn, wgate_ref[...]) + bgate_ref[...]
        capped = A_GATE_CAP * jnp.tanh(pre / A_GATE_CAP)
        lane = lax.broadcasted_iota(jnp.int32, (1, LANES), 1)
        gates_ref[...] = jnp.where(lane < A_HEADS, capped, _log_sigmoid(capped))

    p_ref[...] = _dot(xn_ref[...], w_ref[...])


def _inproj(h, gain, w, wgate, bgate, tm, tn=512):
    rows = h.shape[0]
    return pl.pallas_call(
        _inproj_kernel,
        grid=(rows // tm, A_QKVO // tn),
        in_specs=[
            pl.BlockSpec((tm, D_MODEL), lambda i, j: (i, 0)),
            pl.BlockSpec((1, D_MODEL), lambda i, j: (0, 0)),
            pl.BlockSpec((D_MODEL, tn), lambda i, j: (0, j)),
            pl.BlockSpec((D_MODEL, LANES), lambda i, j: (0, 0)),
            pl.BlockSpec((1, LANES), lambda i, j: (0, 0)),
        ],
        out_specs=[
            pl.BlockSpec((tm, tn), lambda i, j: (i, j)),
            pl.BlockSpec((tm, LANES), lambda i, j: (i, 0)),
        ],
        out_shape=[jax.ShapeDtypeStruct((rows, A_QKVO), F32),
                   jax.ShapeDtypeStruct((rows, LANES), F32)],
        scratch_shapes=[pltpu.VMEM((tm, D_MODEL), BF16)],
        compiler_params=_params(2),
        name="mlstm_inproj",
    )(h, gain, w, wgate, bgate)


def _mlstm_chunk_kernel(q_ref, k_ref, v_ref, g_ref, c0_ref, n0_ref, m0_ref,
                        h_ref, c_ref, n_ref, m_ref, *, n_valid):
    L = A_CHUNK

    @pl.when(pl.program_id(1) == 0)
    def _():
        c_ref[...] = c0_ref[...]
        n_ref[...] = n0_ref[...]
        m_ref[...] = m0_ref[...]

    gates = g_ref[...]
    row = lax.broadcasted_iota(jnp.int32, (L, L), 0)
    col = lax.broadcasted_iota(jnp.int32, (L, L), 1)
    causal = col <= row
    eye = col == row
    masked = n_valid < L
    if masked:
        row_ok = lax.broadcasted_iota(jnp.int32, (L, 1), 0) < n_valid
        gates_lf = jnp.where(row_ok, gates, 0.0)
    else:
        gates_lf = gates
    csum = jnp.dot(causal.astype(F32), gates_lf, precision=lax.Precision.HIGHEST,
                   preferred_element_type=F32)

    def to_row(x_col):
        return jnp.sum(jnp.where(eye, x_col, 0.0), axis=0, keepdims=True)

    for hd in range(A_HEADS):
        b_col = csum[:, A_HEADS + hd:A_HEADS + hd + 1]
        ig_col = gates[:, hd:hd + 1]
        if masked:
            ig_col = jnp.where(row_ok, ig_col, NEG_INF)
        b_row = to_row(b_col)
        ig_row = to_row(ig_col)
        m_prev = m_ref[0, hd][:, 0:1]
        c_prev = c_ref[0, hd]
        n_prev = n_ref[0, hd]

        q = q_ref[:, hd * A_DK:(hd + 1) * A_DK]
        k = k_ref[:, hd * A_DK:(hd + 1) * A_DK] * (A_DK ** -0.5)
        v = v_ref[:, hd * A_DV:(hd + 1) * A_DV]
        qb = q.astype(BF16)
        vb = v.astype(BF16)

        d_log = jnp.where(causal, b_col - b_row + ig_row, NEG_INF)
        inter_log = b_col + m_prev
        m_t = jnp.maximum(inter_log, jnp.max(d_log, axis=-1, keepdims=True))
        w_intra = jnp.exp(d_log - m_t)
        w_inter = jnp.exp(inter_log - m_t)
        s = _dot_nt(qb, k.astype(BF16)) * w_intra
        num = _dot(s.astype(BF16), vb) + w_inter * _dot(qb, c_prev.astype(BF16))
        den = jnp.sum(s, axis=-1, keepdims=True) + w_inter * jnp.sum(q * n_prev, axis=-1, keepdims=True)
        den = jnp.maximum(jnp.abs(den), jnp.exp(-m_t))
        h_ref[:, hd * A_DV:(hd + 1) * A_DV] = num / den

        b_last = b_col[L - 1:L, :]
        w_log = b_last - b_col + ig_col
        m_new = jnp.maximum(b_last + m_prev, jnp.max(w_log, axis=0, keepdims=True))
        w_state = jnp.exp(w_log - m_new)
        decay = jnp.exp(b_last + m_prev - m_new)
        kw = k * w_state
        c_ref[0, hd] = decay * c_prev + _dot_tn(kw.astype(BF16), vb)
        n_ref[0, hd] = decay * n_prev + jnp.sum(kw, axis=0, keepdims=True)
        m_ref[0, hd] = jnp.broadcast_to(m_new, (1, LANES))


def _mlstm_chunks(p, gates, c0, n0, m0, n_seq, n_chunks, row_block0, n_valid, shared_state):
    L = A_CHUNK
    rb = lambda b, c: row_block0 + b * n_chunks + c
    st = (lambda b, c: (0, 0, 0, 0)) if shared_state else (lambda b, c: (b, 0, 0, 0))
    return pl.pallas_call(
        functools.partial(_mlstm_chunk_kernel, n_valid=n_valid),
        grid=(n_seq, n_chunks),
        in_specs=[
            pl.BlockSpec((L, A_HEADS * A_DK), lambda b, c: (rb(b, c), 0)),
            pl.BlockSpec((L, A_HEADS * A_DK), lambda b, c: (rb(b, c), 1)),
            pl.BlockSpec((L, A_HEADS * A_DV), lambda b, c: (rb(b, c), 1)),
            pl.BlockSpec((L, LANES), lambda b, c: (rb(b, c), 0)),
            pl.BlockSpec((1, A_HEADS, A_DK, A_DV), st),
            pl.BlockSpec((1, A_HEADS, 1, A_DK), st),
            pl.BlockSpec((1, A_HEADS, 1, LANES), st),
        ],
        out_specs=[
            pl.BlockSpec((L, A_HEADS * A_DV), lambda b, c: (b * n_chunks + c, 0)),
            pl.BlockSpec((1, A_HEADS, A_DK, A_DV), lambda b, c: (b, 0, 0, 0)),
            pl.BlockSpec((1, A_HEADS, 1, A_DK), lambda b, c: (b, 0, 0, 0)),
            pl.BlockSpec((1, A_HEADS, 1, LANES), lambda b, c: (b, 0, 0, 0)),
        ],
        out_shape=[
            jax.ShapeDtypeStruct((n_seq * n_chunks * L, D_MODEL), F32),
            jax.ShapeDtypeStruct((n_seq, A_HEADS, A_DK, A_DV), F32),
            jax.ShapeDtypeStruct((n_seq, A_HEADS, 1, A_DK), F32),
            jax.ShapeDtypeStruct((n_seq, A_HEADS, 1, LANES), F32),
        ],
        compiler_params=_params(2),
        name="mlstm_chunks",
    )(p, p, p, gates, c0, n0, m0)


def _mlstm_sample_kernel(q_ref, k_ref, v_ref, g_ref, c0_ref, n0_ref, m0_ref,
                         h_ref, c_ref, n_ref, m_ref):
    R = 2 * DEC_SEQ
    PAD = A_CHUNK - R
    gates = g_ref[...]
    r_col = lax.broadcasted_iota(jnp.int32, (R, 1), 0)
    is_a = r_col < DEC_SEQ
    row = lax.broadcasted_iota(jnp.int32, (R, LANES), 0)
    lane = lax.broadcasted_iota(jnp.int32, (R, LANES), 1)
    same = ((lane < DEC_SEQ) & (row < DEC_SEQ)) | ((lane >= DEC_SEQ) & (lane < R) & (row >= DEC_SEQ))
    causal = same & (lane <= row)
    eye = lane == row

    def to_row(x_col):
        return jnp.sum(jnp.where(eye, x_col, 0.0), axis=0, keepdims=True)

    for hd in range(A_HEADS):
        lf_col = gates[:, A_HEADS + hd:A_HEADS + hd + 1]
        ig_col = gates[:, hd:hd + 1]
        lf_row = to_row(lf_col)
        ig_row = to_row(ig_col)
        b_col = jnp.sum(jnp.where(causal, lf_row, 0.0), axis=1, keepdims=True)
        b_row = to_row(b_col)
        m_a = m0_ref[0, hd][:, 0:1]
        m_b = m0_ref[1, hd][:, 0:1]
        m_prev = jnp.where(is_a, m_a, m_b)
        c_a = c0_ref[0, hd]
        c_b = c0_ref[1, hd]
        n_a = n0_ref[0, hd]
        n_b = n0_ref[1, hd]

        q = q_ref[:, hd * A_DK:(hd + 1) * A_DK]
        k = k_ref[:, hd * A_DK:(hd + 1) * A_DK] * (A_DK ** -0.5)
        v = v_ref[:, hd * A_DV:(hd + 1) * A_DV]
        qb = q.astype(BF16)
        k_pad = jnp.concatenate([k, jnp.zeros((PAD, A_DK), F32)], axis=0).astype(BF16)
        v_pad = jnp.concatenate([v, jnp.zeros((PAD, A_DV), F32)], axis=0).astype(BF16)

        d_log = jnp.where(causal, b_col - b_row + ig_row, NEG_INF)
        inter_log = b_col + m_prev
        m_t = jnp.maximum(inter_log, jnp.max(d_log, axis=-1, keepdims=True))
        w_intra = jnp.exp(d_log - m_t)
        w_inter = jnp.exp(inter_log - m_t)
        s = _dot_nt(qb, k_pad) * w_intra
        q_c = jnp.where(is_a, _dot(qb, c_a.astype(BF16)), _dot(qb, c_b.astype(BF16)))
        num = _dot(s.astype(BF16), v_pad) + w_inter * q_c
        q_n = jnp.sum(q * jnp.where(is_a, n_a, n_b), axis=-1, keepdims=True)
        den = jnp.sum(s, axis=-1, keepdims=True) + w_inter * q_n
        den = jnp.maximum(jnp.abs(den), jnp.exp(-m_t))
        h_ref[:, hd * A_DV:(hd + 1) * A_DV] = num / den

        for idx, sel, m_x, c_x, n_x in ((0, is_a, m_a, c_a, n_a),
                                        (1, jnp.logical_not(is_a), m_b, c_b, n_b)):
            last = (idx + 1) * DEC_SEQ - 1
            b_last = b_col[last:last + 1, :]
            w_log = jnp.where(sel, b_last - b_col + ig_col, NEG_INF)
            m_new = jnp.maximum(b_last + m_x, jnp.max(w_log, axis=0, keepdims=True))
            w_state = jnp.exp(w_log - m_new)
            decay = jnp.exp(b_last + m_x - m_new)
            kw = k * w_state
            kw_pad = jnp.concatenate([kw, jnp.zeros((PAD, A_DK), F32)], axis=0).astype(BF16)
            c_ref[idx, hd] = decay * c_x + _dot_tn(kw_pad, v_pad)
            n_ref[idx, hd] = decay * n_x + jnp.sum(kw, axis=0, keepdims=True)
            m_ref[idx, hd] = jnp.broadcast_to(m_new, (1, LANES))


def _mlstm_sample(p, gates, c0, n0, m0):
    R = 2 * DEC_SEQ
    n_pairs = DEC_BATCH // 2
    st = lambda i: (i, 0, 0, 0)
    return pl.pallas_call(
        _mlstm_sample_kernel,
        grid=(n_pairs,),
        in_specs=[
            pl.BlockSpec((R, A_HEADS * A_DK), lambda i: (i, 0)),
            pl.BlockSpec((R, A_HEADS * A_DK), lambda i: (i, 1)),
            pl.BlockSpec((R, A_HEADS * A_DV), lambda i: (i, 1)),
            pl.BlockSpec((R, LANES), lambda i: (i, 0)),
            pl.BlockSpec((2, A_HEADS, A_DK, A_DV), st),
            pl.BlockSpec((2, A_HEADS, 1, A_DK), st),
            pl.BlockSpec((2, A_HEADS, 1, LANES), st),
        ],
        out_specs=[
            pl.BlockSpec((R, A_HEADS * A_DV), lambda i: (i, 0)),
            pl.BlockSpec((2, A_HEADS, A_DK, A_DV), st),
            pl.BlockSpec((2, A_HEADS, 1, A_DK), st),
            pl.BlockSpec((2, A_HEADS, 1, LANES), st),
        ],
        out_shape=[
            jax.ShapeDtypeStruct((S_ROWS, D_MODEL), F32),
            jax.ShapeDtypeStruct((DEC_BATCH, A_HEADS, A_DK, A_DV), F32),
            jax.ShapeDtypeStruct((DEC_BATCH, A_HEADS, 1, A_DK), F32),
            jax.ShapeDtypeStruct((DEC_BATCH, A_HEADS, 1, LANES), F32),
        ],
        compiler_params=_params(1),
        name="mlstm_sample",
    )(p, p, p, gates, c0, n0, m0)


def _mlstm_out_kernel(hm_ref, o_ref, hg_ref, w_ref, res_ref, out_ref):
    cols = []
    for hd in range(A_HEADS):
        x = hm_ref[:, hd * A_DV:(hd + 1) * A_DV]
        cols.append(x * lax.rsqrt(jnp.mean(x * x, axis=-1, keepdims=True) + EPS))
    hn = jnp.concatenate(cols, axis=1) * hg_ref[...]
    o = o_ref[...]
    pre = (hn * (1.0 / (1.0 + jnp.exp(-o)))).astype(BF16)
    out_ref[...] = res_ref[...] + _dot(pre, w_ref[...])


def _mlstm_out(hm, p, head_gain, w, res, tm):
    rows = hm.shape[0]
    return pl.pallas_call(
        _mlstm_out_kernel,
        grid=(rows // tm,),
        in_specs=[
            pl.BlockSpec((tm, D_MODEL), lambda i: (i, 0)),
            pl.BlockSpec((tm, D_MODEL), lambda i: (i, 2)),
            pl.BlockSpec((1, D_MODEL), lambda i: (0, 0)),
            pl.BlockSpec((D_MODEL, D_MODEL), lambda i: (0, 0)),
            pl.BlockSpec((tm, D_MODEL), lambda i: (i, 0)),
        ],
        out_specs=pl.BlockSpec((tm, D_MODEL), lambda i: (i, 0)),
        out_shape=jax.ShapeDtypeStruct((rows, D_MODEL), F32),
        compiler_params=_params(1),
        name="mlstm_out",
    )(hm, p, head_gain, w, res)


def _normproj_kernel(h_ref, g_ref, w_ref, hg_ref, o_ref, *, n_norm):
    xn = _rms(h_ref[...], g_ref[...]).astype(BF16)
    y = _dot(xn, w_ref[...])
    if n_norm == y.shape[1]:
        o_ref[...] = _head_norm64(y, hg_ref[...])
    else:
        o_ref[:, :n_norm] = _head_norm64(y[:, :n_norm], hg_ref[...])
        o_ref[:, n_norm:] = y[:, n_norm:]


def _normproj(h, gain, w, head_gain, tm):
    rows = h.shape[0]
    n = w.shape[1]
    n_norm = head_gain.shape[1]
    return pl.pallas_call(
        functools.partial(_normproj_kernel, n_norm=n_norm),
        grid=(rows // tm,),
        in_specs=[
            pl.BlockSpec((tm, D_MODEL), lambda i: (i, 0)),
            pl.BlockSpec((1, D_MODEL), lambda i: (0, 0)),
            pl.BlockSpec((D_MODEL, n), lambda i: (0, 0)),
            pl.BlockSpec((1, n_norm), lambda i: (0, 0)),
        ],
        out_specs=pl.BlockSpec((tm, n), lambda i: (i, 0)),
        out_shape=jax.ShapeDtypeStruct((rows, n), F32),
        compiler_params=_params(1),
        name="normproj",
    )(h, gain, w, head_gain)


def _matres_kernel(x_ref, w_ref, res_ref, o_ref):
    o_ref[...] = res_ref[...] + _dot(x_ref[...].astype(BF16), w_ref[...])


def _matres(x, w, res, tm):
    rows = x.shape[0]
    return pl.pallas_call(
        _matres_kernel,
        grid=(rows // tm,),
        in_specs=[
            pl.BlockSpec((tm, D_MODEL), lambda i: (i, 0)),
            pl.BlockSpec((D_MODEL, D_MODEL), lambda i: (0, 0)),
            pl.BlockSpec((tm, D_MODEL), lambda i: (i, 0)),
        ],
        out_specs=pl.BlockSpec((tm, D_MODEL), lambda i: (i, 0)),
        out_shape=jax.ShapeDtypeStruct((rows, D_MODEL), F32),
        compiler_params=_params(1),
        name="matres",
    )(x, w, res)


def _alibi_slope(head):
    return 2.0 ** (-8.0 * (head + 1) / B_HEADS)


def _sink_softmax_av(qh, kg, vg, bias, valid, sink):
    s = _dot_nt(qh, kg) * (B_DH ** -0.5) - bias
    s = jnp.where(valid, s, NEG_INF)
    mx = jnp.maximum(jnp.max(s, axis=-1, keepdims=True), sink)
    p = jnp.exp(s - mx)
    den = jnp.sum(p, axis=-1, keepdims=True) + jnp.exp(sink - mx)
    return _dot((p / den).astype(BF16), vg)


def _attn_prompt_kernel(q_ref, kvo_ref, kvp_ref, kvm_ref, sink_ref, o_ref):
    W = WINDOW
    n_keys = 3 * LANES
    j = pl.program_id(1)
    kvo = kvo_ref[...]
    kvp = kvp_ref[...]
    kvm = kvm_ref[...]
    pad = jnp.zeros((n_keys - N_META - 2 * W, B_KV), F32)
    kcat = jnp.concatenate([kvm[:, :B_KV], kvp[:, :B_KV], kvo[:, :B_KV], pad], axis=0).astype(BF16)
    vcat = jnp.concatenate([kvm[:, B_KV:], kvp[:, B_KV:], kvo[:, B_KV:], pad], axis=0).astype(BF16)

    i = lax.broadcasted_iota(jnp.int32, (W, n_keys), 0)
    c = lax.broadcasted_iota(jnp.int32, (W, n_keys), 1)
    is_meta = c < N_META
    is_prev = (c >= N_META) & (c < N_META + W)
    is_own = (c >= N_META + W) & (c < N_META + 2 * W)
    c_prev = c - N_META
    c_own = c - (N_META + W)
    dist_meta = jnp.minimum(j * W + i + N_META - c, W)
    dist = jnp.where(is_meta, dist_meta, jnp.where(is_prev, W + i - c_prev, i - c_own)).astype(F32)
    valid = is_meta | (is_prev & (j > 0) & (c_prev > i)) | (is_own & (c_own <= i))

    outs = []
    for g in range(B_KV_HEADS):
        kg = kcat[:, g * B_DH:(g + 1) * B_DH]
        vg = vcat[:, g * B_DH:(g + 1) * B_DH]
        for r in range(B_GROUP):
            hd = g * B_GROUP + r
            qh = q_ref[:, hd * B_DH:(hd + 1) * B_DH].astype(BF16)
            outs.append(_sink_softmax_av(qh, kg, vg, _alibi_slope(hd) * dist, valid,
                                         sink_ref[0:1, hd:hd + 1]))
    o_ref[...] = jnp.concatenate(outs, axis=1)


def _attn_prompt(q, kv_p, kv_q, sinks_row):
    nb = SEQ // WINDOW
    blk = lambda b, j: b * nb + j
    return pl.pallas_call(
        _attn_prompt_kernel,
        grid=(BATCH, nb),
        in_specs=[
            pl.BlockSpec((WINDOW, D_MODEL), lambda b, j: (blk(b, j), 0)),
            pl.BlockSpec((WINDOW, 2 * B_KV), lambda b, j: (blk(b, j), 0)),
            pl.BlockSpec((WINDOW, 2 * B_KV), lambda b, j: (blk(b, jnp.maximum(j - 1, 0)), 0)),
            pl.BlockSpec((N_META, 2 * B_KV), lambda b, j: (S_ROWS // N_META, 0)),
            pl.BlockSpec((1, LANES), lambda b, j: (0, 0)),
        ],
        out_specs=pl.BlockSpec((WINDOW, D_MODEL), lambda b, j: (blk(b, j), 0)),
        out_shape=jax.ShapeDtypeStruct((P_ROWS, D_MODEL), F32),
        compiler_params=_params(2),
        name="attn_prompt",
    )(q, kv_p, kv_p, kv_q, sinks_row)


GROUP_SHIFT = B_GROUP.bit_length() - 1
assert 1 << GROUP_SHIFT == B_GROUP
SAMPLE_OLD = SUBLANES
SAMPLE_SEQ_PER_STEP = 8


def _attn_sample_kernel(q_ref, km_ref, vm_ref, ko_ref, vo_ref, kw_ref, vw_ref, sink_ref, o_ref):
    W = WINDOW
    n_keys = 2 * LANES
    R = DEC_SEQ * B_GROUP
    off_w = N_META + SAMPLE_OLD
    t = jnp.right_shift(lax.broadcasted_iota(jnp.int32, (R, n_keys), 0), GROUP_SHIFT)
    c = lax.broadcasted_iota(jnp.int32, (R, n_keys), 1)
    is_meta = c < N_META
    is_old = (c >= N_META) & (c < off_w)
    is_win = (c >= off_w) & (c < off_w + W)
    j_old = c - N_META
    d_win = (W - DEC_SEQ) + t - (c - off_w)
    dist = jnp.where(is_meta, W, jnp.where(is_old, W + t - j_old, d_win)).astype(F32)
    valid = is_meta | (is_old & (j_old > t) & (j_old < DEC_SEQ)) | (is_win & (d_win >= 0))
    pad = jnp.zeros((n_keys - off_w - W, B_KV), F32)

    for sq in range(SAMPLE_SEQ_PER_STEP):
        kcat = jnp.concatenate([km_ref[sq], ko_ref[sq], kw_ref[sq], pad], axis=0).astype(BF16)
        vcat = jnp.concatenate([vm_ref[sq], vo_ref[sq], vw_ref[sq], pad], axis=0).astype(BF16)
        for g in range(B_KV_HEADS):
            kg = kcat[:, g * B_DH:(g + 1) * B_DH]
            vg = vcat[:, g * B_DH:(g + 1) * B_DH]
            row = jnp.bitwise_and(lax.broadcasted_iota(jnp.int32, (R, 1), 0), B_GROUP - 1)
            slope = jnp.zeros((R, 1), F32)
            sink = jnp.zeros((R, 1), F32)
            for r in range(B_GROUP):
                hd = g * B_GROUP + r
                slope = jnp.where(row == r, _alibi_slope(hd), slope)
                sink = jnp.where(row == r, sink_ref[0:1, hd:hd + 1], sink)
            o_ref[sq, g] = _sink_softmax_av(q_ref[sq, g].astype(BF16), kg, vg, slope * dist, valid, sink)


def _attn_sample(q4, k_meta, v_meta, k_old, v_old, k_win, v_win, sinks_row):
    nb = SAMPLE_SEQ_PER_STEP
    R = DEC_SEQ * B_GROUP
    seq3 = lambda rows: pl.BlockSpec((nb, rows, B_KV), lambda i: (i, 0, 0))
    qspec = pl.BlockSpec((nb, B_KV_HEADS, R, B_DH), lambda i: (i, 0, 0, 0))
    return pl.pallas_call(
        _attn_sample_kernel,
        grid=(DEC_BATCH // nb,),
        in_specs=[qspec, seq3(N_META), seq3(N_META), seq3(SAMPLE_OLD), seq3(SAMPLE_OLD),
                  seq3(WINDOW), seq3(WINDOW), pl.BlockSpec((1, LANES), lambda i: (0, 0))],
        out_specs=qspec,
        out_shape=jax.ShapeDtypeStruct((DEC_BATCH, B_KV_HEADS, R, B_DH), F32),
        compiler_params=_params(1),
        name="attn_sample",
    )(q4, k_meta, v_meta, k_old, v_old, k_win, v_win, sinks_row)


def kernel(x_prompt, x_sample, state_C, state_n, state_m, cache_k_meta, cache_v_meta, cache_k_win, cache_v_win, meta_tokens, ffn_norm, w_ffn_in, w_ffn_out, mix_norm, w_a_in, b_a_gate, a_head_norm, w_a_out, kv_norm, w_kv, k_norm, w_q, q_norm, sinks, w_b_out):
    assert x_prompt.shape == (BATCH, SEQ, D_MODEL) and x_sample.shape == (DEC_BATCH, DEC_SEQ, D_MODEL)
    assert w_a_in.shape[0] == 1 and w_q.shape[0] == 1 and ffn_norm.shape[0] == 2

    wf_in = w_ffn_in.astype(BF16)
    wf_out = w_ffn_out.astype(BF16)
    wa_in = w_a_in[0].astype(BF16)
    wa_gate = jnp.pad(w_a_in[0][:, A_QKVO:], ((0, 0), (0, LANES - 2 * A_HEADS))).astype(BF16)
    ba_gate = jnp.pad(b_a_gate[0].astype(F32), (0, LANES - 2 * A_HEADS)).reshape(1, LANES)
    wa_out = w_a_out[0].astype(BF16)
    wkv = w_kv.astype(BF16)
    wq = w_q[0].astype(BF16)
    wb_out = w_b_out[0].astype(BF16)
    row = lambda x: x.astype(F32).reshape(1, -1)
    k_gain = jnp.tile(row(k_norm), (1, B_KV_HEADS))
    q_gain = jnp.tile(row(q_norm[0]), (1, B_HEADS))
    sinks_row = jnp.pad(sinks[0].astype(F32), (0, LANES - B_HEADS)).reshape(1, LANES)

    h_p = x_prompt.reshape(P_ROWS, D_MODEL)
    h_q = jnp.concatenate([x_sample.reshape(S_ROWS, D_MODEL), meta_tokens.astype(F32),
                           jnp.zeros((A_CHUNK - N_META, D_MODEL), F32)], axis=0)
    TM_P, TM_W = 512, 256

    h_p = _ffn(h_p, row(ffn_norm[0, 0]), wf_in, wf_out, 0, 0, TM_P)
    h_q = _ffn(h_q, row(ffn_norm[0, 0]), wf_in, wf_out, 0, 0, Q_ROWS)
    p_p, g_p = _inproj(h_p, row(mix_norm[0]), wa_in, wa_gate, ba_gate, TM_P)
    p_q, g_q = _inproj(h_q, row(mix_norm[0]), wa_in, wa_gate, ba_gate, Q_ROWS)

    zc = jnp.zeros((1, A_HEADS, A_DK, A_DV), F32)
    zn = jnp.zeros((1, A_HEADS, 1, A_DK), F32)
    zm = jnp.zeros((1, A_HEADS, 1, LANES), F32)
    hm_m, c_m, n_m, m_m = _mlstm_chunks(p_q, g_q, zc, zn, zm, 1, 1, META_BLOCK, N_META, True)
    hm_p, c_p, n_p, m_p = _mlstm_chunks(p_p, g_p, c_m, n_m, m_m, BATCH, SEQ // A_CHUNK, 0, A_CHUNK, True)
    m0_s = jnp.broadcast_to(state_m[0].astype(F32)[:, :, None, None], (DEC_BATCH, A_HEADS, 1, LANES))
    hm_s, c_s, n_s, m_s = _mlstm_sample(p_q, g_q, state_C[0].astype(F32),
                                        state_n[0].astype(F32)[:, :, None, :], m0_s)
    hm_q = jnp.concatenate([hm_s, hm_m], axis=0)

    h_p = _mlstm_out(hm_p, p_p, row(a_head_norm[0]), wa_out, h_p, TM_W)
    h_q = _mlstm_out(hm_q, p_q, row(a_head_norm[0]), wa_out, h_q, A_CHUNK)
    h_p = _ffn(h_p, row(ffn_norm[0, 1]), wf_in, wf_out, 0, 1, TM_P)
    h_q = _ffn(h_q, row(ffn_norm[0, 1]), wf_in, wf_out, 0, 1, Q_ROWS)

    kv_p = _normproj(h_p, row(kv_norm), wkv, k_gain, TM_P)
    kv_q = _normproj(h_q, row(kv_norm), wkv, k_gain, Q_ROWS)
    k_s = kv_q[:S_ROWS, :B_KV].reshape(DEC_BATCH, DEC_SEQ, B_KV_HEADS, B_DH)
    v_s = kv_q[:S_ROWS, B_KV:].reshape(DEC_BATCH, DEC_SEQ, B_KV_HEADS, B_DH)
    k_win_s = jnp.concatenate([cache_k_win, k_s.astype(cache_k_win.dtype)], axis=1)[:, -WINDOW:]
    v_win_s = jnp.concatenate([cache_v_win, v_s.astype(cache_v_win.dtype)], axis=1)[:, -WINDOW:]

    h_s = h_q[:S_ROWS]
    h_p = _ffn(h_p, row(ffn_norm[1, 0]), wf_in, wf_out, 1, 0, TM_P)
    h_s = _ffn(h_s, row(ffn_norm[1, 0]), wf_in, wf_out, 1, 0, S_ROWS)
    q_p = _normproj(h_p, row(mix_norm[1]), wq, q_gain, TM_W)
    q_s = _normproj(h_s, row(mix_norm[1]), wq, q_gain, TM_W)

    o_p = _attn_prompt(q_p, kv_p, kv_q, sinks_row)
    q4 = q_s.reshape(DEC_BATCH, DEC_SEQ, B_KV_HEADS, B_GROUP, B_DH).transpose(0, 2, 1, 3, 4)
    q4 = q4.reshape(DEC_BATCH, B_KV_HEADS, DEC_SEQ * B_GROUP, B_DH)
    seq3 = lambda x: x.astype(F32).reshape(DEC_BATCH, -1, B_KV)
    o4 = _attn_sample(q4, seq3(cache_k_meta), seq3(cache_v_meta), seq3(cache_k_win), seq3(cache_v_win),
                      seq3(k_win_s), seq3(v_win_s), sinks_row)
    o_s = o4.reshape(DEC_BATCH, B_KV_HEADS, DEC_SEQ, B_GROUP, B_DH).transpose(0, 2, 1, 3, 4)
    o_s = o_s.reshape(S_ROWS, D_MODEL)

    h_p = _matres(o_p, wb_out, h_p, TM_W)
    h_s = _matres(o_s, wb_out, h_s, TM_W)
    h_p = _ffn(h_p, row(ffn_norm[1, 1]), wf_in, wf_out, 1, 1, TM_P)
    h_s = _ffn(h_s, row(ffn_norm[1, 1]), wf_in, wf_out, 1, 1, S_ROWS)

    kv4 = lambda x: x.reshape(x.shape[:-1] + (B_KV_HEADS, B_DH))
    meta_rows = kv_q[S_ROWS:S_ROWS + N_META]
    kv_p3 = kv_p.reshape(BATCH, SEQ, 2 * B_KV)
    st = lambda x, dt: x[None].astype(dt)
    return (
        h_p.reshape(BATCH, SEQ, D_MODEL),
        h_s.reshape(DEC_BATCH, DEC_SEQ, D_MODEL),
        st(c_p, state_C.dtype), st(n_p[:, :, 0, :], state_n.dtype), st(m_p[:, :, 0, 0], state_m.dtype),
        jnp.broadcast_to(kv4(meta_rows[:, :B_KV])[None], (BATCH, N_META, B_KV_HEADS, B_DH)),
        jnp.broadcast_to(kv4(meta_rows[:, B_KV:])[None], (BATCH, N_META, B_KV_HEADS, B_DH)),
        kv4(kv_p3[:, -WINDOW:, :B_KV]), kv4(kv_p3[:, -WINDOW:, B_KV:]),
        st(c_s, state_C.dtype), st(n_s[:, :, 0, :], state_n.dtype), st(m_s[:, :, 0, 0], state_m.dtype),
        k_win_s, v_win_s,
    )
```

```python
import functools

import jax
import jax.numpy as jnp
from jax import lax
from jax.experimental import pallas as pl
from jax.experimental.pallas import tpu as pltpu

D_MODEL = 2048
BATCH = 8
SEQ = 2048
DEC_BATCH = 128
DEC_SEQ = 4
PAST_LEN = 8192
N_META = 16
A_HEADS = 4
A_DV = D_MODEL // A_HEADS
A_DK = A_DV // 2
A_CHUNK = 128
A_GATE_CAP = 15.0
A_QKVO = 2 * A_HEADS * A_DK + 2 * A_HEADS * A_DV
B_HEADS = 32
B_DH = D_MODEL // B_HEADS
B_KV_HEADS = 4
B_GROUP = B_HEADS // B_KV_HEADS
B_KV = B_KV_HEADS * B_DH
WINDOW = 128
D_FF = ((8 * D_MODEL // 3 + 255) // 256) * 256
EPS = 1e-6

LANES = 128
SUBLANES = 8
VMEM_LIMIT = 56 * 1024 * 1024

P_ROWS = BATCH * SEQ
S_ROWS = DEC_BATCH * DEC_SEQ
Q_ROWS = S_ROWS + A_CHUNK
META_BLOCK = S_ROWS // A_CHUNK

F32 = jnp.float32
BF16 = jnp.bfloat16
NEG_INF = float("-inf")


def _params(n_axes):
    return pltpu.CompilerParams(dimension_semantics=("arbitrary",) * n_axes,
                                vmem_limit_bytes=VMEM_LIMIT)


def _resident(shape):
    return pl.BlockSpec(shape, lambda i: (0, 0), pipeline_mode=pl.Buffered(1))


def _rms(x, g):
    return x * lax.rsqrt(jnp.mean(x * x, axis=-1, keepdims=True) + EPS) * g


def _dot(a, b):
    return jnp.dot(a, b, preferred_element_type=F32)


def _dot_nt(a, b):
    return lax.dot_general(a, b, (((1,), (1,)), ((), ())), preferred_element_type=F32)


def _dot_tn(a, b):
    return lax.dot_general(a, b, (((0,), (0,)), ((), ())), preferred_element_type=F32)


def _log_sigmoid(x):
    return -(jnp.maximum(-x, 0.0) + jnp.log1p(jnp.exp(-jnp.abs(x))))


def _head_norm64(y, gain):
    lo = lax.broadcasted_iota(jnp.int32, (1, LANES), 1) < B_DH
    cols = []
    for c in range(y.shape[1] // LANES):
        x = y[:, c * LANES:(c + 1) * LANES]
        xx = x * x
        s_lo = jnp.sum(jnp.where(lo, xx, 0.0), axis=-1, keepdims=True)
        s_hi = jnp.sum(jnp.where(lo, 0.0, xx), axis=-1, keepdims=True)
        scale = jnp.where(lo, lax.rsqrt(s_lo / B_DH + EPS), lax.rsqrt(s_hi / B_DH + EPS))
        cols.append(x * scale * gain[:, c * LANES:(c + 1) * LANES])
    return jnp.concatenate(cols, axis=1)


def _ffn_kernel(h_ref, g_ref, wg_ref, wu_ref, wo_ref, o_ref, xn_ref, *, n_ff):
    j = pl.program_id(1)

    @pl.when(j == 0)
    def _():
        xn_ref[...] = _rms(h_ref[...], g_ref[...]).astype(BF16)
        o_ref[...] = jnp.zeros_like(o_ref)

    xn = xn_ref[...]
    g = _dot(xn, wg_ref[...])
    u = _dot(xn, wu_ref[...])
    a = (g / (1.0 + jnp.exp(-g))) * u
    o_ref[...] += _dot(a.astype(BF16), wo_ref[...])

    @pl.when(j == n_ff - 1)
    def _():
        o_ref[...] = h_ref[...] + 0.5 * o_ref[...]


def _ffn(h, gain, w_in, w_out, layer, which, tm, tf=512):
    rows = h.shape[0]
    n_ff = D_FF // tf
    return pl.pallas_call(
        functools.partial(_ffn_kernel, n_ff=n_ff),
        grid=(rows // tm, n_ff),
        in_specs=[
            pl.BlockSpec((tm, D_MODEL), lambda i, j: (i, 0)),
            pl.BlockSpec((1, D_MODEL), lambda i, j: (0, 0)),
            pl.BlockSpec((None, None, D_MODEL, tf), lambda i, j: (layer, which, 0, j)),
            pl.BlockSpec((None, None, D_MODEL, tf), lambda i, j: (layer, which, 0, j + n_ff)),
            pl.BlockSpec((None, None, tf, D_MODEL), lambda i, j: (layer, which, j, 0)),
        ],
        out_specs=pl.BlockSpec((tm, D_MODEL), lambda i, j: (i, 0)),
        out_shape=jax.ShapeDtypeStruct((rows, D_MODEL), F32),
        scratch_shapes=[pltpu.VMEM((tm, D_MODEL), BF16)],
        compiler_params=_params(2),
        name="ffn",
    )(h, gain, w_in, w_in, w_out)


def _inproj_kernel(h_ref, g_ref, w_ref, wgate_ref, bgate_ref, p_ref, gates_ref, xn_ref):
    j = pl.program_id(1)

    @pl.when(j == 0)
    def _():
        xn = _rms(h_ref[...], g_ref[...]).astype(BF16)
        xn_ref[...] = xn
        pre = _dot(xn, wgate_ref[...]) + bgate_ref[...]
        capped = A_GATE_CAP * jnp.tanh(pre / A_GATE_CAP)
        lane = lax.broadcasted_iota(jnp.int32, (1, LANES), 1)
        gates_ref[...] = jnp.where(lane < A_HEADS, capped, _log_sigmoid(capped))

    p_ref[...] = _dot(xn_ref[...], w_ref[...])


def _inproj(h, gain, w, wgate, bgate, tm, tn=1024):
    rows = h.shape[0]
    return pl.pallas_call(
        _inproj_kernel,
        grid=(rows // tm, A_QKVO // tn),
        in_specs=[
            pl.BlockSpec((tm, D_MODEL), lambda i, j: (i, 0)),
            pl.BlockSpec((1, D_MODEL), lambda i, j: (0, 0)),
            pl.BlockSpec((D_MODEL, tn), lambda i, j: (0, j)),
            pl.BlockSpec((D_MODEL, LANES), lambda i, j: (0, 0)),
            pl.BlockSpec((1, LANES), lambda i, j: (0, 0)),
        ],
        out_specs=[
            pl.BlockSpec((tm, tn), lambda i, j: (i, j)),
            pl.BlockSpec((tm, LANES), lambda i, j: (i, 0)),
        ],
        out_shape=[jax.ShapeDtypeStruct((rows, A_QKVO), F32),
                   jax.ShapeDtypeStruct((rows, LANES), F32)],
        scratch_shapes=[pltpu.VMEM((tm, D_MODEL), BF16)],
        compiler_params=_params(2),
        name="mlstm_inproj",
    )(h, gain, w, wgate, bgate)


def _mlstm_chunk_kernel(q_ref, k_ref, v_ref, g_ref, c0_ref, n0_ref, m0_ref,
                        h_ref, c_ref, n_ref, m_ref, *, n_valid):
    L = A_CHUNK

    @pl.when(pl.program_id(1) == 0)
    def _():
        c_ref[...] = c0_ref[...]
        n_ref[...] = n0_ref[...]
        m_ref[...] = m0_ref[...]

    gates = g_ref[...]
    row = lax.broadcasted_iota(jnp.int32, (L, L), 0)
    col = lax.broadcasted_iota(jnp.int32, (L, L), 1)
    causal = col <= row
    eye = col == row
    masked = n_valid < L
    if masked:
        row_ok = lax.broadcasted_iota(jnp.int32, (L, 1), 0) < n_valid
        gates_lf = jnp.where(row_ok, gates, 0.0)
    else:
        gates_lf = gates
    csum = jnp.dot(causal.astype(F32), gates_lf, precision=lax.Precision.HIGHEST,
                   preferred_element_type=F32)

    def to_row(x_col):
        return jnp.sum(jnp.where(eye, x_col, 0.0), axis=0, keepdims=True)

    for hd in range(A_HEADS):
        b_col = csum[:, A_HEADS + hd:A_HEADS + hd + 1]
        ig_col = gates[:, hd:hd + 1]
        if masked:
            ig_col = jnp.where(row_ok, ig_col, NEG_INF)
        b_row = to_row(b_col)
        ig_row = to_row(ig_col)
        m_prev = m_ref[0, hd][:, 0:1]
        c_prev = c_ref[0, hd]
        n_prev = n_ref[0, hd]

        q = q_ref[:, hd * A_DK:(hd + 1) * A_DK]
        k = k_ref[:, hd * A_DK:(hd + 1) * A_DK] * (A_DK ** -0.5)
        v = v_ref[:, hd * A_DV:(hd + 1) * A_DV]
        qb = q.astype(BF16)
        vb = v.astype(BF16)

        d_log = jnp.where(causal, b_col - b_row + ig_row, NEG_INF)
        inter_log = b_col + m_prev
        m_t = jnp.maximum(inter_log, jnp.max(d_log, axis=-1, keepdims=True))
        w_intra = jnp.exp(d_log - m_t)
        w_inter = jnp.exp(inter_log - m_t)
        s = _dot_nt(qb, k.astype(BF16)) * w_intra
        num = _dot(s.astype(BF16), vb) + w_inter * _dot(qb, c_prev.astype(BF16))
        den = jnp.sum(s, axis=-1, keepdims=True) + w_inter * jnp.sum(q * n_prev, axis=-1, keepdims=True)
        den = jnp.maximum(jnp.abs(den), jnp.exp(-m_t))
        h_ref[:, hd * A_DV:(hd + 1) * A_DV] = num / den

        b_last = b_col[L - 1:L, :]
        w_log = b_last - b_col + ig_col
        m_new = jnp.maximum(b_last + m_prev, jnp.max(w_log, axis=0, keepdims=True))
        w_state = jnp.exp(w_log - m_new)
        decay = jnp.exp(b_last + m_prev - m_new)
        kw = k * w_state
        c_ref[0, hd] = decay * c_prev + _dot_tn(kw.astype(BF16), vb)
        n_ref[0, hd] = decay * n_prev + jnp.sum(kw, axis=0, keepdims=True)
        m_ref[0, hd] = jnp.broadcast_to(m_new, (1, LANES))


def _mlstm_chunks(p, gates, c0, n0, m0, n_seq, n_chunks, row_block0, n_valid, shared_state):
    L = A_CHUNK
    rb = lambda b, c: row_block0 + b * n_chunks + c
    st = (lambda b, c: (0, 0, 0, 0)) if shared_state else (lambda b, c: (b, 0, 0, 0))
    return pl.pallas_call(
        functools.partial(_mlstm_chunk_kernel, n_valid=n_valid),
        grid=(n_seq, n_chunks),
        in_specs=[
            pl.BlockSpec((L, A_HEADS * A_DK), lambda b, c: (rb(b, c), 0)),
            pl.BlockSpec((L, A_HEADS * A_DK), lambda b, c: (rb(b, c), 1)),
            pl.BlockSpec((L, A_HEADS * A_DV), lambda b, c: (rb(b, c), 1)),
            pl.BlockSpec((L, LANES), lambda b, c: (rb(b, c), 0)),
            pl.BlockSpec((1, A_HEADS, A_DK, A_DV), st),
            pl.BlockSpec((1, A_HEADS, 1, A_DK), st),
            pl.BlockSpec((1, A_HEADS, 1, LANES), st),
        ],
        out_specs=[
            pl.BlockSpec((L, A_HEADS * A_DV), lambda b, c: (b * n_chunks + c, 0)),
            pl.BlockSpec((1, A_HEADS, A_DK, A_DV), lambda b, c: (b, 0, 0, 0)),
            pl.BlockSpec((1, A_HEADS, 1, A_DK), lambda b, c: (b, 0, 0, 0)),
            pl.BlockSpec((1, A_HEADS, 1, LANES), lambda b, c: (b, 0, 0, 0)),
        ],
        out_shape=[
            jax.ShapeDtypeStruct((n_seq * n_chunks * L, D_MODEL), F32),
            jax.ShapeDtypeStruct((n_seq, A_HEADS, A_DK, A_DV), F32),
            jax.ShapeDtypeStruct((n_seq, A_HEADS, 1, A_DK), F32),
            jax.ShapeDtypeStruct((n_seq, A_HEADS, 1, LANES), F32),
        ],
        compiler_params=_params(2),
        name="mlstm_chunks",
    )(p, p, p, gates, c0, n0, m0)


def _mlstm_sample_kernel(q_ref, k_ref, v_ref, g_ref, c0_ref, n0_ref, m0_ref,
                         h_ref, c_ref, n_ref, m_ref):
    R = 2 * DEC_SEQ
    PAD = A_CHUNK - R
    gates = g_ref[...]
    r_col = lax.broadcasted_iota(jnp.int32, (R, 1), 0)
    is_a = r_col < DEC_SEQ
    row = lax.broadcasted_iota(jnp.int32, (R, LANES), 0)
    lane = lax.broadcasted_iota(jnp.int32, (R, LANES), 1)
    same = ((lane < DEC_SEQ) & (row < DEC_SEQ)) | ((lane >= DEC_SEQ) & (lane < R) & (row >= DEC_SEQ))
    causal = same & (lane <= row)
    eye = lane == row

    def to_row(x_col):
        return jnp.sum(jnp.where(eye, x_col, 0.0), axis=0, keepdims=True)

    for hd in range(A_HEADS):
        lf_col = gates[:, A_HEADS + hd:A_HEADS + hd + 1]
        ig_col = gates[:, hd:hd + 1]
        lf_row = to_row(lf_col)
        ig_row = to_row(ig_col)
        b_col = jnp.sum(jnp.where(causal, lf_row, 0.0), axis=1, keepdims=True)
        b_row = to_row(b_col)
        m_a = m0_ref[0, hd][:, 0:1]
        m_b = m0_ref[1, hd][:, 0:1]
        m_prev = jnp.where(is_a, m_a, m_b)
        c_a = c0_ref[0, hd]
        c_b = c0_ref[1, hd]
        n_a = n0_ref[0, hd]
        n_b = n0_ref[1, hd]

        q = q_ref[:, hd * A_DK:(hd + 1) * A_DK]
        k = k_ref[:, hd * A_DK:(hd + 1) * A_DK] * (A_DK ** -0.5)
        v = v_ref[:, hd * A_DV:(hd + 1) * A_DV]
        qb = q.astype(BF16)
        k_pad = jnp.concatenate([k, jnp.zeros((PAD, A_DK), F32)], axis=0).astype(BF16)
        v_pad = jnp.concatenate([v, jnp.zeros((PAD, A_DV), F32)], axis=0).astype(BF16)

        d_log = jnp.where(causal, b_col - b_row + ig_row, NEG_INF)
        inter_log = b_col + m_prev
        m_t = jnp.maximum(inter_log, jnp.max(d_log, axis=-1, keepdims=True))
        w_intra = jnp.exp(d_log - m_t)
        w_inter = jnp.exp(inter_log - m_t)
        s = _dot_nt(qb, k_pad) * w_intra
        q_c = jnp.where(is_a, _dot(qb, c_a.astype(BF16)), _dot(qb, c_b.astype(BF16)))
        num = _dot(s.astype(BF16), v_pad) + w_inter * q_c
        q_n = jnp.sum(q * jnp.where(is_a, n_a, n_b), axis=-1, keepdims=True)
        den = jnp.sum(s, axis=-1, keepdims=True) + w_inter * q_n
        den = jnp.maximum(jnp.abs(den), jnp.exp(-m_t))
        h_ref[:, hd * A_DV:(hd + 1) * A_DV] = num / den

        for idx, sel, m_x, c_x, n_x in ((0, is_a, m_a, c_a, n_a),
                                        (1, jnp.logical_not(is_a), m_b, c_b, n_b)):
            last = (idx + 1) * DEC_SEQ - 1
            b_last = b_col[last:last + 1, :]
            w_log = jnp.where(sel, b_last - b_col + ig_col, NEG_INF)
            m_new = jnp.maximum(b_last + m_x, jnp.max(w_log, axis=0, keepdims=True))
            w_state = jnp.exp(w_log - m_new)
            decay = jnp.exp(b_last + m_x - m_new)
            kw = k * w_state
            kw_pad = jnp.concatenate([kw, jnp.zeros((PAD, A_DK), F32)], axis=0).astype(BF16)
            c_ref[idx, hd] = decay * c_x + _dot_tn(kw_pad, v_pad)
            n_ref[idx, hd] = decay * n_x + jnp.sum(kw, axis=0, keepdims=True)
            m_ref[idx, hd] = jnp.broadcast_to(m_new, (1, LANES))


def _mlstm_sample(p, gates, c0, n0, m0):
    R = 2 * DEC_SEQ
    n_pairs = DEC_BATCH // 2
    st = lambda i: (i, 0, 0, 0)
    return pl.pallas_call(
        _mlstm_sample_kernel,
        grid=(n_pairs,),
        in_specs=[
            pl.BlockSpec((R, A_HEADS * A_DK), lambda i: (i, 0)),
            pl.BlockSpec((R, A_HEADS * A_DK), lambda i: (i, 1)),
            pl.BlockSpec((R, A_HEADS * A_DV), lambda i: (i, 1)),
            pl.BlockSpec((R, LANES), lambda i: (i, 0)),
            pl.BlockSpec((2, A_HEADS, A_DK, A_DV), st),
            pl.BlockSpec((2, A_HEADS, 1, A_DK), st),
            pl.BlockSpec((2, A_HEADS, 1, LANES), st),
        ],
        out_specs=[
            pl.BlockSpec((R, A_HEADS * A_DV), lambda i: (i, 0)),
            pl.BlockSpec((2, A_HEADS, A_DK, A_DV), st),
            pl.BlockSpec((2, A_HEADS, 1, A_DK), st),
            pl.BlockSpec((2, A_HEADS, 1, LANES), st),
        ],
        out_shape=[
            jax.ShapeDtypeStruct((S_ROWS, D_MODEL), F32),
            jax.ShapeDtypeStruct((DEC_BATCH, A_HEADS, A_DK, A_DV), F32),
            jax.ShapeDtypeStruct((DEC_BATCH, A_HEADS, 1, A_DK), F32),
            jax.ShapeDtypeStruct((DEC_BATCH, A_HEADS, 1, LANES), F32),
        ],
        compiler_params=_params(1),
        name="mlstm_sample",
    )(p, p, p, gates, c0, n0, m0)


def _mlstm_out_kernel(hm_ref, o_ref, hg_ref, w_ref, res_ref, out_ref):
    cols = []
    for hd in range(A_HEADS):
        x = hm_ref[:, hd * A_DV:(hd + 1) * A_DV]
        cols.append(x * lax.rsqrt(jnp.mean(x * x, axis=-1, keepdims=True) + EPS))
    hn = jnp.concatenate(cols, axis=1) * hg_ref[...]
    o = o_ref[...]
    pre = (hn * (1.0 / (1.0 + jnp.exp(-o)))).astype(BF16)
    out_ref[...] = res_ref[...] + _dot(pre, w_ref[...])


def _mlstm_out(hm, p, head_gain, w, res, tm):
    rows = hm.shape[0]
    return pl.pallas_call(
        _mlstm_out_kernel,
        grid=(rows // tm,),
        in_specs=[
            pl.BlockSpec((tm, D_MODEL), lambda i: (i, 0)),
            pl.BlockSpec((tm, D_MODEL), lambda i: (i, 2)),
            pl.BlockSpec((1, D_MODEL), lambda i: (0, 0)),
            _resident((D_MODEL, D_MODEL)),
            pl.BlockSpec((tm, D_MODEL), lambda i: (i, 0)),
        ],
        out_specs=pl.BlockSpec((tm, D_MODEL), lambda i: (i, 0)),
        out_shape=jax.ShapeDtypeStruct((rows, D_MODEL), F32),
        compiler_params=_params(1),
        name="mlstm_out",
    )(hm, p, head_gain, w, res)


def _normproj_kernel(h_ref, g_ref, w_ref, hg_ref, o_ref, *, n_norm):
    xn = _rms(h_ref[...], g_ref[...]).astype(BF16)
    y = _dot(xn, w_ref[...])
    if n_norm == y.shape[1]:
        o_ref[...] = _head_norm64(y, hg_ref[...])
    else:
        o_ref[:, :n_norm] = _head_norm64(y[:, :n_norm], hg_ref[...])
        o_ref[:, n_norm:] = y[:, n_norm:]


def _normproj(h, gain, w, head_gain, tm):
    rows = h.shape[0]
    n = w.shape[1]
    n_norm = head_gain.shape[1]
    return pl.pallas_call(
        functools.partial(_normproj_kernel, n_norm=n_norm),
        grid=(rows // tm,),
        in_specs=[
            pl.BlockSpec((tm, D_MODEL), lambda i: (i, 0)),
            pl.BlockSpec((1, D_MODEL), lambda i: (0, 0)),
            _resident((D_MODEL, n)),
            pl.BlockSpec((1, n_norm), lambda i: (0, 0)),
        ],
        out_specs=pl.BlockSpec((tm, n), lambda i: (i, 0)),
        out_shape=jax.ShapeDtypeStruct((rows, n), F32),
        compiler_params=_params(1),
        name="normproj",
    )(h, gain, w, head_gain)


def _matres_kernel(x_ref, w_ref, res_ref, o_ref):
    o_ref[...] = res_ref[...] + _dot(x_ref[...].astype(BF16), w_ref[...])


def _matres(x, w, res, tm):
    rows = x.shape[0]
    return pl.pallas_call(
        _matres_kernel,
        grid=(rows // tm,),
        in_specs=[
            pl.BlockSpec((tm, D_MODEL), lambda i: (i, 0)),
            _resident((D_MODEL, D_MODEL)),
            pl.BlockSpec((tm, D_MODEL), lambda i: (i, 0)),
        ],
        out_specs=pl.BlockSpec((tm, D_MODEL), lambda i: (i, 0)),
        out_shape=jax.ShapeDtypeStruct((rows, D_MODEL), F32),
        compiler_params=_params(1),
        name="matres",
    )(x, w, res)


def _alibi_slope(head):
    return 2.0 ** (-8.0 * (head + 1) / B_HEADS)


def _sink_softmax_av(qh, kg, vg, bias, valid, sink):
    s = _dot_nt(qh, kg) * (B_DH ** -0.5) - bias
    s = jnp.where(valid, s, NEG_INF)
    mx = jnp.maximum(jnp.max(s, axis=-1, keepdims=True), sink)
    p = jnp.exp(s - mx)
    den = jnp.sum(p, axis=-1, keepdims=True) + jnp.exp(sink - mx)
    return _dot((p / den).astype(BF16), vg)


PROMPT_SUB = WINDOW // 2
PROMPT_BAND = WINDOW + PROMPT_SUB
PROMPT_KEYS = 2 * LANES
HEAD_PAIRS = B_HEADS // 2
PAIRS_PER_GROUP = B_GROUP // 2


def _pair_rhs(x, c):
    col = x[:, c * LANES:(c + 1) * LANES]
    rol = pltpu.roll(col, B_DH, axis=1)
    lo = lax.broadcasted_iota(jnp.int32, (1, LANES), 1) < B_DH
    even = jnp.concatenate([jnp.where(lo, col, 0.0), jnp.where(lo, 0.0, rol)], axis=0)
    odd = jnp.concatenate([jnp.where(lo, rol, 0.0), jnp.where(lo, 0.0, col)], axis=0)
    return even.astype(BF16), odd.astype(BF16)


def _attn_prompt_kernel(q_ref, kvo_ref, kvp_ref, kvm_ref, sink_ref, o_ref, bias_ref, s_ref, p_ref):
    W = WINDOW
    SB = PROMPT_SUB
    NK = PROMPT_KEYS
    first = (pl.program_id(0) == 0) & (pl.program_id(1) == 0)
    j = pl.program_id(1)
    rows = PAIRS_PER_GROUP * SB
    lo = lax.broadcasted_iota(jnp.int32, (1, LANES), 1) < B_DH

    @pl.when(first)
    def _():
        c = lax.broadcasted_iota(jnp.int32, (SB, NK), 1)
        is_meta = c < N_META
        is_sink = c == NK - 1
        for sub in range(2):
            i = lax.broadcasted_iota(jnp.int32, (SB, NK), 0) + sub * SB
            pos = c - N_META + sub * SB
            rel = W + i - pos
            in_band = (c >= N_META) & (c < N_META + PROMPT_BAND) & (rel >= 0) & (rel < W)
            for variant in range(2):
                if variant == 0:
                    dist = jnp.where(is_meta, jnp.minimum(i + N_META - c, W), rel).astype(F32)
                    valid = is_meta | (in_band & (pos >= W))
                else:
                    dist = jnp.where(is_meta, W, rel).astype(F32)
                    valid = is_meta | in_band
                for hd in range(B_HEADS):
                    r0 = (hd // 2) * SB
                    c0 = (hd % 2) * NK
                    table = jnp.where(valid, -_alibi_slope(hd) * dist, NEG_INF)
                    bias_ref[variant, sub, r0:r0 + SB, c0:c0 + NK] = jnp.where(
                        is_sink, sink_ref[0:1, hd:hd + 1], table)

    variant = jnp.minimum(j, 1)
    kvm = kvm_ref[...]
    band = jnp.concatenate([kvp_ref[...], kvo_ref[...]], axis=0)
    pad = jnp.zeros((NK - N_META - PROMPT_BAND, 2 * B_KV), F32)
    ones_rhs = jnp.concatenate([jnp.broadcast_to(jnp.where(lo, 1.0, 0.0), (NK, LANES)),
                                jnp.broadcast_to(jnp.where(lo, 0.0, 1.0), (NK, LANES))], axis=0).astype(BF16)

    v_rhs = []
    for sub in range(2):
        keys = jnp.concatenate([kvm, band[sub * SB:sub * SB + PROMPT_BAND], pad], axis=0)
        k_rhs = _pair_rhs(keys, 0) + _pair_rhs(keys, 1)
        v_rhs.append(_pair_rhs(keys, 2) + _pair_rhs(keys, 3))
        for g in range(B_KV_HEADS):
            p0 = g * PAIRS_PER_GROUP
            r0 = (sub * B_KV_HEADS + g) * rows
            qg = jnp.concatenate([q_ref[sub * SB:(sub + 1) * SB, (p0 + m) * LANES:(p0 + m + 1) * LANES]
                                  for m in range(PAIRS_PER_GROUP)], axis=0) * (B_DH ** -0.5)
            s_ref[r0:r0 + rows, :] = _dot_nt(qg.astype(BF16), k_rhs[g])

    s = s_ref[...] + bias_ref[variant].reshape(2 * HEAD_PAIRS * SB, 2 * NK)
    halves = []
    for half in range(2):
        sh = s[:, half * NK:(half + 1) * NK]
        halves.append(jnp.exp(sh - jnp.max(sh, axis=-1, keepdims=True)).astype(BF16))
    p_ref[...] = jnp.concatenate(halves, axis=1)

    for sub in range(2):
        for g in range(B_KV_HEADS):
            p0 = g * PAIRS_PER_GROUP
            r0 = (sub * B_KV_HEADS + g) * rows
            od = _dot(p_ref[r0:r0 + rows, :], jnp.concatenate([v_rhs[sub][g], ones_rhs], axis=1))
            o = od[:, :LANES] / od[:, LANES:]
            for m in range(PAIRS_PER_GROUP):
                o_ref[sub * SB:(sub + 1) * SB, (p0 + m) * LANES:(p0 + m + 1) * LANES] = o[m * SB:(m + 1) * SB, :]


def _attn_prompt(q, kv_p, kv_q, sinks_row):
    nb = SEQ // WINDOW
    blk = lambda b, j: b * nb + j
    return pl.pallas_call(
        _attn_prompt_kernel,
        grid=(BATCH, nb),
        in_specs=[
            pl.BlockSpec((WINDOW, D_MODEL), lambda b, j: (blk(b, j), 0)),
            pl.BlockSpec((WINDOW, 2 * B_KV), lambda b, j: (blk(b, j), 0)),
            pl.BlockSpec((WINDOW, 2 * B_KV), lambda b, j: (blk(b, jnp.maximum(j - 1, 0)), 0)),
            pl.BlockSpec((N_META, 2 * B_KV), lambda b, j: (S_ROWS // N_META, 0)),
            pl.BlockSpec((1, LANES), lambda b, j: (0, 0)),
        ],
        out_specs=pl.BlockSpec((WINDOW, D_MODEL), lambda b, j: (blk(b, j), 0)),
        out_shape=jax.ShapeDtypeStruct((P_ROWS, D_MODEL), F32),
        scratch_shapes=[pltpu.VMEM((2, 2, HEAD_PAIRS * PROMPT_SUB, 2 * PROMPT_KEYS), F32),
                        pltpu.VMEM((2 * HEAD_PAIRS * PROMPT_SUB, 2 * PROMPT_KEYS), F32),
                        pltpu.VMEM((2 * HEAD_PAIRS * PROMPT_SUB, 2 * PROMPT_KEYS), BF16)],
        compiler_params=_params(2),
        name="attn_prompt",
    )(q, kv_p, kv_p, kv_q, sinks_row)


GROUP_SHIFT = B_GROUP.bit_length() - 1
assert 1 << GROUP_SHIFT == B_GROUP
SAMPLE_OLD = SUBLANES
SAMPLE_SEQ_PER_STEP = 8


def _attn_sample_kernel(q_ref, km_ref, vm_ref, ko_ref, vo_ref, kw_ref, vw_ref, sink_ref, o_ref):
    W = WINDOW
    n_keys = 2 * LANES
    R = DEC_SEQ * B_GROUP
    off_w = N_META + SAMPLE_OLD
    t = jnp.right_shift(lax.broadcasted_iota(jnp.int32, (R, n_keys), 0), GROUP_SHIFT)
    c = lax.broadcasted_iota(jnp.int32, (R, n_keys), 1)
    is_meta = c < N_META
    is_old = (c >= N_META) & (c < off_w)
    is_win = (c >= off_w) & (c < off_w + W)
    j_old = c - N_META
    d_win = (W - DEC_SEQ) + t - (c - off_w)
    dist = jnp.where(is_meta, W, jnp.where(is_old, W + t - j_old, d_win)).astype(F32)
    valid = is_meta | (is_old & (j_old > t) & (j_old < DEC_SEQ)) | (is_win & (d_win >= 0))
    pad = jnp.zeros((n_keys - off_w - W, B_KV), F32)

    for sq in range(SAMPLE_SEQ_PER_STEP):
        kcat = jnp.concatenate([km_ref[sq], ko_ref[sq], kw_ref[sq], pad], axis=0).astype(BF16)
        vcat = jnp.concatenate([vm_ref[sq], vo_ref[sq], vw_ref[sq], pad], axis=0).astype(BF16)
        for g in range(B_KV_HEADS):
            kg = kcat[:, g * B_DH:(g + 1) * B_DH]
            vg = vcat[:, g * B_DH:(g + 1) * B_DH]
            row = jnp.bitwise_and(lax.broadcasted_iota(jnp.int32, (R, 1), 0), B_GROUP - 1)
            slope = jnp.zeros((R, 1), F32)
            sink = jnp.zeros((R, 1), F32)
            for r in range(B_GROUP):
                hd = g * B_GROUP + r
                slope = jnp.where(row == r, _alibi_slope(hd), slope)
                sink = jnp.where(row == r, sink_ref[0:1, hd:hd + 1], sink)
            o_ref[sq, g] = _sink_softmax_av(q_ref[sq, g].astype(BF16), kg, vg, slope * dist, valid, sink)


def _attn_sample(q4, k_meta, v_meta, k_old, v_old, k_win, v_win, sinks_row):
    nb = SAMPLE_SEQ_PER_STEP
    R = DEC_SEQ * B_GROUP
    seq3 = lambda rows: pl.BlockSpec((nb, rows, B_KV), lambda i: (i, 0, 0))
    qspec = pl.BlockSpec((nb, B_KV_HEADS, R, B_DH), lambda i: (i, 0, 0, 0))
    return pl.pallas_call(
        _attn_sample_kernel,
        grid=(DEC_BATCH // nb,),
        in_specs=[qspec, seq3(N_META), seq3(N_META), seq3(SAMPLE_OLD), seq3(SAMPLE_OLD),
                  seq3(WINDOW), seq3(WINDOW), pl.BlockSpec((1, LANES), lambda i: (0, 0))],
        out_specs=qspec,
        out_shape=jax.ShapeDtypeStruct((DEC_BATCH, B_KV_HEADS, R, B_DH), F32),
        compiler_params=_params(1),
        name="attn_sample",
    )(q4, k_meta, v_meta, k_old, v_old, k_win, v_win, sinks_row)


def kernel(x_prompt, x_sample, state_C, state_n, state_m, cache_k_meta, cache_v_meta, cache_k_win, cache_v_win, meta_tokens, ffn_norm, w_ffn_in, w_ffn_out, mix_norm, w_a_in, b_a_gate, a_head_norm, w_a_out, kv_norm, w_kv, k_norm, w_q, q_norm, sinks, w_b_out):
    assert x_prompt.shape == (BATCH, SEQ, D_MODEL) and x_sample.shape == (DEC_BATCH, DEC_SEQ, D_MODEL)
    assert w_a_in.shape[0] == 1 and w_q.shape[0] == 1 and ffn_norm.shape[0] == 2

    wf_in = w_ffn_in.astype(BF16)
    wf_out = w_ffn_out.astype(BF16)
    wa_in = w_a_in[0].astype(BF16)
    wa_gate = jnp.pad(w_a_in[0][:, A_QKVO:], ((0, 0), (0, LANES - 2 * A_HEADS))).astype(BF16)
    ba_gate = jnp.pad(b_a_gate[0].astype(F32), (0, LANES - 2 * A_HEADS)).reshape(1, LANES)
    wa_out = w_a_out[0].astype(BF16)
    wkv = w_kv.astype(BF16)
    wq = w_q[0].astype(BF16)
    wb_out = w_b_out[0].astype(BF16)
    row = lambda x: x.astype(F32).reshape(1, -1)
    k_gain = jnp.tile(row(k_norm), (1, B_KV_HEADS))
    q_gain = jnp.tile(row(q_norm[0]), (1, B_HEADS))
    sinks_row = jnp.pad(sinks[0].astype(F32), (0, LANES - B_HEADS)).reshape(1, LANES)

    h_p = x_prompt.reshape(P_ROWS, D_MODEL)
    h_q = jnp.concatenate([x_sample.reshape(S_ROWS, D_MODEL), meta_tokens.astype(F32),
                           jnp.zeros((A_CHUNK - N_META, D_MODEL), F32)], axis=0)
    TM_P, TM_W = 1024, 512

    h_p = _ffn(h_p, row(ffn_norm[0, 0]), wf_in, wf_out, 0, 0, TM_P)
    h_q = _ffn(h_q, row(ffn_norm[0, 0]), wf_in, wf_out, 0, 0, Q_ROWS)
    p_p, g_p = _inproj(h_p, row(mix_norm[0]), wa_in, wa_gate, ba_gate, TM_P)
    p_q, g_q = _inproj(h_q, row(mix_norm[0]), wa_in, wa_gate, ba_gate, Q_ROWS)

    zc = jnp.zeros((1, A_HEADS, A_DK, A_DV), F32)
    zn = jnp.zeros((1, A_HEADS, 1, A_DK), F32)
    zm = jnp.zeros((1, A_HEADS, 1, LANES), F32)
    hm_m, c_m, n_m, m_m = _mlstm_chunks(p_q, g_q, zc, zn, zm, 1, 1, META_BLOCK, N_META, True)
    hm_p, c_p, n_p, m_p = _mlstm_chunks(p_p, g_p, c_m, n_m, m_m, BATCH, SEQ // A_CHUNK, 0, A_CHUNK, True)
    m0_s = jnp.broadcast_to(state_m[0].astype(F32)[:, :, None, None], (DEC_BATCH, A_HEADS, 1, LANES))
    hm_s, c_s, n_s, m_s = _mlstm_sample(p_q, g_q, state_C[0].astype(F32),
                                        state_n[0].astype(F32)[:, :, None, :], m0_s)
    hm_q = jnp.concatenate([hm_s, hm_m], axis=0)

    h_p = _mlstm_out(hm_p, p_p, row(a_head_norm[0]), wa_out, h_p, TM_W)
    h_q = _mlstm_out(hm_q, p_q, row(a_head_norm[0]), wa_out, h_q, A_CHUNK)
    h_p = _ffn(h_p, row(ffn_norm[0, 1]), wf_in, wf_out, 0, 1, TM_P)
    h_q = _ffn(h_q, row(ffn_norm[0, 1]), wf_in, wf_out, 0, 1, Q_ROWS)

    kv_p = _normproj(h_p, row(kv_norm), wkv, k_gain, TM_P)
    kv_q = _normproj(h_q, row(kv_norm), wkv, k_gain, Q_ROWS)
    k_s = kv_q[:S_ROWS, :B_KV].reshape(DEC_BATCH, DEC_SEQ, B_KV_HEADS, B_DH)
    v_s = kv_q[:S_ROWS, B_KV:].reshape(DEC_BATCH, DEC_SEQ, B_KV_HEADS, B_DH)
    k_win_s = jnp.concatenate([cache_k_win, k_s.astype(cache_k_win.dtype)], axis=1)[:, -WINDOW:]
    v_win_s = jnp.concatenate([cache_v_win, v_s.astype(cache_v_win.dtype)], axis=1)[:, -WINDOW:]

    h_s = h_q[:S_ROWS]
    h_p = _ffn(h_p, row(ffn_norm[1, 0]), wf_in, wf_out, 1, 0, TM_P)
    h_s = _ffn(h_s, row(ffn_norm[1, 0]), wf_in, wf_out, 1, 0, S_ROWS)
    q_p = _normproj(h_p, row(mix_norm[1]), wq, q_gain, TM_W)
    q_s = _normproj(h_s, row(mix_norm[1]), wq, q_gain, TM_W)

    o_p = _attn_prompt(q_p, kv_p, kv_q, sinks_row)
    q4 = q_s.reshape(DEC_BATCH, DEC_SEQ, B_KV_HEADS, B_GROUP, B_DH).transpose(0, 2, 1, 3, 4)
    q4 = q4.reshape(DEC_BATCH, B_KV_HEADS, DEC_SEQ * B_GROUP, B_DH)
    seq3 = lambda x: x.astype(F32).reshape(DEC_BATCH, -1, B_KV)
    o4 = _attn_sample(q4, seq3(cache_k_meta), seq3(cache_v_meta), seq3(cache_k_win), seq3(cache_v_win),
                      seq3(k_win_s), seq3(v_win_s), sinks_row)
    o_s = o4.reshape(DEC_BATCH, B_KV_HEADS, DEC_SEQ, B_GROUP, B_DH).transpose(0, 2, 1, 3, 4)
    o_s = o_s.reshape(S_ROWS, D_MODEL)

    h_p = _matres(o_p, wb_out, h_p, TM_W)
    h_s = _matres(o_s, wb_out, h_s, TM_W)
    h_p = _ffn(h_p, row(ffn_norm[1, 1]), wf_in, wf_out, 1, 1, TM_P)
    h_s = _ffn(h_s, row(ffn_norm[1, 1]), wf_in, wf_out, 1, 1, S_ROWS)

    kv4 = lambda x: x.reshape(x.shape[:-1] + (B_KV_HEADS, B_DH))
    meta_rows = kv_q[S_ROWS:S_ROWS + N_META]
    kv_p3 = kv_p.reshape(BATCH, SEQ, 2 * B_KV)
    st = lambda x, dt: x[None].astype(dt)
    return (
        h_p.reshape(BATCH, SEQ, D_MODEL),
        h_s.reshape(DEC_BATCH, DEC_SEQ, D_MODEL),
        st(c_p, state_C.dtype), st(n_p[:, :, 0, :], state_n.dtype), st(m_p[:, :, 0, 0], state_m.dtype),
        jnp.broadcast_to(kv4(meta_rows[:, :B_KV])[None], (BATCH, N_META, B_KV_HEADS, B_DH)),
        jnp.broadcast_to(kv4(meta_rows[:, B_KV:])[None], (BATCH, N_META, B_KV_HEADS, B_DH)),
        kv4(kv_p3[:, -WINDOW:, :B_KV]), kv4(kv_p3[:, -WINDOW:, B_KV:]),
        st(c_s, state_C.dtype), st(n_s[:, :, 0, :], state_n.dtype), st(m_s[:, :, 0, 0], state_m.dtype),
        k_win_s, v_win_s,
    )
```

```python
import functools

import jax
import jax.numpy as jnp
from jax import lax
from jax.experimental import pallas as pl
from jax.experimental.pallas import tpu as pltpu

D_MODEL = 2048
BATCH = 8
SEQ = 2048
DEC_BATCH = 128
DEC_SEQ = 4
PAST_LEN = 8192
N_META = 16
A_HEADS = 4
A_DV = D_MODEL // A_HEADS
A_DK = A_DV // 2
A_CHUNK = 128
A_GATE_CAP = 15.0
A_QKVO = 2 * A_HEADS * A_DK + 2 * A_HEADS * A_DV
B_HEADS = 32
B_DH = D_MODEL // B_HEADS
B_KV_HEADS = 4
B_GROUP = B_HEADS // B_KV_HEADS
B_KV = B_KV_HEADS * B_DH
WINDOW = 128
D_FF = ((8 * D_MODEL // 3 + 255) // 256) * 256
EPS = 1e-6

LANES = 128
SUBLANES = 8
VMEM_LIMIT = 60 * 1024 * 1024

P_ROWS = BATCH * SEQ
S_ROWS = DEC_BATCH * DEC_SEQ
Q_ROWS = S_ROWS + A_CHUNK
META_BLOCK = S_ROWS // A_CHUNK

F32 = jnp.float32
BF16 = jnp.bfloat16
NEG_INF = float("-inf")


def _params(n_axes):
    return pltpu.CompilerParams(dimension_semantics=("arbitrary",) * n_axes,
                                vmem_limit_bytes=VMEM_LIMIT)


def _resident(shape):
    return pl.BlockSpec(shape, lambda i: (0, 0), pipeline_mode=pl.Buffered(1))


def _rms(x, g):
    return x * lax.rsqrt(jnp.mean(x * x, axis=-1, keepdims=True) + EPS) * g


def _dot(a, b):
    return jnp.dot(a, b, preferred_element_type=F32)


def _dot_nt(a, b):
    return lax.dot_general(a, b, (((1,), (1,)), ((), ())), preferred_element_type=F32)


def _dot_tn(a, b):
    return lax.dot_general(a, b, (((0,), (0,)), ((), ())), preferred_element_type=F32)


def _log_sigmoid(x):
    return -(jnp.maximum(-x, 0.0) + jnp.log1p(jnp.exp(-jnp.abs(x))))


def _head_norm64(y, gain):
    lo = lax.broadcasted_iota(jnp.int32, (1, LANES), 1) < B_DH
    cols = []
    for c in range(y.shape[1] // LANES):
        x = y[:, c * LANES:(c + 1) * LANES]
        xx = x * x
        s_lo = jnp.sum(jnp.where(lo, xx, 0.0), axis=-1, keepdims=True)
        s_hi = jnp.sum(jnp.where(lo, 0.0, xx), axis=-1, keepdims=True)
        scale = jnp.where(lo, lax.rsqrt(s_lo / B_DH + EPS), lax.rsqrt(s_hi / B_DH + EPS))
        cols.append(x * scale * gain[:, c * LANES:(c + 1) * LANES])
    return jnp.concatenate(cols, axis=1)


def _ffn_kernel(h_ref, g_ref, wg_ref, wu_ref, wo_ref, o_ref, *rest, emit_bf16):
    j = pl.program_id(1)
    if emit_bf16:
        wgb_ref, wub_ref, wob_ref, xn_ref = rest
        wg = wg_ref[...].astype(BF16)
        wu = wu_ref[...].astype(BF16)
        wo = wo_ref[...].astype(BF16)
        wgb_ref[...] = wg
        wub_ref[...] = wu
        wob_ref[...] = wo
    else:
        (xn_ref,) = rest
        wg, wu, wo = wg_ref[...], wu_ref[...], wo_ref[...]

    def half_ffn(xn):
        g = _dot(xn, wg)
        u = _dot(xn, wu)
        a = (g / (1.0 + jnp.exp(-g))) * (0.5 * u)
        return _dot(a.astype(BF16), wo)

    @pl.when(j == 0)
    def _():
        h = h_ref[...]
        xn = _rms(h, g_ref[...]).astype(BF16)
        xn_ref[...] = xn
        o_ref[...] = h + half_ffn(xn)

    @pl.when(j > 0)
    def _():
        o_ref[...] += half_ffn(xn_ref[...])


def _ffn(h, gain, wg, wu, wo, tm, tf=512):
    rows = h.shape[0]
    n_ff = D_FF // tf
    return pl.pallas_call(
        functools.partial(_ffn_kernel, emit_bf16=False),
        grid=(rows // tm, n_ff),
        in_specs=[
            pl.BlockSpec((tm, D_MODEL), lambda i, j: (i, 0)),
            pl.BlockSpec((1, D_MODEL), lambda i, j: (0, 0)),
            pl.BlockSpec((D_MODEL, tf), lambda i, j: (0, j)),
            pl.BlockSpec((D_MODEL, tf), lambda i, j: (0, j)),
            pl.BlockSpec((tf, D_MODEL), lambda i, j: (j, 0)),
        ],
        out_specs=pl.BlockSpec((tm, D_MODEL), lambda i, j: (i, 0)),
        out_shape=jax.ShapeDtypeStruct((rows, D_MODEL), F32),
        scratch_shapes=[pltpu.VMEM((tm, D_MODEL), BF16)],
        compiler_params=_params(2),
        name="ffn",
    )(h, gain, wg, wu, wo)


def _ffn_cast(h, gain, w_in, w_out, layer, which, tf=256):
    rows = h.shape[0]
    n_ff = D_FF // tf
    return pl.pallas_call(
        functools.partial(_ffn_kernel, emit_bf16=True),
        grid=(1, n_ff),
        in_specs=[
            pl.BlockSpec((rows, D_MODEL), lambda i, j: (0, 0)),
            pl.BlockSpec((1, D_MODEL), lambda i, j: (0, 0)),
            pl.BlockSpec((None, None, D_MODEL, tf), lambda i, j: (layer, which, 0, j)),
            pl.BlockSpec((None, None, D_MODEL, tf), lambda i, j: (layer, which, 0, j + n_ff)),
            pl.BlockSpec((None, None, tf, D_MODEL), lambda i, j: (layer, which, j, 0)),
        ],
        out_specs=[
            pl.BlockSpec((rows, D_MODEL), lambda i, j: (0, 0)),
            pl.BlockSpec((D_MODEL, tf), lambda i, j: (0, j)),
            pl.BlockSpec((D_MODEL, tf), lambda i, j: (0, j)),
            pl.BlockSpec((tf, D_MODEL), lambda i, j: (j, 0)),
        ],
        out_shape=[
            jax.ShapeDtypeStruct((rows, D_MODEL), F32),
            jax.ShapeDtypeStruct((D_MODEL, D_FF), BF16),
            jax.ShapeDtypeStruct((D_MODEL, D_FF), BF16),
            jax.ShapeDtypeStruct((D_FF, D_MODEL), BF16),
        ],
        scratch_shapes=[pltpu.VMEM((rows, D_MODEL), BF16)],
        compiler_params=_params(2),
        name="ffn_cast",
    )(h, gain, w_in, w_in, w_out)


def _inproj_kernel(h_ref, g_ref, w_ref, wgate_ref, bgate_ref, p_ref, gates_ref, *rest, emit_bf16):
    j = pl.program_id(1)
    if emit_bf16:
        wb_ref, xn_ref = rest
        w = w_ref[...].astype(BF16)
        wb_ref[...] = w
    else:
        (xn_ref,) = rest
        w = w_ref[...]

    @pl.when(j == 0)
    def _():
        xn = _rms(h_ref[...], g_ref[...]).astype(BF16)
        xn_ref[...] = xn
        pre = _dot(xn, wgate_ref[...]) + bgate_ref[...]
        capped = A_GATE_CAP * jnp.tanh(pre / A_GATE_CAP)
        lane = lax.broadcasted_iota(jnp.int32, (1, LANES), 1)
        gates_ref[...] = jnp.where(lane < A_HEADS, capped, _log_sigmoid(capped))

    p_ref[...] = _dot(xn_ref[...], w)


def _inproj(h, gain, w, wgate, bgate, tm, emit_bf16=False, tn=1024):
    rows = h.shape[0]
    assert not emit_bf16 or rows == tm
    out_specs = [
        pl.BlockSpec((tm, tn), lambda i, j: (i, j)),
        pl.BlockSpec((tm, LANES), lambda i, j: (i, 0)),
    ]
    out_shape = [jax.ShapeDtypeStruct((rows, A_QKVO), F32),
                 jax.ShapeDtypeStruct((rows, LANES), F32)]
    if emit_bf16:
        out_specs.append(pl.BlockSpec((D_MODEL, tn), lambda i, j: (0, j)))
        out_shape.append(jax.ShapeDtypeStruct((D_MODEL, A_QKVO), BF16))
    return pl.pallas_call(
        functools.partial(_inproj_kernel, emit_bf16=emit_bf16),
        grid=(rows // tm, A_QKVO // tn),
        in_specs=[
            pl.BlockSpec((tm, D_MODEL), lambda i, j: (i, 0)),
            pl.BlockSpec((1, D_MODEL), lambda i, j: (0, 0)),
            pl.BlockSpec((D_MODEL, tn), lambda i, j: (0, j)),
            pl.BlockSpec((D_MODEL, LANES), lambda i, j: (0, 0)),
            pl.BlockSpec((1, LANES), lambda i, j: (0, 0)),
        ],
        out_specs=out_specs,
        out_shape=out_shape,
        scratch_shapes=[pltpu.VMEM((tm, D_MODEL), BF16)],
        compiler_params=_params(2),
        name="mlstm_inproj",
    )(h, gain, w, wgate, bgate)


def _mlstm_chunk_kernel(q_ref, k_ref, v_ref, g_ref, c0_ref, n0_ref, m0_ref,
                        h_ref, c_ref, n_ref, m_ref, *, n_valid):
    L = A_CHUNK

    @pl.when(pl.program_id(1) == 0)
    def _():
        c_ref[...] = c0_ref[...]
        n_ref[...] = n0_ref[...]
        m_ref[...] = m0_ref[...]

    gates = g_ref[...]
    row = lax.broadcasted_iota(jnp.int32, (L, L), 0)
    col = lax.broadcasted_iota(jnp.int32, (L, L), 1)
    causal = col <= row
    eye = col == row
    masked = n_valid < L
    if masked:
        row_ok = lax.broadcasted_iota(jnp.int32, (L, 1), 0) < n_valid
        gates_lf = jnp.where(row_ok, gates, 0.0)
    else:
        gates_lf = gates
    csum = jnp.dot(causal.astype(F32), gates_lf, precision=lax.Precision.HIGHEST,
                   preferred_element_type=F32)

    def to_row(x_col):
        return jnp.sum(jnp.where(eye, x_col, 0.0), axis=0, keepdims=True)

    for hd in range(A_HEADS):
        b_col = csum[:, A_HEADS + hd:A_HEADS + hd + 1]
        ig_col = gates[:, hd:hd + 1]
        if masked:
            ig_col = jnp.where(row_ok, ig_col, NEG_INF)
        b_row = to_row(b_col)
        ig_row = to_row(ig_col)
        m_prev = m_ref[0, hd][:, 0:1]
        c_prev = c_ref[0, hd]
        n_prev = n_ref[0, hd]

        q = q_ref[:, hd * A_DK:(hd + 1) * A_DK]
        k = k_ref[:, hd * A_DK:(hd + 1) * A_DK] * (A_DK ** -0.5)
        v = v_ref[:, hd * A_DV:(hd + 1) * A_DV]
        qb = q.astype(BF16)
        vb = v.astype(BF16)

        d_log = jnp.where(causal, b_col - b_row + ig_row, NEG_INF)
        inter_log = b_col + m_prev
        m_t = jnp.maximum(inter_log, jnp.max(d_log, axis=-1, keepdims=True))
        w_intra = jnp.exp(d_log - m_t)
        w_inter = jnp.exp(inter_log - m_t)
        s = _dot_nt(qb, k.astype(BF16)) * w_intra
        num = _dot(s.astype(BF16), vb) + w_inter * _dot(qb, c_prev.astype(BF16))
        den = jnp.sum(s, axis=-1, keepdims=True) + w_inter * jnp.sum(q * n_prev, axis=-1, keepdims=True)
        den = jnp.maximum(jnp.abs(den), jnp.exp(-m_t))
        h_ref[:, hd * A_DV:(hd + 1) * A_DV] = num / den

        b_last = b_col[L - 1:L, :]
        w_log = b_last - b_col + ig_col
        m_new = jnp.maximum(b_last + m_prev, jnp.max(w_log, axis=0, keepdims=True))
        w_state = jnp.exp(w_log - m_new)
        decay = jnp.exp(b_last + m_prev - m_new)
        kw = k * w_state
        c_ref[0, hd] = decay * c_prev + _dot_tn(kw.astype(BF16), vb)
        n_ref[0, hd] = decay * n_prev + jnp.sum(kw, axis=0, keepdims=True)
        m_ref[0, hd] = jnp.broadcast_to(m_new, (1, LANES))


def _mlstm_chunks(p, gates, c0, n0, m0, n_seq, n_chunks, row_block0, n_valid, shared_state):
    L = A_CHUNK
    rb = lambda b, c: row_block0 + b * n_chunks + c
    st = (lambda b, c: (0, 0, 0, 0)) if shared_state else (lambda b, c: (b, 0, 0, 0))
    return pl.pallas_call(
        functools.partial(_mlstm_chunk_kernel, n_valid=n_valid),
        grid=(n_seq, n_chunks),
        in_specs=[
            pl.BlockSpec((L, A_HEADS * A_DK), lambda b, c: (rb(b, c), 0)),
            pl.BlockSpec((L, A_HEADS * A_DK), lambda b, c: (rb(b, c), 1)),
            pl.BlockSpec((L, A_HEADS * A_DV), lambda b, c: (rb(b, c), 1)),
            pl.BlockSpec((L, LANES), lambda b, c: (rb(b, c), 0)),
            pl.BlockSpec((1, A_HEADS, A_DK, A_DV), st),
            pl.BlockSpec((1, A_HEADS, 1, A_DK), st),
            pl.BlockSpec((1, A_HEADS, 1, LANES), st),
        ],
        out_specs=[
            pl.BlockSpec((L, A_HEADS * A_DV), lambda b, c: (b * n_chunks + c, 0)),
            pl.BlockSpec((1, A_HEADS, A_DK, A_DV), lambda b, c: (b, 0, 0, 0)),
            pl.BlockSpec((1, A_HEADS, 1, A_DK), lambda b, c: (b, 0, 0, 0)),
            pl.BlockSpec((1, A_HEADS, 1, LANES), lambda b, c: (b, 0, 0, 0)),
        ],
        out_shape=[
            jax.ShapeDtypeStruct((n_seq * n_chunks * L, D_MODEL), F32),
            jax.ShapeDtypeStruct((n_seq, A_HEADS, A_DK, A_DV), F32),
            jax.ShapeDtypeStruct((n_seq, A_HEADS, 1, A_DK), F32),
            jax.ShapeDtypeStruct((n_seq, A_HEADS, 1, LANES), F32),
        ],
        compiler_params=_params(2),
        name="mlstm_chunks",
    )(p, p, p, gates, c0, n0, m0)


def _mlstm_sample_kernel(q_ref, k_ref, v_ref, g_ref, c0_ref, n0_ref, m0_ref,
                         h_ref, c_ref, n_ref, m_ref):
    R = 2 * DEC_SEQ
    PAD = A_CHUNK - R
    gates = g_ref[...]
    r_col = lax.broadcasted_iota(jnp.int32, (R, 1), 0)
    is_a = r_col < DEC_SEQ
    row = lax.broadcasted_iota(jnp.int32, (R, LANES), 0)
    lane = lax.broadcasted_iota(jnp.int32, (R, LANES), 1)
    same = ((lane < DEC_SEQ) & (row < DEC_SEQ)) | ((lane >= DEC_SEQ) & (lane < R) & (row >= DEC_SEQ))
    causal = same & (lane <= row)
    eye = lane == row

    def to_row(x_col):
        return jnp.sum(jnp.where(eye, x_col, 0.0), axis=0, keepdims=True)

    for hd in range(A_HEADS):
        lf_col = gates[:, A_HEADS + hd:A_HEADS + hd + 1]
        ig_col = gates[:, hd:hd + 1]
        lf_row = to_row(lf_col)
        ig_row = to_row(ig_col)
        b_col = jnp.sum(jnp.where(causal, lf_row, 0.0), axis=1, keepdims=True)
        b_row = to_row(b_col)
        m_a = m0_ref[0, hd][:, 0:1]
        m_b = m0_ref[1, hd][:, 0:1]
        m_prev = jnp.where(is_a, m_a, m_b)
        c_a = c0_ref[0, hd]
        c_b = c0_ref[1, hd]
        n_a = n0_ref[0, hd]
        n_b = n0_ref[1, hd]

        q = q_ref[:, hd * A_DK:(hd + 1) * A_DK]
        k = k_ref[:, hd * A_DK:(hd + 1) * A_DK] * (A_DK ** -0.5)
        v = v_ref[:, hd * A_DV:(hd + 1) * A_DV]
        qb = q.astype(BF16)
        k_pad = jnp.concatenate([k, jnp.zeros((PAD, A_DK), F32)], axis=0).astype(BF16)
        v_pad = jnp.concatenate([v, jnp.zeros((PAD, A_DV), F32)], axis=0).astype(BF16)

        d_log = jnp.where(causal, b_col - b_row + ig_row, NEG_INF)
        inter_log = b_col + m_prev
        m_t = jnp.maximum(inter_log, jnp.max(d_log, axis=-1, keepdims=True))
        w_intra = jnp.exp(d_log - m_t)
        w_inter = jnp.exp(inter_log - m_t)
        s = _dot_nt(qb, k_pad) * w_intra
        q_c = jnp.where(is_a, _dot(qb, c_a.astype(BF16)), _dot(qb, c_b.astype(BF16)))
        num = _dot(s.astype(BF16), v_pad) + w_inter * q_c
        q_n = jnp.sum(q * jnp.where(is_a, n_a, n_b), axis=-1, keepdims=True)
        den = jnp.sum(s, axis=-1, keepdims=True) + w_inter * q_n
        den = jnp.maximum(jnp.abs(den), jnp.exp(-m_t))
        h_ref[:, hd * A_DV:(hd + 1) * A_DV] = num / den

        for idx, sel, m_x, c_x, n_x in ((0, is_a, m_a, c_a, n_a),
                                        (1, jnp.logical_not(is_a), m_b, c_b, n_b)):
            last = (idx + 1) * DEC_SEQ - 1
            b_last = b_col[last:last + 1, :]
            w_log = jnp.where(sel, b_last - b_col + ig_col, NEG_INF)
            m_new = jnp.maximum(b_last + m_x, jnp.max(w_log, axis=0, keepdims=True))
            w_state = jnp.exp(w_log - m_new)
            decay = jnp.exp(b_last + m_x - m_new)
            kw = k * w_state
            kw_pad = jnp.concatenate([kw, jnp.zeros((PAD, A_DK), F32)], axis=0).astype(BF16)
            c_ref[idx, hd] = decay * c_x + _dot_tn(kw_pad, v_pad)
            n_ref[idx, hd] = decay * n_x + jnp.sum(kw, axis=0, keepdims=True)
            m_ref[idx, hd] = jnp.broadcast_to(m_new, (1, LANES))


def _mlstm_sample(p, gates, c0, n0, m0):
    R = 2 * DEC_SEQ
    n_pairs = DEC_BATCH // 2
    st = lambda i: (i, 0, 0, 0)
    return pl.pallas_call(
        _mlstm_sample_kernel,
        grid=(n_pairs,),
        in_specs=[
            pl.BlockSpec((R, A_HEADS * A_DK), lambda i: (i, 0)),
            pl.BlockSpec((R, A_HEADS * A_DK), lambda i: (i, 1)),
            pl.BlockSpec((R, A_HEADS * A_DV), lambda i: (i, 1)),
            pl.BlockSpec((R, LANES), lambda i: (i, 0)),
            pl.BlockSpec((2, A_HEADS, A_DK, A_DV), st),
            pl.BlockSpec((2, A_HEADS, 1, A_DK), st),
            pl.BlockSpec((2, A_HEADS, 1, LANES), st),
        ],
        out_specs=[
            pl.BlockSpec((R, A_HEADS * A_DV), lambda i: (i, 0)),
            pl.BlockSpec((2, A_HEADS, A_DK, A_DV), st),
            pl.BlockSpec((2, A_HEADS, 1, A_DK), st),
            pl.BlockSpec((2, A_HEADS, 1, LANES), st),
        ],
        out_shape=[
            jax.ShapeDtypeStruct((S_ROWS, D_MODEL), F32),
            jax.ShapeDtypeStruct((DEC_BATCH, A_HEADS, A_DK, A_DV), F32),
            jax.ShapeDtypeStruct((DEC_BATCH, A_HEADS, 1, A_DK), F32),
            jax.ShapeDtypeStruct((DEC_BATCH, A_HEADS, 1, LANES), F32),
        ],
        compiler_params=_params(1),
        name="mlstm_sample",
    )(p, p, p, gates, c0, n0, m0)


def _mlstm_out_kernel(hm_ref, o_ref, hg_ref, w_ref, res_ref, out_ref):
    cols = []
    for hd in range(A_HEADS):
        x = hm_ref[:, hd * A_DV:(hd + 1) * A_DV]
        cols.append(x * lax.rsqrt(jnp.mean(x * x, axis=-1, keepdims=True) + EPS))
    hn = jnp.concatenate(cols, axis=1) * hg_ref[...]
    o = o_ref[...]
    pre = (hn * (1.0 / (1.0 + jnp.exp(-o)))).astype(BF16)
    out_ref[...] = res_ref[...] + _dot(pre, w_ref[...])


def _mlstm_out(hm, p, head_gain, w, res, tm):
    rows = hm.shape[0]
    return pl.pallas_call(
        _mlstm_out_kernel,
        grid=(rows // tm,),
        in_specs=[
            pl.BlockSpec((tm, D_MODEL), lambda i: (i, 0)),
            pl.BlockSpec((tm, D_MODEL), lambda i: (i, 2)),
            pl.BlockSpec((1, D_MODEL), lambda i: (0, 0)),
            _resident((D_MODEL, D_MODEL)),
            pl.BlockSpec((tm, D_MODEL), lambda i: (i, 0)),
        ],
        out_specs=pl.BlockSpec((tm, D_MODEL), lambda i: (i, 0)),
        out_shape=jax.ShapeDtypeStruct((rows, D_MODEL), F32),
        compiler_params=_params(1),
        name="mlstm_out",
    )(hm, p, head_gain, w, res)


def _normproj_kernel(h_ref, g_ref, w_ref, hg_ref, o_ref, *, n_norm):
    xn = _rms(h_ref[...], g_ref[...]).astype(BF16)
    y = _dot(xn, w_ref[...])
    if n_norm == y.shape[1]:
        o_ref[...] = _head_norm64(y, hg_ref[...])
    else:
        o_ref[:, :n_norm] = _head_norm64(y[:, :n_norm], hg_ref[...])
        o_ref[:, n_norm:] = y[:, n_norm:]


def _normproj(h, gain, w, head_gain, tm):
    rows = h.shape[0]
    n = w.shape[1]
    n_norm = head_gain.shape[1]
    return pl.pallas_call(
        functools.partial(_normproj_kernel, n_norm=n_norm),
        grid=(rows // tm,),
        in_specs=[
            pl.BlockSpec((tm, D_MODEL), lambda i: (i, 0)),
            pl.BlockSpec((1, D_MODEL), lambda i: (0, 0)),
            _resident((D_MODEL, n)),
            pl.BlockSpec((1, n_norm), lambda i: (0, 0)),
        ],
        out_specs=pl.BlockSpec((tm, n), lambda i: (i, 0)),
        out_shape=jax.ShapeDtypeStruct((rows, n), F32),
        compiler_params=_params(1),
        name="normproj",
    )(h, gain, w, head_gain)


def _matres_kernel(x_ref, w_ref, res_ref, o_ref):
    o_ref[...] = res_ref[...] + _dot(x_ref[...].astype(BF16), w_ref[...])


def _matres(x, w, res, tm):
    rows = x.shape[0]
    return pl.pallas_call(
        _matres_kernel,
        grid=(rows // tm,),
        in_specs=[
            pl.BlockSpec((tm, D_MODEL), lambda i: (i, 0)),
            _resident((D_MODEL, D_MODEL)),
            pl.BlockSpec((tm, D_MODEL), lambda i: (i, 0)),
        ],
        out_specs=pl.BlockSpec((tm, D_MODEL), lambda i: (i, 0)),
        out_shape=jax.ShapeDtypeStruct((rows, D_MODEL), F32),
        compiler_params=_params(1),
        name="matres",
    )(x, w, res)


def _alibi_slope(head):
    return 2.0 ** (-8.0 * (head + 1) / B_HEADS)


PROMPT_SUB = WINDOW // 2
PROMPT_BAND = WINDOW + PROMPT_SUB
PROMPT_KEYS = 2 * LANES
HEAD_PAIRS = B_HEADS // 2
PAIRS_PER_GROUP = B_GROUP // 2


def _pair_rhs(x, c):
    col = x[:, c * LANES:(c + 1) * LANES]
    rol = pltpu.roll(col, B_DH, axis=1)
    lo = lax.broadcasted_iota(jnp.int32, (1, LANES), 1) < B_DH
    even = jnp.concatenate([jnp.where(lo, col, 0.0), jnp.where(lo, 0.0, rol)], axis=0)
    odd = jnp.concatenate([jnp.where(lo, rol, 0.0), jnp.where(lo, 0.0, col)], axis=0)
    return even.astype(BF16), odd.astype(BF16)


def _attn_prompt_kernel(q_ref, kvo_ref, kvp_ref, kvm_ref, sink_ref, o_ref, bias_ref, s_ref, p_ref):
    W = WINDOW
    SB = PROMPT_SUB
    NK = PROMPT_KEYS
    first = (pl.program_id(0) == 0) & (pl.program_id(1) == 0)
    j = pl.program_id(1)
    rows = PAIRS_PER_GROUP * SB
    lo = lax.broadcasted_iota(jnp.int32, (1, LANES), 1) < B_DH

    @pl.when(first)
    def _():
        c = lax.broadcasted_iota(jnp.int32, (SB, NK), 1)
        is_meta = c < N_META
        is_sink = c == NK - 1
        for sub in range(2):
            i = lax.broadcasted_iota(jnp.int32, (SB, NK), 0) + sub * SB
            pos = c - N_META + sub * SB
            rel = W + i - pos
            in_band = (c >= N_META) & (c < N_META + PROMPT_BAND) & (rel >= 0) & (rel < W)
            for variant in range(2):
                if variant == 0:
                    dist = jnp.where(is_meta, jnp.minimum(i + N_META - c, W), rel).astype(F32)
                    valid = is_meta | (in_band & (pos >= W))
                else:
                    dist = jnp.where(is_meta, W, rel).astype(F32)
                    valid = is_meta | in_band
                for hd in range(B_HEADS):
                    r0 = (hd // 2) * SB
                    c0 = (hd % 2) * NK
                    table = jnp.where(valid, -_alibi_slope(hd) * dist, NEG_INF)
                    bias_ref[variant, sub, r0:r0 + SB, c0:c0 + NK] = jnp.where(
                        is_sink, sink_ref[0:1, hd:hd + 1], table)

    variant = jnp.minimum(j, 1)
    kvm = kvm_ref[...]
    band = jnp.concatenate([kvp_ref[...], kvo_ref[...]], axis=0)
    pad = jnp.zeros((NK - N_META - PROMPT_BAND, 2 * B_KV), F32)
    ones_rhs = jnp.concatenate([jnp.broadcast_to(jnp.where(lo, 1.0, 0.0), (NK, LANES)),
                                jnp.broadcast_to(jnp.where(lo, 0.0, 1.0), (NK, LANES))], axis=0).astype(BF16)

    sub_rows = B_KV_HEADS * rows
    for sub in range(2):
        keys = jnp.concatenate([kvm, band[sub * SB:sub * SB + PROMPT_BAND], pad], axis=0)
        k_rhs = _pair_rhs(keys, 0) + _pair_rhs(keys, 1)
        v_rhs = _pair_rhs(keys, 2) + _pair_rhs(keys, 3)
        for g in range(B_KV_HEADS):
            p0 = g * PAIRS_PER_GROUP
            r0 = (sub * B_KV_HEADS + g) * rows
            qg = jnp.concatenate([q_ref[sub * SB:(sub + 1) * SB, (p0 + m) * LANES:(p0 + m + 1) * LANES]
                                  for m in range(PAIRS_PER_GROUP)], axis=0) * (B_DH ** -0.5)
            s_ref[r0:r0 + rows, :] = _dot_nt(qg.astype(BF16), k_rhs[g])

        s = s_ref[sub * sub_rows:(sub + 1) * sub_rows, :] + bias_ref[variant, sub]
        halves = []
        for half in range(2):
            sh = s[:, half * NK:(half + 1) * NK]
            halves.append(jnp.exp(sh - jnp.max(sh, axis=-1, keepdims=True)).astype(BF16))
        p_ref[sub * sub_rows:(sub + 1) * sub_rows, :] = jnp.concatenate(halves, axis=1)

        for g in range(B_KV_HEADS):
            p0 = g * PAIRS_PER_GROUP
            r0 = (sub * B_KV_HEADS + g) * rows
            od = _dot(p_ref[r0:r0 + rows, :], jnp.concatenate([v_rhs[g], ones_rhs], axis=1))
            o = od[:, :LANES] / od[:, LANES:]
            for m in range(PAIRS_PER_GROUP):
                o_ref[sub * SB:(sub + 1) * SB, (p0 + m) * LANES:(p0 + m + 1) * LANES] = o[m * SB:(m + 1) * SB, :]


def _attn_prompt(q, kv_p, kv_q, sinks_row):
    nb = SEQ // WINDOW
    blk = lambda b, j: b * nb + j
    return pl.pallas_call(
        _attn_prompt_kernel,
        grid=(BATCH, nb),
        in_specs=[
            pl.BlockSpec((WINDOW, D_MODEL), lambda b, j: (blk(b, j), 0)),
            pl.BlockSpec((WINDOW, 2 * B_KV), lambda b, j: (blk(b, j), 0)),
            pl.BlockSpec((WINDOW, 2 * B_KV), lambda b, j: (blk(b, jnp.maximum(j - 1, 0)), 0)),
            pl.BlockSpec((N_META, 2 * B_KV), lambda b, j: (S_ROWS // N_META, 0)),
            pl.BlockSpec((1, LANES), lambda b, j: (0, 0)),
        ],
        out_specs=pl.BlockSpec((WINDOW, D_MODEL), lambda b, j: (blk(b, j), 0)),
        out_shape=jax.ShapeDtypeStruct((P_ROWS, D_MODEL), F32),
        scratch_shapes=[pltpu.VMEM((2, 2, HEAD_PAIRS * PROMPT_SUB, 2 * PROMPT_KEYS), F32),
                        pltpu.VMEM((2 * HEAD_PAIRS * PROMPT_SUB, 2 * PROMPT_KEYS), F32),
                        pltpu.VMEM((2 * HEAD_PAIRS * PROMPT_SUB, 2 * PROMPT_KEYS), BF16)],
        compiler_params=_params(2),
        name="attn_prompt",
    )(q, kv_p, kv_p, kv_q, sinks_row)


GROUP_SHIFT = B_GROUP.bit_length() - 1
assert 1 << GROUP_SHIFT == B_GROUP
SAMPLE_OLD = SUBLANES
SAMPLE_KEYS = 2 * LANES
SAMPLE_SEQ_PER_STEP = 8
SAMPLE_ROWS = DEC_SEQ * B_GROUP
KV_PAIRS = B_KV_HEADS // 2


def _split_rhs(col):
    lo = lax.broadcasted_iota(jnp.int32, (1, LANES), 1) < B_DH
    return jnp.concatenate([jnp.where(lo, col, 0.0), jnp.where(lo, 0.0, col)], axis=0).astype(BF16)


def _attn_sample_kernel(q_ref, km_ref, vm_ref, ko_ref, vo_ref, kw_ref, vw_ref, sink_ref, o_ref,
                        bias_ref, s_ref, p_ref):
    W = WINDOW
    NK = SAMPLE_KEYS
    R = SAMPLE_ROWS
    off_w = N_META + SAMPLE_OLD
    lo = lax.broadcasted_iota(jnp.int32, (1, LANES), 1) < B_DH

    @pl.when(pl.program_id(0) == 0)
    def _():
        row = lax.broadcasted_iota(jnp.int32, (R, 1), 0)
        r_in_group = jnp.bitwise_and(row, B_GROUP - 1)
        t = jnp.right_shift(lax.broadcasted_iota(jnp.int32, (R, NK), 0), GROUP_SHIFT)
        c = lax.broadcasted_iota(jnp.int32, (R, NK), 1)
        is_meta = c < N_META
        is_old = (c >= N_META) & (c < off_w)
        is_win = (c >= off_w) & (c < off_w + W)
        j_old = c - N_META
        d_win = (W - DEC_SEQ) + t - (c - off_w)
        dist = jnp.where(is_meta, W, jnp.where(is_old, W + t - j_old, d_win)).astype(F32)
        valid = is_meta | (is_old & (j_old > t) & (j_old < DEC_SEQ)) | (is_win & (d_win >= 0))
        for pair in range(KV_PAIRS):
            for e in range(2):
                slope = jnp.zeros((R, 1), F32)
                sink = jnp.zeros((R, 1), F32)
                for r in range(B_GROUP):
                    hd = (2 * pair + e) * B_GROUP + r
                    slope = jnp.where(r_in_group == r, _alibi_slope(hd), slope)
                    sink = jnp.where(r_in_group == r, sink_ref[0:1, hd:hd + 1], sink)
                table = jnp.where(c == NK - 1, sink, jnp.where(valid, -slope * dist, NEG_INF))
                for sq in range(SAMPLE_SEQ_PER_STEP):
                    r0 = (sq * KV_PAIRS + pair) * R
                    bias_ref[r0:r0 + R, e * NK:(e + 1) * NK] = table

    pad = jnp.zeros((NK - off_w - W, B_KV), F32)
    ones_rhs = jnp.concatenate([jnp.broadcast_to(jnp.where(lo, 1.0, 0.0), (NK, LANES)),
                                jnp.broadcast_to(jnp.where(lo, 0.0, 1.0), (NK, LANES))], axis=0).astype(BF16)
    v_rhs = []
    for sq in range(SAMPLE_SEQ_PER_STEP):
        kcat = jnp.concatenate([km_ref[sq], ko_ref[sq], kw_ref[sq], pad], axis=0)
        vcat = jnp.concatenate([vm_ref[sq], vo_ref[sq], vw_ref[sq], pad], axis=0)
        for pair in range(KV_PAIRS):
            r0 = (sq * KV_PAIRS + pair) * R
            k_rhs = _split_rhs(kcat[:, pair * LANES:(pair + 1) * LANES])
            v_rhs.append(_split_rhs(vcat[:, pair * LANES:(pair + 1) * LANES]))
            q = (q_ref[sq, pair] * (B_DH ** -0.5)).astype(BF16)
            s_ref[r0:r0 + R, :] = _dot_nt(q, k_rhs)

    s = s_ref[...] + bias_ref[...]
    halves = []
    for e in range(2):
        sh = s[:, e * NK:(e + 1) * NK]
        halves.append(jnp.exp(sh - jnp.max(sh, axis=-1, keepdims=True)).astype(BF16))
    p_ref[...] = jnp.concatenate(halves, axis=1)

    for sq in range(SAMPLE_SEQ_PER_STEP):
        for pair in range(KV_PAIRS):
            b = sq * KV_PAIRS + pair
            od = _dot(p_ref[b * R:(b + 1) * R, :], jnp.concatenate([v_rhs[b], ones_rhs], axis=1))
            o_ref[sq, pair] = od[:, :LANES] / od[:, LANES:]


def _attn_sample(q4, k_meta, v_meta, k_old, v_old, k_win, v_win, sinks_row):
    nb = SAMPLE_SEQ_PER_STEP
    R = SAMPLE_ROWS
    n_rows = nb * KV_PAIRS * R
    seq3 = lambda rows: pl.BlockSpec((nb, rows, B_KV), lambda i: (i, 0, 0))
    qspec = pl.BlockSpec((nb, KV_PAIRS, R, LANES), lambda i: (i, 0, 0, 0))
    return pl.pallas_call(
        _attn_sample_kernel,
        grid=(DEC_BATCH // nb,),
        in_specs=[qspec, seq3(N_META), seq3(N_META), seq3(SAMPLE_OLD), seq3(SAMPLE_OLD),
                  seq3(WINDOW), seq3(WINDOW), pl.BlockSpec((1, LANES), lambda i: (0, 0))],
        out_specs=qspec,
        out_shape=jax.ShapeDtypeStruct((DEC_BATCH, KV_PAIRS, R, LANES), F32),
        scratch_shapes=[pltpu.VMEM((n_rows, 2 * SAMPLE_KEYS), F32),
                        pltpu.VMEM((n_rows, 2 * SAMPLE_KEYS), F32),
                        pltpu.VMEM((n_rows, 2 * SAMPLE_KEYS), BF16)],
        compiler_params=_params(1),
        name="attn_sample",
    )(q4, k_meta, v_meta, k_old, v_old, k_win, v_win, sinks_row)


def kernel(x_prompt, x_sample, state_C, state_n, state_m, cache_k_meta, cache_v_meta, cache_k_win, cache_v_win, meta_tokens, ffn_norm, w_ffn_in, w_ffn_out, mix_norm, w_a_in, b_a_gate, a_head_norm, w_a_out, kv_norm, w_kv, k_norm, w_q, q_norm, sinks, w_b_out):
    assert x_prompt.shape == (BATCH, SEQ, D_MODEL) and x_sample.shape == (DEC_BATCH, DEC_SEQ, D_MODEL)
    assert w_a_in.shape[0] == 1 and w_q.shape[0] == 1 and ffn_norm.shape[0] == 2

    wa_gate = jnp.pad(w_a_in[0][:, A_QKVO:], ((0, 0), (0, LANES - 2 * A_HEADS))).astype(BF16)
    ba_gate = jnp.pad(b_a_gate[0].astype(F32), (0, LANES - 2 * A_HEADS)).reshape(1, LANES)
    wa_out = w_a_out[0].astype(BF16)
    wkv = w_kv.astype(BF16)
    wq = w_q[0].astype(BF16)
    wb_out = w_b_out[0].astype(BF16)
    row = lambda x: x.astype(F32).reshape(1, -1)
    k_gain = jnp.tile(row(k_norm), (1, B_KV_HEADS))
    q_gain = jnp.tile(row(q_norm[0]), (1, B_HEADS))
    sinks_row = jnp.pad(sinks[0].astype(F32), (0, LANES - B_HEADS)).reshape(1, LANES)

    h_p = x_prompt.reshape(P_ROWS, D_MODEL)
    h_q = jnp.concatenate([x_sample.reshape(S_ROWS, D_MODEL), meta_tokens.astype(F32),
                           jnp.zeros((A_CHUNK - N_META, D_MODEL), F32)], axis=0)
    TM_P, TM_W = 1024, 512

    h_q, *wf = _ffn_cast(h_q, row(ffn_norm[0, 0]), w_ffn_in, w_ffn_out, 0, 0)
    h_p = _ffn(h_p, row(ffn_norm[0, 0]), *wf, TM_P)
    p_q, g_q, wa_in = _inproj(h_q, row(mix_norm[0]), w_a_in[0], wa_gate, ba_gate, Q_ROWS, emit_bf16=True)
    p_p, g_p = _inproj(h_p, row(mix_norm[0]), wa_in, wa_gate, ba_gate, TM_P)

    zc = jnp.zeros((1, A_HEADS, A_DK, A_DV), F32)
    zn = jnp.zeros((1, A_HEADS, 1, A_DK), F32)
    zm = jnp.zeros((1, A_HEADS, 1, LANES), F32)
    hm_m, c_m, n_m, m_m = _mlstm_chunks(p_q, g_q, zc, zn, zm, 1, 1, META_BLOCK, N_META, True)
    hm_p, c_p, n_p, m_p = _mlstm_chunks(p_p, g_p, c_m, n_m, m_m, BATCH, SEQ // A_CHUNK, 0, A_CHUNK, True)
    m0_s = jnp.broadcast_to(state_m[0].astype(F32)[:, :, None, None], (DEC_BATCH, A_HEADS, 1, LANES))
    hm_s, c_s, n_s, m_s = _mlstm_sample(p_q, g_q, state_C[0].astype(F32),
                                        state_n[0].astype(F32)[:, :, None, :], m0_s)
    hm_q = jnp.concatenate([hm_s, hm_m], axis=0)

    h_p = _mlstm_out(hm_p, p_p, row(a_head_norm[0]), wa_out, h_p, TM_W)
    h_q = _mlstm_out(hm_q, p_q, row(a_head_norm[0]), wa_out, h_q, A_CHUNK)
    h_q, *wf = _ffn_cast(h_q, row(ffn_norm[0, 1]), w_ffn_in, w_ffn_out, 0, 1)
    h_p = _ffn(h_p, row(ffn_norm[0, 1]), *wf, TM_P)

    kv_p = _normproj(h_p, row(kv_norm), wkv, k_gain, TM_P)
    kv_q = _normproj(h_q, row(kv_norm), wkv, k_gain, Q_ROWS)
    k_s = kv_q[:S_ROWS, :B_KV].reshape(DEC_BATCH, DEC_SEQ, B_KV_HEADS, B_DH)
    v_s = kv_q[:S_ROWS, B_KV:].reshape(DEC_BATCH, DEC_SEQ, B_KV_HEADS, B_DH)
    k_win_s = jnp.concatenate([cache_k_win, k_s.astype(cache_k_win.dtype)], axis=1)[:, -WINDOW:]
    v_win_s = jnp.concatenate([cache_v_win, v_s.astype(cache_v_win.dtype)], axis=1)[:, -WINDOW:]

    h_s = h_q[:S_ROWS]
    h_s, *wf = _ffn_cast(h_s, row(ffn_norm[1, 0]), w_ffn_in, w_ffn_out, 1, 0)
    h_p = _ffn(h_p, row(ffn_norm[1, 0]), *wf, TM_P)
    q_p = _normproj(h_p, row(mix_norm[1]), wq, q_gain, TM_W)
    q_s = _normproj(h_s, row(mix_norm[1]), wq, q_gain, TM_W)

    o_p = _attn_prompt(q_p, kv_p, kv_q, sinks_row)
    q4 = q_s.reshape(DEC_BATCH, DEC_SEQ, KV_PAIRS, 2, B_GROUP, B_DH).transpose(0, 2, 1, 4, 3, 5)
    q4 = q4.reshape(DEC_BATCH, KV_PAIRS, SAMPLE_ROWS, LANES)
    seq3 = lambda x: x.astype(F32).reshape(DEC_BATCH, -1, B_KV)
    o4 = _attn_sample(q4, seq3(cache_k_meta), seq3(cache_v_meta),
                      seq3(cache_k_win[:, :SAMPLE_OLD]), seq3(cache_v_win[:, :SAMPLE_OLD]),
                      seq3(k_win_s), seq3(v_win_s), sinks_row)
    o_s = o4.reshape(DEC_BATCH, KV_PAIRS, DEC_SEQ, B_GROUP, 2, B_DH).transpose(0, 2, 1, 4, 3, 5)
    o_s = o_s.reshape(S_ROWS, D_MODEL)

    h_p = _matres(o_p, wb_out, h_p, TM_W)
    h_s = _matres(o_s, wb_out, h_s, TM_W)
    h_s, *wf = _ffn_cast(h_s, row(ffn_norm[1, 1]), w_ffn_in, w_ffn_out, 1, 1)
    h_p = _ffn(h_p, row(ffn_norm[1, 1]), *wf, TM_P)

    kv4 = lambda x: x.reshape(x.shape[:-1] + (B_KV_HEADS, B_DH))
    meta_rows = kv_q[S_ROWS:S_ROWS + N_META]
    kv_p3 = kv_p.reshape(BATCH, SEQ, 2 * B_KV)
    st = lambda x, dt: x[None].astype(dt)
    return (
        h_p.reshape(BATCH, SEQ, D_MODEL),
        h_s.reshape(DEC_BATCH, DEC_SEQ, D_MODEL),
        st(c_p, state_C.dtype), st(n_p[:, :, 0, :], state_n.dtype), st(m_p[:, :, 0, 0], state_m.dtype),
        jnp.broadcast_to(kv4(meta_rows[:, :B_KV])[None], (BATCH, N_META, B_KV_HEADS, B_DH)),
        jnp.broadcast_to(kv4(meta_rows[:, B_KV:])[None], (BATCH, N_META, B_KV_HEADS, B_DH)),
        kv4(kv_p3[:, -WINDOW:, :B_KV]), kv4(kv_p3[:, -WINDOW:, B_KV:]),
        st(c_s, state_C.dtype), st(n_s[:, :, 0, :], state_n.dtype), st(m_s[:, :, 0, 0], state_m.dtype),
        k_win_s, v_win_s,
    )
```

```python
import functools

import jax
import jax.numpy as jnp
from jax import lax
from jax.experimental import pallas as pl
from jax.experimental.pallas import tpu as pltpu

D_MODEL = 2048
BATCH = 8
SEQ = 2048
DEC_BATCH = 128
DEC_SEQ = 4
PAST_LEN = 8192
N_META = 16
A_HEADS = 4
A_DV = D_MODEL // A_HEADS
A_DK = A_DV // 2
A_CHUNK = 128
A_GATE_CAP = 15.0
A_QKVO = 2 * A_HEADS * A_DK + 2 * A_HEADS * A_DV
B_HEADS = 32
B_DH = D_MODEL // B_HEADS
B_KV_HEADS = 4
B_GROUP = B_HEADS // B_KV_HEADS
B_KV = B_KV_HEADS * B_DH
WINDOW = 128
D_FF = ((8 * D_MODEL // 3 + 255) // 256) * 256
EPS = 1e-6

LANES = 128
SUBLANES = 8
VMEM_DEFAULT_MIB = 48
VMEM_FFN_MIB = 60

P_ROWS = BATCH * SEQ
S_ROWS = DEC_BATCH * DEC_SEQ
Q_ROWS = S_ROWS + A_CHUNK
META_BLOCK = S_ROWS // A_CHUNK

F32 = jnp.float32
BF16 = jnp.bfloat16
NEG_INF = float("-inf")


def _params(n_axes, vmem_mib=VMEM_DEFAULT_MIB):
    return pltpu.CompilerParams(dimension_semantics=("arbitrary",) * n_axes,
                                vmem_limit_bytes=vmem_mib * 1024 * 1024)


def _resident(shape):
    return pl.BlockSpec(shape, lambda i: (0, 0), pipeline_mode=pl.Buffered(1))


def _rms(x, g):
    return x * lax.rsqrt(jnp.mean(x * x, axis=-1, keepdims=True) + EPS) * g


def _dot(a, b):
    return jnp.dot(a, b, preferred_element_type=F32)


def _dot_nt(a, b):
    return lax.dot_general(a, b, (((1,), (1,)), ((), ())), preferred_element_type=F32)


def _dot_tn(a, b):
    return lax.dot_general(a, b, (((0,), (0,)), ((), ())), preferred_element_type=F32)


def _log_sigmoid(x):
    return -(jnp.maximum(-x, 0.0) + jnp.log1p(jnp.exp(-jnp.abs(x))))


def _head_norm64(y, gain):
    lo = lax.broadcasted_iota(jnp.int32, (1, LANES), 1) < B_DH
    cols = []
    for c in range(y.shape[1] // LANES):
        x = y[:, c * LANES:(c + 1) * LANES]
        xx = x * x
        s_lo = jnp.sum(jnp.where(lo, xx, 0.0), axis=-1, keepdims=True)
        s_hi = jnp.sum(jnp.where(lo, 0.0, xx), axis=-1, keepdims=True)
        scale = jnp.where(lo, lax.rsqrt(s_lo / B_DH + EPS), lax.rsqrt(s_hi / B_DH + EPS))
        cols.append(x * scale * gain[:, c * LANES:(c + 1) * LANES])
    return jnp.concatenate(cols, axis=1)


def _ffn_kernel(h_ref, g_ref, wg_ref, wu_ref, wo_ref, o_ref, *rest, emit_bf16):
    j = pl.program_id(1)
    if emit_bf16:
        wgb_ref, wub_ref, wob_ref, xn_ref = rest
        wg = wg_ref[...].astype(BF16)
        wu = wu_ref[...].astype(BF16)
        wo = wo_ref[...].astype(BF16)
        wgb_ref[...] = wg
        wub_ref[...] = wu
        wob_ref[...] = wo
    else:
        (xn_ref,) = rest
        wg = jnp.concatenate([wg_ref[b] for b in range(wg_ref.shape[0])], axis=1)
        wu = jnp.concatenate([wu_ref[b] for b in range(wu_ref.shape[0])], axis=1)
        wo = wo_ref[...]

    def half_ffn(xn):
        g = _dot(xn, wg)
        u = _dot(xn, wu)
        a = (g / (1.0 + jnp.exp(-g))) * (0.5 * u)
        return _dot(a.astype(BF16), wo)

    @pl.when(j == 0)
    def _():
        h = h_ref[...]
        xn_ref[...] = _rms(h, g_ref[...]).astype(BF16)
        o_ref[...] = h

    o_ref[...] += half_ffn(xn_ref[...])


FFN_CAST_TF = 256


def _ffn(h, gain, wg, wu, wo, tm, tf=512):
    rows = h.shape[0]
    n_ff = D_FF // tf
    sub = tf // FFN_CAST_TF
    return pl.pallas_call(
        functools.partial(_ffn_kernel, emit_bf16=False),
        grid=(rows // tm, n_ff),
        in_specs=[
            pl.BlockSpec((tm, D_MODEL), lambda i, j: (i, 0)),
            pl.BlockSpec((1, D_MODEL), lambda i, j: (0, 0)),
            pl.BlockSpec((sub, D_MODEL, FFN_CAST_TF), lambda i, j: (j, 0, 0)),
            pl.BlockSpec((sub, D_MODEL, FFN_CAST_TF), lambda i, j: (j, 0, 0)),
            pl.BlockSpec((tf, D_MODEL), lambda i, j: (j, 0)),
        ],
        out_specs=pl.BlockSpec((tm, D_MODEL), lambda i, j: (i, 0)),
        out_shape=jax.ShapeDtypeStruct((rows, D_MODEL), F32),
        scratch_shapes=[pltpu.VMEM((tm, D_MODEL), BF16)],
        compiler_params=_params(2, VMEM_FFN_MIB),
        name="ffn",
    )(h, gain, wg, wu, wo)


def _ffn_cast(h, gain, w_in, w_out, layer, which):
    rows = h.shape[0]
    tf = FFN_CAST_TF
    n_ff = D_FF // tf
    return pl.pallas_call(
        functools.partial(_ffn_kernel, emit_bf16=True),
        grid=(1, n_ff),
        in_specs=[
            pl.BlockSpec((rows, D_MODEL), lambda i, j: (0, 0)),
            pl.BlockSpec((1, D_MODEL), lambda i, j: (0, 0)),
            pl.BlockSpec((None, None, D_MODEL, tf), lambda i, j: (layer, which, 0, j)),
            pl.BlockSpec((None, None, D_MODEL, tf), lambda i, j: (layer, which, 0, j + n_ff)),
            pl.BlockSpec((None, None, tf, D_MODEL), lambda i, j: (layer, which, j, 0)),
        ],
        out_specs=[
            pl.BlockSpec((rows, D_MODEL), lambda i, j: (0, 0)),
            pl.BlockSpec((None, D_MODEL, tf), lambda i, j: (j, 0, 0)),
            pl.BlockSpec((None, D_MODEL, tf), lambda i, j: (j, 0, 0)),
            pl.BlockSpec((tf, D_MODEL), lambda i, j: (j, 0)),
        ],
        out_shape=[
            jax.ShapeDtypeStruct((rows, D_MODEL), F32),
            jax.ShapeDtypeStruct((n_ff, D_MODEL, tf), BF16),
            jax.ShapeDtypeStruct((n_ff, D_MODEL, tf), BF16),
            jax.ShapeDtypeStruct((D_FF, D_MODEL), BF16),
        ],
        scratch_shapes=[pltpu.VMEM((rows, D_MODEL), BF16)],
        compiler_params=_params(2),
        name="ffn_cast",
    )(h, gain, w_in, w_in, w_out)


GATE_ROWS = 2 * A_HEADS


def _inproj_kernel(h_ref, g_ref, w_ref, wgate_ref, bgate_ref, p_ref, gates_ref, *rest, emit_bf16):
    j = pl.program_id(1)
    if emit_bf16:
        wb_ref, xn_ref = rest
        w = w_ref[...].astype(BF16)
        wb_ref[...] = w
    else:
        (xn_ref,) = rest
        w = w_ref[...]

    @pl.when(j == 0)
    def _():
        xn = _rms(h_ref[...], g_ref[...]).astype(BF16)
        xn_ref[...] = xn
        wgate = jnp.concatenate([wgate_ref[...], jnp.zeros((LANES - GATE_ROWS, D_MODEL), F32)], axis=0)
        pre = _dot_nt(xn, wgate.astype(BF16)) + bgate_ref[...]
        capped = A_GATE_CAP * jnp.tanh(pre / A_GATE_CAP)
        lane = lax.broadcasted_iota(jnp.int32, (1, LANES), 1)
        gates_ref[...] = jnp.where(lane < A_HEADS, capped, _log_sigmoid(capped))

    p_ref[...] = _dot_nt(xn_ref[...], w)


def _inproj(h, gain, w_t, w_gate_t, bgate, tm, emit_bf16=False, tn=1024):
    rows = h.shape[0]
    assert not emit_bf16 or rows == tm
    out_specs = [
        pl.BlockSpec((tm, tn), lambda i, j: (i, j)),
        pl.BlockSpec((tm, LANES), lambda i, j: (i, 0)),
    ]
    out_shape = [jax.ShapeDtypeStruct((rows, A_QKVO), F32),
                 jax.ShapeDtypeStruct((rows, LANES), F32)]
    if emit_bf16:
        out_specs.append(pl.BlockSpec((tn, D_MODEL), lambda i, j: (j, 0)))
        out_shape.append(jax.ShapeDtypeStruct((A_QKVO, D_MODEL), BF16))
    return pl.pallas_call(
        functools.partial(_inproj_kernel, emit_bf16=emit_bf16),
        grid=(rows // tm, A_QKVO // tn),
        in_specs=[
            pl.BlockSpec((tm, D_MODEL), lambda i, j: (i, 0)),
            pl.BlockSpec((1, D_MODEL), lambda i, j: (0, 0)),
            pl.BlockSpec((tn, D_MODEL), lambda i, j: (j, 0)),
            pl.BlockSpec((GATE_ROWS, D_MODEL), lambda i, j: (A_QKVO // GATE_ROWS, 0)),
            pl.BlockSpec((1, LANES), lambda i, j: (0, 0)),
        ],
        out_specs=out_specs,
        out_shape=out_shape,
        scratch_shapes=[pltpu.VMEM((tm, D_MODEL), BF16)],
        compiler_params=_params(2),
        name="mlstm_inproj",
    )(h, gain, w_t, w_gate_t, bgate)


def _mlstm_chunk_kernel(q_ref, k_ref, v_ref, g_ref, c0_ref, n0_ref, m0_ref,
                        h_ref, c_ref, n_ref, m_ref, qk_ref, qc_ref, sb_ref, *, n_valid):
    L = A_CHUNK
    heads = range(A_HEADS)

    @pl.when(pl.program_id(1) == 0)
    def _():
        c_ref[...] = c0_ref[...]
        n_ref[...] = n0_ref[...]
        m_ref[...] = m0_ref[...]

    for hd in heads:
        qb = q_ref[:, hd * A_DK:(hd + 1) * A_DK].astype(BF16)
        kb = (k_ref[:, hd * A_DK:(hd + 1) * A_DK] * (A_DK ** -0.5)).astype(BF16)
        qk_ref[hd] = _dot_nt(qb, kb)
        qc_ref[hd] = _dot(qb, c_ref[0, hd].astype(BF16))

    gates = g_ref[...]
    row = lax.broadcasted_iota(jnp.int32, (L, L), 0)
    col = lax.broadcasted_iota(jnp.int32, (L, L), 1)
    causal = col <= row
    eye = col == row
    masked = n_valid < L
    if masked:
        row_ok = lax.broadcasted_iota(jnp.int32, (L, 1), 0) < n_valid
        gates_lf = jnp.where(row_ok, gates, 0.0)
    else:
        gates_lf = gates
    csum = jnp.dot(causal.astype(F32), gates_lf, precision=lax.Precision.HIGHEST,
                   preferred_element_type=F32)

    def to_row(x_col):
        return jnp.sum(jnp.where(eye, x_col, 0.0), axis=0, keepdims=True)

    w_intra, w_inter, floor, w_state, decay, m_new = [], [], [], [], [], []
    for hd in heads:
        b_col = csum[:, A_HEADS + hd:A_HEADS + hd + 1]
        ig_col = gates[:, hd:hd + 1]
        if masked:
            ig_col = jnp.where(row_ok, ig_col, NEG_INF)
        b_row = to_row(b_col)
        ig_row = to_row(ig_col)
        m_prev = m_ref[0, hd][:, 0:1]
        d_log = jnp.where(causal, b_col - b_row + ig_row, NEG_INF)
        inter_log = b_col + m_prev
        m_t = jnp.maximum(inter_log, jnp.max(d_log, axis=-1, keepdims=True))
        w_intra.append(jnp.exp(d_log - m_t))
        w_inter.append(jnp.exp(inter_log - m_t))
        floor.append(jnp.exp(-m_t))
        b_last = b_col[L - 1:L, :]
        w_log = b_last - b_col + ig_col
        m_new.append(jnp.maximum(b_last + m_prev, jnp.max(w_log, axis=0, keepdims=True)))
        w_state.append(jnp.exp(w_log - m_new[hd]))
        decay.append(jnp.exp(b_last + m_prev - m_new[hd]))

    den = []
    for hd in heads:
        s = qk_ref[hd] * w_intra[hd]
        sb_ref[hd] = s.astype(BF16)
        q = q_ref[:, hd * A_DK:(hd + 1) * A_DK]
        d = jnp.sum(s, axis=-1, keepdims=True) + w_inter[hd] * jnp.sum(q * n_ref[0, hd], axis=-1, keepdims=True)
        den.append(jnp.maximum(jnp.abs(d), floor[hd]))

    for hd in heads:
        k = k_ref[:, hd * A_DK:(hd + 1) * A_DK] * (A_DK ** -0.5)
        vb = v_ref[:, hd * A_DV:(hd + 1) * A_DV].astype(BF16)
        num = _dot(sb_ref[hd], vb) + w_inter[hd] * qc_ref[hd]
        h_ref[:, hd * A_DV:(hd + 1) * A_DV] = num / den[hd]
        kw = k * w_state[hd]
        c_ref[0, hd] = decay[hd] * c_ref[0, hd] + _dot_tn(kw.astype(BF16), vb)
        n_ref[0, hd] = decay[hd] * n_ref[0, hd] + jnp.sum(kw, axis=0, keepdims=True)
        m_ref[0, hd] = jnp.broadcast_to(m_new[hd], (1, LANES))


def _mlstm_chunks(p, gates, c0, n0, m0, n_seq, n_chunks, row_block0, n_valid, shared_state):
    L = A_CHUNK
    rb = lambda b, c: row_block0 + b * n_chunks + c
    st = (lambda b, c: (0, 0, 0, 0)) if shared_state else (lambda b, c: (b, 0, 0, 0))
    return pl.pallas_call(
        functools.partial(_mlstm_chunk_kernel, n_valid=n_valid),
        grid=(n_seq, n_chunks),
        in_specs=[
            pl.BlockSpec((L, A_HEADS * A_DK), lambda b, c: (rb(b, c), 0)),
            pl.BlockSpec((L, A_HEADS * A_DK), lambda b, c: (rb(b, c), 1)),
            pl.BlockSpec((L, A_HEADS * A_DV), lambda b, c: (rb(b, c), 1)),
            pl.BlockSpec((L, LANES), lambda b, c: (rb(b, c), 0)),
            pl.BlockSpec((1, A_HEADS, A_DK, A_DV), st),
            pl.BlockSpec((1, A_HEADS, 1, A_DK), st),
            pl.BlockSpec((1, A_HEADS, 1, LANES), st),
        ],
        out_specs=[
            pl.BlockSpec((L, A_HEADS * A_DV), lambda b, c: (b * n_chunks + c, 0)),
            pl.BlockSpec((1, A_HEADS, A_DK, A_DV), lambda b, c: (b, 0, 0, 0)),
            pl.BlockSpec((1, A_HEADS, 1, A_DK), lambda b, c: (b, 0, 0, 0)),
            pl.BlockSpec((1, A_HEADS, 1, LANES), lambda b, c: (b, 0, 0, 0)),
        ],
        out_shape=[
            jax.ShapeDtypeStruct((n_seq * n_chunks * L, D_MODEL), F32),
            jax.ShapeDtypeStruct((n_seq, A_HEADS, A_DK, A_DV), F32),
            jax.ShapeDtypeStruct((n_seq, A_HEADS, 1, A_DK), F32),
            jax.ShapeDtypeStruct((n_seq, A_HEADS, 1, LANES), F32),
        ],
        scratch_shapes=[pltpu.VMEM((A_HEADS, L, L), F32),
                        pltpu.VMEM((A_HEADS, L, A_DV), F32),
                        pltpu.VMEM((A_HEADS, L, L), BF16)],
        compiler_params=_params(2),
        name="mlstm_chunks",
    )(p, p, p, gates, c0, n0, m0)


def _mlstm_sample_kernel(q_ref, k_ref, v_ref, g_ref, c0_ref, n0_ref, m0_ref,
                         h_ref, c_ref, n_ref, m_ref):
    R = 2 * DEC_SEQ
    PAD = A_CHUNK - R
    gates = g_ref[...]
    r_col = lax.broadcasted_iota(jnp.int32, (R, 1), 0)
    is_a = r_col < DEC_SEQ
    row = lax.broadcasted_iota(jnp.int32, (R, LANES), 0)
    lane = lax.broadcasted_iota(jnp.int32, (R, LANES), 1)
    same = ((lane < DEC_SEQ) & (row < DEC_SEQ)) | ((lane >= DEC_SEQ) & (lane < R) & (row >= DEC_SEQ))
    causal = same & (lane <= row)
    eye = lane == row

    def to_row(x_col):
        return jnp.sum(jnp.where(eye, x_col, 0.0), axis=0, keepdims=True)

    for hd in range(A_HEADS):
        lf_col = gates[:, A_HEADS + hd:A_HEADS + hd + 1]
        ig_col = gates[:, hd:hd + 1]
        lf_row = to_row(lf_col)
        ig_row = to_row(ig_col)
        b_col = jnp.sum(jnp.where(causal, lf_row, 0.0), axis=1, keepdims=True)
        b_row = to_row(b_col)
        m_a = m0_ref[0, hd][:, 0:1]
        m_b = m0_ref[1, hd][:, 0:1]
        m_prev = jnp.where(is_a, m_a, m_b)
        c_a = c0_ref[0, hd]
        c_b = c0_ref[1, hd]
        n_a = n0_ref[0, hd]
        n_b = n0_ref[1, hd]

        q = q_ref[:, hd * A_DK:(hd + 1) * A_DK]
        k = k_ref[:, hd * A_DK:(hd + 1) * A_DK] * (A_DK ** -0.5)
        v = v_ref[:, hd * A_DV:(hd + 1) * A_DV]
        qb = q.astype(BF16)
        k_pad = jnp.concatenate([k, jnp.zeros((PAD, A_DK), F32)], axis=0).astype(BF16)
        v_pad = jnp.concatenate([v, jnp.zeros((PAD, A_DV), F32)], axis=0).astype(BF16)

        d_log = jnp.where(causal, b_col - b_row + ig_row, NEG_INF)
        inter_log = b_col + m_prev
        m_t = jnp.maximum(inter_log, jnp.max(d_log, axis=-1, keepdims=True))
        w_intra = jnp.exp(d_log - m_t)
        w_inter = jnp.exp(inter_log - m_t)
        s = _dot_nt(qb, k_pad) * w_intra
        q_c = jnp.where(is_a, _dot(qb, c_a.astype(BF16)), _dot(qb, c_b.astype(BF16)))
        num = _dot(s.astype(BF16), v_pad) + w_inter * q_c
        q_n = jnp.sum(q * jnp.where(is_a, n_a, n_b), axis=-1, keepdims=True)
        den = jnp.sum(s, axis=-1, keepdims=True) + w_inter * q_n
        den = jnp.maximum(jnp.abs(den), jnp.exp(-m_t))
        h_ref[:, hd * A_DV:(hd + 1) * A_DV] = num / den

        for idx, sel, m_x, c_x, n_x in ((0, is_a, m_a, c_a, n_a),
                                        (1, jnp.logical_not(is_a), m_b, c_b, n_b)):
            last = (idx + 1) * DEC_SEQ - 1
            b_last = b_col[last:last + 1, :]
            w_log = jnp.where(sel, b_last - b_col + ig_col, NEG_INF)
            m_new = jnp.maximum(b_last + m_x, jnp.max(w_log, axis=0, keepdims=True))
            w_state = jnp.exp(w_log - m_new)
            decay = jnp.exp(b_last + m_x - m_new)
            kw = k * w_state
            kw_pad = jnp.concatenate([kw, jnp.zeros((PAD, A_DK), F32)], axis=0).astype(BF16)
            c_ref[idx, hd] = decay * c_x + _dot_tn(kw_pad, v_pad)
            n_ref[idx, hd] = decay * n_x + jnp.sum(kw, axis=0, keepdims=True)
            m_ref[idx, hd] = jnp.broadcast_to(m_new, (1, LANES))


def _mlstm_sample(p, gates, c0, n0, m0):
    R = 2 * DEC_SEQ
    n_pairs = DEC_BATCH // 2
    st = lambda i: (i, 0, 0, 0)
    return pl.pallas_call(
        _mlstm_sample_kernel,
        grid=(n_pairs,),
        in_specs=[
            pl.BlockSpec((R, A_HEADS * A_DK), lambda i: (i, 0)),
            pl.BlockSpec((R, A_HEADS * A_DK), lambda i: (i, 1)),
            pl.BlockSpec((R, A_HEADS * A_DV), lambda i: (i, 1)),
            pl.BlockSpec((R, LANES), lambda i: (i, 0)),
            pl.BlockSpec((2, A_HEADS, A_DK, A_DV), st),
            pl.BlockSpec((2, A_HEADS, 1, A_DK), st),
            pl.BlockSpec((2, A_HEADS, 1, LANES), st),
        ],
        out_specs=[
            pl.BlockSpec((R, A_HEADS * A_DV), lambda i: (i, 0)),
            pl.BlockSpec((2, A_HEADS, A_DK, A_DV), st),
            pl.BlockSpec((2, A_HEADS, 1, A_DK), st),
            pl.BlockSpec((2, A_HEADS, 1, LANES), st),
        ],
        out_shape=[
            jax.ShapeDtypeStruct((S_ROWS, D_MODEL), F32),
            jax.ShapeDtypeStruct((DEC_BATCH, A_HEADS, A_DK, A_DV), F32),
            jax.ShapeDtypeStruct((DEC_BATCH, A_HEADS, 1, A_DK), F32),
            jax.ShapeDtypeStruct((DEC_BATCH, A_HEADS, 1, LANES), F32),
        ],
        compiler_params=_params(1),
        name="mlstm_sample",
    )(p, p, p, gates, c0, n0, m0)


def _mlstm_out_kernel(hm_ref, o_ref, hg_ref, w_ref, res_ref, out_ref):
    cols = []
    for hd in range(A_HEADS):
        x = hm_ref[:, hd * A_DV:(hd + 1) * A_DV]
        cols.append(x * lax.rsqrt(jnp.mean(x * x, axis=-1, keepdims=True) + EPS))
    hn = jnp.concatenate(cols, axis=1) * hg_ref[...]
    o = o_ref[...]
    pre = (hn * (1.0 / (1.0 + jnp.exp(-o)))).astype(BF16)
    out_ref[...] = res_ref[...] + _dot(pre, w_ref[...])


def _mlstm_out(hm, p, head_gain, w, res, tm):
    rows = hm.shape[0]
    return pl.pallas_call(
        _mlstm_out_kernel,
        grid=(rows // tm,),
        in_specs=[
            pl.BlockSpec((tm, D_MODEL), lambda i: (i, 0)),
            pl.BlockSpec((tm, D_MODEL), lambda i: (i, 2)),
            pl.BlockSpec((1, D_MODEL), lambda i: (0, 0)),
            _resident((D_MODEL, D_MODEL)),
            pl.BlockSpec((tm, D_MODEL), lambda i: (i, 0)),
        ],
        out_specs=pl.BlockSpec((tm, D_MODEL), lambda i: (i, 0)),
        out_shape=jax.ShapeDtypeStruct((rows, D_MODEL), F32),
        compiler_params=_params(1),
        name="mlstm_out",
    )(hm, p, head_gain, w, res)


def _normproj_kernel(h_ref, g_ref, w_ref, hg_ref, o_ref, *, n_norm):
    xn = _rms(h_ref[...], g_ref[...]).astype(BF16)
    y = _dot(xn, w_ref[...])
    if n_norm == y.shape[1]:
        o_ref[...] = _head_norm64(y, hg_ref[...])
    else:
        o_ref[:, :n_norm] = _head_norm64(y[:, :n_norm], hg_ref[...])
        o_ref[:, n_norm:] = y[:, n_norm:]


def _normproj(h, gain, w, head_gain, tm):
    rows = h.shape[0]
    n = w.shape[1]
    n_norm = head_gain.shape[1]
    return pl.pallas_call(
        functools.partial(_normproj_kernel, n_norm=n_norm),
        grid=(rows // tm,),
        in_specs=[
            pl.BlockSpec((tm, D_MODEL), lambda i: (i, 0)),
            pl.BlockSpec((1, D_MODEL), lambda i: (0, 0)),
            _resident((D_MODEL, n)),
            pl.BlockSpec((1, n_norm), lambda i: (0, 0)),
        ],
        out_specs=pl.BlockSpec((tm, n), lambda i: (i, 0)),
        out_shape=jax.ShapeDtypeStruct((rows, n), F32),
        compiler_params=_params(1),
        name="normproj",
    )(h, gain, w, head_gain)


def _matres_kernel(x_ref, w_ref, res_ref, o_ref):
    o_ref[...] = res_ref[...] + _dot(x_ref[...].astype(BF16), w_ref[...])


def _matres(x, w, res, tm):
    rows = x.shape[0]
    return pl.pallas_call(
        _matres_kernel,
        grid=(rows // tm,),
        in_specs=[
            pl.BlockSpec((tm, D_MODEL), lambda i: (i, 0)),
            _resident((D_MODEL, D_MODEL)),
            pl.BlockSpec((tm, D_MODEL), lambda i: (i, 0)),
        ],
        out_specs=pl.BlockSpec((tm, D_MODEL), lambda i: (i, 0)),
        out_shape=jax.ShapeDtypeStruct((rows, D_MODEL), F32),
        compiler_params=_params(1),
        name="matres",
    )(x, w, res)


def _alibi_slope(head):
    return 2.0 ** (-8.0 * (head + 1) / B_HEADS)


PROMPT_SUB = WINDOW // 2
PROMPT_BAND = WINDOW + PROMPT_SUB
PROMPT_KEYS = 2 * LANES
HEAD_PAIRS = B_HEADS // 2
PAIRS_PER_GROUP = B_GROUP // 2


def _pair_rhs(x, c):
    col = x[:, c * LANES:(c + 1) * LANES]
    rol = pltpu.roll(col, B_DH, axis=1)
    lo = lax.broadcasted_iota(jnp.int32, (1, LANES), 1) < B_DH
    even = jnp.concatenate([jnp.where(lo, col, 0.0), jnp.where(lo, 0.0, rol)], axis=0)
    odd = jnp.concatenate([jnp.where(lo, rol, 0.0), jnp.where(lo, 0.0, col)], axis=0)
    return even.astype(BF16), odd.astype(BF16)


def _attn_prompt_kernel(q_ref, kvo_ref, kvp_ref, kvm_ref, sink_ref, o_ref, bias_ref, s_ref, p_ref):
    W = WINDOW
    SB = PROMPT_SUB
    NK = PROMPT_KEYS
    first = (pl.program_id(0) == 0) & (pl.program_id(1) == 0)
    j = pl.program_id(1)
    rows = PAIRS_PER_GROUP * SB
    lo = lax.broadcasted_iota(jnp.int32, (1, LANES), 1) < B_DH

    @pl.when(first)
    def _():
        c = lax.broadcasted_iota(jnp.int32, (SB, NK), 1)
        is_meta = c < N_META
        is_sink = c == NK - 1
        for sub in range(2):
            i = lax.broadcasted_iota(jnp.int32, (SB, NK), 0) + sub * SB
            pos = c - N_META + sub * SB
            rel = W + i - pos
            in_band = (c >= N_META) & (c < N_META + PROMPT_BAND) & (rel >= 0) & (rel < W)
            for variant in range(2):
                if variant == 0:
                    dist = jnp.where(is_meta, jnp.minimum(i + N_META - c, W), rel).astype(F32)
                    valid = is_meta | (in_band & (pos >= W))
                else:
                    dist = jnp.where(is_meta, W, rel).astype(F32)
                    valid = is_meta | in_band
                for hd in range(B_HEADS):
                    r0 = (hd // 2) * SB
                    c0 = (hd % 2) * NK
                    table = jnp.where(valid, -_alibi_slope(hd) * dist, NEG_INF)
                    bias_ref[variant, sub, r0:r0 + SB, c0:c0 + NK] = jnp.where(
                        is_sink, sink_ref[0:1, hd:hd + 1], table)

    variant = jnp.minimum(j, 1)
    kvm = kvm_ref[...]
    band = jnp.concatenate([kvp_ref[...], kvo_ref[...]], axis=0)
    pad = jnp.zeros((NK - N_META - PROMPT_BAND, 2 * B_KV), F32)
    ones_rhs = jnp.concatenate([jnp.broadcast_to(jnp.where(lo, 1.0, 0.0), (NK, LANES)),
                                jnp.broadcast_to(jnp.where(lo, 0.0, 1.0), (NK, LANES))], axis=0).astype(BF16)

    sub_rows = B_KV_HEADS * rows
    for sub in range(2):
        keys = jnp.concatenate([kvm, band[sub * SB:sub * SB + PROMPT_BAND], pad], axis=0)
        k_rhs = _pair_rhs(keys, 0) + _pair_rhs(keys, 1)
        v_rhs = _pair_rhs(keys, 2) + _pair_rhs(keys, 3)
        for g in range(B_KV_HEADS):
            p0 = g * PAIRS_PER_GROUP
            r0 = (sub * B_KV_HEADS + g) * rows
            qg = jnp.concatenate([q_ref[sub * SB:(sub + 1) * SB, (p0 + m) * LANES:(p0 + m + 1) * LANES]
                                  for m in range(PAIRS_PER_GROUP)], axis=0) * (B_DH ** -0.5)
            s_ref[r0:r0 + rows, :] = _dot_nt(qg.astype(BF16), k_rhs[g])

        s = s_ref[sub * sub_rows:(sub + 1) * sub_rows, :] + bias_ref[variant, sub]
        halves = []
        for half in range(2):
            sh = s[:, half * NK:(half + 1) * NK]
            halves.append(jnp.exp(sh - jnp.max(sh, axis=-1, keepdims=True)).astype(BF16))
        p_ref[sub * sub_rows:(sub + 1) * sub_rows, :] = jnp.concatenate(halves, axis=1)

        for g in range(B_KV_HEADS):
            p0 = g * PAIRS_PER_GROUP
            r0 = (sub * B_KV_HEADS + g) * rows
            od = _dot(p_ref[r0:r0 + rows, :], jnp.concatenate([v_rhs[g], ones_rhs], axis=1))
            o = od[:, :LANES] / od[:, LANES:]
            for m in range(PAIRS_PER_GROUP):
                o_ref[sub * SB:(sub + 1) * SB, (p0 + m) * LANES:(p0 + m + 1) * LANES] = o[m * SB:(m + 1) * SB, :]


def _attn_prompt(q, kv_p, kv_q, sinks_row):
    nb = SEQ // WINDOW
    blk = lambda b, j: b * nb + j
    return pl.pallas_call(
        _attn_prompt_kernel,
        grid=(BATCH, nb),
        in_specs=[
            pl.BlockSpec((WINDOW, D_MODEL), lambda b, j: (blk(b, j), 0)),
            pl.BlockSpec((WINDOW, 2 * B_KV), lambda b, j: (blk(b, j), 0)),
            pl.BlockSpec((WINDOW, 2 * B_KV), lambda b, j: (blk(b, jnp.maximum(j - 1, 0)), 0)),
            pl.BlockSpec((N_META, 2 * B_KV), lambda b, j: (S_ROWS // N_META, 0)),
            pl.BlockSpec((1, LANES), lambda b, j: (0, 0)),
        ],
        out_specs=pl.BlockSpec((WINDOW, D_MODEL), lambda b, j: (blk(b, j), 0)),
        out_shape=jax.ShapeDtypeStruct((P_ROWS, D_MODEL), F32),
        scratch_shapes=[pltpu.VMEM((2, 2, HEAD_PAIRS * PROMPT_SUB, 2 * PROMPT_KEYS), F32),
                        pltpu.VMEM((2 * HEAD_PAIRS * PROMPT_SUB, 2 * PROMPT_KEYS), F32),
                        pltpu.VMEM((2 * HEAD_PAIRS * PROMPT_SUB, 2 * PROMPT_KEYS), BF16)],
        compiler_params=_params(2),
        name="attn_prompt",
    )(q, kv_p, kv_p, kv_q, sinks_row)


GROUP_SHIFT = B_GROUP.bit_length() - 1
assert 1 << GROUP_SHIFT == B_GROUP
SAMPLE_OLD = SUBLANES
SAMPLE_KEYS = 2 * LANES
SAMPLE_SEQ_PER_STEP = 8
SAMPLE_ROWS = DEC_SEQ * B_GROUP
KV_PAIRS = B_KV_HEADS // 2


def _split_rhs(col):
    lo = lax.broadcasted_iota(jnp.int32, (1, LANES), 1) < B_DH
    return jnp.concatenate([jnp.where(lo, col, 0.0), jnp.where(lo, 0.0, col)], axis=0).astype(BF16)


def _attn_sample_kernel(q_ref, km_ref, vm_ref, ko_ref, vo_ref, kw_ref, vw_ref, sink_ref, o_ref,
                        bias_ref, s_ref, p_ref):
    W = WINDOW
    NK = SAMPLE_KEYS
    R = SAMPLE_ROWS
    off_w = N_META + SAMPLE_OLD
    lo = lax.broadcasted_iota(jnp.int32, (1, LANES), 1) < B_DH

    @pl.when(pl.program_id(0) == 0)
    def _():
        row = lax.broadcasted_iota(jnp.int32, (R, 1), 0)
        r_in_group = jnp.bitwise_and(row, B_GROUP - 1)
        t = jnp.right_shift(lax.broadcasted_iota(jnp.int32, (R, NK), 0), GROUP_SHIFT)
        c = lax.broadcasted_iota(jnp.int32, (R, NK), 1)
        is_meta = c < N_META
        is_old = (c >= N_META) & (c < off_w)
        is_win = (c >= off_w) & (c < off_w + W)
        j_old = c - N_META
        d_win = (W - DEC_SEQ) + t - (c - off_w)
        dist = jnp.where(is_meta, W, jnp.where(is_old, W + t - j_old, d_win)).astype(F32)
        valid = is_meta | (is_old & (j_old > t) & (j_old < DEC_SEQ)) | (is_win & (d_win >= 0))
        for pair in range(KV_PAIRS):
            for e in range(2):
                slope = jnp.zeros((R, 1), F32)
                sink = jnp.zeros((R, 1), F32)
                for r in range(B_GROUP):
                    hd = (2 * pair + e) * B_GROUP + r
                    slope = jnp.where(r_in_group == r, _alibi_slope(hd), slope)
                    sink = jnp.where(r_in_group == r, sink_ref[0:1, hd:hd + 1], sink)
                table = jnp.where(c == NK - 1, sink, jnp.where(valid, -slope * dist, NEG_INF))
                for sq in range(SAMPLE_SEQ_PER_STEP):
                    r0 = (sq * KV_PAIRS + pair) * R
                    bias_ref[r0:r0 + R, e * NK:(e + 1) * NK] = table

    pad = jnp.zeros((NK - off_w - W, B_KV), F32)
    ones_rhs = jnp.concatenate([jnp.broadcast_to(jnp.where(lo, 1.0, 0.0), (NK, LANES)),
                                jnp.broadcast_to(jnp.where(lo, 0.0, 1.0), (NK, LANES))], axis=0).astype(BF16)
    v_rhs = []
    for sq in range(SAMPLE_SEQ_PER_STEP):
        kcat = jnp.concatenate([km_ref[sq], ko_ref[sq], kw_ref[sq], pad], axis=0)
        vcat = jnp.concatenate([vm_ref[sq], vo_ref[sq], vw_ref[sq], pad], axis=0)
        for pair in range(KV_PAIRS):
            r0 = (sq * KV_PAIRS + pair) * R
            k_rhs = _split_rhs(kcat[:, pair * LANES:(pair + 1) * LANES])
            v_rhs.append(_split_rhs(vcat[:, pair * LANES:(pair + 1) * LANES]))
            q = (q_ref[sq, pair] * (B_DH ** -0.5)).astype(BF16)
            s_ref[r0:r0 + R, :] = _dot_nt(q, k_rhs)

    s = s_ref[...] + bias_ref[...]
    halves = []
    for e in range(2):
        sh = s[:, e * NK:(e + 1) * NK]
        halves.append(jnp.exp(sh - jnp.max(sh, axis=-1, keepdims=True)).astype(BF16))
    p_ref[...] = jnp.concatenate(halves, axis=1)

    for sq in range(SAMPLE_SEQ_PER_STEP):
        for pair in range(KV_PAIRS):
            b = sq * KV_PAIRS + pair
            od = _dot(p_ref[b * R:(b + 1) * R, :], jnp.concatenate([v_rhs[b], ones_rhs], axis=1))
            o_ref[sq, pair] = od[:, :LANES] / od[:, LANES:]


def _attn_sample(q4, k_meta, v_meta, k_old, v_old, k_win, v_win, sinks_row):
    nb = SAMPLE_SEQ_PER_STEP
    R = SAMPLE_ROWS
    n_rows = nb * KV_PAIRS * R
    seq3 = lambda rows: pl.BlockSpec((nb, rows, B_KV), lambda i: (i, 0, 0))
    qspec = pl.BlockSpec((nb, KV_PAIRS, R, LANES), lambda i: (i, 0, 0, 0))
    return pl.pallas_call(
        _attn_sample_kernel,
        grid=(DEC_BATCH // nb,),
        in_specs=[qspec, seq3(N_META), seq3(N_META), seq3(SAMPLE_OLD), seq3(SAMPLE_OLD),
                  seq3(WINDOW), seq3(WINDOW), pl.BlockSpec((1, LANES), lambda i: (0, 0))],
        out_specs=qspec,
        out_shape=jax.ShapeDtypeStruct((DEC_BATCH, KV_PAIRS, R, LANES), F32),
        scratch_shapes=[pltpu.VMEM((n_rows, 2 * SAMPLE_KEYS), F32),
                        pltpu.VMEM((n_rows, 2 * SAMPLE_KEYS), F32),
                        pltpu.VMEM((n_rows, 2 * SAMPLE_KEYS), BF16)],
        compiler_params=_params(1),
        name="attn_sample",
    )(q4, k_meta, v_meta, k_old, v_old, k_win, v_win, sinks_row)


def kernel(x_prompt, x_sample, state_C, state_n, state_m, cache_k_meta, cache_v_meta, cache_k_win, cache_v_win, meta_tokens, ffn_norm, w_ffn_in, w_ffn_out, mix_norm, w_a_in, b_a_gate, a_head_norm, w_a_out, kv_norm, w_kv, k_norm, w_q, q_norm, sinks, w_b_out):
    assert x_prompt.shape == (BATCH, SEQ, D_MODEL) and x_sample.shape == (DEC_BATCH, DEC_SEQ, D_MODEL)
    assert w_a_in.shape[0] == 1 and w_q.shape[0] == 1 and ffn_norm.shape[0] == 2

    wa_in_t = jnp.swapaxes(w_a_in[0], 0, 1)
    ba_gate = jnp.pad(b_a_gate[0].astype(F32), (0, LANES - 2 * A_HEADS)).reshape(1, LANES)
    wa_out = w_a_out[0].astype(BF16)
    wkv = w_kv.astype(BF16)
    wq = w_q[0].astype(BF16)
    wb_out = w_b_out[0].astype(BF16)
    row = lambda x: x.astype(F32).reshape(1, -1)
    k_gain = jnp.tile(row(k_norm), (1, B_KV_HEADS))
    q_gain = jnp.tile(row(q_norm[0]), (1, B_HEADS))
    sinks_row = jnp.pad(sinks[0].astype(F32), (0, LANES - B_HEADS)).reshape(1, LANES)

    h_p = x_prompt.reshape(P_ROWS, D_MODEL)
    h_q = jnp.concatenate([x_sample.reshape(S_ROWS, D_MODEL), meta_tokens.astype(F32),
                           jnp.zeros((A_CHUNK - N_META, D_MODEL), F32)], axis=0)
    TM_P, TM_W = 1024, 512

    h_q, *wf = _ffn_cast(h_q, row(ffn_norm[0, 0]), w_ffn_in, w_ffn_out, 0, 0)
    h_p = _ffn(h_p, row(ffn_norm[0, 0]), *wf, TM_P)
    p_q, g_q, wa_in = _inproj(h_q, row(mix_norm[0]), wa_in_t, wa_in_t, ba_gate, Q_ROWS, emit_bf16=True)
    p_p, g_p = _inproj(h_p, row(mix_norm[0]), wa_in, wa_in_t, ba_gate, TM_P)

    zc = jnp.zeros((1, A_HEADS, A_DK, A_DV), F32)
    zn = jnp.zeros((1, A_HEADS, 1, A_DK), F32)
    zm = jnp.zeros((1, A_HEADS, 1, LANES), F32)
    hm_m, c_m, n_m, m_m = _mlstm_chunks(p_q, g_q, zc, zn, zm, 1, 1, META_BLOCK, N_META, True)
    hm_p, c_p, n_p, m_p = _mlstm_chunks(p_p, g_p, c_m, n_m, m_m, BATCH, SEQ // A_CHUNK, 0, A_CHUNK, True)
    m0_s = jnp.broadcast_to(state_m[0].astype(F32)[:, :, None, None], (DEC_BATCH, A_HEADS, 1, LANES))
    hm_s, c_s, n_s, m_s = _mlstm_sample(p_q, g_q, state_C[0].astype(F32),
                                        state_n[0].astype(F32)[:, :, None, :], m0_s)
    hm_q = jnp.concatenate([hm_s, hm_m], axis=0)

    h_p = _mlstm_out(hm_p, p_p, row(a_head_norm[0]), wa_out, h_p, TM_W)
    h_q = _mlstm_out(hm_q, p_q, row(a_head_norm[0]), wa_out, h_q, A_CHUNK)
    h_q, *wf = _ffn_cast(h_q, row(ffn_norm[0, 1]), w_ffn_in, w_ffn_out, 0, 1)
    h_p = _ffn(h_p, row(ffn_norm[0, 1]), *wf, TM_P)

    kv_p = _normproj(h_p, row(kv_norm), wkv, k_gain, TM_P)
    kv_q = _normproj(h_q, row(kv_norm), wkv, k_gain, Q_ROWS)
    k_s = kv_q[:S_ROWS, :B_KV].reshape(DEC_BATCH, DEC_SEQ, B_KV_HEADS, B_DH)
    v_s = kv_q[:S_ROWS, B_KV:].reshape(DEC_BATCH, DEC_SEQ, B_KV_HEADS, B_DH)
    k_win_s = jnp.concatenate([cache_k_win, k_s.astype(cache_k_win.dtype)], axis=1)[:, -WINDOW:]
    v_win_s = jnp.concatenate([cache_v_win, v_s.astype(cache_v_win.dtype)], axis=1)[:, -WINDOW:]

    h_s = h_q[:S_ROWS]
    h_s, *wf = _ffn_cast(h_s, row(ffn_norm[1, 0]), w_ffn_in, w_ffn_out, 1, 0)
    h_p = _ffn(h_p, row(ffn_norm[1, 0]), *wf, TM_P)
    q_p = _normproj(h_p, row(mix_norm[1]), wq, q_gain, TM_W)
    q_s = _normproj(h_s, row(mix_norm[1]), wq, q_gain, TM_W)

    o_p = _attn_prompt(q_p, kv_p, kv_q, sinks_row)
    q4 = q_s.reshape(DEC_BATCH, DEC_SEQ, KV_PAIRS, 2, B_GROUP, B_DH).transpose(0, 2, 1, 4, 3, 5)
    q4 = q4.reshape(DEC_BATCH, KV_PAIRS, SAMPLE_ROWS, LANES)
    seq3 = lambda x: x.astype(F32).reshape(DEC_BATCH, -1, B_KV)
    o4 = _attn_sample(q4, seq3(cache_k_meta), seq3(cache_v_meta),
                      seq3(cache_k_win[:, :SAMPLE_OLD]), seq3(cache_v_win[:, :SAMPLE_OLD]),
                      seq3(k_win_s), seq3(v_win_s), sinks_row)
    o_s = o4.reshape(DEC_BATCH, KV_PAIRS, DEC_SEQ, B_GROUP, 2, B_DH).transpose(0, 2, 1, 4, 3, 5)
    o_s = o_s.reshape(S_ROWS, D_MODEL)

    h_p = _matres(o_p, wb_out, h_p, TM_W)
    h_s = _matres(o_s, wb_out, h_s, TM_W)
    h_s, *wf = _ffn_cast(h_s, row(ffn_norm[1, 1]), w_ffn_in, w_ffn_out, 1, 1)
    h_p = _ffn(h_p, row(ffn_norm[1, 1]), *wf, TM_P)

    kv4 = lambda x: x.reshape(x.shape[:-1] + (B_KV_HEADS, B_DH))
    meta_rows = kv_q[S_ROWS:S_ROWS + N_META]
    kv_p3 = kv_p.reshape(BATCH, SEQ, 2 * B_KV)
    st = lambda x, dt: x[None].astype(dt)
    return (
        h_p.reshape(BATCH, SEQ, D_MODEL),
        h_s.reshape(DEC_BATCH, DEC_SEQ, D_MODEL),
        st(c_p, state_C.dtype), st(n_p[:, :, 0, :], state_n.dtype), st(m_p[:, :, 0, 0], state_m.dtype),
        jnp.broadcast_to(kv4(meta_rows[:, :B_KV])[None], (BATCH, N_META, B_KV_HEADS, B_DH)),
        jnp.broadcast_to(kv4(meta_rows[:, B_KV:])[None], (BATCH, N_META, B_KV_HEADS, B_DH)),
        kv4(kv_p3[:, -WINDOW:, :B_KV]), kv4(kv_p3[:, -WINDOW:, B_KV:]),
        st(c_s, state_C.dtype), st(n_s[:, :, 0, :], state_n.dtype), st(m_s[:, :, 0, 0], state_m.dtype),
        k_win_s, v_win_s,
    )
```

```python
import functools

import jax
import jax.numpy as jnp
from jax import lax
from jax.experimental import pallas as pl
from jax.experimental.pallas import tpu as pltpu

D_MODEL = 2048
BATCH = 8
SEQ = 2048
DEC_BATCH = 128
DEC_SEQ = 4
PAST_LEN = 8192
N_META = 16
A_HEADS = 4
A_DV = D_MODEL // A_HEADS
A_DK = A_DV // 2
A_CHUNK = 128
A_GATE_CAP = 15.0
A_QKVO = 2 * A_HEADS * A_DK + 2 * A_HEADS * A_DV
B_HEADS = 32
B_DH = D_MODEL // B_HEADS
B_KV_HEADS = 4
B_GROUP = B_HEADS // B_KV_HEADS
B_KV = B_KV_HEADS * B_DH
WINDOW = 128
D_FF = ((8 * D_MODEL // 3 + 255) // 256) * 256
EPS = 1e-6

LANES = 128
SUBLANES = 8
VMEM_DEFAULT_MIB = 48
VMEM_FFN_MIB = 60

P_ROWS = BATCH * SEQ
S_ROWS = DEC_BATCH * DEC_SEQ
Q_ROWS = S_ROWS + A_CHUNK
META_BLOCK = S_ROWS // A_CHUNK

F32 = jnp.float32
BF16 = jnp.bfloat16
NEG_INF = float("-inf")


def _params(n_axes, vmem_mib=VMEM_DEFAULT_MIB):
    return pltpu.CompilerParams(dimension_semantics=("arbitrary",) * n_axes,
                                vmem_limit_bytes=vmem_mib * 1024 * 1024)


def _resident(shape):
    return pl.BlockSpec(shape, lambda i: (0, 0), pipeline_mode=pl.Buffered(1))


def _rms(x, g):
    return x * lax.rsqrt(jnp.mean(x * x, axis=-1, keepdims=True) + EPS) * g


def _dot(a, b):
    return jnp.dot(a, b, preferred_element_type=F32)


def _dot_nt(a, b):
    return lax.dot_general(a, b, (((1,), (1,)), ((), ())), preferred_element_type=F32)


def _dot_tn(a, b):
    return lax.dot_general(a, b, (((0,), (0,)), ((), ())), preferred_element_type=F32)


def _log_sigmoid(x):
    return -(jnp.maximum(-x, 0.0) + jnp.log1p(jnp.exp(-jnp.abs(x))))


def _head_norm64(y, gain):
    lo = lax.broadcasted_iota(jnp.int32, (1, LANES), 1) < B_DH
    cols = []
    for c in range(y.shape[1] // LANES):
        x = y[:, c * LANES:(c + 1) * LANES]
        xx = x * x
        s_lo = jnp.sum(jnp.where(lo, xx, 0.0), axis=-1, keepdims=True)
        s_hi = jnp.sum(jnp.where(lo, 0.0, xx), axis=-1, keepdims=True)
        scale = jnp.where(lo, lax.rsqrt(s_lo / B_DH + EPS), lax.rsqrt(s_hi / B_DH + EPS))
        cols.append(x * scale * gain[:, c * LANES:(c + 1) * LANES])
    return jnp.concatenate(cols, axis=1)


def _ffn_kernel(h_ref, g_ref, wg_ref, wu_ref, wo_ref, o_ref, *rest, emit_bf16):
    j = pl.program_id(1)
    if emit_bf16:
        wgb_ref, wub_ref, wob_ref, xn_ref = rest
        wg = wg_ref[...].astype(BF16)
        wu = wu_ref[...].astype(BF16)
        wo = wo_ref[...].astype(BF16)
        wgb_ref[...] = wg
        wub_ref[...] = wu
        wob_ref[...] = wo
    else:
        (xn_ref,) = rest
        wo = wo_ref[...]

    def half_ffn(xn):
        if emit_bf16:
            blocks = [(wg, wu)]
        else:
            blocks = [(wg_ref[b], wu_ref[b]) for b in range(wg_ref.shape[0])]
        acts = []
        for wg_b, wu_b in blocks:
            g = _dot(xn, wg_b)
            u = _dot(xn, wu_b)
            acts.append(((g / (1.0 + jnp.exp(-g))) * (0.5 * u)).astype(BF16))
        return _dot(jnp.concatenate(acts, axis=1), wo)

    @pl.when(j == 0)
    def _():
        h = h_ref[...]
        xn_ref[...] = _rms(h, g_ref[...]).astype(BF16)
        o_ref[...] = h

    o_ref[...] += half_ffn(xn_ref[...])


FFN_CAST_TF = 256


def _ffn(h, gain, wg, wu, wo, tm, tf=512):
    rows = h.shape[0]
    n_ff = D_FF // tf
    sub = tf // FFN_CAST_TF
    return pl.pallas_call(
        functools.partial(_ffn_kernel, emit_bf16=False),
        grid=(rows // tm, n_ff),
        in_specs=[
            pl.BlockSpec((tm, D_MODEL), lambda i, j: (i, 0)),
            pl.BlockSpec((1, D_MODEL), lambda i, j: (0, 0)),
            pl.BlockSpec((sub, D_MODEL, FFN_CAST_TF), lambda i, j: (j, 0, 0)),
            pl.BlockSpec((sub, D_MODEL, FFN_CAST_TF), lambda i, j: (j, 0, 0)),
            pl.BlockSpec((tf, D_MODEL), lambda i, j: (j, 0)),
        ],
        out_specs=pl.BlockSpec((tm, D_MODEL), lambda i, j: (i, 0)),
        out_shape=jax.ShapeDtypeStruct((rows, D_MODEL), F32),
        scratch_shapes=[pltpu.VMEM((tm, D_MODEL), BF16)],
        compiler_params=_params(2, VMEM_FFN_MIB),
        name="ffn",
    )(h, gain, wg, wu, wo)


def _ffn_cast(h, gain, w_in, w_out, layer, which):
    rows = h.shape[0]
    tf = FFN_CAST_TF
    n_ff = D_FF // tf
    return pl.pallas_call(
        functools.partial(_ffn_kernel, emit_bf16=True),
        grid=(1, n_ff),
        in_specs=[
            pl.BlockSpec((rows, D_MODEL), lambda i, j: (0, 0)),
            pl.BlockSpec((1, D_MODEL), lambda i, j: (0, 0)),
            pl.BlockSpec((None, None, D_MODEL, tf), lambda i, j: (layer, which, 0, j)),
            pl.BlockSpec((None, None, D_MODEL, tf), lambda i, j: (layer, which, 0, j + n_ff)),
            pl.BlockSpec((None, None, tf, D_MODEL), lambda i, j: (layer, which, j, 0)),
        ],
        out_specs=[
            pl.BlockSpec((rows, D_MODEL), lambda i, j: (0, 0)),
            pl.BlockSpec((None, D_MODEL, tf), lambda i, j: (j, 0, 0)),
            pl.BlockSpec((None, D_MODEL, tf), lambda i, j: (j, 0, 0)),
            pl.BlockSpec((tf, D_MODEL), lambda i, j: (j, 0)),
        ],
        out_shape=[
            jax.ShapeDtypeStruct((rows, D_MODEL), F32),
            jax.ShapeDtypeStruct((n_ff, D_MODEL, tf), BF16),
            jax.ShapeDtypeStruct((n_ff, D_MODEL, tf), BF16),
            jax.ShapeDtypeStruct((D_FF, D_MODEL), BF16),
        ],
        scratch_shapes=[pltpu.VMEM((rows, D_MODEL), BF16)],
        compiler_params=_params(2),
        name="ffn_cast",
    )(h, gain, w_in, w_in, w_out)


GATE_ROWS = 2 * A_HEADS


def _inproj_kernel(h_ref, g_ref, w_ref, wgate_ref, bgate_ref, p_ref, gates_ref, *rest, emit_bf16):
    j = pl.program_id(1)
    if emit_bf16:
        wb_ref, xn_ref = rest
        w = w_ref[...].astype(BF16)
        wb_ref[...] = w
    else:
        (xn_ref,) = rest
        w = w_ref[...]

    @pl.when(j == 0)
    def _():
        xn = _rms(h_ref[...], g_ref[...]).astype(BF16)
        xn_ref[...] = xn
        wgate = jnp.concatenate([wgate_ref[...], jnp.zeros((LANES - GATE_ROWS, D_MODEL), F32)], axis=0)
        pre = _dot_nt(xn, wgate.astype(BF16)) + bgate_ref[...]
        capped = A_GATE_CAP * jnp.tanh(pre / A_GATE_CAP)
        lane = lax.broadcasted_iota(jnp.int32, (1, LANES), 1)
        gates_ref[...] = jnp.where(lane < A_HEADS, capped, _log_sigmoid(capped))

    p_ref[...] = _dot_nt(xn_ref[...], w)


def _inproj(h, gain, w_t, w_gate_t, bgate, tm, emit_bf16=False, tn=1024):
    rows = h.shape[0]
    assert not emit_bf16 or rows == tm
    out_specs = [
        pl.BlockSpec((tm, tn), lambda i, j: (i, j)),
        pl.BlockSpec((tm, LANES), lambda i, j: (i, 0)),
    ]
    out_shape = [jax.ShapeDtypeStruct((rows, A_QKVO), F32),
                 jax.ShapeDtypeStruct((rows, LANES), F32)]
    if emit_bf16:
        out_specs.append(pl.BlockSpec((tn, D_MODEL), lambda i, j: (j, 0)))
        out_shape.append(jax.ShapeDtypeStruct((A_QKVO, D_MODEL), BF16))
    return pl.pallas_call(
        functools.partial(_inproj_kernel, emit_bf16=emit_bf16),
        grid=(rows // tm, A_QKVO // tn),
        in_specs=[
            pl.BlockSpec((tm, D_MODEL), lambda i, j: (i, 0)),
            pl.BlockSpec((1, D_MODEL), lambda i, j: (0, 0)),
            pl.BlockSpec((tn, D_MODEL), lambda i, j: (j, 0)),
            pl.BlockSpec((GATE_ROWS, D_MODEL), lambda i, j: (A_QKVO // GATE_ROWS, 0)),
            pl.BlockSpec((1, LANES), lambda i, j: (0, 0)),
        ],
        out_specs=out_specs,
        out_shape=out_shape,
        scratch_shapes=[pltpu.VMEM((tm, D_MODEL), BF16)],
        compiler_params=_params(2),
        name="mlstm_inproj",
    )(h, gain, w_t, w_gate_t, bgate)


def _mlstm_chunk_kernel(q_ref, k_ref, v_ref, g_ref, c0_ref, n0_ref, m0_ref,
                        h_ref, c_ref, n_ref, m_ref, qk_ref, qc_ref, sb_ref, *, n_valid):
    L = A_CHUNK
    heads = range(A_HEADS)

    @pl.when(pl.program_id(1) == 0)
    def _():
        c_ref[...] = c0_ref[...]
        n_ref[...] = n0_ref[...]
        m_ref[...] = m0_ref[...]

    for hd in heads:
        qb = q_ref[:, hd * A_DK:(hd + 1) * A_DK].astype(BF16)
        kb = (k_ref[:, hd * A_DK:(hd + 1) * A_DK] * (A_DK ** -0.5)).astype(BF16)
        qk_ref[hd] = _dot_nt(qb, kb)
        qc_ref[hd] = _dot(qb, c_ref[0, hd].astype(BF16))

    gates = g_ref[...]
    row = lax.broadcasted_iota(jnp.int32, (L, L), 0)
    col = lax.broadcasted_iota(jnp.int32, (L, L), 1)
    causal = col <= row
    eye = col == row
    masked = n_valid < L
    if masked:
        row_ok = lax.broadcasted_iota(jnp.int32, (L, 1), 0) < n_valid
        gates_lf = jnp.where(row_ok, gates, 0.0)
    else:
        gates_lf = gates
    csum = jnp.dot(causal.astype(F32), gates_lf, precision=lax.Precision.HIGHEST,
                   preferred_element_type=F32)

    def to_row(x_col):
        return jnp.sum(jnp.where(eye, x_col, 0.0), axis=0, keepdims=True)

    w_intra, w_inter, floor, w_state, decay, m_new = [], [], [], [], [], []
    for hd in heads:
        b_col = csum[:, A_HEADS + hd:A_HEADS + hd + 1]
        ig_col = gates[:, hd:hd + 1]
        if masked:
            ig_col = jnp.where(row_ok, ig_col, NEG_INF)
        b_row = to_row(b_col)
        ig_row = to_row(ig_col)
        m_prev = m_ref[0, hd][:, 0:1]
        d_log = jnp.where(causal, b_col - b_row + ig_row, NEG_INF)
        inter_log = b_col + m_prev
        m_t = jnp.maximum(inter_log, jnp.max(d_log, axis=-1, keepdims=True))
        w_intra.append(jnp.exp(d_log - m_t))
        w_inter.append(jnp.exp(inter_log - m_t))
        floor.append(jnp.exp(-m_t))
        b_last = b_col[L - 1:L, :]
        w_log = b_last - b_col + ig_col
        m_new.append(jnp.maximum(b_last + m_prev, jnp.max(w_log, axis=0, keepdims=True)))
        w_state.append(jnp.exp(w_log - m_new[hd]))
        decay.append(jnp.exp(b_last + m_prev - m_new[hd]))

    den = []
    for hd in heads:
        s = qk_ref[hd] * w_intra[hd]
        sb_ref[hd] = s.astype(BF16)
        q = q_ref[:, hd * A_DK:(hd + 1) * A_DK]
        d = jnp.sum(s, axis=-1, keepdims=True) + w_inter[hd] * jnp.sum(q * n_ref[0, hd], axis=-1, keepdims=True)
        den.append(jnp.maximum(jnp.abs(d), floor[hd]))

    for hd in heads:
        k = k_ref[:, hd * A_DK:(hd + 1) * A_DK] * (A_DK ** -0.5)
        vb = v_ref[:, hd * A_DV:(hd + 1) * A_DV].astype(BF16)
        num = _dot(sb_ref[hd], vb) + w_inter[hd] * qc_ref[hd]
        h_ref[:, hd * A_DV:(hd + 1) * A_DV] = num / den[hd]
        kw = k * w_state[hd]
        c_ref[0, hd] = decay[hd] * c_ref[0, hd] + _dot_tn(kw.astype(BF16), vb)
        n_ref[0, hd] = decay[hd] * n_ref[0, hd] + jnp.sum(kw, axis=0, keepdims=True)
        m_ref[0, hd] = jnp.broadcast_to(m_new[hd], (1, LANES))


def _mlstm_chunks(p, gates, c0, n0, m0, n_seq, n_chunks, row_block0, n_valid, shared_state):
    L = A_CHUNK
    rb = lambda b, c: row_block0 + b * n_chunks + c
    st = (lambda b, c: (0, 0, 0, 0)) if shared_state else (lambda b, c: (b, 0, 0, 0))
    return pl.pallas_call(
        functools.partial(_mlstm_chunk_kernel, n_valid=n_valid),
        grid=(n_seq, n_chunks),
        in_specs=[
            pl.BlockSpec((L, A_HEADS * A_DK), lambda b, c: (rb(b, c), 0)),
            pl.BlockSpec((L, A_HEADS * A_DK), lambda b, c: (rb(b, c), 1)),
            pl.BlockSpec((L, A_HEADS * A_DV), lambda b, c: (rb(b, c), 1)),
            pl.BlockSpec((L, LANES), lambda b, c: (rb(b, c), 0)),
            pl.BlockSpec((1, A_HEADS, A_DK, A_DV), st),
            pl.BlockSpec((1, A_HEADS, 1, A_DK), st),
            pl.BlockSpec((1, A_HEADS, 1, LANES), st),
        ],
        out_specs=[
            pl.BlockSpec((L, A_HEADS * A_DV), lambda b, c: (b * n_chunks + c, 0)),
            pl.BlockSpec((1, A_HEADS, A_DK, A_DV), lambda b, c: (b, 0, 0, 0)),
            pl.BlockSpec((1, A_HEADS, 1, A_DK), lambda b, c: (b, 0, 0, 0)),
            pl.BlockSpec((1, A_HEADS, 1, LANES), lambda b, c: (b, 0, 0, 0)),
        ],
        out_shape=[
            jax.ShapeDtypeStruct((n_seq * n_chunks * L, D_MODEL), F32),
            jax.ShapeDtypeStruct((n_seq, A_HEADS, A_DK, A_DV), F32),
            jax.ShapeDtypeStruct((n_seq, A_HEADS, 1, A_DK), F32),
            jax.ShapeDtypeStruct((n_seq, A_HEADS, 1, LANES), F32),
        ],
        scratch_shapes=[pltpu.VMEM((A_HEADS, L, L), F32),
                        pltpu.VMEM((A_HEADS, L, A_DV), F32),
                        pltpu.VMEM((A_HEADS, L, L), BF16)],
        compiler_params=_params(2),
        name="mlstm_chunks",
    )(p, p, p, gates, c0, n0, m0)


def _mlstm_sample_kernel(q_ref, k_ref, v_ref, g_ref, c0_ref, n0_ref, m0_ref,
                         h_ref, c_ref, n_ref, m_ref):
    R = 2 * DEC_SEQ
    PAD = A_CHUNK - R
    gates = g_ref[...]
    r_col = lax.broadcasted_iota(jnp.int32, (R, 1), 0)
    is_a = r_col < DEC_SEQ
    row = lax.broadcasted_iota(jnp.int32, (R, LANES), 0)
    lane = lax.broadcasted_iota(jnp.int32, (R, LANES), 1)
    same = ((lane < DEC_SEQ) & (row < DEC_SEQ)) | ((lane >= DEC_SEQ) & (lane < R) & (row >= DEC_SEQ))
    causal = same & (lane <= row)
    eye = lane == row

    def to_row(x_col):
        return jnp.sum(jnp.where(eye, x_col, 0.0), axis=0, keepdims=True)

    for hd in range(A_HEADS):
        lf_col = gates[:, A_HEADS + hd:A_HEADS + hd + 1]
        ig_col = gates[:, hd:hd + 1]
        lf_row = to_row(lf_col)
        ig_row = to_row(ig_col)
        b_col = jnp.sum(jnp.where(causal, lf_row, 0.0), axis=1, keepdims=True)
        b_row = to_row(b_col)
        m_a = m0_ref[0, hd][:, 0:1]
        m_b = m0_ref[1, hd][:, 0:1]
        m_prev = jnp.where(is_a, m_a, m_b)
        c_a = c0_ref[0, hd]
        c_b = c0_ref[1, hd]
        n_a = n0_ref[0, hd]
        n_b = n0_ref[1, hd]

        q = q_ref[:, hd * A_DK:(hd + 1) * A_DK]
        k = k_ref[:, hd * A_DK:(hd + 1) * A_DK] * (A_DK ** -0.5)
        v = v_ref[:, hd * A_DV:(hd + 1) * A_DV]
        qb = q.astype(BF16)
        k_pad = jnp.concatenate([k, jnp.zeros((PAD, A_DK), F32)], axis=0).astype(BF16)
        v_pad = jnp.concatenate([v, jnp.zeros((PAD, A_DV), F32)], axis=0).astype(BF16)

        d_log = jnp.where(causal, b_col - b_row + ig_row, NEG_INF)
        inter_log = b_col + m_prev
        m_t = jnp.maximum(inter_log, jnp.max(d_log, axis=-1, keepdims=True))
        w_intra = jnp.exp(d_log - m_t)
        w_inter = jnp.exp(inter_log - m_t)
        s = _dot_nt(qb, k_pad) * w_intra
        q_c = jnp.where(is_a, _dot(qb, c_a.astype(BF16)), _dot(qb, c_b.astype(BF16)))
        num = _dot(s.astype(BF16), v_pad) + w_inter * q_c
        q_n = jnp.sum(q * jnp.where(is_a, n_a, n_b), axis=-1, keepdims=True)
        den = jnp.sum(s, axis=-1, keepdims=True) + w_inter * q_n
        den = jnp.maximum(jnp.abs(den), jnp.exp(-m_t))
        h_ref[:, hd * A_DV:(hd + 1) * A_DV] = num / den

        for idx, sel, m_x, c_x, n_x in ((0, is_a, m_a, c_a, n_a),
                                        (1, jnp.logical_not(is_a), m_b, c_b, n_b)):
            last = (idx + 1) * DEC_SEQ - 1
            b_last = b_col[last:last + 1, :]
            w_log = jnp.where(sel, b_last - b_col + ig_col, NEG_INF)
            m_new = jnp.maximum(b_last + m_x, jnp.max(w_log, axis=0, keepdims=True))
            w_state = jnp.exp(w_log - m_new)
            decay = jnp.exp(b_last + m_x - m_new)
            kw = k * w_state
            kw_pad = jnp.concatenate([kw, jnp.zeros((PAD, A_DK), F32)], axis=0).astype(BF16)
            c_ref[idx, hd] = decay * c_x + _dot_tn(kw_pad, v_pad)
            n_ref[idx, hd] = decay * n_x + jnp.sum(kw, axis=0, keepdims=True)
            m_ref[idx, hd] = jnp.broadcast_to(m_new, (1, LANES))


def _mlstm_sample(p, gates, c0, n0, m0):
    R = 2 * DEC_SEQ
    n_pairs = DEC_BATCH // 2
    st = lambda i: (i, 0, 0, 0)
    return pl.pallas_call(
        _mlstm_sample_kernel,
        grid=(n_pairs,),
        in_specs=[
            pl.BlockSpec((R, A_HEADS * A_DK), lambda i: (i, 0)),
            pl.BlockSpec((R, A_HEADS * A_DK), lambda i: (i, 1)),
            pl.BlockSpec((R, A_HEADS * A_DV), lambda i: (i, 1)),
            pl.BlockSpec((R, LANES), lambda i: (i, 0)),
            pl.BlockSpec((2, A_HEADS, A_DK, A_DV), st),
            pl.BlockSpec((2, A_HEADS, 1, A_DK), st),
            pl.BlockSpec((2, A_HEADS, 1, LANES), st),
        ],
        out_specs=[
            pl.BlockSpec((R, A_HEADS * A_DV), lambda i: (i, 0)),
            pl.BlockSpec((2, A_HEADS, A_DK, A_DV), st),
            pl.BlockSpec((2, A_HEADS, 1, A_DK), st),
            pl.BlockSpec((2, A_HEADS, 1, LANES), st),
        ],
        out_shape=[
            jax.ShapeDtypeStruct((S_ROWS, D_MODEL), F32),
            jax.ShapeDtypeStruct((DEC_BATCH, A_HEADS, A_DK, A_DV), F32),
            jax.ShapeDtypeStruct((DEC_BATCH, A_HEADS, 1, A_DK), F32),
            jax.ShapeDtypeStruct((DEC_BATCH, A_HEADS, 1, LANES), F32),
        ],
        compiler_params=_params(1),
        name="mlstm_sample",
    )(p, p, p, gates, c0, n0, m0)


def _mlstm_out_kernel(hm_ref, o_ref, hg_ref, w_ref, res_ref, out_ref):
    cols = []
    for hd in range(A_HEADS):
        x = hm_ref[:, hd * A_DV:(hd + 1) * A_DV]
        cols.append(x * lax.rsqrt(jnp.mean(x * x, axis=-1, keepdims=True) + EPS))
    hn = jnp.concatenate(cols, axis=1) * hg_ref[...]
    o = o_ref[...]
    pre = (hn * (1.0 / (1.0 + jnp.exp(-o)))).astype(BF16)
    out_ref[...] = res_ref[...] + _dot(pre, w_ref[...])


def _mlstm_out(hm, p, head_gain, w, res, tm):
    rows = hm.shape[0]
    return pl.pallas_call(
        _mlstm_out_kernel,
        grid=(rows // tm,),
        in_specs=[
            pl.BlockSpec((tm, D_MODEL), lambda i: (i, 0)),
            pl.BlockSpec((tm, D_MODEL), lambda i: (i, 2)),
            pl.BlockSpec((1, D_MODEL), lambda i: (0, 0)),
            _resident((D_MODEL, D_MODEL)),
            pl.BlockSpec((tm, D_MODEL), lambda i: (i, 0)),
        ],
        out_specs=pl.BlockSpec((tm, D_MODEL), lambda i: (i, 0)),
        out_shape=jax.ShapeDtypeStruct((rows, D_MODEL), F32),
        compiler_params=_params(1),
        name="mlstm_out",
    )(hm, p, head_gain, w, res)


def _normproj_kernel(h_ref, g_ref, w_ref, hg_ref, o_ref, *, n_norm):
    xn = _rms(h_ref[...], g_ref[...]).astype(BF16)
    y = _dot(xn, w_ref[...])
    if n_norm == y.shape[1]:
        o_ref[...] = _head_norm64(y, hg_ref[...])
    else:
        o_ref[:, :n_norm] = _head_norm64(y[:, :n_norm], hg_ref[...])
        o_ref[:, n_norm:] = y[:, n_norm:]


def _normproj(h, gain, w, head_gain, tm):
    rows = h.shape[0]
    n = w.shape[1]
    n_norm = head_gain.shape[1]
    return pl.pallas_call(
        functools.partial(_normproj_kernel, n_norm=n_norm),
        grid=(rows // tm,),
        in_specs=[
            pl.BlockSpec((tm, D_MODEL), lambda i: (i, 0)),
            pl.BlockSpec((1, D_MODEL), lambda i: (0, 0)),
            _resident((D_MODEL, n)),
            pl.BlockSpec((1, n_norm), lambda i: (0, 0)),
        ],
        out_specs=pl.BlockSpec((tm, n), lambda i: (i, 0)),
        out_shape=jax.ShapeDtypeStruct((rows, n), F32),
        compiler_params=_params(1),
        name="normproj",
    )(h, gain, w, head_gain)


def _matres_kernel(x_ref, w_ref, res_ref, o_ref):
    o_ref[...] = res_ref[...] + _dot(x_ref[...].astype(BF16), w_ref[...])


def _matres(x, w, res, tm):
    rows = x.shape[0]
    return pl.pallas_call(
        _matres_kernel,
        grid=(rows // tm,),
        in_specs=[
            pl.BlockSpec((tm, D_MODEL), lambda i: (i, 0)),
            _resident((D_MODEL, D_MODEL)),
            pl.BlockSpec((tm, D_MODEL), lambda i: (i, 0)),
        ],
        out_specs=pl.BlockSpec((tm, D_MODEL), lambda i: (i, 0)),
        out_shape=jax.ShapeDtypeStruct((rows, D_MODEL), F32),
        compiler_params=_params(1),
        name="matres",
    )(x, w, res)


def _alibi_slope(head):
    return 2.0 ** (-8.0 * (head + 1) / B_HEADS)


PROMPT_SUB = WINDOW // 2
PROMPT_BAND = WINDOW + PROMPT_SUB
PROMPT_KEYS = 2 * LANES
HEAD_PAIRS = B_HEADS // 2
PAIRS_PER_GROUP = B_GROUP // 2


def _pair_rhs(x, c):
    col = x[:, c * LANES:(c + 1) * LANES]
    rol = pltpu.roll(col, B_DH, axis=1)
    lo = lax.broadcasted_iota(jnp.int32, (1, LANES), 1) < B_DH
    even = jnp.concatenate([jnp.where(lo, col, 0.0), jnp.where(lo, 0.0, rol)], axis=0)
    odd = jnp.concatenate([jnp.where(lo, rol, 0.0), jnp.where(lo, 0.0, col)], axis=0)
    return even.astype(BF16), odd.astype(BF16)


def _attn_prompt_kernel(q_ref, kvo_ref, kvp_ref, kvm_ref, sink_ref, o_ref, bias_ref, s_ref, p_ref):
    W = WINDOW
    SB = PROMPT_SUB
    NK = PROMPT_KEYS
    first = (pl.program_id(0) == 0) & (pl.program_id(1) == 0)
    j = pl.program_id(1)
    rows = PAIRS_PER_GROUP * SB
    lo = lax.broadcasted_iota(jnp.int32, (1, LANES), 1) < B_DH

    @pl.when(first)
    def _():
        c = lax.broadcasted_iota(jnp.int32, (SB, NK), 1)
        is_meta = c < N_META
        is_sink = c == NK - 1
        for sub in range(2):
            i = lax.broadcasted_iota(jnp.int32, (SB, NK), 0) + sub * SB
            pos = c - N_META + sub * SB
            rel = W + i - pos
            in_band = (c >= N_META) & (c < N_META + PROMPT_BAND) & (rel >= 0) & (rel < W)
            for variant in range(2):
                if variant == 0:
                    dist = jnp.where(is_meta, jnp.minimum(i + N_META - c, W), rel).astype(F32)
                    valid = is_meta | (in_band & (pos >= W))
                else:
                    dist = jnp.where(is_meta, W, rel).astype(F32)
                    valid = is_meta | in_band
                for hd in range(B_HEADS):
                    r0 = (hd // 2) * SB
                    c0 = (hd % 2) * NK
                    table = jnp.where(valid, -_alibi_slope(hd) * dist, NEG_INF)
                    bias_ref[variant, sub, r0:r0 + SB, c0:c0 + NK] = jnp.where(
                        is_sink, sink_ref[0:1, hd:hd + 1], table)

    variant = jnp.minimum(j, 1)
    kvm = kvm_ref[...]
    band = jnp.concatenate([kvp_ref[...], kvo_ref[...]], axis=0)
    pad = jnp.zeros((NK - N_META - PROMPT_BAND, 2 * B_KV), F32)
    ones_rhs = jnp.concatenate([jnp.broadcast_to(jnp.where(lo, 1.0, 0.0), (NK, LANES)),
                                jnp.broadcast_to(jnp.where(lo, 0.0, 1.0), (NK, LANES))], axis=0).astype(BF16)

    sub_rows = B_KV_HEADS * rows
    for sub in range(2):
        keys = jnp.concatenate([kvm, band[sub * SB:sub * SB + PROMPT_BAND], pad], axis=0)
        k_rhs = _pair_rhs(keys, 0) + _pair_rhs(keys, 1)
        v_rhs = _pair_rhs(keys, 2) + _pair_rhs(keys, 3)
        for g in range(B_KV_HEADS):
            p0 = g * PAIRS_PER_GROUP
            r0 = (sub * B_KV_HEADS + g) * rows
            qg = jnp.concatenate([q_ref[sub * SB:(sub + 1) * SB, (p0 + m) * LANES:(p0 + m + 1) * LANES]
                                  for m in range(PAIRS_PER_GROUP)], axis=0) * (B_DH ** -0.5)
            s_ref[r0:r0 + rows, :] = _dot_nt(qg.astype(BF16), k_rhs[g])

        s = s_ref[sub * sub_rows:(sub + 1) * sub_rows, :] + bias_ref[variant, sub]
        halves = []
        for half in range(2):
            sh = s[:, half * NK:(half + 1) * NK]
            halves.append(jnp.exp(sh - jnp.max(sh, axis=-1, keepdims=True)).astype(BF16))
        p_ref[sub * sub_rows:(sub + 1) * sub_rows, :] = jnp.concatenate(halves, axis=1)

        for g in range(B_KV_HEADS):
            p0 = g * PAIRS_PER_GROUP
            r0 = (sub * B_KV_HEADS + g) * rows
            od = _dot(p_ref[r0:r0 + rows, :], jnp.concatenate([v_rhs[g], ones_rhs], axis=1))
            o = od[:, :LANES] / od[:, LANES:]
            for m in range(PAIRS_PER_GROUP):
                o_ref[sub * SB:(sub + 1) * SB, (p0 + m) * LANES:(p0 + m + 1) * LANES] = o[m * SB:(m + 1) * SB, :]


def _attn_prompt(q, kv_p, kv_q, sinks_row):
    nb = SEQ // WINDOW
    blk = lambda b, j: b * nb + j
    return pl.pallas_call(
        _attn_prompt_kernel,
        grid=(BATCH, nb),
        in_specs=[
            pl.BlockSpec((WINDOW, D_MODEL), lambda b, j: (blk(b, j), 0)),
            pl.BlockSpec((WINDOW, 2 * B_KV), lambda b, j: (blk(b, j), 0)),
            pl.BlockSpec((WINDOW, 2 * B_KV), lambda b, j: (blk(b, jnp.maximum(j - 1, 0)), 0)),
            pl.BlockSpec((N_META, 2 * B_KV), lambda b, j: (S_ROWS // N_META, 0)),
            pl.BlockSpec((1, LANES), lambda b, j: (0, 0)),
        ],
        out_specs=pl.BlockSpec((WINDOW, D_MODEL), lambda b, j: (blk(b, j), 0)),
        out_shape=jax.ShapeDtypeStruct((P_ROWS, D_MODEL), F32),
        scratch_shapes=[pltpu.VMEM((2, 2, HEAD_PAIRS * PROMPT_SUB, 2 * PROMPT_KEYS), F32),
                        pltpu.VMEM((2 * HEAD_PAIRS * PROMPT_SUB, 2 * PROMPT_KEYS), F32),
                        pltpu.VMEM((2 * HEAD_PAIRS * PROMPT_SUB, 2 * PROMPT_KEYS), BF16)],
        compiler_params=_params(2),
        name="attn_prompt",
    )(q, kv_p, kv_p, kv_q, sinks_row)


GROUP_SHIFT = B_GROUP.bit_length() - 1
assert 1 << GROUP_SHIFT == B_GROUP
SAMPLE_OLD = SUBLANES
SAMPLE_KEYS = 2 * LANES
SAMPLE_SEQ_PER_STEP = 8
SAMPLE_ROWS = DEC_SEQ * B_GROUP
KV_PAIRS = B_KV_HEADS // 2


def _split_rhs(col):
    lo = lax.broadcasted_iota(jnp.int32, (1, LANES), 1) < B_DH
    return jnp.concatenate([jnp.where(lo, col, 0.0), jnp.where(lo, 0.0, col)], axis=0).astype(BF16)


def _attn_sample_kernel(q_ref, km_ref, vm_ref, ko_ref, vo_ref, kw_ref, vw_ref, sink_ref, o_ref,
                        bias_ref, s_ref, p_ref):
    W = WINDOW
    NK = SAMPLE_KEYS
    R = SAMPLE_ROWS
    off_w = N_META + SAMPLE_OLD
    lo = lax.broadcasted_iota(jnp.int32, (1, LANES), 1) < B_DH

    @pl.when(pl.program_id(0) == 0)
    def _():
        row = lax.broadcasted_iota(jnp.int32, (R, 1), 0)
        r_in_group = jnp.bitwise_and(row, B_GROUP - 1)
        t = jnp.right_shift(lax.broadcasted_iota(jnp.int32, (R, NK), 0), GROUP_SHIFT)
        c = lax.broadcasted_iota(jnp.int32, (R, NK), 1)
        is_meta = c < N_META
        is_old = (c >= N_META) & (c < off_w)
        is_win = (c >= off_w) & (c < off_w + W)
        j_old = c - N_META
        d_win = (W - DEC_SEQ) + t - (c - off_w)
        dist = jnp.where(is_meta, W, jnp.where(is_old, W + t - j_old, d_win)).astype(F32)
        valid = is_meta | (is_old & (j_old > t) & (j_old < DEC_SEQ)) | (is_win & (d_win >= 0))
        for pair in range(KV_PAIRS):
            for e in range(2):
                slope = jnp.zeros((R, 1), F32)
                sink = jnp.zeros((R, 1), F32)
                for r in range(B_GROUP):
                    hd = (2 * pair + e) * B_GROUP + r
                    slope = jnp.where(r_in_group == r, _alibi_slope(hd), slope)
                    sink = jnp.where(r_in_group == r, sink_ref[0:1, hd:hd + 1], sink)
                table = jnp.where(c == NK - 1, sink, jnp.where(valid, -slope * dist, NEG_INF))
                for sq in range(SAMPLE_SEQ_PER_STEP):
                    r0 = (sq * KV_PAIRS + pair) * R
                    bias_ref[r0:r0 + R, e * NK:(e + 1) * NK] = table

    pad = jnp.zeros((NK - off_w - W, B_KV), F32)
    ones_rhs = jnp.concatenate([jnp.broadcast_to(jnp.where(lo, 1.0, 0.0), (NK, LANES)),
                                jnp.broadcast_to(jnp.where(lo, 0.0, 1.0), (NK, LANES))], axis=0).astype(BF16)
    v_rhs = []
    for sq in range(SAMPLE_SEQ_PER_STEP):
        kcat = jnp.concatenate([km_ref[sq], ko_ref[sq], kw_ref[sq], pad], axis=0)
        vcat = jnp.concatenate([vm_ref[sq], vo_ref[sq], vw_ref[sq], pad], axis=0)
        for pair in range(KV_PAIRS):
            r0 = (sq * KV_PAIRS + pair) * R
            k_rhs = _split_rhs(kcat[:, pair * LANES:(pair + 1) * LANES])
            v_rhs.append(_split_rhs(vcat[:, pair * LANES:(pair + 1) * LANES]))
            q = (q_ref[sq, pair] * (B_DH ** -0.5)).astype(BF16)
            s_ref[r0:r0 + R, :] = _dot_nt(q, k_rhs)

    s = s_ref[...] + bias_ref[...]
    halves = []
    for e in range(2):
        sh = s[:, e * NK:(e + 1) * NK]
        halves.append(jnp.exp(sh - jnp.max(sh, axis=-1, keepdims=True)).astype(BF16))
    p_ref[...] = jnp.concatenate(halves, axis=1)

    for sq in range(SAMPLE_SEQ_PER_STEP):
        for pair in range(KV_PAIRS):
            b = sq * KV_PAIRS + pair
            od = _dot(p_ref[b * R:(b + 1) * R, :], jnp.concatenate([v_rhs[b], ones_rhs], axis=1))
            o_ref[sq, pair] = od[:, :LANES] / od[:, LANES:]


def _attn_sample(q4, k_meta, v_meta, k_old, v_old, k_win, v_win, sinks_row):
    nb = SAMPLE_SEQ_PER_STEP
    R = SAMPLE_ROWS
    n_rows = nb * KV_PAIRS * R
    seq3 = lambda rows: pl.BlockSpec((nb, rows, B_KV), lambda i: (i, 0, 0))
    qspec = pl.BlockSpec((nb, KV_PAIRS, R, LANES), lambda i: (i, 0, 0, 0))
    return pl.pallas_call(
        _attn_sample_kernel,
        grid=(DEC_BATCH // nb,),
        in_specs=[qspec, seq3(N_META), seq3(N_META), seq3(SAMPLE_OLD), seq3(SAMPLE_OLD),
                  seq3(WINDOW), seq3(WINDOW), pl.BlockSpec((1, LANES), lambda i: (0, 0))],
        out_specs=qspec,
        out_shape=jax.ShapeDtypeStruct((DEC_BATCH, KV_PAIRS, R, LANES), F32),
        scratch_shapes=[pltpu.VMEM((n_rows, 2 * SAMPLE_KEYS), F32),
                        pltpu.VMEM((n_rows, 2 * SAMPLE_KEYS), F32),
                        pltpu.VMEM((n_rows, 2 * SAMPLE_KEYS), BF16)],
        compiler_params=_params(1),
        name="attn_sample",
    )(q4, k_meta, v_meta, k_old, v_old, k_win, v_win, sinks_row)


def _ffn_kernel_v2(h_ref, g_ref, wg_ref, wu_ref, wo_ref, o_ref, xn_ref, *, n_ff):
    j = pl.program_id(1)

    @pl.when(j == 0)
    def _():
        xn_ref[...] = _rms(h_ref[...], g_ref[...]).astype(BF16)
        o_ref[...] = jnp.zeros_like(o_ref)

    xn = xn_ref[...]
    g = _dot(xn, wg_ref[...])
    u = _dot(xn, wu_ref[...])
    a = (g / (1.0 + jnp.exp(-g))) * u
    o_ref[...] += _dot(a.astype(BF16), wo_ref[...])

    @pl.when(j == n_ff - 1)
    def _():
        o_ref[...] = h_ref[...] + 0.5 * o_ref[...]


def _ffn_v2(h, gain, w_in, w_out, layer, which, tm, tf=512):
    rows = h.shape[0]
    n_ff = D_FF // tf
    return pl.pallas_call(
        functools.partial(_ffn_kernel_v2, n_ff=n_ff),
        grid=(rows // tm, n_ff),
        in_specs=[
            pl.BlockSpec((tm, D_MODEL), lambda i, j: (i, 0)),
            pl.BlockSpec((1, D_MODEL), lambda i, j: (0, 0)),
            pl.BlockSpec((None, None, D_MODEL, tf), lambda i, j: (layer, which, 0, j)),
            pl.BlockSpec((None, None, D_MODEL, tf), lambda i, j: (layer, which, 0, j + n_ff)),
            pl.BlockSpec((None, None, tf, D_MODEL), lambda i, j: (layer, which, j, 0)),
        ],
        out_specs=pl.BlockSpec((tm, D_MODEL), lambda i, j: (i, 0)),
        out_shape=jax.ShapeDtypeStruct((rows, D_MODEL), F32),
        scratch_shapes=[pltpu.VMEM((tm, D_MODEL), BF16)],
        compiler_params=_params(2, 56),
        name="ffn_v2",
    )(h, gain, w_in, w_in, w_out)


def _ffn_kernel_flat(h_ref, g_ref, wg_ref, wu_ref, wo_ref, o_ref, xn_ref):
    j = pl.program_id(1)

    @pl.when(j == 0)
    def _():
        h = h_ref[...]
        xn_ref[...] = _rms(h, g_ref[...]).astype(BF16)
        o_ref[...] = h

    xn = xn_ref[...]
    g = _dot(xn, wg_ref[...])
    u = _dot(xn, wu_ref[...])
    a = (g / (1.0 + jnp.exp(-g))) * (0.5 * u)
    o_ref[...] += _dot(a.astype(BF16), wo_ref[...])


def _ffn_flat(h, gain, wg3, wu3, wo, tm, tf, vmem, name):
    rows = h.shape[0]
    n_ff = D_FF // tf
    wg = jnp.swapaxes(wg3, 0, 1).reshape(D_MODEL, D_FF)
    wu = jnp.swapaxes(wu3, 0, 1).reshape(D_MODEL, D_FF)
    return pl.pallas_call(
        _ffn_kernel_flat,
        grid=(rows // tm, n_ff),
        in_specs=[
            pl.BlockSpec((tm, D_MODEL), lambda i, j: (i, 0)),
            pl.BlockSpec((1, D_MODEL), lambda i, j: (0, 0)),
            pl.BlockSpec((D_MODEL, tf), lambda i, j: (0, j)),
            pl.BlockSpec((D_MODEL, tf), lambda i, j: (0, j)),
            pl.BlockSpec((tf, D_MODEL), lambda i, j: (j, 0)),
        ],
        out_specs=pl.BlockSpec((tm, D_MODEL), lambda i, j: (i, 0)),
        out_shape=jax.ShapeDtypeStruct((rows, D_MODEL), F32),
        scratch_shapes=[pltpu.VMEM((tm, D_MODEL), BF16)],
        compiler_params=_params(2, vmem),
        name=name,
    )(h, gain, wg, wu, wo)


def kernel(x_prompt, x_sample, state_C, state_n, state_m, cache_k_meta, cache_v_meta, cache_k_win, cache_v_win, meta_tokens, ffn_norm, w_ffn_in, w_ffn_out, mix_norm, w_a_in, b_a_gate, a_head_norm, w_a_out, kv_norm, w_kv, k_norm, w_q, q_norm, sinks, w_b_out):
    assert x_prompt.shape == (BATCH, SEQ, D_MODEL) and x_sample.shape == (DEC_BATCH, DEC_SEQ, D_MODEL)
    assert w_a_in.shape[0] == 1 and w_q.shape[0] == 1 and ffn_norm.shape[0] == 2

    wa_in_t = jnp.swapaxes(w_a_in[0], 0, 1)
    ba_gate = jnp.pad(b_a_gate[0].astype(F32), (0, LANES - 2 * A_HEADS)).reshape(1, LANES)
    wa_out = w_a_out[0].astype(BF16)
    wkv = w_kv.astype(BF16)
    wq = w_q[0].astype(BF16)
    wb_out = w_b_out[0].astype(BF16)
    row = lambda x: x.astype(F32).reshape(1, -1)
    k_gain = jnp.tile(row(k_norm), (1, B_KV_HEADS))
    q_gain = jnp.tile(row(q_norm[0]), (1, B_HEADS))
    sinks_row = jnp.pad(sinks[0].astype(F32), (0, LANES - B_HEADS)).reshape(1, LANES)

    h_p = x_prompt.reshape(P_ROWS, D_MODEL)
    h_q = jnp.concatenate([x_sample.reshape(S_ROWS, D_MODEL), meta_tokens.astype(F32),
                           jnp.zeros((A_CHUNK - N_META, D_MODEL), F32)], axis=0)
    TM_P, TM_W = 1024, 512

    h_q, *wf = _ffn_cast(h_q, row(ffn_norm[0, 0]), w_ffn_in, w_ffn_out, 0, 0)
    h_p = _ffn(h_p, row(ffn_norm[0, 0]), *wf, TM_P)
    p_q, g_q, wa_in = _inproj(h_q, row(mix_norm[0]), wa_in_t, wa_in_t, ba_gate, Q_ROWS, emit_bf16=True)
    p_p, g_p = _inproj(h_p, row(mix_norm[0]), wa_in, wa_in_t, ba_gate, TM_P)

    zc = jnp.zeros((1, A_HEADS, A_DK, A_DV), F32)
    zn = jnp.zeros((1, A_HEADS, 1, A_DK), F32)
    zm = jnp.zeros((1, A_HEADS, 1, LANES), F32)
    hm_m, c_m, n_m, m_m = _mlstm_chunks(p_q, g_q, zc, zn, zm, 1, 1, META_BLOCK, N_META, True)
    hm_p, c_p, n_p, m_p = _mlstm_chunks(p_p, g_p, c_m, n_m, m_m, BATCH, SEQ // A_CHUNK, 0, A_CHUNK, True)
    m0_s = jnp.broadcast_to(state_m[0].astype(F32)[:, :, None, None], (DEC_BATCH, A_HEADS, 1, LANES))
    hm_s, c_s, n_s, m_s = _mlstm_sample(p_q, g_q, state_C[0].astype(F32),
                                        state_n[0].astype(F32)[:, :, None, :], m0_s)
    hm_q = jnp.concatenate([hm_s, hm_m], axis=0)

    h_p = _mlstm_out(hm_p, p_p, row(a_head_norm[0]), wa_out, h_p, TM_W)
    h_q = _mlstm_out(hm_q, p_q, row(a_head_norm[0]), wa_out, h_q, A_CHUNK)
    h_q, *wf = _ffn_cast(h_q, row(ffn_norm[0, 1]), w_ffn_in, w_ffn_out, 0, 1)
    h_p = _ffn_flat(h_p, row(ffn_norm[0, 1]), *wf, TM_P, 512, VMEM_FFN_MIB, "ffn_flat1024")

    kv_p = _normproj(h_p, row(kv_norm), wkv, k_gain, TM_P)
    kv_q = _normproj(h_q, row(kv_norm), wkv, k_gain, Q_ROWS)
    k_s = kv_q[:S_ROWS, :B_KV].reshape(DEC_BATCH, DEC_SEQ, B_KV_HEADS, B_DH)
    v_s = kv_q[:S_ROWS, B_KV:].reshape(DEC_BATCH, DEC_SEQ, B_KV_HEADS, B_DH)
    k_win_s = jnp.concatenate([cache_k_win, k_s.astype(cache_k_win.dtype)], axis=1)[:, -WINDOW:]
    v_win_s = jnp.concatenate([cache_v_win, v_s.astype(cache_v_win.dtype)], axis=1)[:, -WINDOW:]

    h_s = h_q[:S_ROWS]
    h_s, *wf = _ffn_cast(h_s, row(ffn_norm[1, 0]), w_ffn_in, w_ffn_out, 1, 0)
    h_p = _ffn_flat(h_p, row(ffn_norm[1, 0]), *wf, 512, 512, VMEM_DEFAULT_MIB, "ffn_flat512")
    q_p = _normproj(h_p, row(mix_norm[1]), wq, q_gain, TM_W)
    q_s = _normproj(h_s, row(mix_norm[1]), wq, q_gain, TM_W)

    o_p = _attn_prompt(q_p, kv_p, kv_q, sinks_row)
    q4 = q_s.reshape(DEC_BATCH, DEC_SEQ, KV_PAIRS, 2, B_GROUP, B_DH).transpose(0, 2, 1, 4, 3, 5)
    q4 = q4.reshape(DEC_BATCH, KV_PAIRS, SAMPLE_ROWS, LANES)
    seq3 = lambda x: x.astype(F32).reshape(DEC_BATCH, -1, B_KV)
    o4 = _attn_sample(q4, seq3(cache_k_meta), seq3(cache_v_meta),
                      seq3(cache_k_win[:, :SAMPLE_OLD]), seq3(cache_v_win[:, :SAMPLE_OLD]),
                      seq3(k_win_s), seq3(v_win_s), sinks_row)
    o_s = o4.reshape(DEC_BATCH, KV_PAIRS, DEC_SEQ, B_GROUP, 2, B_DH).transpose(0, 2, 1, 4, 3, 5)
    o_s = o_s.reshape(S_ROWS, D_MODEL)

    h_p = _matres(o_p, wb_out, h_p, TM_W)
    h_s = _matres(o_s, wb_out, h_s, TM_W)
    h_s, *wf = _ffn_cast(h_s, row(ffn_norm[1, 1]), w_ffn_in, w_ffn_out, 1, 1)
    h_p = _ffn_v2(h_p, row(ffn_norm[1, 1]), w_ffn_in.astype(BF16), w_ffn_out.astype(BF16), 1, 1, TM_P)

    kv4 = lambda x: x.reshape(x.shape[:-1] + (B_KV_HEADS, B_DH))
    meta_rows = kv_q[S_ROWS:S_ROWS + N_META]
    kv_p3 = kv_p.reshape(BATCH, SEQ, 2 * B_KV)
    st = lambda x, dt: x[None].astype(dt)
    return (
        h_p.reshape(BATCH, SEQ, D_MODEL),
        h_s.reshape(DEC_BATCH, DEC_SEQ, D_MODEL),
        st(c_p, state_C.dtype), st(n_p[:, :, 0, :], state_n.dtype), st(m_p[:, :, 0, 0], state_m.dtype),
        jnp.broadcast_to(kv4(meta_rows[:, :B_KV])[None], (BATCH, N_META, B_KV_HEADS, B_DH)),
        jnp.broadcast_to(kv4(meta_rows[:, B_KV:])[None], (BATCH, N_META, B_KV_HEADS, B_DH)),
        kv4(kv_p3[:, -WINDOW:, :B_KV]), kv4(kv_p3[:, -WINDOW:, B_KV:]),
        st(c_s, state_C.dtype), st(n_s[:, :, 0, :], state_n.dtype), st(m_s[:, :, 0, 0], state_m.dtype),
        k_win_s, v_win_s,
    )
```

```python
import functools

import jax
import jax.numpy as jnp
from jax import lax
from jax.experimental import pallas as pl
from jax.experimental.pallas import tpu as pltpu

D_MODEL = 2048
BATCH = 8
SEQ = 2048
DEC_BATCH = 128
DEC_SEQ = 4
PAST_LEN = 8192
N_META = 16
A_HEADS = 4
A_DV = D_MODEL // A_HEADS
A_DK = A_DV // 2
A_CHUNK = 128
A_GATE_CAP = 15.0
A_QKVO = 2 * A_HEADS * A_DK + 2 * A_HEADS * A_DV
B_HEADS = 32
B_DH = D_MODEL // B_HEADS
B_KV_HEADS = 4
B_GROUP = B_HEADS // B_KV_HEADS
B_KV = B_KV_HEADS * B_DH
WINDOW = 128
D_FF = ((8 * D_MODEL // 3 + 255) // 256) * 256
EPS = 1e-6

LANES = 128
SUBLANES = 8
VMEM_DEFAULT_MIB = 48
VMEM_FFN_MIB = 60

P_ROWS = BATCH * SEQ
S_ROWS = DEC_BATCH * DEC_SEQ
Q_ROWS = S_ROWS + A_CHUNK
META_BLOCK = S_ROWS // A_CHUNK

F32 = jnp.float32
BF16 = jnp.bfloat16
NEG_INF = float("-inf")


def _params(n_axes, vmem_mib=VMEM_DEFAULT_MIB):
    return pltpu.CompilerParams(dimension_semantics=("arbitrary",) * n_axes,
                                vmem_limit_bytes=vmem_mib * 1024 * 1024)


def _resident(shape):
    return pl.BlockSpec(shape, lambda i: (0, 0), pipeline_mode=pl.Buffered(1))


def _rms(x, g):
    return x * lax.rsqrt(jnp.mean(x * x, axis=-1, keepdims=True) + EPS) * g


def _dot(a, b):
    return jnp.dot(a, b, preferred_element_type=F32)


def _dot_nt(a, b):
    return lax.dot_general(a, b, (((1,), (1,)), ((), ())), preferred_element_type=F32)


def _dot_tn(a, b):
    return lax.dot_general(a, b, (((0,), (0,)), ((), ())), preferred_element_type=F32)


def _log_sigmoid(x):
    return -(jnp.maximum(-x, 0.0) + jnp.log1p(jnp.exp(-jnp.abs(x))))


def _head_norm64(y, gain):
    lo = lax.broadcasted_iota(jnp.int32, (1, LANES), 1) < B_DH
    cols = []
    for c in range(y.shape[1] // LANES):
        x = y[:, c * LANES:(c + 1) * LANES]
        xx = x * x
        s_lo = jnp.sum(jnp.where(lo, xx, 0.0), axis=-1, keepdims=True)
        s_hi = jnp.sum(jnp.where(lo, 0.0, xx), axis=-1, keepdims=True)
        scale = jnp.where(lo, lax.rsqrt(s_lo / B_DH + EPS), lax.rsqrt(s_hi / B_DH + EPS))
        cols.append(x * scale * gain[:, c * LANES:(c + 1) * LANES])
    return jnp.concatenate(cols, axis=1)


def _ffn_kernel(h_ref, g_ref, wg_ref, wu_ref, wo_ref, o_ref, *rest, emit_bf16):
    j = pl.program_id(1)
    if emit_bf16:
        wgb_ref, wub_ref, wob_ref, xn_ref = rest
        wg = wg_ref[...].astype(BF16)
        wu = wu_ref[...].astype(BF16)
        wo = wo_ref[...].astype(BF16)
        wgb_ref[...] = wg
        wub_ref[...] = wu
        wob_ref[...] = wo
    else:
        (xn_ref,) = rest
        wg, wu, wo = wg_ref[...], wu_ref[...], wo_ref[...]

    def half_ffn(xn):
        g = _dot(xn, wg)
        u = _dot(xn, wu)
        a = (g / (1.0 + jnp.exp(-g))) * (0.5 * u)
        return _dot(a.astype(BF16), wo)

    @pl.when(j == 0)
    def _():
        h = h_ref[...]
        xn_ref[...] = _rms(h, g_ref[...]).astype(BF16)
        o_ref[...] = h

    o_ref[...] += half_ffn(xn_ref[...])


FFN_CAST_TF = 256


def _ffn(h, gain, wg, wu, wo, tm, tf=512):
    rows = h.shape[0]
    n_ff = D_FF // tf
    return pl.pallas_call(
        functools.partial(_ffn_kernel, emit_bf16=False),
        grid=(rows // tm, n_ff),
        in_specs=[
            pl.BlockSpec((tm, D_MODEL), lambda i, j: (i, 0)),
            pl.BlockSpec((1, D_MODEL), lambda i, j: (0, 0)),
            pl.BlockSpec((D_MODEL, tf), lambda i, j: (0, j)),
            pl.BlockSpec((D_MODEL, tf), lambda i, j: (0, j)),
            pl.BlockSpec((tf, D_MODEL), lambda i, j: (j, 0)),
        ],
        out_specs=pl.BlockSpec((tm, D_MODEL), lambda i, j: (i, 0)),
        out_shape=jax.ShapeDtypeStruct((rows, D_MODEL), F32),
        scratch_shapes=[pltpu.VMEM((tm, D_MODEL), BF16)],
        compiler_params=_params(2, VMEM_FFN_MIB),
        name="ffn",
    )(h, gain, wg, wu, wo)


def _ffn_cast(h, gain, w_in, w_out, layer, which):
    rows = h.shape[0]
    tf = FFN_CAST_TF
    n_ff = D_FF // tf
    return pl.pallas_call(
        functools.partial(_ffn_kernel, emit_bf16=True),
        grid=(1, n_ff),
        in_specs=[
            pl.BlockSpec((rows, D_MODEL), lambda i, j: (0, 0)),
            pl.BlockSpec((1, D_MODEL), lambda i, j: (0, 0)),
            pl.BlockSpec((None, None, D_MODEL, tf), lambda i, j: (layer, which, 0, j)),
            pl.BlockSpec((None, None, D_MODEL, tf), lambda i, j: (layer, which, 0, j + n_ff)),
            pl.BlockSpec((None, None, tf, D_MODEL), lambda i, j: (layer, which, j, 0)),
        ],
        out_specs=[
            pl.BlockSpec((rows, D_MODEL), lambda i, j: (0, 0)),
            pl.BlockSpec((D_MODEL, tf), lambda i, j: (0, j)),
            pl.BlockSpec((D_MODEL, tf), lambda i, j: (0, j)),
            pl.BlockSpec((tf, D_MODEL), lambda i, j: (j, 0)),
        ],
        out_shape=[
            jax.ShapeDtypeStruct((rows, D_MODEL), F32),
            jax.ShapeDtypeStruct((D_MODEL, D_FF), BF16),
            jax.ShapeDtypeStruct((D_MODEL, D_FF), BF16),
            jax.ShapeDtypeStruct((D_FF, D_MODEL), BF16),
        ],
        scratch_shapes=[pltpu.VMEM((rows, D_MODEL), BF16)],
        compiler_params=_params(2),
        name="ffn_cast",
    )(h, gain, w_in, w_in, w_out)


GATE_ROWS = 2 * A_HEADS


def _inproj_kernel(h_ref, g_ref, w_ref, wgate_ref, bgate_ref, p_ref, gates_ref, *rest, emit_bf16):
    j = pl.program_id(1)
    if emit_bf16:
        wb_ref, xn_ref = rest
        w = w_ref[...].astype(BF16)
        wb_ref[...] = w
    else:
        (xn_ref,) = rest
        w = w_ref[...]

    @pl.when(j == 0)
    def _():
        xn = _rms(h_ref[...], g_ref[...]).astype(BF16)
        xn_ref[...] = xn
        wgate = jnp.concatenate([wgate_ref[...], jnp.zeros((LANES - GATE_ROWS, D_MODEL), F32)], axis=0)
        pre = _dot_nt(xn, wgate.astype(BF16)) + bgate_ref[...]
        capped = A_GATE_CAP * jnp.tanh(pre / A_GATE_CAP)
        lane = lax.broadcasted_iota(jnp.int32, (1, LANES), 1)
        gates_ref[...] = jnp.where(lane < A_HEADS, capped, _log_sigmoid(capped))

    p_ref[...] = _dot_nt(xn_ref[...], w)


def _inproj(h, gain, w_t, w_gate_t, bgate, tm, emit_bf16=False, tn=1024):
    rows = h.shape[0]
    assert not emit_bf16 or rows == tm
    out_specs = [
        pl.BlockSpec((tm, tn), lambda i, j: (i, j)),
        pl.BlockSpec((tm, LANES), lambda i, j: (i, 0)),
    ]
    out_shape = [jax.ShapeDtypeStruct((rows, A_QKVO), F32),
                 jax.ShapeDtypeStruct((rows, LANES), F32)]
    if emit_bf16:
        out_specs.append(pl.BlockSpec((tn, D_MODEL), lambda i, j: (j, 0)))
        out_shape.append(jax.ShapeDtypeStruct((A_QKVO, D_MODEL), BF16))
    return pl.pallas_call(
        functools.partial(_inproj_kernel, emit_bf16=emit_bf16),
        grid=(rows // tm, A_QKVO // tn),
        in_specs=[
            pl.BlockSpec((tm, D_MODEL), lambda i, j: (i, 0)),
            pl.BlockSpec((1, D_MODEL), lambda i, j: (0, 0)),
            pl.BlockSpec((tn, D_MODEL), lambda i, j: (j, 0)),
            pl.BlockSpec((GATE_ROWS, D_MODEL), lambda i, j: (A_QKVO // GATE_ROWS, 0)),
            pl.BlockSpec((1, LANES), lambda i, j: (0, 0)),
        ],
        out_specs=out_specs,
        out_shape=out_shape,
        scratch_shapes=[pltpu.VMEM((tm, D_MODEL), BF16)],
        compiler_params=_params(2),
        name="mlstm_inproj",
    )(h, gain, w_t, w_gate_t, bgate)


def _mlstm_chunk_kernel(q_ref, k_ref, v_ref, g_ref, c0_ref, n0_ref, m0_ref,
                        h_ref, c_ref, n_ref, m_ref, qk_ref, qc_ref, sb_ref, *, n_valid):
    L = A_CHUNK
    heads = range(A_HEADS)

    @pl.when(pl.program_id(1) == 0)
    def _():
        c_ref[...] = c0_ref[...]
        n_ref[...] = n0_ref[...]
        m_ref[...] = m0_ref[...]

    for hd in heads:
        qb = q_ref[:, hd * A_DK:(hd + 1) * A_DK].astype(BF16)
        kb = (k_ref[:, hd * A_DK:(hd + 1) * A_DK] * (A_DK ** -0.5)).astype(BF16)
        qk_ref[hd] = _dot_nt(qb, kb)
        qc_ref[hd] = _dot(qb, c_ref[0, hd].astype(BF16))

    gates = g_ref[...]
    row = lax.broadcasted_iota(jnp.int32, (L, L), 0)
    col = lax.broadcasted_iota(jnp.int32, (L, L), 1)
    causal = col <= row
    eye = col == row
    masked = n_valid < L
    if masked:
        row_ok = lax.broadcasted_iota(jnp.int32, (L, 1), 0) < n_valid
        gates_lf = jnp.where(row_ok, gates, 0.0)
    else:
        gates_lf = gates
    csum = jnp.dot(causal.astype(F32), gates_lf, precision=lax.Precision.HIGHEST,
                   preferred_element_type=F32)

    def to_row(x_col):
        return jnp.sum(jnp.where(eye, x_col, 0.0), axis=0, keepdims=True)

    w_intra, w_inter, floor, w_state, decay, m_new = [], [], [], [], [], []
    for hd in heads:
        b_col = csum[:, A_HEADS + hd:A_HEADS + hd + 1]
        ig_col = gates[:, hd:hd + 1]
        if masked:
            ig_col = jnp.where(row_ok, ig_col, NEG_INF)
        b_row = to_row(b_col)
        ig_row = to_row(ig_col)
        m_prev = m_ref[0, hd][:, 0:1]
        d_log = jnp.where(causal, b_col - b_row + ig_row, NEG_INF)
        inter_log = b_col + m_prev
        m_t = jnp.maximum(inter_log, jnp.max(d_log, axis=-1, keepdims=True))
        w_intra.append(jnp.exp(d_log - m_t))
        w_inter.append(jnp.exp(inter_log - m_t))
        floor.append(jnp.exp(-m_t))
        b_last = b_col[L - 1:L, :]
        w_log = b_last - b_col + ig_col
        m_new.append(jnp.maximum(b_last + m_prev, jnp.max(w_log, axis=0, keepdims=True)))
        w_state.append(jnp.exp(w_log - m_new[hd]))
        decay.append(jnp.exp(b_last + m_prev - m_new[hd]))

    den = []
    for hd in heads:
        s = qk_ref[hd] * w_intra[hd]
        sb_ref[hd] = s.astype(BF16)
        q = q_ref[:, hd * A_DK:(hd + 1) * A_DK]
        d = jnp.sum(s, axis=-1, keepdims=True) + w_inter[hd] * jnp.sum(q * n_ref[0, hd], axis=-1, keepdims=True)
        den.append(jnp.maximum(jnp.abs(d), floor[hd]))

    for hd in heads:
        k = k_ref[:, hd * A_DK:(hd + 1) * A_DK] * (A_DK ** -0.5)
        vb = v_ref[:, hd * A_DV:(hd + 1) * A_DV].astype(BF16)
        num = _dot(sb_ref[hd], vb) + w_inter[hd] * qc_ref[hd]
        h_ref[:, hd * A_DV:(hd + 1) * A_DV] = num / den[hd]
        kw = k * w_state[hd]
        c_ref[0, hd] = decay[hd] * c_ref[0, hd] + _dot_tn(kw.astype(BF16), vb)
        n_ref[0, hd] = decay[hd] * n_ref[0, hd] + jnp.sum(kw, axis=0, keepdims=True)
        m_ref[0, hd] = jnp.broadcast_to(m_new[hd], (1, LANES))


def _mlstm_chunks(p, gates, c0, n0, m0, n_seq, n_chunks, row_block0, n_valid, shared_state):
    L = A_CHUNK
    rb = lambda b, c: row_block0 + b * n_chunks + c
    st = (lambda b, c: (0, 0, 0, 0)) if shared_state else (lambda b, c: (b, 0, 0, 0))
    return pl.pallas_call(
        functools.partial(_mlstm_chunk_kernel, n_valid=n_valid),
        grid=(n_seq, n_chunks),
        in_specs=[
            pl.BlockSpec((L, A_HEADS * A_DK), lambda b, c: (rb(b, c), 0)),
            pl.BlockSpec((L, A_HEADS * A_DK), lambda b, c: (rb(b, c), 1)),
            pl.BlockSpec((L, A_HEADS * A_DV), lambda b, c: (rb(b, c), 1)),
            pl.BlockSpec((L, LANES), lambda b, c: (rb(b, c), 0)),
            pl.BlockSpec((1, A_HEADS, A_DK, A_DV), st),
            pl.BlockSpec((1, A_HEADS, 1, A_DK), st),
            pl.BlockSpec((1, A_HEADS, 1, LANES), st),
        ],
        out_specs=[
            pl.BlockSpec((L, A_HEADS * A_DV), lambda b, c: (b * n_chunks + c, 0)),
            pl.BlockSpec((1, A_HEADS, A_DK, A_DV), lambda b, c: (b, 0, 0, 0)),
            pl.BlockSpec((1, A_HEADS, 1, A_DK), lambda b, c: (b, 0, 0, 0)),
            pl.BlockSpec((1, A_HEADS, 1, LANES), lambda b, c: (b, 0, 0, 0)),
        ],
        out_shape=[
            jax.ShapeDtypeStruct((n_seq * n_chunks * L, D_MODEL), F32),
            jax.ShapeDtypeStruct((n_seq, A_HEADS, A_DK, A_DV), F32),
            jax.ShapeDtypeStruct((n_seq, A_HEADS, 1, A_DK), F32),
            jax.ShapeDtypeStruct((n_seq, A_HEADS, 1, LANES), F32),
        ],
        scratch_shapes=[pltpu.VMEM((A_HEADS, L, L), F32),
                        pltpu.VMEM((A_HEADS, L, A_DV), F32),
                        pltpu.VMEM((A_HEADS, L, L), BF16)],
        compiler_params=_params(2),
        name="mlstm_chunks",
    )(p, p, p, gates, c0, n0, m0)


def _mlstm_sample_kernel(q_ref, k_ref, v_ref, g_ref, c0_ref, n0_ref, m0_ref,
                         h_ref, c_ref, n_ref, m_ref):
    R = 2 * DEC_SEQ
    PAD = A_CHUNK - R
    gates = g_ref[...]
    r_col = lax.broadcasted_iota(jnp.int32, (R, 1), 0)
    is_a = r_col < DEC_SEQ
    row = lax.broadcasted_iota(jnp.int32, (R, LANES), 0)
    lane = lax.broadcasted_iota(jnp.int32, (R, LANES), 1)
    same = ((lane < DEC_SEQ) & (row < DEC_SEQ)) | ((lane >= DEC_SEQ) & (lane < R) & (row >= DEC_SEQ))
    causal = same & (lane <= row)
    eye = lane == row

    def to_row(x_col):
        return jnp.sum(jnp.where(eye, x_col, 0.0), axis=0, keepdims=True)

    for hd in range(A_HEADS):
        lf_col = gates[:, A_HEADS + hd:A_HEADS + hd + 1]
        ig_col = gates[:, hd:hd + 1]
        lf_row = to_row(lf_col)
        ig_row = to_row(ig_col)
        b_col = jnp.sum(jnp.where(causal, lf_row, 0.0), axis=1, keepdims=True)
        b_row = to_row(b_col)
        m_a = m0_ref[0, hd][:, 0:1]
        m_b = m0_ref[1, hd][:, 0:1]
        m_prev = jnp.where(is_a, m_a, m_b)
        c_a = c0_ref[0, hd]
        c_b = c0_ref[1, hd]
        n_a = n0_ref[0, hd]
        n_b = n0_ref[1, hd]

        q = q_ref[:, hd * A_DK:(hd + 1) * A_DK]
        k = k_ref[:, hd * A_DK:(hd + 1) * A_DK] * (A_DK ** -0.5)
        v = v_ref[:, hd * A_DV:(hd + 1) * A_DV]
        qb = q.astype(BF16)
        k_pad = jnp.concatenate([k, jnp.zeros((PAD, A_DK), F32)], axis=0).astype(BF16)
        v_pad = jnp.concatenate([v, jnp.zeros((PAD, A_DV), F32)], axis=0).astype(BF16)

        d_log = jnp.where(causal, b_col - b_row + ig_row, NEG_INF)
        inter_log = b_col + m_prev
        m_t = jnp.maximum(inter_log, jnp.max(d_log, axis=-1, keepdims=True))
        w_intra = jnp.exp(d_log - m_t)
        w_inter = jnp.exp(inter_log - m_t)
        s = _dot_nt(qb, k_pad) * w_intra
        q_c = jnp.where(is_a, _dot(qb, c_a.astype(BF16)), _dot(qb, c_b.astype(BF16)))
        num = _dot(s.astype(BF16), v_pad) + w_inter * q_c
        q_n = jnp.sum(q * jnp.where(is_a, n_a, n_b), axis=-1, keepdims=True)
        den = jnp.sum(s, axis=-1, keepdims=True) + w_inter * q_n
        den = jnp.maximum(jnp.abs(den), jnp.exp(-m_t))
        h_ref[:, hd * A_DV:(hd + 1) * A_DV] = num / den

        for idx, sel, m_x, c_x, n_x in ((0, is_a, m_a, c_a, n_a),
                                        (1, jnp.logical_not(is_a), m_b, c_b, n_b)):
            last = (idx + 1) * DEC_SEQ - 1
            b_last = b_col[last:last + 1, :]
            w_log = jnp.where(sel, b_last - b_col + ig_col, NEG_INF)
            m_new = jnp.maximum(b_last + m_x, jnp.max(w_log, axis=0, keepdims=True))
            w_state = jnp.exp(w_log - m_new)
            decay = jnp.exp(b_last + m_x - m_new)
            kw = k * w_state
            kw_pad = jnp.concatenate([kw, jnp.zeros((PAD, A_DK), F32)], axis=0).astype(BF16)
            c_ref[idx, hd] = decay * c_x + _dot_tn(kw_pad, v_pad)
            n_ref[idx, hd] = decay * n_x + jnp.sum(kw, axis=0, keepdims=True)
            m_ref[idx, hd] = jnp.broadcast_to(m_new, (1, LANES))


def _mlstm_sample(p, gates, c0, n0, m0):
    R = 2 * DEC_SEQ
    n_pairs = DEC_BATCH // 2
    st = lambda i: (i, 0, 0, 0)
    return pl.pallas_call(
        _mlstm_sample_kernel,
        grid=(n_pairs,),
        in_specs=[
            pl.BlockSpec((R, A_HEADS * A_DK), lambda i: (i, 0)),
            pl.BlockSpec((R, A_HEADS * A_DK), lambda i: (i, 1)),
            pl.BlockSpec((R, A_HEADS * A_DV), lambda i: (i, 1)),
            pl.BlockSpec((R, LANES), lambda i: (i, 0)),
            pl.BlockSpec((2, A_HEADS, A_DK, A_DV), st),
            pl.BlockSpec((2, A_HEADS, 1, A_DK), st),
            pl.BlockSpec((2, A_HEADS, 1, LANES), st),
        ],
        out_specs=[
            pl.BlockSpec((R, A_HEADS * A_DV), lambda i: (i, 0)),
            pl.BlockSpec((2, A_HEADS, A_DK, A_DV), st),
            pl.BlockSpec((2, A_HEADS, 1, A_DK), st),
            pl.BlockSpec((2, A_HEADS, 1, LANES), st),
        ],
        out_shape=[
            jax.ShapeDtypeStruct((S_ROWS, D_MODEL), F32),
            jax.ShapeDtypeStruct((DEC_BATCH, A_HEADS, A_DK, A_DV), F32),
            jax.ShapeDtypeStruct((DEC_BATCH, A_HEADS, 1, A_DK), F32),
            jax.ShapeDtypeStruct((DEC_BATCH, A_HEADS, 1, LANES), F32),
        ],
        compiler_params=_params(1),
        name="mlstm_sample",
    )(p, p, p, gates, c0, n0, m0)


def _mlstm_out_kernel(hm_ref, o_ref, hg_ref, w_ref, res_ref, out_ref):
    cols = []
    for hd in range(A_HEADS):
        x = hm_ref[:, hd * A_DV:(hd + 1) * A_DV]
        cols.append(x * lax.rsqrt(jnp.mean(x * x, axis=-1, keepdims=True) + EPS))
    hn = jnp.concatenate(cols, axis=1) * hg_ref[...]
    o = o_ref[...]
    pre = (hn * (1.0 / (1.0 + jnp.exp(-o)))).astype(BF16)
    out_ref[...] = res_ref[...] + _dot(pre, w_ref[...])


def _mlstm_out(hm, p, head_gain, w, res, tm):
    rows = hm.shape[0]
    return pl.pallas_call(
        _mlstm_out_kernel,
        grid=(rows // tm,),
        in_specs=[
            pl.BlockSpec((tm, D_MODEL), lambda i: (i, 0)),
            pl.BlockSpec((tm, D_MODEL), lambda i: (i, 2)),
            pl.BlockSpec((1, D_MODEL), lambda i: (0, 0)),
            _resident((D_MODEL, D_MODEL)),
            pl.BlockSpec((tm, D_MODEL), lambda i: (i, 0)),
        ],
        out_specs=pl.BlockSpec((tm, D_MODEL), lambda i: (i, 0)),
        out_shape=jax.ShapeDtypeStruct((rows, D_MODEL), F32),
        compiler_params=_params(1),
        name="mlstm_out",
    )(hm, p, head_gain, w, res)


def _normproj_kernel(h_ref, g_ref, w_ref, hg_ref, o_ref, *, n_norm):
    xn = _rms(h_ref[...], g_ref[...]).astype(BF16)
    y = _dot(xn, w_ref[...])
    if n_norm == y.shape[1]:
        o_ref[...] = _head_norm64(y, hg_ref[...])
    else:
        o_ref[:, :n_norm] = _head_norm64(y[:, :n_norm], hg_ref[...])
        o_ref[:, n_norm:] = y[:, n_norm:]


def _normproj(h, gain, w, head_gain, tm):
    rows = h.shape[0]
    n = w.shape[1]
    n_norm = head_gain.shape[1]
    return pl.pallas_call(
        functools.partial(_normproj_kernel, n_norm=n_norm),
        grid=(rows // tm,),
        in_specs=[
            pl.BlockSpec((tm, D_MODEL), lambda i: (i, 0)),
            pl.BlockSpec((1, D_MODEL), lambda i: (0, 0)),
            _resident((D_MODEL, n)),
            pl.BlockSpec((1, n_norm), lambda i: (0, 0)),
        ],
        out_specs=pl.BlockSpec((tm, n), lambda i: (i, 0)),
        out_shape=jax.ShapeDtypeStruct((rows, n), F32),
        compiler_params=_params(1),
        name="normproj",
    )(h, gain, w, head_gain)


def _matres_kernel(x_ref, w_ref, res_ref, o_ref):
    o_ref[...] = res_ref[...] + _dot(x_ref[...].astype(BF16), w_ref[...])


def _matres(x, w, res, tm):
    rows = x.shape[0]
    return pl.pallas_call(
        _matres_kernel,
        grid=(rows // tm,),
        in_specs=[
            pl.BlockSpec((tm, D_MODEL), lambda i: (i, 0)),
            _resident((D_MODEL, D_MODEL)),
            pl.BlockSpec((tm, D_MODEL), lambda i: (i, 0)),
        ],
        out_specs=pl.BlockSpec((tm, D_MODEL), lambda i: (i, 0)),
        out_shape=jax.ShapeDtypeStruct((rows, D_MODEL), F32),
        compiler_params=_params(1),
        name="matres",
    )(x, w, res)


def _alibi_slope(head):
    return 2.0 ** (-8.0 * (head + 1) / B_HEADS)


PROMPT_SUB = WINDOW // 2
PROMPT_BAND = WINDOW + PROMPT_SUB
PROMPT_KEYS = 2 * LANES
HEAD_PAIRS = B_HEADS // 2
PAIRS_PER_GROUP = B_GROUP // 2


def _pair_rhs(x, c):
    col = x[:, c * LANES:(c + 1) * LANES]
    rol = pltpu.roll(col, B_DH, axis=1)
    lo = lax.broadcasted_iota(jnp.int32, (1, LANES), 1) < B_DH
    even = jnp.concatenate([jnp.where(lo, col, 0.0), jnp.where(lo, 0.0, rol)], axis=0)
    odd = jnp.concatenate([jnp.where(lo, rol, 0.0), jnp.where(lo, 0.0, col)], axis=0)
    return even.astype(BF16), odd.astype(BF16)


def _attn_prompt_kernel(q_ref, kvo_ref, kvp_ref, kvm_ref, sink_ref, o_ref, bias_ref, s_ref, p_ref):
    W = WINDOW
    SB = PROMPT_SUB
    NK = PROMPT_KEYS
    first = (pl.program_id(0) == 0) & (pl.program_id(1) == 0)
    j = pl.program_id(1)
    rows = PAIRS_PER_GROUP * SB
    lo = lax.broadcasted_iota(jnp.int32, (1, LANES), 1) < B_DH

    @pl.when(first)
    def _():
        c = lax.broadcasted_iota(jnp.int32, (SB, NK), 1)
        is_meta = c < N_META
        is_sink = c == NK - 1
        for sub in range(2):
            i = lax.broadcasted_iota(jnp.int32, (SB, NK), 0) + sub * SB
            pos = c - N_META + sub * SB
            rel = W + i - pos
            in_band = (c >= N_META) & (c < N_META + PROMPT_BAND) & (rel >= 0) & (rel < W)
            for variant in range(2):
                if variant == 0:
                    dist = jnp.where(is_meta, jnp.minimum(i + N_META - c, W), rel).astype(F32)
                    valid = is_meta | (in_band & (pos >= W))
                else:
                    dist = jnp.where(is_meta, W, rel).astype(F32)
                    valid = is_meta | in_band
                for hd in range(B_HEADS):
                    r0 = (hd // 2) * SB
                    c0 = (hd % 2) * NK
                    table = jnp.where(valid, -_alibi_slope(hd) * dist, NEG_INF)
                    bias_ref[variant, sub, r0:r0 + SB, c0:c0 + NK] = jnp.where(
                        is_sink, sink_ref[0:1, hd:hd + 1], table)

    variant = jnp.minimum(j, 1)
    kvm = kvm_ref[...]
    band = jnp.concatenate([kvp_ref[...], kvo_ref[...]], axis=0)
    pad = jnp.zeros((NK - N_META - PROMPT_BAND, 2 * B_KV), F32)
    ones_rhs = jnp.concatenate([jnp.broadcast_to(jnp.where(lo, 1.0, 0.0), (NK, LANES)),
                                jnp.broadcast_to(jnp.where(lo, 0.0, 1.0), (NK, LANES))], axis=0).astype(BF16)

    sub_rows = B_KV_HEADS * rows
    for sub in range(2):
        keys = jnp.concatenate([kvm, band[sub * SB:sub * SB + PROMPT_BAND], pad], axis=0)
        k_rhs = _pair_rhs(keys, 0) + _pair_rhs(keys, 1)
        v_rhs = _pair_rhs(keys, 2) + _pair_rhs(keys, 3)
        for g in range(B_KV_HEADS):
            p0 = g * PAIRS_PER_GROUP
            r0 = (sub * B_KV_HEADS + g) * rows
            qg = jnp.concatenate([q_ref[sub * SB:(sub + 1) * SB, (p0 + m) * LANES:(p0 + m + 1) * LANES]
                                  for m in range(PAIRS_PER_GROUP)], axis=0) * (B_DH ** -0.5)
            s_ref[r0:r0 + rows, :] = _dot_nt(qg.astype(BF16), k_rhs[g])

        s = s_ref[sub * sub_rows:(sub + 1) * sub_rows, :] + bias_ref[variant, sub]
        halves = []
        for half in range(2):
            sh = s[:, half * NK:(half + 1) * NK]
            halves.append(jnp.exp(sh - jnp.max(sh, axis=-1, keepdims=True)).astype(BF16))
        p_ref[sub * sub_rows:(sub + 1) * sub_rows, :] = jnp.concatenate(halves, axis=1)

        for g in range(B_KV_HEADS):
            p0 = g * PAIRS_PER_GROUP
            r0 = (sub * B_KV_HEADS + g) * rows
            od = _dot(p_ref[r0:r0 + rows, :], jnp.concatenate([v_rhs[g], ones_rhs], axis=1))
            o = od[:, :LANES] / od[:, LANES:]
            for m in range(PAIRS_PER_GROUP):
                o_ref[sub * SB:(sub + 1) * SB, (p0 + m) * LANES:(p0 + m + 1) * LANES] = o[m * SB:(m + 1) * SB, :]


def _attn_prompt(q, kv_p, kv_q, sinks_row):
    nb = SEQ // WINDOW
    blk = lambda b, j: b * nb + j
    return pl.pallas_call(
        _attn_prompt_kernel,
        grid=(BATCH, nb),
        in_specs=[
            pl.BlockSpec((WINDOW, D_MODEL), lambda b, j: (blk(b, j), 0)),
            pl.BlockSpec((WINDOW, 2 * B_KV), lambda b, j: (blk(b, j), 0)),
            pl.BlockSpec((WINDOW, 2 * B_KV), lambda b, j: (blk(b, jnp.maximum(j - 1, 0)), 0)),
            pl.BlockSpec((N_META, 2 * B_KV), lambda b, j: (S_ROWS // N_META, 0)),
            pl.BlockSpec((1, LANES), lambda b, j: (0, 0)),
        ],
        out_specs=pl.BlockSpec((WINDOW, D_MODEL), lambda b, j: (blk(b, j), 0)),
        out_shape=jax.ShapeDtypeStruct((P_ROWS, D_MODEL), F32),
        scratch_shapes=[pltpu.VMEM((2, 2, HEAD_PAIRS * PROMPT_SUB, 2 * PROMPT_KEYS), F32),
                        pltpu.VMEM((2 * HEAD_PAIRS * PROMPT_SUB, 2 * PROMPT_KEYS), F32),
                        pltpu.VMEM((2 * HEAD_PAIRS * PROMPT_SUB, 2 * PROMPT_KEYS), BF16)],
        compiler_params=_params(2),
        name="attn_prompt",
    )(q, kv_p, kv_p, kv_q, sinks_row)


GROUP_SHIFT = B_GROUP.bit_length() - 1
assert 1 << GROUP_SHIFT == B_GROUP
SAMPLE_OLD = SUBLANES
SAMPLE_KEYS = 2 * LANES
SAMPLE_SEQ_PER_STEP = 8
SAMPLE_ROWS = DEC_SEQ * B_GROUP
KV_PAIRS = B_KV_HEADS // 2


def _split_rhs(col):
    lo = lax.broadcasted_iota(jnp.int32, (1, LANES), 1) < B_DH
    return jnp.concatenate([jnp.where(lo, col, 0.0), jnp.where(lo, 0.0, col)], axis=0).astype(BF16)


def _attn_sample_kernel(q_ref, km_ref, vm_ref, ko_ref, vo_ref, kw_ref, vw_ref, sink_ref, o_ref,
                        bias_ref, s_ref, p_ref):
    W = WINDOW
    NK = SAMPLE_KEYS
    R = SAMPLE_ROWS
    off_w = N_META + SAMPLE_OLD
    lo = lax.broadcasted_iota(jnp.int32, (1, LANES), 1) < B_DH

    @pl.when(pl.program_id(0) == 0)
    def _():
        row = lax.broadcasted_iota(jnp.int32, (R, 1), 0)
        r_in_group = jnp.bitwise_and(row, B_GROUP - 1)
        t = jnp.right_shift(lax.broadcasted_iota(jnp.int32, (R, NK), 0), GROUP_SHIFT)
        c = lax.broadcasted_iota(jnp.int32, (R, NK), 1)
        is_meta = c < N_META
        is_old = (c >= N_META) & (c < off_w)
        is_win = (c >= off_w) & (c < off_w + W)
        j_old = c - N_META
        d_win = (W - DEC_SEQ) + t - (c - off_w)
        dist = jnp.where(is_meta, W, jnp.where(is_old, W + t - j_old, d_win)).astype(F32)
        valid = is_meta | (is_old & (j_old > t) & (j_old < DEC_SEQ)) | (is_win & (d_win >= 0))
        for pair in range(KV_PAIRS):
            for e in range(2):
                slope = jnp.zeros((R, 1), F32)
                sink = jnp.zeros((R, 1), F32)
                for r in range(B_GROUP):
                    hd = (2 * pair + e) * B_GROUP + r
                    slope = jnp.where(r_in_group == r, _alibi_slope(hd), slope)
                    sink = jnp.where(r_in_group == r, sink_ref[0:1, hd:hd + 1], sink)
                table = jnp.where(c == NK - 1, sink, jnp.where(valid, -slope * dist, NEG_INF))
                for sq in range(SAMPLE_SEQ_PER_STEP):
                    r0 = (sq * KV_PAIRS + pair) * R
                    bias_ref[r0:r0 + R, e * NK:(e + 1) * NK] = table

    pad = jnp.zeros((NK - off_w - W, B_KV), F32)
    ones_rhs = jnp.concatenate([jnp.broadcast_to(jnp.where(lo, 1.0, 0.0), (NK, LANES)),
                                jnp.broadcast_to(jnp.where(lo, 0.0, 1.0), (NK, LANES))], axis=0).astype(BF16)
    v_rhs = []
    for sq in range(SAMPLE_SEQ_PER_STEP):
        kcat = jnp.concatenate([km_ref[sq], ko_ref[sq], kw_ref[sq], pad], axis=0)
        vcat = jnp.concatenate([vm_ref[sq], vo_ref[sq], vw_ref[sq], pad], axis=0)
        for pair in range(KV_PAIRS):
            r0 = (sq * KV_PAIRS + pair) * R
            k_rhs = _split_rhs(kcat[:, pair * LANES:(pair + 1) * LANES])
            v_rhs.append(_split_rhs(vcat[:, pair * LANES:(pair + 1) * LANES]))
            q = (q_ref[sq, pair] * (B_DH ** -0.5)).astype(BF16)
            s_ref[r0:r0 + R, :] = _dot_nt(q, k_rhs)

    s = s_ref[...] + bias_ref[...]
    halves = []
    for e in range(2):
        sh = s[:, e * NK:(e + 1) * NK]
        halves.append(jnp.exp(sh - jnp.max(sh, axis=-1, keepdims=True)).astype(BF16))
    p_ref[...] = jnp.concatenate(halves, axis=1)

    for sq in range(SAMPLE_SEQ_PER_STEP):
        for pair in range(KV_PAIRS):
            b = sq * KV_PAIRS + pair
            od = _dot(p_ref[b * R:(b + 1) * R, :], jnp.concatenate([v_rhs[b], ones_rhs], axis=1))
            o_ref[sq, pair] = od[:, :LANES] / od[:, LANES:]


def _attn_sample(q4, k_meta, v_meta, k_old, v_old, k_win, v_win, sinks_row):
    nb = SAMPLE_SEQ_PER_STEP
    R = SAMPLE_ROWS
    n_rows = nb * KV_PAIRS * R
    seq3 = lambda rows: pl.BlockSpec((nb, rows, B_KV), lambda i: (i, 0, 0))
    qspec = pl.BlockSpec((nb, KV_PAIRS, R, LANES), lambda i: (i, 0, 0, 0))
    return pl.pallas_call(
        _attn_sample_kernel,
        grid=(DEC_BATCH // nb,),
        in_specs=[qspec, seq3(N_META), seq3(N_META), seq3(SAMPLE_OLD), seq3(SAMPLE_OLD),
                  seq3(WINDOW), seq3(WINDOW), pl.BlockSpec((1, LANES), lambda i: (0, 0))],
        out_specs=qspec,
        out_shape=jax.ShapeDtypeStruct((DEC_BATCH, KV_PAIRS, R, LANES), F32),
        scratch_shapes=[pltpu.VMEM((n_rows, 2 * SAMPLE_KEYS), F32),
                        pltpu.VMEM((n_rows, 2 * SAMPLE_KEYS), F32),
                        pltpu.VMEM((n_rows, 2 * SAMPLE_KEYS), BF16)],
        compiler_params=_params(1),
        name="attn_sample",
    )(q4, k_meta, v_meta, k_old, v_old, k_win, v_win, sinks_row)


def kernel(x_prompt, x_sample, state_C, state_n, state_m, cache_k_meta, cache_v_meta, cache_k_win, cache_v_win, meta_tokens, ffn_norm, w_ffn_in, w_ffn_out, mix_norm, w_a_in, b_a_gate, a_head_norm, w_a_out, kv_norm, w_kv, k_norm, w_q, q_norm, sinks, w_b_out):
    assert x_prompt.shape == (BATCH, SEQ, D_MODEL) and x_sample.shape == (DEC_BATCH, DEC_SEQ, D_MODEL)
    assert w_a_in.shape[0] == 1 and w_q.shape[0] == 1 and ffn_norm.shape[0] == 2

    wa_in_t = jnp.swapaxes(w_a_in[0], 0, 1)
    ba_gate = jnp.pad(b_a_gate[0].astype(F32), (0, LANES - 2 * A_HEADS)).reshape(1, LANES)
    wa_out = w_a_out[0].astype(BF16)
    wkv = w_kv.astype(BF16)
    wq = w_q[0].astype(BF16)
    wb_out = w_b_out[0].astype(BF16)
    row = lambda x: x.astype(F32).reshape(1, -1)
    k_gain = jnp.tile(row(k_norm), (1, B_KV_HEADS))
    q_gain = jnp.tile(row(q_norm[0]), (1, B_HEADS))
    sinks_row = jnp.pad(sinks[0].astype(F32), (0, LANES - B_HEADS)).reshape(1, LANES)

    h_p = x_prompt.reshape(P_ROWS, D_MODEL)
    h_q = jnp.concatenate([x_sample.reshape(S_ROWS, D_MODEL), meta_tokens.astype(F32),
                           jnp.zeros((A_CHUNK - N_META, D_MODEL), F32)], axis=0)
    TM_P, TM_W = 1024, 512

    h_q, *wf = _ffn_cast(h_q, row(ffn_norm[0, 0]), w_ffn_in, w_ffn_out, 0, 0)
    h_p = _ffn(h_p, row(ffn_norm[0, 0]), *wf, TM_P)
    p_q, g_q, wa_in = _inproj(h_q, row(mix_norm[0]), wa_in_t, wa_in_t, ba_gate, Q_ROWS, emit_bf16=True)
    p_p, g_p = _inproj(h_p, row(mix_norm[0]), wa_in, wa_in_t, ba_gate, TM_P)

    zc = jnp.zeros((1, A_HEADS, A_DK, A_DV), F32)
    zn = jnp.zeros((1, A_HEADS, 1, A_DK), F32)
    zm = jnp.zeros((1, A_HEADS, 1, LANES), F32)
    hm_m, c_m, n_m, m_m = _mlstm_chunks(p_q, g_q, zc, zn, zm, 1, 1, META_BLOCK, N_META, True)
    hm_p, c_p, n_p, m_p = _mlstm_chunks(p_p, g_p, c_m, n_m, m_m, BATCH, SEQ // A_CHUNK, 0, A_CHUNK, True)
    m0_s = jnp.broadcast_to(state_m[0].astype(F32)[:, :, None, None], (DEC_BATCH, A_HEADS, 1, LANES))
    hm_s, c_s, n_s, m_s = _mlstm_sample(p_q, g_q, state_C[0].astype(F32),
                                        state_n[0].astype(F32)[:, :, None, :], m0_s)
    hm_q = jnp.concatenate([hm_s, hm_m], axis=0)

    h_p = _mlstm_out(hm_p, p_p, row(a_head_norm[0]), wa_out, h_p, TM_W)
    h_q = _mlstm_out(hm_q, p_q, row(a_head_norm[0]), wa_out, h_q, A_CHUNK)
    h_q, *wf = _ffn_cast(h_q, row(ffn_norm[0, 1]), w_ffn_in, w_ffn_out, 0, 1)
    h_p = _ffn(h_p, row(ffn_norm[0, 1]), *wf, TM_P)

    kv_p = _normproj(h_p, row(kv_norm), wkv, k_gain, TM_P)
    kv_q = _normproj(h_q, row(kv_norm), wkv, k_gain, Q_ROWS)
    k_s = kv_q[:S_ROWS, :B_KV].reshape(DEC_BATCH, DEC_SEQ, B_KV_HEADS, B_DH)
    v_s = kv_q[:S_ROWS, B_KV:].reshape(DEC_BATCH, DEC_SEQ, B_KV_HEADS, B_DH)
    k_win_s = jnp.concatenate([cache_k_win, k_s.astype(cache_k_win.dtype)], axis=1)[:, -WINDOW:]
    v_win_s = jnp.concatenate([cache_v_win, v_s.astype(cache_v_win.dtype)], axis=1)[:, -WINDOW:]

    h_s = h_q[:S_ROWS]
    h_s, *wf = _ffn_cast(h_s, row(ffn_norm[1, 0]), w_ffn_in, w_ffn_out, 1, 0)
    h_p = _ffn(h_p, row(ffn_norm[1, 0]), *wf, TM_P)
    q_p = _normproj(h_p, row(mix_norm[1]), wq, q_gain, TM_W)
    q_s = _normproj(h_s, row(mix_norm[1]), wq, q_gain, TM_W)

    o_p = _attn_prompt(q_p, kv_p, kv_q, sinks_row)
    q4 = q_s.reshape(DEC_BATCH, DEC_SEQ, KV_PAIRS, 2, B_GROUP, B_DH).transpose(0, 2, 1, 4, 3, 5)
    q4 = q4.reshape(DEC_BATCH, KV_PAIRS, SAMPLE_ROWS, LANES)
    seq3 = lambda x: x.astype(F32).reshape(DEC_BATCH, -1, B_KV)
    o4 = _attn_sample(q4, seq3(cache_k_meta), seq3(cache_v_meta),
                      seq3(cache_k_win[:, :SAMPLE_OLD]), seq3(cache_v_win[:, :SAMPLE_OLD]),
                      seq3(k_win_s), seq3(v_win_s), sinks_row)
    o_s = o4.reshape(DEC_BATCH, KV_PAIRS, DEC_SEQ, B_GROUP, 2, B_DH).transpose(0, 2, 1, 4, 3, 5)
    o_s = o_s.reshape(S_ROWS, D_MODEL)

    h_p = _matres(o_p, wb_out, h_p, TM_W)
    h_s = _matres(o_s, wb_out, h_s, TM_W)
    h_s, *wf = _ffn_cast(h_s, row(ffn_norm[1, 1]), w_ffn_in, w_ffn_out, 1, 1)
    h_p = _ffn(h_p, row(ffn_norm[1, 1]), *wf, TM_P)

    kv4 = lambda x: x.reshape(x.shape[:-1] + (B_KV_HEADS, B_DH))
    meta_rows = kv_q[S_ROWS:S_ROWS + N_META]
    kv_p3 = kv_p.reshape(BATCH, SEQ, 2 * B_KV)
    st = lambda x, dt: x[None].astype(dt)
    return (
        h_p.reshape(BATCH, SEQ, D_MODEL),
        h_s.reshape(DEC_BATCH, DEC_SEQ, D_MODEL),
        st(c_p, state_C.dtype), st(n_p[:, :, 0, :], state_n.dtype), st(m_p[:, :, 0, 0], state_m.dtype),
        jnp.broadcast_to(kv4(meta_rows[:, :B_KV])[None], (BATCH, N_META, B_KV_HEADS, B_DH)),
        jnp.broadcast_to(kv4(meta_rows[:, B_KV:])[None], (BATCH, N_META, B_KV_HEADS, B_DH)),
        kv4(kv_p3[:, -WINDOW:, :B_KV]), kv4(kv_p3[:, -WINDOW:, B_KV:]),
        st(c_s, state_C.dtype), st(n_s[:, :, 0, :], state_n.dtype), st(m_s[:, :, 0, 0], state_m.dtype),
        k_win_s, v_win_s,
    )
```

```python
import functools

import jax
import jax.numpy as jnp
from jax import lax
from jax.experimental import pallas as pl
from jax.experimental.pallas import tpu as pltpu

D_MODEL = 2048
BATCH = 8
SEQ = 2048
DEC_BATCH = 128
DEC_SEQ = 4
PAST_LEN = 8192
N_META = 16
A_HEADS = 4
A_DV = D_MODEL // A_HEADS
A_DK = A_DV // 2
A_CHUNK = 128
A_GATE_CAP = 15.0
A_QKVO = 2 * A_HEADS * A_DK + 2 * A_HEADS * A_DV
B_HEADS = 32
B_DH = D_MODEL // B_HEADS
B_KV_HEADS = 4
B_GROUP = B_HEADS // B_KV_HEADS
B_KV = B_KV_HEADS * B_DH
WINDOW = 128
D_FF = ((8 * D_MODEL // 3 + 255) // 256) * 256
EPS = 1e-6

LANES = 128
SUBLANES = 8
VMEM_DEFAULT_MIB = 48
VMEM_FFN_MIB = 60

P_ROWS = BATCH * SEQ
S_ROWS = DEC_BATCH * DEC_SEQ
Q_ROWS = S_ROWS + A_CHUNK
META_BLOCK = S_ROWS // A_CHUNK

F32 = jnp.float32
BF16 = jnp.bfloat16
NEG_INF = float("-inf")


def _params(n_axes, vmem_mib=VMEM_DEFAULT_MIB):
    return pltpu.CompilerParams(dimension_semantics=("arbitrary",) * n_axes,
                                vmem_limit_bytes=vmem_mib * 1024 * 1024)


def _resident(shape):
    return pl.BlockSpec(shape, lambda i: (0, 0), pipeline_mode=pl.Buffered(1))


def _rms(x, g):
    return x * lax.rsqrt(jnp.mean(x * x, axis=-1, keepdims=True) + EPS) * g


def _dot(a, b):
    return jnp.dot(a, b, preferred_element_type=F32)


def _dot_nt(a, b):
    return lax.dot_general(a, b, (((1,), (1,)), ((), ())), preferred_element_type=F32)


def _dot_tn(a, b):
    return lax.dot_general(a, b, (((0,), (0,)), ((), ())), preferred_element_type=F32)


def _log_sigmoid(x):
    return -(jnp.maximum(-x, 0.0) + jnp.log1p(jnp.exp(-jnp.abs(x))))


def _head_norm64(y, gain):
    lo = lax.broadcasted_iota(jnp.int32, (1, LANES), 1) < B_DH
    cols = []
    for c in range(y.shape[1] // LANES):
        x = y[:, c * LANES:(c + 1) * LANES]
        xx = x * x
        s_lo = jnp.sum(jnp.where(lo, xx, 0.0), axis=-1, keepdims=True)
        s_hi = jnp.sum(jnp.where(lo, 0.0, xx), axis=-1, keepdims=True)
        scale = jnp.where(lo, lax.rsqrt(s_lo / B_DH + EPS), lax.rsqrt(s_hi / B_DH + EPS))
        cols.append(x * scale * gain[:, c * LANES:(c + 1) * LANES])
    return jnp.concatenate(cols, axis=1)


def _ffn_kernel(h_ref, g_ref, wg_ref, wu_ref, wo_ref, o_ref, *rest, emit_bf16):
    j = pl.program_id(1)
    if emit_bf16:
        wgb_ref, wub_ref, wob_ref, xn_ref = rest
        wg = wg_ref[...].astype(BF16)
        wu = wu_ref[...].astype(BF16)
        wo = wo_ref[...].astype(BF16)
        wgb_ref[...] = wg
        wub_ref[...] = wu
        wob_ref[...] = wo
    else:
        (xn_ref,) = rest
        wo = wo_ref[...]

    def half_ffn(xn):
        if emit_bf16:
            blocks = [(wg, wu)]
        else:
            blocks = [(wg_ref[b], wu_ref[b]) for b in range(wg_ref.shape[0])]
        acts = []
        for wg_b, wu_b in blocks:
            g = _dot(xn, wg_b)
            u = _dot(xn, wu_b)
            acts.append(((g / (1.0 + jnp.exp(-g))) * (0.5 * u)).astype(BF16))
        return _dot(jnp.concatenate(acts, axis=1), wo)

    @pl.when(j == 0)
    def _():
        h = h_ref[...]
        xn_ref[...] = _rms(h, g_ref[...]).astype(BF16)
        o_ref[...] = h

    o_ref[...] += half_ffn(xn_ref[...])


FFN_CAST_TF = 256


def _ffn(h, gain, wg, wu, wo, tm, tf=512):
    rows = h.shape[0]
    n_ff = D_FF // tf
    sub = tf // FFN_CAST_TF
    return pl.pallas_call(
        functools.partial(_ffn_kernel, emit_bf16=False),
        grid=(rows // tm, n_ff),
        in_specs=[
            pl.BlockSpec((tm, D_MODEL), lambda i, j: (i, 0)),
            pl.BlockSpec((1, D_MODEL), lambda i, j: (0, 0)),
            pl.BlockSpec((sub, D_MODEL, FFN_CAST_TF), lambda i, j: (j, 0, 0)),
            pl.BlockSpec((sub, D_MODEL, FFN_CAST_TF), lambda i, j: (j, 0, 0)),
            pl.BlockSpec((tf, D_MODEL), lambda i, j: (j, 0)),
        ],
        out_specs=pl.BlockSpec((tm, D_MODEL), lambda i, j: (i, 0)),
        out_shape=jax.ShapeDtypeStruct((rows, D_MODEL), F32),
        scratch_shapes=[pltpu.VMEM((tm, D_MODEL), BF16)],
        compiler_params=_params(2, VMEM_FFN_MIB),
        name="ffn",
    )(h, gain, wg, wu, wo)


def _ffn_cast(h, gain, w_in, w_out, layer, which):
    rows = h.shape[0]
    tf = FFN_CAST_TF
    n_ff = D_FF // tf
    return pl.pallas_call(
        functools.partial(_ffn_kernel, emit_bf16=True),
        grid=(1, n_ff),
        in_specs=[
            pl.BlockSpec((rows, D_MODEL), lambda i, j: (0, 0)),
            pl.BlockSpec((1, D_MODEL), lambda i, j: (0, 0)),
            pl.BlockSpec((None, None, D_MODEL, tf), lambda i, j: (layer, which, 0, j)),
            pl.BlockSpec((None, None, D_MODEL, tf), lambda i, j: (layer, which, 0, j + n_ff)),
            pl.BlockSpec((None, None, tf, D_MODEL), lambda i, j: (layer, which, j, 0)),
        ],
        out_specs=[
            pl.BlockSpec((rows, D_MODEL), lambda i, j: (0, 0)),
            pl.BlockSpec((None, D_MODEL, tf), lambda i, j: (j, 0, 0)),
            pl.BlockSpec((None, D_MODEL, tf), lambda i, j: (j, 0, 0)),
            pl.BlockSpec((tf, D_MODEL), lambda i, j: (j, 0)),
        ],
        out_shape=[
            jax.ShapeDtypeStruct((rows, D_MODEL), F32),
            jax.ShapeDtypeStruct((n_ff, D_MODEL, tf), BF16),
            jax.ShapeDtypeStruct((n_ff, D_MODEL, tf), BF16),
            jax.ShapeDtypeStruct((D_FF, D_MODEL), BF16),
        ],
        scratch_shapes=[pltpu.VMEM((rows, D_MODEL), BF16)],
        compiler_params=_params(2),
        name="ffn_cast",
    )(h, gain, w_in, w_in, w_out)


GATE_ROWS = 2 * A_HEADS


def _inproj_kernel(h_ref, g_ref, w_ref, wgate_ref, bgate_ref, p_ref, gates_ref, *rest, emit_bf16):
    j = pl.program_id(1)
    if emit_bf16:
        wb_ref, xn_ref = rest
        w = w_ref[...].astype(BF16)
        wb_ref[...] = w
    else:
        (xn_ref,) = rest
        w = w_ref[...]

    @pl.when(j == 0)
    def _():
        xn = _rms(h_ref[...], g_ref[...]).astype(BF16)
        xn_ref[...] = xn
        wgate = jnp.concatenate([wgate_ref[...], jnp.zeros((LANES - GATE_ROWS, D_MODEL), F32)], axis=0)
        pre = _dot_nt(xn, wgate.astype(BF16)) + bgate_ref[...]
        capped = A_GATE_CAP * jnp.tanh(pre / A_GATE_CAP)
        lane = lax.broadcasted_iota(jnp.int32, (1, LANES), 1)
        gates_ref[...] = jnp.where(lane < A_HEADS, capped, _log_sigmoid(capped))

    p_ref[...] = _dot_nt(xn_ref[...], w)


def _inproj(h, gain, w_t, w_gate_t, bgate, tm, emit_bf16=False, tn=1024):
    rows = h.shape[0]
    assert not emit_bf16 or rows == tm
    out_specs = [
        pl.BlockSpec((tm, tn), lambda i, j: (i, j)),
        pl.BlockSpec((tm, LANES), lambda i, j: (i, 0)),
    ]
    out_shape = [jax.ShapeDtypeStruct((rows, A_QKVO), F32),
                 jax.ShapeDtypeStruct((rows, LANES), F32)]
    if emit_bf16:
        out_specs.append(pl.BlockSpec((tn, D_MODEL), lambda i, j: (j, 0)))
        out_shape.append(jax.ShapeDtypeStruct((A_QKVO, D_MODEL), BF16))
    return pl.pallas_call(
        functools.partial(_inproj_kernel, emit_bf16=emit_bf16),
        grid=(rows // tm, A_QKVO // tn),
        in_specs=[
            pl.BlockSpec((tm, D_MODEL), lambda i, j: (i, 0)),
            pl.BlockSpec((1, D_MODEL), lambda i, j: (0, 0)),
            pl.BlockSpec((tn, D_MODEL), lambda i, j: (j, 0)),
            pl.BlockSpec((GATE_ROWS, D_MODEL), lambda i, j: (A_QKVO // GATE_ROWS, 0)),
            pl.BlockSpec((1, LANES), lambda i, j: (0, 0)),
        ],
        out_specs=out_specs,
        out_shape=out_shape,
        scratch_shapes=[pltpu.VMEM((tm, D_MODEL), BF16)],
        compiler_params=_params(2),
        name="mlstm_inproj",
    )(h, gain, w_t, w_gate_t, bgate)


def _mlstm_chunk_kernel(q_ref, k_ref, v_ref, g_ref, c0_ref, n0_ref, m0_ref,
                        h_ref, c_ref, n_ref, m_ref, qk_ref, qc_ref, sb_ref, *, n_valid):
    L = A_CHUNK
    heads = range(A_HEADS)

    @pl.when(pl.program_id(1) == 0)
    def _():
        c_ref[...] = c0_ref[...]
        n_ref[...] = n0_ref[...]
        m_ref[...] = m0_ref[...]

    for hd in heads:
        qb = q_ref[:, hd * A_DK:(hd + 1) * A_DK].astype(BF16)
        kb = (k_ref[:, hd * A_DK:(hd + 1) * A_DK] * (A_DK ** -0.5)).astype(BF16)
        qk_ref[hd] = _dot_nt(qb, kb)
        qc_ref[hd] = _dot(qb, c_ref[0, hd].astype(BF16))

    gates = g_ref[...]
    row = lax.broadcasted_iota(jnp.int32, (L, L), 0)
    col = lax.broadcasted_iota(jnp.int32, (L, L), 1)
    causal = col <= row
    eye = col == row
    masked = n_valid < L
    if masked:
        row_ok = lax.broadcasted_iota(jnp.int32, (L, 1), 0) < n_valid
        gates_lf = jnp.where(row_ok, gates, 0.0)
    else:
        gates_lf = gates
    csum = jnp.dot(causal.astype(F32), gates_lf, precision=lax.Precision.HIGHEST,
                   preferred_element_type=F32)

    def to_row(x_col):
        return jnp.sum(jnp.where(eye, x_col, 0.0), axis=0, keepdims=True)

    w_intra, w_inter, floor, w_state, decay, m_new = [], [], [], [], [], []
    for hd in heads:
        b_col = csum[:, A_HEADS + hd:A_HEADS + hd + 1]
        ig_col = gates[:, hd:hd + 1]
        if masked:
            ig_col = jnp.where(row_ok, ig_col, NEG_INF)
        b_row = to_row(b_col)
        ig_row = to_row(ig_col)
        m_prev = m_ref[0, hd][:, 0:1]
        d_log = jnp.where(causal, b_col - b_row + ig_row, NEG_INF)
        inter_log = b_col + m_prev
        m_t = jnp.maximum(inter_log, jnp.max(d_log, axis=-1, keepdims=True))
        w_intra.append(jnp.exp(d_log - m_t))
        w_inter.append(jnp.exp(inter_log - m_t))
        floor.append(jnp.exp(-m_t))
        b_last = b_col[L - 1:L, :]
        w_log = b_last - b_col + ig_col
        m_new.append(jnp.maximum(b_last + m_prev, jnp.max(w_log, axis=0, keepdims=True)))
        w_state.append(jnp.exp(w_log - m_new[hd]))
        decay.append(jnp.exp(b_last + m_prev - m_new[hd]))

    den = []
    for hd in heads:
        s = qk_ref[hd] * w_intra[hd]
        sb_ref[hd] = s.astype(BF16)
        q = q_ref[:, hd * A_DK:(hd + 1) * A_DK]
        d = jnp.sum(s, axis=-1, keepdims=True) + w_inter[hd] * jnp.sum(q * n_ref[0, hd], axis=-1, keepdims=True)
        den.append(jnp.maximum(jnp.abs(d), floor[hd]))

    for hd in heads:
        k = k_ref[:, hd * A_DK:(hd + 1) * A_DK] * (A_DK ** -0.5)
        vb = v_ref[:, hd * A_DV:(hd + 1) * A_DV].astype(BF16)
        num = _dot(sb_ref[hd], vb) + w_inter[hd] * qc_ref[hd]
        h_ref[:, hd * A_DV:(hd + 1) * A_DV] = num / den[hd]
        kw = k * w_state[hd]
        c_ref[0, hd] = decay[hd] * c_ref[0, hd] + _dot_tn(kw.astype(BF16), vb)
        n_ref[0, hd] = decay[hd] * n_ref[0, hd] + jnp.sum(kw, axis=0, keepdims=True)
        m_ref[0, hd] = jnp.broadcast_to(m_new[hd], (1, LANES))


def _mlstm_chunks(p, gates, c0, n0, m0, n_seq, n_chunks, row_block0, n_valid, shared_state):
    L = A_CHUNK
    rb = lambda b, c: row_block0 + b * n_chunks + c
    st = (lambda b, c: (0, 0, 0, 0)) if shared_state else (lambda b, c: (b, 0, 0, 0))
    return pl.pallas_call(
        functools.partial(_mlstm_chunk_kernel, n_valid=n_valid),
        grid=(n_seq, n_chunks),
        in_specs=[
            pl.BlockSpec((L, A_HEADS * A_DK), lambda b, c: (rb(b, c), 0)),
            pl.BlockSpec((L, A_HEADS * A_DK), lambda b, c: (rb(b, c), 1)),
            pl.BlockSpec((L, A_HEADS * A_DV), lambda b, c: (rb(b, c), 1)),
            pl.BlockSpec((L, LANES), lambda b, c: (rb(b, c), 0)),
            pl.BlockSpec((1, A_HEADS, A_DK, A_DV), st),
            pl.BlockSpec((1, A_HEADS, 1, A_DK), st),
            pl.BlockSpec((1, A_HEADS, 1, LANES), st),
        ],
        out_specs=[
            pl.BlockSpec((L, A_HEADS * A_DV), lambda b, c: (b * n_chunks + c, 0)),
            pl.BlockSpec((1, A_HEADS, A_DK, A_DV), lambda b, c: (b, 0, 0, 0)),
            pl.BlockSpec((1, A_HEADS, 1, A_DK), lambda b, c: (b, 0, 0, 0)),
            pl.BlockSpec((1, A_HEADS, 1, LANES), lambda b, c: (b, 0, 0, 0)),
        ],
        out_shape=[
            jax.ShapeDtypeStruct((n_seq * n_chunks * L, D_MODEL), F32),
            jax.ShapeDtypeStruct((n_seq, A_HEADS, A_DK, A_DV), F32),
            jax.ShapeDtypeStruct((n_seq, A_HEADS, 1, A_DK), F32),
            jax.ShapeDtypeStruct((n_seq, A_HEADS, 1, LANES), F32),
        ],
        scratch_shapes=[pltpu.VMEM((A_HEADS, L, L), F32),
                        pltpu.VMEM((A_HEADS, L, A_DV), F32),
                        pltpu.VMEM((A_HEADS, L, L), BF16)],
        compiler_params=_params(2),
        name="mlstm_chunks",
    )(p, p, p, gates, c0, n0, m0)


def _mlstm_sample_kernel(q_ref, k_ref, v_ref, g_ref, c0_ref, n0_ref, m0_ref,
                         h_ref, c_ref, n_ref, m_ref):
    R = 2 * DEC_SEQ
    PAD = A_CHUNK - R
    gates = g_ref[...]
    r_col = lax.broadcasted_iota(jnp.int32, (R, 1), 0)
    is_a = r_col < DEC_SEQ
    row = lax.broadcasted_iota(jnp.int32, (R, LANES), 0)
    lane = lax.broadcasted_iota(jnp.int32, (R, LANES), 1)
    same = ((lane < DEC_SEQ) & (row < DEC_SEQ)) | ((lane >= DEC_SEQ) & (lane < R) & (row >= DEC_SEQ))
    causal = same & (lane <= row)
    eye = lane == row

    def to_row(x_col):
        return jnp.sum(jnp.where(eye, x_col, 0.0), axis=0, keepdims=True)

    for hd in range(A_HEADS):
        lf_col = gates[:, A_HEADS + hd:A_HEADS + hd + 1]
        ig_col = gates[:, hd:hd + 1]
        lf_row = to_row(lf_col)
        ig_row = to_row(ig_col)
        b_col = jnp.sum(jnp.where(causal, lf_row, 0.0), axis=1, keepdims=True)
        b_row = to_row(b_col)
        m_a = m0_ref[0, hd][:, 0:1]
        m_b = m0_ref[1, hd][:, 0:1]
        m_prev = jnp.where(is_a, m_a, m_b)
        c_a = c0_ref[0, hd]
        c_b = c0_ref[1, hd]
        n_a = n0_ref[0, hd]
        n_b = n0_ref[1, hd]

        q = q_ref[:, hd * A_DK:(hd + 1) * A_DK]
        k = k_ref[:, hd * A_DK:(hd + 1) * A_DK] * (A_DK ** -0.5)
        v = v_ref[:, hd * A_DV:(hd + 1) * A_DV]
        qb = q.astype(BF16)
        k_pad = jnp.concatenate([k, jnp.zeros((PAD, A_DK), F32)], axis=0).astype(BF16)
        v_pad = jnp.concatenate([v, jnp.zeros((PAD, A_DV), F32)], axis=0).astype(BF16)

        d_log = jnp.where(causal, b_col - b_row + ig_row, NEG_INF)
        inter_log = b_col + m_prev
        m_t = jnp.maximum(inter_log, jnp.max(d_log, axis=-1, keepdims=True))
        w_intra = jnp.exp(d_log - m_t)
        w_inter = jnp.exp(inter_log - m_t)
        s = _dot_nt(qb, k_pad) * w_intra
        q_c = jnp.where(is_a, _dot(qb, c_a.astype(BF16)), _dot(qb, c_b.astype(BF16)))
        num = _dot(s.astype(BF16), v_pad) + w_inter * q_c
        q_n = jnp.sum(q * jnp.where(is_a, n_a, n_b), axis=-1, keepdims=True)
        den = jnp.sum(s, axis=-1, keepdims=True) + w_inter * q_n
        den = jnp.maximum(jnp.abs(den), jnp.exp(-m_t))
        h_ref[:, hd * A_DV:(hd + 1) * A_DV] = num / den

        for idx, sel, m_x, c_x, n_x in ((0, is_a, m_a, c_a, n_a),
                                        (1, jnp.logical_not(is_a), m_b, c_b, n_b)):
            last = (idx + 1) * DEC_SEQ - 1
            b_last = b_col[last:last + 1, :]
            w_log = jnp.where(sel, b_last - b_col + ig_col, NEG_INF)
            m_new = jnp.maximum(b_last + m_x, jnp.max(w_log, axis=0, keepdims=True))
            w_state = jnp.exp(w_log - m_new)
            decay = jnp.exp(b_last + m_x - m_new)
            kw = k * w_state
            kw_pad = jnp.concatenate([kw, jnp.zeros((PAD, A_DK), F32)], axis=0).astype(BF16)
            c_ref[idx, hd] = decay * c_x + _dot_tn(kw_pad, v_pad)
            n_ref[idx, hd] = decay * n_x + jnp.sum(kw, axis=0, keepdims=True)
            m_ref[idx, hd] = jnp.broadcast_to(m_new, (1, LANES))


def _mlstm_sample(p, gates, c0, n0, m0):
    R = 2 * DEC_SEQ
    n_pairs = DEC_BATCH // 2
    st = lambda i: (i, 0, 0, 0)
    return pl.pallas_call(
        _mlstm_sample_kernel,
        grid=(n_pairs,),
        in_specs=[
            pl.BlockSpec((R, A_HEADS * A_DK), lambda i: (i, 0)),
            pl.BlockSpec((R, A_HEADS * A_DK), lambda i: (i, 1)),
            pl.BlockSpec((R, A_HEADS * A_DV), lambda i: (i, 1)),
            pl.BlockSpec((R, LANES), lambda i: (i, 0)),
            pl.BlockSpec((2, A_HEADS, A_DK, A_DV), st),
            pl.BlockSpec((2, A_HEADS, 1, A_DK), st),
            pl.BlockSpec((2, A_HEADS, 1, LANES), st),
        ],
        out_specs=[
            pl.BlockSpec((R, A_HEADS * A_DV), lambda i: (i, 0)),
            pl.BlockSpec((2, A_HEADS, A_DK, A_DV), st),
            pl.BlockSpec((2, A_HEADS, 1, A_DK), st),
            pl.BlockSpec((2, A_HEADS, 1, LANES), st),
        ],
        out_shape=[
            jax.ShapeDtypeStruct((S_ROWS, D_MODEL), F32),
            jax.ShapeDtypeStruct((DEC_BATCH, A_HEADS, A_DK, A_DV), F32),
            jax.ShapeDtypeStruct((DEC_BATCH, A_HEADS, 1, A_DK), F32),
            jax.ShapeDtypeStruct((DEC_BATCH, A_HEADS, 1, LANES), F32),
        ],
        compiler_params=_params(1),
        name="mlstm_sample",
    )(p, p, p, gates, c0, n0, m0)


def _mlstm_out_kernel(hm_ref, o_ref, hg_ref, w_ref, res_ref, out_ref):
    cols = []
    for hd in range(A_HEADS):
        x = hm_ref[:, hd * A_DV:(hd + 1) * A_DV]
        cols.append(x * lax.rsqrt(jnp.mean(x * x, axis=-1, keepdims=True) + EPS))
    hn = jnp.concatenate(cols, axis=1) * hg_ref[...]
    o = o_ref[...]
    pre = (hn * (1.0 / (1.0 + jnp.exp(-o)))).astype(BF16)
    out_ref[...] = res_ref[...] + _dot(pre, w_ref[...])


def _mlstm_out(hm, p, head_gain, w, res, tm):
    rows = hm.shape[0]
    return pl.pallas_call(
        _mlstm_out_kernel,
        grid=(rows // tm,),
        in_specs=[
            pl.BlockSpec((tm, D_MODEL), lambda i: (i, 0)),
            pl.BlockSpec((tm, D_MODEL), lambda i: (i, 2)),
            pl.BlockSpec((1, D_MODEL), lambda i: (0, 0)),
            _resident((D_MODEL, D_MODEL)),
            pl.BlockSpec((tm, D_MODEL), lambda i: (i, 0)),
        ],
        out_specs=pl.BlockSpec((tm, D_MODEL), lambda i: (i, 0)),
        out_shape=jax.ShapeDtypeStruct((rows, D_MODEL), F32),
        compiler_params=_params(1),
        name="mlstm_out",
    )(hm, p, head_gain, w, res)


def _normproj_kernel(h_ref, g_ref, w_ref, hg_ref, o_ref, *, n_norm):
    xn = _rms(h_ref[...], g_ref[...]).astype(BF16)
    y = _dot(xn, w_ref[...])
    if n_norm == y.shape[1]:
        o_ref[...] = _head_norm64(y, hg_ref[...])
    else:
        o_ref[:, :n_norm] = _head_norm64(y[:, :n_norm], hg_ref[...])
        o_ref[:, n_norm:] = y[:, n_norm:]


def _normproj(h, gain, w, head_gain, tm):
    rows = h.shape[0]
    n = w.shape[1]
    n_norm = head_gain.shape[1]
    return pl.pallas_call(
        functools.partial(_normproj_kernel, n_norm=n_norm),
        grid=(rows // tm,),
        in_specs=[
            pl.BlockSpec((tm, D_MODEL), lambda i: (i, 0)),
            pl.BlockSpec((1, D_MODEL), lambda i: (0, 0)),
            _resident((D_MODEL, n)),
            pl.BlockSpec((1, n_norm), lambda i: (0, 0)),
        ],
        out_specs=pl.BlockSpec((tm, n), lambda i: (i, 0)),
        out_shape=jax.ShapeDtypeStruct((rows, n), F32),
        compiler_params=_params(1),
        name="normproj",
    )(h, gain, w, head_gain)


def _matres_kernel(x_ref, w_ref, res_ref, o_ref):
    o_ref[...] = res_ref[...] + _dot(x_ref[...].astype(BF16), w_ref[...])


def _matres(x, w, res, tm):
    rows = x.shape[0]
    return pl.pallas_call(
        _matres_kernel,
        grid=(rows // tm,),
        in_specs=[
            pl.BlockSpec((tm, D_MODEL), lambda i: (i, 0)),
            _resident((D_MODEL, D_MODEL)),
            pl.BlockSpec((tm, D_MODEL), lambda i: (i, 0)),
        ],
        out_specs=pl.BlockSpec((tm, D_MODEL), lambda i: (i, 0)),
        out_shape=jax.ShapeDtypeStruct((rows, D_MODEL), F32),
        compiler_params=_params(1),
        name="matres",
    )(x, w, res)


def _alibi_slope(head):
    return 2.0 ** (-8.0 * (head + 1) / B_HEADS)


PROMPT_SUB = WINDOW // 2
PROMPT_BAND = WINDOW + PROMPT_SUB
PROMPT_KEYS = 2 * LANES
HEAD_PAIRS = B_HEADS // 2
PAIRS_PER_GROUP = B_GROUP // 2


def _pair_rhs(x, c):
    col = x[:, c * LANES:(c + 1) * LANES]
    rol = pltpu.roll(col, B_DH, axis=1)
    lo = lax.broadcasted_iota(jnp.int32, (1, LANES), 1) < B_DH
    even = jnp.concatenate([jnp.where(lo, col, 0.0), jnp.where(lo, 0.0, rol)], axis=0)
    odd = jnp.concatenate([jnp.where(lo, rol, 0.0), jnp.where(lo, 0.0, col)], axis=0)
    return even.astype(BF16), odd.astype(BF16)


def _attn_prompt_kernel(q_ref, kvo_ref, kvp_ref, kvm_ref, sink_ref, o_ref, bias_ref, s_ref, p_ref):
    W = WINDOW
    SB = PROMPT_SUB
    NK = PROMPT_KEYS
    first = (pl.program_id(0) == 0) & (pl.program_id(1) == 0)
    j = pl.program_id(1)
    rows = PAIRS_PER_GROUP * SB
    lo = lax.broadcasted_iota(jnp.int32, (1, LANES), 1) < B_DH

    @pl.when(first)
    def _():
        c = lax.broadcasted_iota(jnp.int32, (SB, NK), 1)
        is_meta = c < N_META
        is_sink = c == NK - 1
        for sub in range(2):
            i = lax.broadcasted_iota(jnp.int32, (SB, NK), 0) + sub * SB
            pos = c - N_META + sub * SB
            rel = W + i - pos
            in_band = (c >= N_META) & (c < N_META + PROMPT_BAND) & (rel >= 0) & (rel < W)
            for variant in range(2):
                if variant == 0:
                    dist = jnp.where(is_meta, jnp.minimum(i + N_META - c, W), rel).astype(F32)
                    valid = is_meta | (in_band & (pos >= W))
                else:
                    dist = jnp.where(is_meta, W, rel).astype(F32)
                    valid = is_meta | in_band
                for hd in range(B_HEADS):
                    r0 = (hd // 2) * SB
                    c0 = (hd % 2) * NK
                    table = jnp.where(valid, -_alibi_slope(hd) * dist, NEG_INF)
                    bias_ref[variant, sub, r0:r0 + SB, c0:c0 + NK] = jnp.where(
                        is_sink, sink_ref[0:1, hd:hd + 1], table)

    variant = jnp.minimum(j, 1)
    kvm = kvm_ref[...]
    band = jnp.concatenate([kvp_ref[...], kvo_ref[...]], axis=0)
    pad = jnp.zeros((NK - N_META - PROMPT_BAND, 2 * B_KV), F32)
    ones_rhs = jnp.concatenate([jnp.broadcast_to(jnp.where(lo, 1.0, 0.0), (NK, LANES)),
                                jnp.broadcast_to(jnp.where(lo, 0.0, 1.0), (NK, LANES))], axis=0).astype(BF16)

    sub_rows = B_KV_HEADS * rows
    for sub in range(2):
        keys = jnp.concatenate([kvm, band[sub * SB:sub * SB + PROMPT_BAND], pad], axis=0)
        k_rhs = _pair_rhs(keys, 0) + _pair_rhs(keys, 1)
        v_rhs = _pair_rhs(keys, 2) + _pair_rhs(keys, 3)
        for g in range(B_KV_HEADS):
            p0 = g * PAIRS_PER_GROUP
            r0 = (sub * B_KV_HEADS + g) * rows
            qg = jnp.concatenate([q_ref[sub * SB:(sub + 1) * SB, (p0 + m) * LANES:(p0 + m + 1) * LANES]
                                  for m in range(PAIRS_PER_GROUP)], axis=0) * (B_DH ** -0.5)
            s_ref[r0:r0 + rows, :] = _dot_nt(qg.astype(BF16), k_rhs[g])

        s = s_ref[sub * sub_rows:(sub + 1) * sub_rows, :] + bias_ref[variant, sub]
        halves = []
        for half in range(2):
            sh = s[:, half * NK:(half + 1) * NK]
            halves.append(jnp.exp(sh - jnp.max(sh, axis=-1, keepdims=True)).astype(BF16))
        p_ref[sub * sub_rows:(sub + 1) * sub_rows, :] = jnp.concatenate(halves, axis=1)

        for g in range(B_KV_HEADS):
            p0 = g * PAIRS_PER_GROUP
            r0 = (sub * B_KV_HEADS + g) * rows
            od = _dot(p_ref[r0:r0 + rows, :], jnp.concatenate([v_rhs[g], ones_rhs], axis=1))
            o = od[:, :LANES] / od[:, LANES:]
            for m in range(PAIRS_PER_GROUP):
                o_ref[sub * SB:(sub + 1) * SB, (p0 + m) * LANES:(p0 + m + 1) * LANES] = o[m * SB:(m + 1) * SB, :]


def _attn_prompt(q, kv_p, kv_q, sinks_row):
    nb = SEQ // WINDOW
    blk = lambda b, j: b * nb + j
    return pl.pallas_call(
        _attn_prompt_kernel,
        grid=(BATCH, nb),
        in_specs=[
            pl.BlockSpec((WINDOW, D_MODEL), lambda b, j: (blk(b, j), 0)),
            pl.BlockSpec((WINDOW, 2 * B_KV), lambda b, j: (blk(b, j), 0)),
            pl.BlockSpec((WINDOW, 2 * B_KV), lambda b, j: (blk(b, jnp.maximum(j - 1, 0)), 0)),
            pl.BlockSpec((N_META, 2 * B_KV), lambda b, j: (S_ROWS // N_META, 0)),
            pl.BlockSpec((1, LANES), lambda b, j: (0, 0)),
        ],
        out_specs=pl.BlockSpec((WINDOW, D_MODEL), lambda b, j: (blk(b, j), 0)),
        out_shape=jax.ShapeDtypeStruct((P_ROWS, D_MODEL), F32),
        scratch_shapes=[pltpu.VMEM((2, 2, HEAD_PAIRS * PROMPT_SUB, 2 * PROMPT_KEYS), F32),
                        pltpu.VMEM((2 * HEAD_PAIRS * PROMPT_SUB, 2 * PROMPT_KEYS), F32),
                        pltpu.VMEM((2 * HEAD_PAIRS * PROMPT_SUB, 2 * PROMPT_KEYS), BF16)],
        compiler_params=_params(2),
        name="attn_prompt",
    )(q, kv_p, kv_p, kv_q, sinks_row)


GROUP_SHIFT = B_GROUP.bit_length() - 1
assert 1 << GROUP_SHIFT == B_GROUP
SAMPLE_OLD = SUBLANES
SAMPLE_KEYS = 2 * LANES
SAMPLE_SEQ_PER_STEP = 8
SAMPLE_ROWS = DEC_SEQ * B_GROUP
KV_PAIRS = B_KV_HEADS // 2


def _split_rhs(col):
    lo = lax.broadcasted_iota(jnp.int32, (1, LANES), 1) < B_DH
    return jnp.concatenate([jnp.where(lo, col, 0.0), jnp.where(lo, 0.0, col)], axis=0).astype(BF16)


def _attn_sample_kernel(q_ref, km_ref, vm_ref, ko_ref, vo_ref, kw_ref, vw_ref, sink_ref, o_ref,
                        bias_ref, s_ref, p_ref):
    W = WINDOW
    NK = SAMPLE_KEYS
    R = SAMPLE_ROWS
    off_w = N_META + SAMPLE_OLD
    lo = lax.broadcasted_iota(jnp.int32, (1, LANES), 1) < B_DH

    @pl.when(pl.program_id(0) == 0)
    def _():
        row = lax.broadcasted_iota(jnp.int32, (R, 1), 0)
        r_in_group = jnp.bitwise_and(row, B_GROUP - 1)
        t = jnp.right_shift(lax.broadcasted_iota(jnp.int32, (R, NK), 0), GROUP_SHIFT)
        c = lax.broadcasted_iota(jnp.int32, (R, NK), 1)
        is_meta = c < N_META
        is_old = (c >= N_META) & (c < off_w)
        is_win = (c >= off_w) & (c < off_w + W)
        j_old = c - N_META
        d_win = (W - DEC_SEQ) + t - (c - off_w)
        dist = jnp.where(is_meta, W, jnp.where(is_old, W + t - j_old, d_win)).astype(F32)
        valid = is_meta | (is_old & (j_old > t) & (j_old < DEC_SEQ)) | (is_win & (d_win >= 0))
        for pair in range(KV_PAIRS):
            for e in range(2):
                slope = jnp.zeros((R, 1), F32)
                sink = jnp.zeros((R, 1), F32)
                for r in range(B_GROUP):
                    hd = (2 * pair + e) * B_GROUP + r
                    slope = jnp.where(r_in_group == r, _alibi_slope(hd), slope)
                    sink = jnp.where(r_in_group == r, sink_ref[0:1, hd:hd + 1], sink)
                table = jnp.where(c == NK - 1, sink, jnp.where(valid, -slope * dist, NEG_INF))
                for sq in range(SAMPLE_SEQ_PER_STEP):
                    r0 = (sq * KV_PAIRS + pair) * R
                    bias_ref[r0:r0 + R, e * NK:(e + 1) * NK] = table

    pad = jnp.zeros((NK - off_w - W, B_KV), F32)
    ones_rhs = jnp.concatenate([jnp.broadcast_to(jnp.where(lo, 1.0, 0.0), (NK, LANES)),
                                jnp.broadcast_to(jnp.where(lo, 0.0, 1.0), (NK, LANES))], axis=0).astype(BF16)
    v_rhs = []
    for sq in range(SAMPLE_SEQ_PER_STEP):
        kcat = jnp.concatenate([km_ref[sq], ko_ref[sq], kw_ref[sq], pad], axis=0)
        vcat = jnp.concatenate([vm_ref[sq], vo_ref[sq], vw_ref[sq], pad], axis=0)
        for pair in range(KV_PAIRS):
            r0 = (sq * KV_PAIRS + pair) * R
            k_rhs = _split_rhs(kcat[:, pair * LANES:(pair + 1) * LANES])
            v_rhs.append(_split_rhs(vcat[:, pair * LANES:(pair + 1) * LANES]))
            q = (q_ref[sq, pair] * (B_DH ** -0.5)).astype(BF16)
            s_ref[r0:r0 + R, :] = _dot_nt(q, k_rhs)

    s = s_ref[...] + bias_ref[...]
    halves = []
    for e in range(2):
        sh = s[:, e * NK:(e + 1) * NK]
        halves.append(jnp.exp(sh - jnp.max(sh, axis=-1, keepdims=True)).astype(BF16))
    p_ref[...] = jnp.concatenate(halves, axis=1)

    for sq in range(SAMPLE_SEQ_PER_STEP):
        for pair in range(KV_PAIRS):
            b = sq * KV_PAIRS + pair
            od = _dot(p_ref[b * R:(b + 1) * R, :], jnp.concatenate([v_rhs[b], ones_rhs], axis=1))
            o_ref[sq, pair] = od[:, :LANES] / od[:, LANES:]


def _attn_sample(q4, k_meta, v_meta, k_old, v_old, k_win, v_win, sinks_row):
    nb = SAMPLE_SEQ_PER_STEP
    R = SAMPLE_ROWS
    n_rows = nb * KV_PAIRS * R
    seq3 = lambda rows: pl.BlockSpec((nb, rows, B_KV), lambda i: (i, 0, 0))
    qspec = pl.BlockSpec((nb, KV_PAIRS, R, LANES), lambda i: (i, 0, 0, 0))
    return pl.pallas_call(
        _attn_sample_kernel,
        grid=(DEC_BATCH // nb,),
        in_specs=[qspec, seq3(N_META), seq3(N_META), seq3(SAMPLE_OLD), seq3(SAMPLE_OLD),
                  seq3(WINDOW), seq3(WINDOW), pl.BlockSpec((1, LANES), lambda i: (0, 0))],
        out_specs=qspec,
        out_shape=jax.ShapeDtypeStruct((DEC_BATCH, KV_PAIRS, R, LANES), F32),
        scratch_shapes=[pltpu.VMEM((n_rows, 2 * SAMPLE_KEYS), F32),
                        pltpu.VMEM((n_rows, 2 * SAMPLE_KEYS), F32),
                        pltpu.VMEM((n_rows, 2 * SAMPLE_KEYS), BF16)],
        compiler_params=_params(1),
        name="attn_sample",
    )(q4, k_meta, v_meta, k_old, v_old, k_win, v_win, sinks_row)


def kernel(x_prompt, x_sample, state_C, state_n, state_m, cache_k_meta, cache_v_meta, cache_k_win, cache_v_win, meta_tokens, ffn_norm, w_ffn_in, w_ffn_out, mix_norm, w_a_in, b_a_gate, a_head_norm, w_a_out, kv_norm, w_kv, k_norm, w_q, q_norm, sinks, w_b_out):
    assert x_prompt.shape == (BATCH, SEQ, D_MODEL) and x_sample.shape == (DEC_BATCH, DEC_SEQ, D_MODEL)
    assert w_a_in.shape[0] == 1 and w_q.shape[0] == 1 and ffn_norm.shape[0] == 2

    wa_in_t = jnp.swapaxes(w_a_in[0], 0, 1)
    ba_gate = jnp.pad(b_a_gate[0].astype(F32), (0, LANES - 2 * A_HEADS)).reshape(1, LANES)
    wa_out = w_a_out[0].astype(BF16)
    wkv = w_kv.astype(BF16)
    wq = w_q[0].astype(BF16)
    wb_out = w_b_out[0].astype(BF16)
    row = lambda x: x.astype(F32).reshape(1, -1)
    k_gain = jnp.tile(row(k_norm), (1, B_KV_HEADS))
    q_gain = jnp.tile(row(q_norm[0]), (1, B_HEADS))
    sinks_row = jnp.pad(sinks[0].astype(F32), (0, LANES - B_HEADS)).reshape(1, LANES)

    h_p = x_prompt.reshape(P_ROWS, D_MODEL)
    h_q = jnp.concatenate([x_sample.reshape(S_ROWS, D_MODEL), meta_tokens.astype(F32),
                           jnp.zeros((A_CHUNK - N_META, D_MODEL), F32)], axis=0)
    TM_P, TM_W = 1024, 512

    h_q, *wf = _ffn_cast(h_q, row(ffn_norm[0, 0]), w_ffn_in, w_ffn_out, 0, 0)
    h_p = _ffn(h_p, row(ffn_norm[0, 0]), *wf, TM_P)
    p_q, g_q, wa_in = _inproj(h_q, row(mix_norm[0]), wa_in_t, wa_in_t, ba_gate, Q_ROWS, emit_bf16=True)
    p_p, g_p = _inproj(h_p, row(mix_norm[0]), wa_in, wa_in_t, ba_gate, TM_P)

    zc = jnp.zeros((1, A_HEADS, A_DK, A_DV), F32)
    zn = jnp.zeros((1, A_HEADS, 1, A_DK), F32)
    zm = jnp.zeros((1, A_HEADS, 1, LANES), F32)
    hm_m, c_m, n_m, m_m = _mlstm_chunks(p_q, g_q, zc, zn, zm, 1, 1, META_BLOCK, N_META, True)
    hm_p, c_p, n_p, m_p = _mlstm_chunks(p_p, g_p, c_m, n_m, m_m, BATCH, SEQ // A_CHUNK, 0, A_CHUNK, True)
    m0_s = jnp.broadcast_to(state_m[0].astype(F32)[:, :, None, None], (DEC_BATCH, A_HEADS, 1, LANES))
    hm_s, c_s, n_s, m_s = _mlstm_sample(p_q, g_q, state_C[0].astype(F32),
                                        state_n[0].astype(F32)[:, :, None, :], m0_s)
    hm_q = jnp.concatenate([hm_s, hm_m], axis=0)

    h_p = _mlstm_out(hm_p, p_p, row(a_head_norm[0]), wa_out, h_p, TM_W)
    h_q = _mlstm_out(hm_q, p_q, row(a_head_norm[0]), wa_out, h_q, A_CHUNK)
    h_q, *wf = _ffn_cast(h_q, row(ffn_norm[0, 1]), w_ffn_in, w_ffn_out, 0, 1)
    h_p = _ffn(h_p, row(ffn_norm[0, 1]), *wf, TM_P)

    kv_p = _normproj(h_p, row(kv_norm), wkv, k_gain, TM_P)
    kv_q = _normproj(h_q, row(kv_norm), wkv, k_gain, Q_ROWS)
    k_s = kv_q[:S_ROWS, :B_KV].reshape(DEC_BATCH, DEC_SEQ, B_KV_HEADS, B_DH)
    v_s = kv_q[:S_ROWS, B_KV:].reshape(DEC_BATCH, DEC_SEQ, B_KV_HEADS, B_DH)
    k_win_s = jnp.concatenate([cache_k_win, k_s.astype(cache_k_win.dtype)], axis=1)[:, -WINDOW:]
    v_win_s = jnp.concatenate([cache_v_win, v_s.astype(cache_v_win.dtype)], axis=1)[:, -WINDOW:]

    h_s = h_q[:S_ROWS]
    h_s, *wf = _ffn_cast(h_s, row(ffn_norm[1, 0]), w_ffn_in, w_ffn_out, 1, 0)
    h_p = _ffn(h_p, row(ffn_norm[1, 0]), *wf, TM_P)
    q_p = _normproj(h_p, row(mix_norm[1]), wq, q_gain, TM_W)
    q_s = _normproj(h_s, row(mix_norm[1]), wq, q_gain, TM_W)

    o_p = _attn_prompt(q_p, kv_p, kv_q, sinks_row)
    q4 = q_s.reshape(DEC_BATCH, DEC_SEQ, KV_PAIRS, 2, B_GROUP, B_DH).transpose(0, 2, 1, 4, 3, 5)
    q4 = q4.reshape(DEC_BATCH, KV_PAIRS, SAMPLE_ROWS, LANES)
    seq3 = lambda x: x.astype(F32).reshape(DEC_BATCH, -1, B_KV)
    o4 = _attn_sample(q4, seq3(cache_k_meta), seq3(cache_v_meta),
                      seq3(cache_k_win[:, :SAMPLE_OLD]), seq3(cache_v_win[:, :SAMPLE_OLD]),
                      seq3(k_win_s), seq3(v_win_s), sinks_row)
    o_s = o4.reshape(DEC_BATCH, KV_PAIRS, DEC_SEQ, B_GROUP, 2, B_DH).transpose(0, 2, 1, 4, 3, 5)
    o_s = o_s.reshape(S_ROWS, D_MODEL)

    h_p = _matres(o_p, wb_out, h_p, TM_W)
    h_s = _matres(o_s, wb_out, h_s, TM_W)
    h_s, *wf = _ffn_cast(h_s, row(ffn_norm[1, 1]), w_ffn_in, w_ffn_out, 1, 1)
    h_p = _ffn(h_p, row(ffn_norm[1, 1]), *wf, TM_P)

    kv4 = lambda x: x.reshape(x.shape[:-1] + (B_KV_HEADS, B_DH))
    meta_rows = kv_q[S_ROWS:S_ROWS + N_META]
    kv_p3 = kv_p.reshape(BATCH, SEQ, 2 * B_KV)
    st = lambda x, dt: x[None].astype(dt)
    return (
        h_p.reshape(BATCH, SEQ, D_MODEL),
        h_s.reshape(DEC_BATCH, DEC_SEQ, D_MODEL),
        st(c_p, state_C.dtype), st(n_p[:, :, 0, :], state_n.dtype), st(m_p[:, :, 0, 0], state_m.dtype),
        jnp.broadcast_to(kv4(meta_rows[:, :B_KV])[None], (BATCH, N_META, B_KV_HEADS, B_DH)),
        jnp.broadcast_to(kv4(meta_rows[:, B_KV:])[None], (BATCH, N_META, B_KV_HEADS, B_DH)),
        kv4(kv_p3[:, -WINDOW:, :B_KV]), kv4(kv_p3[:, -WINDOW:, B_KV:]),
        st(c_s, state_C.dtype), st(n_s[:, :, 0, :], state_n.dtype), st(m_s[:, :, 0, 0], state_m.dtype),
        k_win_s, v_win_s,
    )
```

```python
import functools

import jax
import jax.numpy as jnp
from jax import lax
from jax.experimental import pallas as pl
from jax.experimental.pallas import tpu as pltpu

D_MODEL = 2048
BATCH = 8
SEQ = 2048
DEC_BATCH = 128
DEC_SEQ = 4
PAST_LEN = 8192
N_META = 16
A_HEADS = 4
A_DV = D_MODEL // A_HEADS
A_DK = A_DV // 2
A_CHUNK = 128
A_GATE_CAP = 15.0
A_QKVO = 2 * A_HEADS * A_DK + 2 * A_HEADS * A_DV
B_HEADS = 32
B_DH = D_MODEL // B_HEADS
B_KV_HEADS = 4
B_GROUP = B_HEADS // B_KV_HEADS
B_KV = B_KV_HEADS * B_DH
WINDOW = 128
D_FF = ((8 * D_MODEL // 3 + 255) // 256) * 256
EPS = 1e-6

LANES = 128
SUBLANES = 8
VMEM_DEFAULT_MIB = 48
VMEM_FFN_MIB = 60

P_ROWS = BATCH * SEQ
S_ROWS = DEC_BATCH * DEC_SEQ
Q_ROWS = S_ROWS + A_CHUNK
META_BLOCK = S_ROWS // A_CHUNK

F32 = jnp.float32
BF16 = jnp.bfloat16
NEG_INF = float("-inf")


def _params(n_axes, vmem_mib=VMEM_DEFAULT_MIB):
    return pltpu.CompilerParams(dimension_semantics=("arbitrary",) * n_axes,
                                vmem_limit_bytes=vmem_mib * 1024 * 1024)


def _resident(shape):
    return pl.BlockSpec(shape, lambda i: (0, 0), pipeline_mode=pl.Buffered(1))


def _rms(x, g):
    return x * lax.rsqrt(jnp.mean(x * x, axis=-1, keepdims=True) + EPS) * g


def _dot(a, b):
    return jnp.dot(a, b, preferred_element_type=F32)


def _dot_nt(a, b):
    return lax.dot_general(a, b, (((1,), (1,)), ((), ())), preferred_element_type=F32)


def _dot_tn(a, b):
    return lax.dot_general(a, b, (((0,), (0,)), ((), ())), preferred_element_type=F32)


def _log_sigmoid(x):
    return -(jnp.maximum(-x, 0.0) + jnp.log1p(jnp.exp(-jnp.abs(x))))


def _head_norm64(y, gain):
    lo = lax.broadcasted_iota(jnp.int32, (1, LANES), 1) < B_DH
    cols = []
    for c in range(y.shape[1] // LANES):
        x = y[:, c * LANES:(c + 1) * LANES]
        xx = x * x
        s_lo = jnp.sum(jnp.where(lo, xx, 0.0), axis=-1, keepdims=True)
        s_hi = jnp.sum(jnp.where(lo, 0.0, xx), axis=-1, keepdims=True)
        scale = jnp.where(lo, lax.rsqrt(s_lo / B_DH + EPS), lax.rsqrt(s_hi / B_DH + EPS))
        cols.append(x * scale * gain[:, c * LANES:(c + 1) * LANES])
    return jnp.concatenate(cols, axis=1)


FFN_TF = 512
FFN_CAST_TF = 256


def _ffn_kernel(h_ref, g_ref, wg_ref, wu_ref, wo_ref, o_ref, *rest, emit_bf16):
    j = pl.program_id(1)
    if emit_bf16:
        wgb_ref, wub_ref, wob_ref, xn_ref = rest
        wg = wg_ref[...].astype(BF16)
        wu = wu_ref[...].astype(BF16)
        wo = wo_ref[...].astype(BF16)
        wob_ref[...] = wo
        for part in range(FFN_TF // FFN_CAST_TF):
            @pl.when(j % (FFN_TF // FFN_CAST_TF) == part)
            def _(part=part):
                wgb_ref[:, part * FFN_CAST_TF:(part + 1) * FFN_CAST_TF] = wg
                wub_ref[:, part * FFN_CAST_TF:(part + 1) * FFN_CAST_TF] = wu
    else:
        (xn_ref,) = rest
        wg, wu, wo = wg_ref[...], wu_ref[...], wo_ref[...]

    def half_ffn(xn):
        g = _dot(xn, wg)
        u = _dot(xn, wu)
        a = (g / (1.0 + jnp.exp(-g))) * (0.5 * u)
        return _dot(a.astype(BF16), wo)

    @pl.when(j == 0)
    def _():
        h = h_ref[...]
        xn_ref[...] = _rms(h, g_ref[...]).astype(BF16)
        o_ref[...] = h

    o_ref[...] += half_ffn(xn_ref[...])


def _ffn(h, gain, wg, wu, wo, tm):
    rows = h.shape[0]
    tf = FFN_TF
    n_ff = D_FF // tf
    return pl.pallas_call(
        functools.partial(_ffn_kernel, emit_bf16=False),
        grid=(rows // tm, n_ff),
        in_specs=[
            pl.BlockSpec((tm, D_MODEL), lambda i, j: (i, 0)),
            pl.BlockSpec((1, D_MODEL), lambda i, j: (0, 0)),
            pl.BlockSpec((None, D_MODEL, tf), lambda i, j: (j, 0, 0)),
            pl.BlockSpec((None, D_MODEL, tf), lambda i, j: (j, 0, 0)),
            pl.BlockSpec((tf, D_MODEL), lambda i, j: (j, 0)),
        ],
        out_specs=pl.BlockSpec((tm, D_MODEL), lambda i, j: (i, 0)),
        out_shape=jax.ShapeDtypeStruct((rows, D_MODEL), F32),
        scratch_shapes=[pltpu.VMEM((tm, D_MODEL), BF16)],
        compiler_params=_params(2, VMEM_FFN_MIB),
        name="ffn",
    )(h, gain, wg, wu, wo)


def _ffn_cast(h, gain, w_in, w_out, layer, which):
    rows = h.shape[0]
    tf = FFN_CAST_TF
    n_ff = D_FF // tf
    per = FFN_TF // tf
    return pl.pallas_call(
        functools.partial(_ffn_kernel, emit_bf16=True),
        grid=(1, n_ff),
        in_specs=[
            pl.BlockSpec((rows, D_MODEL), lambda i, j: (0, 0)),
            pl.BlockSpec((1, D_MODEL), lambda i, j: (0, 0)),
            pl.BlockSpec((None, None, D_MODEL, tf), lambda i, j: (layer, which, 0, j)),
            pl.BlockSpec((None, None, D_MODEL, tf), lambda i, j: (layer, which, 0, j + n_ff)),
            pl.BlockSpec((None, None, tf, D_MODEL), lambda i, j: (layer, which, j, 0)),
        ],
        out_specs=[
            pl.BlockSpec((rows, D_MODEL), lambda i, j: (0, 0)),
            pl.BlockSpec((None, D_MODEL, FFN_TF), lambda i, j: (j // per, 0, 0)),
            pl.BlockSpec((None, D_MODEL, FFN_TF), lambda i, j: (j // per, 0, 0)),
            pl.BlockSpec((tf, D_MODEL), lambda i, j: (j, 0)),
        ],
        out_shape=[
            jax.ShapeDtypeStruct((rows, D_MODEL), F32),
            jax.ShapeDtypeStruct((D_FF // FFN_TF, D_MODEL, FFN_TF), BF16),
            jax.ShapeDtypeStruct((D_FF // FFN_TF, D_MODEL, FFN_TF), BF16),
            jax.ShapeDtypeStruct((D_FF, D_MODEL), BF16),
        ],
        scratch_shapes=[pltpu.VMEM((rows, D_MODEL), BF16)],
        compiler_params=_params(2),
        name="ffn_cast",
    )(h, gain, w_in, w_in, w_out)


GATE_ROWS = 2 * A_HEADS


def _inproj_kernel(h_ref, g_ref, w_ref, wgate_ref, bgate_ref, p_ref, gates_ref, *rest, emit_bf16):
    j = pl.program_id(1)
    if emit_bf16:
        wb_ref, xn_ref = rest
        w = w_ref[...].astype(BF16)
        wb_ref[...] = w
    else:
        (xn_ref,) = rest
        w = w_ref[...]

    @pl.when(j == 0)
    def _():
        xn = _rms(h_ref[...], g_ref[...]).astype(BF16)
        xn_ref[...] = xn
        wgate = jnp.concatenate([wgate_ref[...], jnp.zeros((LANES - GATE_ROWS, D_MODEL), F32)], axis=0)
        pre = _dot_nt(xn, wgate.astype(BF16)) + bgate_ref[...]
        capped = A_GATE_CAP * jnp.tanh(pre / A_GATE_CAP)
        lane = lax.broadcasted_iota(jnp.int32, (1, LANES), 1)
        gates_ref[...] = jnp.where(lane < A_HEADS, capped, _log_sigmoid(capped))

    p_ref[...] = _dot_nt(xn_ref[...], w)


P_PLANES = A_QKVO // D_MODEL


def _inproj(h, gain, w_t, w_gate_t, bgate, tm, tn, emit_bf16=False):
    rows = h.shape[0]
    assert not emit_bf16 or rows == tm
    per = D_MODEL // tn
    out_specs = [
        pl.BlockSpec((None, tm, tn), lambda i, j: (j // per, i, j % per)),
        pl.BlockSpec((tm, LANES), lambda i, j: (i, 0)),
    ]
    out_shape = [jax.ShapeDtypeStruct((P_PLANES, rows, D_MODEL), F32),
                 jax.ShapeDtypeStruct((rows, LANES), F32)]
    if emit_bf16:
        out_specs.append(pl.BlockSpec((tn, D_MODEL), lambda i, j: (j, 0)))
        out_shape.append(jax.ShapeDtypeStruct((A_QKVO, D_MODEL), BF16))
    return pl.pallas_call(
        functools.partial(_inproj_kernel, emit_bf16=emit_bf16),
        grid=(rows // tm, A_QKVO // tn),
        in_specs=[
            pl.BlockSpec((tm, D_MODEL), lambda i, j: (i, 0)),
            pl.BlockSpec((1, D_MODEL), lambda i, j: (0, 0)),
            pl.BlockSpec((tn, D_MODEL), lambda i, j: (j, 0)),
            pl.BlockSpec((GATE_ROWS, D_MODEL), lambda i, j: (A_QKVO // GATE_ROWS, 0)),
            pl.BlockSpec((1, LANES), lambda i, j: (0, 0)),
        ],
        out_specs=out_specs,
        out_shape=out_shape,
        scratch_shapes=[pltpu.VMEM((tm, D_MODEL), BF16)],
        compiler_params=_params(2),
        name="mlstm_inproj",
    )(h, gain, w_t, w_gate_t, bgate)


def _mlstm_chunk_kernel(q_ref, k_ref, v_ref, g_ref, c0_ref, n0_ref, m0_ref,
                        h_ref, c_ref, n_ref, m_ref, qk_ref, qc_ref, sb_ref, *, n_valid):
    L = A_CHUNK
    heads = range(A_HEADS)

    @pl.when(pl.program_id(1) == 0)
    def _():
        c_ref[...] = c0_ref[...]
        n_ref[...] = n0_ref[...]
        m_ref[...] = m0_ref[...]

    for hd in heads:
        qb = q_ref[:, hd * A_DK:(hd + 1) * A_DK].astype(BF16)
        kb = (k_ref[:, hd * A_DK:(hd + 1) * A_DK] * (A_DK ** -0.5)).astype(BF16)
        qk_ref[hd] = _dot_nt(qb, kb)
        qc_ref[hd] = _dot(qb, c_ref[0, hd].astype(BF16))

    gates = g_ref[...]
    row = lax.broadcasted_iota(jnp.int32, (L, L), 0)
    col = lax.broadcasted_iota(jnp.int32, (L, L), 1)
    causal = col <= row
    eye = col == row
    masked = n_valid < L
    if masked:
        row_ok = lax.broadcasted_iota(jnp.int32, (L, 1), 0) < n_valid
        gates_lf = jnp.where(row_ok, gates, 0.0)
    else:
        gates_lf = gates
    csum = jnp.dot(causal.astype(F32), gates_lf, precision=lax.Precision.HIGHEST,
                   preferred_element_type=F32)

    def to_row(x_col):
        return jnp.sum(jnp.where(eye, x_col, 0.0), axis=0, keepdims=True)

    w_intra, w_inter, floor, w_state, decay, m_new = [], [], [], [], [], []
    for hd in heads:
        b_col = csum[:, A_HEADS + hd:A_HEADS + hd + 1]
        ig_col = gates[:, hd:hd + 1]
        if masked:
            ig_col = jnp.where(row_ok, ig_col, NEG_INF)
        b_row = to_row(b_col)
        ig_row = to_row(ig_col)
        m_prev = m_ref[0, hd][:, 0:1]
        d_log = jnp.where(causal, b_col - b_row + ig_row, NEG_INF)
        inter_log = b_col + m_prev
        m_t = jnp.maximum(inter_log, jnp.max(d_log, axis=-1, keepdims=True))
        w_intra.append(jnp.exp(d_log - m_t))
        w_inter.append(jnp.exp(inter_log - m_t))
        floor.append(jnp.exp(-m_t))
        b_last = b_col[L - 1:L, :]
        w_log = b_last - b_col + ig_col
        m_new.append(jnp.maximum(b_last + m_prev, jnp.max(w_log, axis=0, keepdims=True)))
        w_state.append(jnp.exp(w_log - m_new[hd]))
        decay.append(jnp.exp(b_last + m_prev - m_new[hd]))

    den = []
    for hd in heads:
        s = qk_ref[hd] * w_intra[hd]
        sb_ref[hd] = s.astype(BF16)
        q = q_ref[:, hd * A_DK:(hd + 1) * A_DK]
        d = jnp.sum(s, axis=-1, keepdims=True) + w_inter[hd] * jnp.sum(q * n_ref[0, hd], axis=-1, keepdims=True)
        den.append(jnp.maximum(jnp.abs(d), floor[hd]))

    for hd in heads:
        k = k_ref[:, hd * A_DK:(hd + 1) * A_DK] * (A_DK ** -0.5)
        vb = v_ref[:, hd * A_DV:(hd + 1) * A_DV].astype(BF16)
        num = _dot(sb_ref[hd], vb) + w_inter[hd] * qc_ref[hd]
        h_ref[:, hd * A_DV:(hd + 1) * A_DV] = num / den[hd]
        kw = k * w_state[hd]
        c_ref[0, hd] = decay[hd] * c_ref[0, hd] + _dot_tn(kw.astype(BF16), vb)
        n_ref[0, hd] = decay[hd] * n_ref[0, hd] + jnp.sum(kw, axis=0, keepdims=True)
        m_ref[0, hd] = jnp.broadcast_to(m_new[hd], (1, LANES))


def _mlstm_chunks(p, gates, c0, n0, m0, n_seq, n_chunks, row_block0, n_valid, shared_state):
    L = A_CHUNK
    rb = lambda b, c: row_block0 + b * n_chunks + c
    st = (lambda b, c: (0, 0, 0, 0)) if shared_state else (lambda b, c: (b, 0, 0, 0))
    return pl.pallas_call(
        functools.partial(_mlstm_chunk_kernel, n_valid=n_valid),
        grid=(n_seq, n_chunks),
        in_specs=[
            pl.BlockSpec((None, L, A_HEADS * A_DK), lambda b, c: (0, rb(b, c), 0)),
            pl.BlockSpec((None, L, A_HEADS * A_DK), lambda b, c: (0, rb(b, c), 1)),
            pl.BlockSpec((None, L, A_HEADS * A_DV), lambda b, c: (1, rb(b, c), 0)),
            pl.BlockSpec((L, LANES), lambda b, c: (rb(b, c), 0)),
            pl.BlockSpec((1, A_HEADS, A_DK, A_DV), st),
            pl.BlockSpec((1, A_HEADS, 1, A_DK), st),
            pl.BlockSpec((1, A_HEADS, 1, LANES), st),
        ],
        out_specs=[
            pl.BlockSpec((L, A_HEADS * A_DV), lambda b, c: (b * n_chunks + c, 0)),
            pl.BlockSpec((1, A_HEADS, A_DK, A_DV), lambda b, c: (b, 0, 0, 0)),
            pl.BlockSpec((1, A_HEADS, 1, A_DK), lambda b, c: (b, 0, 0, 0)),
            pl.BlockSpec((1, A_HEADS, 1, LANES), lambda b, c: (b, 0, 0, 0)),
        ],
        out_shape=[
            jax.ShapeDtypeStruct((n_seq * n_chunks * L, D_MODEL), F32),
            jax.ShapeDtypeStruct((n_seq, A_HEADS, A_DK, A_DV), F32),
            jax.ShapeDtypeStruct((n_seq, A_HEADS, 1, A_DK), F32),
            jax.ShapeDtypeStruct((n_seq, A_HEADS, 1, LANES), F32),
        ],
        scratch_shapes=[pltpu.VMEM((A_HEADS, L, L), F32),
                        pltpu.VMEM((A_HEADS, L, A_DV), F32),
                        pltpu.VMEM((A_HEADS, L, L), BF16)],
        compiler_params=_params(2),
        name="mlstm_chunks",
    )(p, p, p, gates, c0, n0, m0)


def _mlstm_sample_kernel(q_ref, k_ref, v_ref, g_ref, c0_ref, n0_ref, m0_ref,
                         h_ref, c_ref, n_ref, m_ref):
    R = 2 * DEC_SEQ
    PAD = A_CHUNK - R
    gates = g_ref[...]
    r_col = lax.broadcasted_iota(jnp.int32, (R, 1), 0)
    is_a = r_col < DEC_SEQ
    row = lax.broadcasted_iota(jnp.int32, (R, LANES), 0)
    lane = lax.broadcasted_iota(jnp.int32, (R, LANES), 1)
    same = ((lane < DEC_SEQ) & (row < DEC_SEQ)) | ((lane >= DEC_SEQ) & (lane < R) & (row >= DEC_SEQ))
    causal = same & (lane <= row)
    eye = lane == row

    def to_row(x_col):
        return jnp.sum(jnp.where(eye, x_col, 0.0), axis=0, keepdims=True)

    for hd in range(A_HEADS):
        lf_col = gates[:, A_HEADS + hd:A_HEADS + hd + 1]
        ig_col = gates[:, hd:hd + 1]
        lf_row = to_row(lf_col)
        ig_row = to_row(ig_col)
        b_col = jnp.sum(jnp.where(causal, lf_row, 0.0), axis=1, keepdims=True)
        b_row = to_row(b_col)
        m_a = m0_ref[0, hd][:, 0:1]
        m_b = m0_ref[1, hd][:, 0:1]
        m_prev = jnp.where(is_a, m_a, m_b)
        c_a = c0_ref[0, hd]
        c_b = c0_ref[1, hd]
        n_a = n0_ref[0, hd]
        n_b = n0_ref[1, hd]

        q = q_ref[:, hd * A_DK:(hd + 1) * A_DK]
        k = k_ref[:, hd * A_DK:(hd + 1) * A_DK] * (A_DK ** -0.5)
        v = v_ref[:, hd * A_DV:(hd + 1) * A_DV]
        qb = q.astype(BF16)
        k_pad = jnp.concatenate([k, jnp.zeros((PAD, A_DK), F32)], axis=0).astype(BF16)
        v_pad = jnp.concatenate([v, jnp.zeros((PAD, A_DV), F32)], axis=0).astype(BF16)

        d_log = jnp.where(causal, b_col - b_row + ig_row, NEG_INF)
        inter_log = b_col + m_prev
        m_t = jnp.maximum(inter_log, jnp.max(d_log, axis=-1, keepdims=True))
        w_intra = jnp.exp(d_log - m_t)
        w_inter = jnp.exp(inter_log - m_t)
        s = _dot_nt(qb, k_pad) * w_intra
        q_c = jnp.where(is_a, _dot(qb, c_a.astype(BF16)), _dot(qb, c_b.astype(BF16)))
        num = _dot(s.astype(BF16), v_pad) + w_inter * q_c
        q_n = jnp.sum(q * jnp.where(is_a, n_a, n_b), axis=-1, keepdims=True)
        den = jnp.sum(s, axis=-1, keepdims=True) + w_inter * q_n
        den = jnp.maximum(jnp.abs(den), jnp.exp(-m_t))
        h_ref[:, hd * A_DV:(hd + 1) * A_DV] = num / den

        for idx, sel, m_x, c_x, n_x in ((0, is_a, m_a, c_a, n_a),
                                        (1, jnp.logical_not(is_a), m_b, c_b, n_b)):
            last = (idx + 1) * DEC_SEQ - 1
            b_last = b_col[last:last + 1, :]
            w_log = jnp.where(sel, b_last - b_col + ig_col, NEG_INF)
            m_new = jnp.maximum(b_last + m_x, jnp.max(w_log, axis=0, keepdims=True))
            w_state = jnp.exp(w_log - m_new)
            decay = jnp.exp(b_last + m_x - m_new)
            kw = k * w_state
            kw_pad = jnp.concatenate([kw, jnp.zeros((PAD, A_DK), F32)], axis=0).astype(BF16)
            c_ref[idx, hd] = decay * c_x + _dot_tn(kw_pad, v_pad)
            n_ref[idx, hd] = decay * n_x + jnp.sum(kw, axis=0, keepdims=True)
            m_ref[idx, hd] = jnp.broadcast_to(m_new, (1, LANES))


def _mlstm_sample(p, gates, c0, n0, m0):
    R = 2 * DEC_SEQ
    n_pairs = DEC_BATCH // 2
    st = lambda i: (i, 0, 0, 0)
    return pl.pallas_call(
        _mlstm_sample_kernel,
        grid=(n_pairs,),
        in_specs=[
            pl.BlockSpec((None, R, A_HEADS * A_DK), lambda i: (0, i, 0)),
            pl.BlockSpec((None, R, A_HEADS * A_DK), lambda i: (0, i, 1)),
            pl.BlockSpec((None, R, A_HEADS * A_DV), lambda i: (1, i, 0)),
            pl.BlockSpec((R, LANES), lambda i: (i, 0)),
            pl.BlockSpec((2, A_HEADS, A_DK, A_DV), st),
            pl.BlockSpec((2, A_HEADS, 1, A_DK), st),
            pl.BlockSpec((2, A_HEADS, 1, LANES), st),
        ],
        out_specs=[
            pl.BlockSpec((R, A_HEADS * A_DV), lambda i: (i, 0)),
            pl.BlockSpec((2, A_HEADS, A_DK, A_DV), st),
            pl.BlockSpec((2, A_HEADS, 1, A_DK), st),
            pl.BlockSpec((2, A_HEADS, 1, LANES), st),
        ],
        out_shape=[
            jax.ShapeDtypeStruct((S_ROWS, D_MODEL), F32),
            jax.ShapeDtypeStruct((DEC_BATCH, A_HEADS, A_DK, A_DV), F32),
            jax.ShapeDtypeStruct((DEC_BATCH, A_HEADS, 1, A_DK), F32),
            jax.ShapeDtypeStruct((DEC_BATCH, A_HEADS, 1, LANES), F32),
        ],
        compiler_params=_params(1),
        name="mlstm_sample",
    )(p, p, p, gates, c0, n0, m0)


def _mlstm_out_kernel(hm_ref, o_ref, hg_ref, w_ref, res_ref, out_ref):
    cols = []
    for hd in range(A_HEADS):
        x = hm_ref[:, hd * A_DV:(hd + 1) * A_DV]
        cols.append(x * lax.rsqrt(jnp.mean(x * x, axis=-1, keepdims=True) + EPS))
    hn = jnp.concatenate(cols, axis=1) * hg_ref[...]
    o = o_ref[...]
    pre = (hn * (1.0 / (1.0 + jnp.exp(-o)))).astype(BF16)
    out_ref[...] = res_ref[...] + _dot(pre, w_ref[...])


def _mlstm_out(hm, p, head_gain, w, res, tm):
    rows = hm.shape[0]
    return pl.pallas_call(
        _mlstm_out_kernel,
        grid=(rows // tm,),
        in_specs=[
            pl.BlockSpec((tm, D_MODEL), lambda i: (i, 0)),
            pl.BlockSpec((None, tm, D_MODEL), lambda i: (2, i, 0)),
            pl.BlockSpec((1, D_MODEL), lambda i: (0, 0)),
            _resident((D_MODEL, D_MODEL)),
            pl.BlockSpec((tm, D_MODEL), lambda i: (i, 0)),
        ],
        out_specs=pl.BlockSpec((tm, D_MODEL), lambda i: (i, 0)),
        out_shape=jax.ShapeDtypeStruct((rows, D_MODEL), F32),
        compiler_params=_params(1),
        name="mlstm_out",
    )(hm, p, head_gain, w, res)


def _normproj_kernel(h_ref, g_ref, w_ref, hg_ref, o_ref, *, n_norm):
    xn = _rms(h_ref[...], g_ref[...]).astype(BF16)
    y = _dot(xn, w_ref[...])
    if n_norm == y.shape[1]:
        o_ref[...] = _head_norm64(y, hg_ref[...])
    else:
        o_ref[:, :n_norm] = _head_norm64(y[:, :n_norm], hg_ref[...])
        o_ref[:, n_norm:] = y[:, n_norm:]


def _normproj(h, gain, w, head_gain, tm):
    rows = h.shape[0]
    n = w.shape[1]
    n_norm = head_gain.shape[1]
    return pl.pallas_call(
        functools.partial(_normproj_kernel, n_norm=n_norm),
        grid=(rows // tm,),
        in_specs=[
            pl.BlockSpec((tm, D_MODEL), lambda i: (i, 0)),
            pl.BlockSpec((1, D_MODEL), lambda i: (0, 0)),
            _resident((D_MODEL, n)),
            pl.BlockSpec((1, n_norm), lambda i: (0, 0)),
        ],
        out_specs=pl.BlockSpec((tm, n), lambda i: (i, 0)),
        out_shape=jax.ShapeDtypeStruct((rows, n), F32),
        compiler_params=_params(1),
        name="normproj",
    )(h, gain, w, head_gain)


def _matres_kernel(x_ref, w_ref, res_ref, o_ref):
    o_ref[...] = res_ref[...] + _dot(x_ref[...].astype(BF16), w_ref[...])


def _matres(x, w, res, tm):
    rows = x.shape[0]
    return pl.pallas_call(
        _matres_kernel,
        grid=(rows // tm,),
        in_specs=[
            pl.BlockSpec((tm, D_MODEL), lambda i: (i, 0)),
            _resident((D_MODEL, D_MODEL)),
            pl.BlockSpec((tm, D_MODEL), lambda i: (i, 0)),
        ],
        out_specs=pl.BlockSpec((tm, D_MODEL), lambda i: (i, 0)),
        out_shape=jax.ShapeDtypeStruct((rows, D_MODEL), F32),
        compiler_params=_params(1),
        name="matres",
    )(x, w, res)


def _alibi_slope(head):
    return 2.0 ** (-8.0 * (head + 1) / B_HEADS)


PROMPT_SUB = WINDOW // 2
PROMPT_BAND = WINDOW + PROMPT_SUB
PROMPT_KEYS = 2 * LANES
HEAD_PAIRS = B_HEADS // 2
PAIRS_PER_GROUP = B_GROUP // 2


def _pair_rhs(x, c):
    col = x[:, c * LANES:(c + 1) * LANES]
    rol = pltpu.roll(col, B_DH, axis=1)
    lo = lax.broadcasted_iota(jnp.int32, (1, LANES), 1) < B_DH
    even = jnp.concatenate([jnp.where(lo, col, 0.0), jnp.where(lo, 0.0, rol)], axis=0)
    odd = jnp.concatenate([jnp.where(lo, rol, 0.0), jnp.where(lo, 0.0, col)], axis=0)
    return even.astype(BF16), odd.astype(BF16)


def _attn_prompt_kernel(q_ref, kvo_ref, kvp_ref, kvm_ref, sink_ref, o_ref, bias_ref, s_ref, p_ref):
    W = WINDOW
    SB = PROMPT_SUB
    NK = PROMPT_KEYS
    first = (pl.program_id(0) == 0) & (pl.program_id(1) == 0)
    j = pl.program_id(1)
    rows = PAIRS_PER_GROUP * SB
    lo = lax.broadcasted_iota(jnp.int32, (1, LANES), 1) < B_DH

    @pl.when(first)
    def _():
        c = lax.broadcasted_iota(jnp.int32, (SB, NK), 1)
        is_meta = c < N_META
        is_sink = c == NK - 1
        for sub in range(2):
            i = lax.broadcasted_iota(jnp.int32, (SB, NK), 0) + sub * SB
            pos = c - N_META + sub * SB
            rel = W + i - pos
            in_band = (c >= N_META) & (c < N_META + PROMPT_BAND) & (rel >= 0) & (rel < W)
            for variant in range(2):
                if variant == 0:
                    dist = jnp.where(is_meta, jnp.minimum(i + N_META - c, W), rel).astype(F32)
                    valid = is_meta | (in_band & (pos >= W))
                else:
                    dist = jnp.where(is_meta, W, rel).astype(F32)
                    valid = is_meta | in_band
                for hd in range(B_HEADS):
                    r0 = (hd // 2) * SB
                    c0 = (hd % 2) * NK
                    table = jnp.where(valid, -_alibi_slope(hd) * dist, NEG_INF)
                    bias_ref[variant, sub, r0:r0 + SB, c0:c0 + NK] = jnp.where(
                        is_sink, sink_ref[0:1, hd:hd + 1], table)

    variant = jnp.minimum(j, 1)
    kvm = kvm_ref[...]
    band = jnp.concatenate([kvp_ref[...], kvo_ref[...]], axis=0)
    pad = jnp.zeros((NK - N_META - PROMPT_BAND, 2 * B_KV), F32)
    ones_rhs = jnp.concatenate([jnp.broadcast_to(jnp.where(lo, 1.0, 0.0), (NK, LANES)),
                                jnp.broadcast_to(jnp.where(lo, 0.0, 1.0), (NK, LANES))], axis=0).astype(BF16)

    sub_rows = B_KV_HEADS * rows
    for sub in range(2):
        keys = jnp.concatenate([kvm, band[sub * SB:sub * SB + PROMPT_BAND], pad], axis=0)
        k_rhs = _pair_rhs(keys, 0) + _pair_rhs(keys, 1)
        v_rhs = _pair_rhs(keys, 2) + _pair_rhs(keys, 3)
        for g in range(B_KV_HEADS):
            p0 = g * PAIRS_PER_GROUP
            r0 = (sub * B_KV_HEADS + g) * rows
            qg = jnp.concatenate([q_ref[sub * SB:(sub + 1) * SB, (p0 + m) * LANES:(p0 + m + 1) * LANES]
                                  for m in range(PAIRS_PER_GROUP)], axis=0) * (B_DH ** -0.5)
            s_ref[r0:r0 + rows, :] = _dot_nt(qg.astype(BF16), k_rhs[g])

        s = s_ref[sub * sub_rows:(sub + 1) * sub_rows, :] + bias_ref[variant, sub]
        halves = []
        for half in range(2):
            sh = s[:, half * NK:(half + 1) * NK]
            halves.append(jnp.exp(sh - jnp.max(sh, axis=-1, keepdims=True)).astype(BF16))
        p_ref[sub * sub_rows:(sub + 1) * sub_rows, :] = jnp.concatenate(halves, axis=1)

        for g in range(B_KV_HEADS):
            p0 = g * PAIRS_PER_GROUP
            r0 = (sub * B_KV_HEADS + g) * rows
            od = _dot(p_ref[r0:r0 + rows, :], jnp.concatenate([v_rhs[g], ones_rhs], axis=1))
            o = od[:, :LANES] / od[:, LANES:]
            for m in range(PAIRS_PER_GROUP):
                o_ref[sub * SB:(sub + 1) * SB, (p0 + m) * LANES:(p0 + m + 1) * LANES] = o[m * SB:(m + 1) * SB, :]


def _attn_prompt(q, kv_p, kv_q, sinks_row):
    nb = SEQ // WINDOW
    blk = lambda b, j: b * nb + j
    return pl.pallas_call(
        _attn_prompt_kernel,
        grid=(BATCH, nb),
        in_specs=[
            pl.BlockSpec((WINDOW, D_MODEL), lambda b, j: (blk(b, j), 0)),
            pl.BlockSpec((WINDOW, 2 * B_KV), lambda b, j: (blk(b, j), 0)),
            pl.BlockSpec((WINDOW, 2 * B_KV), lambda b, j: (blk(b, jnp.maximum(j - 1, 0)), 0)),
            pl.BlockSpec((N_META, 2 * B_KV), lambda b, j: (S_ROWS // N_META, 0)),
            pl.BlockSpec((1, LANES), lambda b, j: (0, 0)),
        ],
        out_specs=pl.BlockSpec((WINDOW, D_MODEL), lambda b, j: (blk(b, j), 0)),
        out_shape=jax.ShapeDtypeStruct((P_ROWS, D_MODEL), F32),
        scratch_shapes=[pltpu.VMEM((2, 2, HEAD_PAIRS * PROMPT_SUB, 2 * PROMPT_KEYS), F32),
                        pltpu.VMEM((2 * HEAD_PAIRS * PROMPT_SUB, 2 * PROMPT_KEYS), F32),
                        pltpu.VMEM((2 * HEAD_PAIRS * PROMPT_SUB, 2 * PROMPT_KEYS), BF16)],
        compiler_params=_params(2),
        name="attn_prompt",
    )(q, kv_p, kv_p, kv_q, sinks_row)


GROUP_SHIFT = B_GROUP.bit_length() - 1
assert 1 << GROUP_SHIFT == B_GROUP
SAMPLE_OLD = SUBLANES
SAMPLE_KEYS = 2 * LANES
SAMPLE_SEQ_PER_STEP = 8
SAMPLE_ROWS = DEC_SEQ * B_GROUP
KV_PAIRS = B_KV_HEADS // 2


def _split_rhs(col):
    lo = lax.broadcasted_iota(jnp.int32, (1, LANES), 1) < B_DH
    return jnp.concatenate([jnp.where(lo, col, 0.0), jnp.where(lo, 0.0, col)], axis=0).astype(BF16)


def _attn_sample_kernel(q_ref, km_ref, vm_ref, ko_ref, vo_ref, kw_ref, vw_ref, sink_ref, o_ref,
                        bias_ref, s_ref, p_ref):
    W = WINDOW
    NK = SAMPLE_KEYS
    R = SAMPLE_ROWS
    off_w = N_META + SAMPLE_OLD
    lo = lax.broadcasted_iota(jnp.int32, (1, LANES), 1) < B_DH

    @pl.when(pl.program_id(0) == 0)
    def _():
        row = lax.broadcasted_iota(jnp.int32, (R, 1), 0)
        r_in_group = jnp.bitwise_and(row, B_GROUP - 1)
        t = jnp.right_shift(lax.broadcasted_iota(jnp.int32, (R, NK), 0), GROUP_SHIFT)
        c = lax.broadcasted_iota(jnp.int32, (R, NK), 1)
        is_meta = c < N_META
        is_old = (c >= N_META) & (c < off_w)
        is_win = (c >= off_w) & (c < off_w + W)
        j_old = c - N_META
        d_win = (W - DEC_SEQ) + t - (c - off_w)
        dist = jnp.where(is_meta, W, jnp.where(is_old, W + t - j_old, d_win)).astype(F32)
        valid = is_meta | (is_old & (j_old > t) & (j_old < DEC_SEQ)) | (is_win & (d_win >= 0))
        for pair in range(KV_PAIRS):
            for e in range(2):
                slope = jnp.zeros((R, 1), F32)
                sink = jnp.zeros((R, 1), F32)
                for r in range(B_GROUP):
                    hd = (2 * pair + e) * B_GROUP + r
                    slope = jnp.where(r_in_group == r, _alibi_slope(hd), slope)
                    sink = jnp.where(r_in_group == r, sink_ref[0:1, hd:hd + 1], sink)
                table = jnp.where(c == NK - 1, sink, jnp.where(valid, -slope * dist, NEG_INF))
                for sq in range(SAMPLE_SEQ_PER_STEP):
                    r0 = (sq * KV_PAIRS + pair) * R
                    bias_ref[r0:r0 + R, e * NK:(e + 1) * NK] = table

    pad = jnp.zeros((NK - off_w - W, B_KV), F32)
    ones_rhs = jnp.concatenate([jnp.broadcast_to(jnp.where(lo, 1.0, 0.0), (NK, LANES)),
                                jnp.broadcast_to(jnp.where(lo, 0.0, 1.0), (NK, LANES))], axis=0).astype(BF16)
    v_rhs = []
    for sq in range(SAMPLE_SEQ_PER_STEP):
        kcat = jnp.concatenate([km_ref[sq], ko_ref[sq], kw_ref[sq], pad], axis=0)
        vcat = jnp.concatenate([vm_ref[sq], vo_ref[sq], vw_ref[sq], pad], axis=0)
        for pair in range(KV_PAIRS):
            r0 = (sq * KV_PAIRS + pair) * R
            k_rhs = _split_rhs(kcat[:, pair * LANES:(pair + 1) * LANES])
            v_rhs.append(_split_rhs(vcat[:, pair * LANES:(pair + 1) * LANES]))
            q = (q_ref[sq, pair] * (B_DH ** -0.5)).astype(BF16)
            s_ref[r0:r0 + R, :] = _dot_nt(q, k_rhs)

    s = s_ref[...] + bias_ref[...]
    halves = []
    for e in range(2):
        sh = s[:, e * NK:(e + 1) * NK]
        halves.append(jnp.exp(sh - jnp.max(sh, axis=-1, keepdims=True)).astype(BF16))
    p_ref[...] = jnp.concatenate(halves, axis=1)

    for sq in range(SAMPLE_SEQ_PER_STEP):
        for pair in range(KV_PAIRS):
            b = sq * KV_PAIRS + pair
            od = _dot(p_ref[b * R:(b + 1) * R, :], jnp.concatenate([v_rhs[b], ones_rhs], axis=1))
            o_ref[sq, pair] = od[:, :LANES] / od[:, LANES:]


def _attn_sample(q4, k_meta, v_meta, k_old, v_old, k_win, v_win, sinks_row):
    nb = SAMPLE_SEQ_PER_STEP
    R = SAMPLE_ROWS
    n_rows = nb * KV_PAIRS * R
    seq3 = lambda rows: pl.BlockSpec((nb, rows, B_KV), lambda i: (i, 0, 0))
    qspec = pl.BlockSpec((nb, KV_PAIRS, R, LANES), lambda i: (i, 0, 0, 0))
    return pl.pallas_call(
        _attn_sample_kernel,
        grid=(DEC_BATCH // nb,),
        in_specs=[qspec, seq3(N_META), seq3(N_META), seq3(SAMPLE_OLD), seq3(SAMPLE_OLD),
                  seq3(WINDOW), seq3(WINDOW), pl.BlockSpec((1, LANES), lambda i: (0, 0))],
        out_specs=qspec,
        out_shape=jax.ShapeDtypeStruct((DEC_BATCH, KV_PAIRS, R, LANES), F32),
        scratch_shapes=[pltpu.VMEM((n_rows, 2 * SAMPLE_KEYS), F32),
                        pltpu.VMEM((n_rows, 2 * SAMPLE_KEYS), F32),
                        pltpu.VMEM((n_rows, 2 * SAMPLE_KEYS), BF16)],
        compiler_params=_params(1),
        name="attn_sample",
    )(q4, k_meta, v_meta, k_old, v_old, k_win, v_win, sinks_row)


def kernel(x_prompt, x_sample, state_C, state_n, state_m, cache_k_meta, cache_v_meta, cache_k_win, cache_v_win, meta_tokens, ffn_norm, w_ffn_in, w_ffn_out, mix_norm, w_a_in, b_a_gate, a_head_norm, w_a_out, kv_norm, w_kv, k_norm, w_q, q_norm, sinks, w_b_out):
    assert x_prompt.shape == (BATCH, SEQ, D_MODEL) and x_sample.shape == (DEC_BATCH, DEC_SEQ, D_MODEL)
    assert w_a_in.shape[0] == 1 and w_q.shape[0] == 1 and ffn_norm.shape[0] == 2

    wa_in_t = jnp.swapaxes(w_a_in[0], 0, 1)
    ba_gate = jnp.pad(b_a_gate[0].astype(F32), (0, LANES - 2 * A_HEADS)).reshape(1, LANES)
    wa_out = w_a_out[0].astype(BF16)
    wkv = w_kv.astype(BF16)
    wq = w_q[0].astype(BF16)
    wb_out = w_b_out[0].astype(BF16)
    row = lambda x: x.astype(F32).reshape(1, -1)
    k_gain = jnp.tile(row(k_norm), (1, B_KV_HEADS))
    q_gain = jnp.tile(row(q_norm[0]), (1, B_HEADS))
    sinks_row = jnp.pad(sinks[0].astype(F32), (0, LANES - B_HEADS)).reshape(1, LANES)

    h_p = x_prompt.reshape(P_ROWS, D_MODEL)
    h_q = jnp.concatenate([x_sample.reshape(S_ROWS, D_MODEL), meta_tokens.astype(F32),
                           jnp.zeros((A_CHUNK - N_META, D_MODEL), F32)], axis=0)
    TM_P, TM_W = 1024, 512

    h_q, *wf = _ffn_cast(h_q, row(ffn_norm[0, 0]), w_ffn_in, w_ffn_out, 0, 0)
    h_p = _ffn(h_p, row(ffn_norm[0, 0]), *wf, TM_P)
    p_q, g_q, wa_in = _inproj(h_q, row(mix_norm[0]), wa_in_t, wa_in_t, ba_gate, Q_ROWS, 1024, emit_bf16=True)
    p_p, g_p = _inproj(h_p, row(mix_norm[0]), wa_in, wa_in_t, ba_gate, TM_W, D_MODEL)

    zc = jnp.zeros((1, A_HEADS, A_DK, A_DV), F32)
    zn = jnp.zeros((1, A_HEADS, 1, A_DK), F32)
    zm = jnp.zeros((1, A_HEADS, 1, LANES), F32)
    hm_m, c_m, n_m, m_m = _mlstm_chunks(p_q, g_q, zc, zn, zm, 1, 1, META_BLOCK, N_META, True)
    hm_p, c_p, n_p, m_p = _mlstm_chunks(p_p, g_p, c_m, n_m, m_m, BATCH, SEQ // A_CHUNK, 0, A_CHUNK, True)
    m0_s = jnp.broadcast_to(state_m[0].astype(F32)[:, :, None, None], (DEC_BATCH, A_HEADS, 1, LANES))
    hm_s, c_s, n_s, m_s = _mlstm_sample(p_q, g_q, state_C[0].astype(F32),
                                        state_n[0].astype(F32)[:, :, None, :], m0_s)
    hm_q = jnp.concatenate([hm_s, hm_m], axis=0)

    h_p = _mlstm_out(hm_p, p_p, row(a_head_norm[0]), wa_out, h_p, TM_W)
    h_q = _mlstm_out(hm_q, p_q, row(a_head_norm[0]), wa_out, h_q, A_CHUNK)
    h_q, *wf = _ffn_cast(h_q, row(ffn_norm[0, 1]), w_ffn_in, w_ffn_out, 0, 1)
    h_p = _ffn(h_p, row(ffn_norm[0, 1]), *wf, TM_P)

    kv_p = _normproj(h_p, row(kv_norm), wkv, k_gain, TM_P)
    kv_q = _normproj(h_q, row(kv_norm), wkv, k_gain, Q_ROWS)
    k_s = kv_q[:S_ROWS, :B_KV].reshape(DEC_BATCH, DEC_SEQ, B_KV_HEADS, B_DH)
    v_s = kv_q[:S_ROWS, B_KV:].reshape(DEC_BATCH, DEC_SEQ, B_KV_HEADS, B_DH)
    k_win_s = jnp.concatenate([cache_k_win, k_s.astype(cache_k_win.dtype)], axis=1)[:, -WINDOW:]
    v_win_s = jnp.concatenate([cache_v_win, v_s.astype(cache_v_win.dtype)], axis=1)[:, -WINDOW:]

    h_s = h_q[:S_ROWS]
    h_s, *wf = _ffn_cast(h_s, row(ffn_norm[1, 0]), w_ffn_in, w_ffn_out, 1, 0)
    h_p = _ffn(h_p, row(ffn_norm[1, 0]), *wf, TM_P)
    q_p = _normproj(h_p, row(mix_norm[1]), wq, q_gain, TM_W)
    q_s = _normproj(h_s, row(mix_norm[1]), wq, q_gain, TM_W)

    o_p = _attn_prompt(q_p, kv_p, kv_q, sinks_row)
    q4 = q_s.reshape(DEC_BATCH, DEC_SEQ, KV_PAIRS, 2, B_GROUP, B_DH).transpose(0, 2, 1, 4, 3, 5)
    q4 = q4.reshape(DEC_BATCH, KV_PAIRS, SAMPLE_ROWS, LANES)
    seq3 = lambda x: x.astype(F32).reshape(DEC_BATCH, -1, B_KV)
    o4 = _attn_sample(q4, seq3(cache_k_meta), seq3(cache_v_meta),
                      seq3(cache_k_win[:, :SAMPLE_OLD]), seq3(cache_v_win[:, :SAMPLE_OLD]),
                      seq3(k_win_s), seq3(v_win_s), sinks_row)
    o_s = o4.reshape(DEC_BATCH, KV_PAIRS, DEC_SEQ, B_GROUP, 2, B_DH).transpose(0, 2, 1, 4, 3, 5)
    o_s = o_s.reshape(S_ROWS, D_MODEL)

    h_p = _matres(o_p, wb_out, h_p, TM_W)
    h_s = _matres(o_s, wb_out, h_s, TM_W)
    h_s, *wf = _ffn_cast(h_s, row(ffn_norm[1, 1]), w_ffn_in, w_ffn_out, 1, 1)
    h_p = _ffn(h_p, row(ffn_norm[1, 1]), *wf, TM_P)

    kv4 = lambda x: x.reshape(x.shape[:-1] + (B_KV_HEADS, B_DH))
    meta_rows = kv_q[S_ROWS:S_ROWS + N_META]
    kv_p3 = kv_p.reshape(BATCH, SEQ, 2 * B_KV)
    st = lambda x, dt: x[None].astype(dt)
    return (
        h_p.reshape(BATCH, SEQ, D_MODEL),
        h_s.reshape(DEC_BATCH, DEC_SEQ, D_MODEL),
        st(c_p, state_C.dtype), st(n_p[:, :, 0, :], state_n.dtype), st(m_p[:, :, 0, 0], state_m.dtype),
        jnp.broadcast_to(kv4(meta_rows[:, :B_KV])[None], (BATCH, N_META, B_KV_HEADS, B_DH)),
        jnp.broadcast_to(kv4(meta_rows[:, B_KV:])[None], (BATCH, N_META, B_KV_HEADS, B_DH)),
        kv4(kv_p3[:, -WINDOW:, :B_KV]), kv4(kv_p3[:, -WINDOW:, B_KV:]),
        st(c_s, state_C.dtype), st(n_s[:, :, 0, :], state_n.dtype), st(m_s[:, :, 0, 0], state_m.dtype),
        k_win_s, v_win_s,
    )
```

```python
import functools

import jax
import jax.numpy as jnp
from jax import lax
from jax.experimental import pallas as pl
from jax.experimental.pallas import tpu as pltpu

D_MODEL = 2048
BATCH = 8
SEQ = 2048
DEC_BATCH = 128
DEC_SEQ = 4
PAST_LEN = 8192
N_META = 16
A_HEADS = 4
A_DV = D_MODEL // A_HEADS
A_DK = A_DV // 2
A_CHUNK = 128
A_GATE_CAP = 15.0
A_QKVO = 2 * A_HEADS * A_DK + 2 * A_HEADS * A_DV
B_HEADS = 32
B_DH = D_MODEL // B_HEADS
B_KV_HEADS = 4
B_GROUP = B_HEADS // B_KV_HEADS
B_KV = B_KV_HEADS * B_DH
WINDOW = 128
D_FF = ((8 * D_MODEL // 3 + 255) // 256) * 256
EPS = 1e-6

LANES = 128
SUBLANES = 8
VMEM_DEFAULT_MIB = 48
VMEM_FFN_MIB = 60

P_ROWS = BATCH * SEQ
S_ROWS = DEC_BATCH * DEC_SEQ
Q_ROWS = S_ROWS + A_CHUNK
META_BLOCK = S_ROWS // A_CHUNK

F32 = jnp.float32
BF16 = jnp.bfloat16
NEG_INF = float("-inf")


def _params(n_axes, vmem_mib=VMEM_DEFAULT_MIB):
    return pltpu.CompilerParams(dimension_semantics=("arbitrary",) * n_axes,
                                vmem_limit_bytes=vmem_mib * 1024 * 1024)


def _resident(shape):
    return pl.BlockSpec(shape, lambda i: (0, 0), pipeline_mode=pl.Buffered(1))


def _rms(x, g):
    return x * lax.rsqrt(jnp.mean(x * x, axis=-1, keepdims=True) + EPS) * g


def _dot(a, b):
    return jnp.dot(a, b, preferred_element_type=F32)


def _dot_nt(a, b):
    return lax.dot_general(a, b, (((1,), (1,)), ((), ())), preferred_element_type=F32)


def _dot_tn(a, b):
    return lax.dot_general(a, b, (((0,), (0,)), ((), ())), preferred_element_type=F32)


def _log_sigmoid(x):
    return -(jnp.maximum(-x, 0.0) + jnp.log1p(jnp.exp(-jnp.abs(x))))


def _head_norm64(y, gain):
    lo = lax.broadcasted_iota(jnp.int32, (1, LANES), 1) < B_DH
    cols = []
    for c in range(y.shape[1] // LANES):
        x = y[:, c * LANES:(c + 1) * LANES]
        xx = x * x
        s_lo = jnp.sum(jnp.where(lo, xx, 0.0), axis=-1, keepdims=True)
        s_hi = jnp.sum(jnp.where(lo, 0.0, xx), axis=-1, keepdims=True)
        scale = jnp.where(lo, lax.rsqrt(s_lo / B_DH + EPS), lax.rsqrt(s_hi / B_DH + EPS))
        cols.append(x * scale * gain[:, c * LANES:(c + 1) * LANES])
    return jnp.concatenate(cols, axis=1)


FFN_TF = 512
FFN_CAST_TF = 256


def _ffn_kernel(h_ref, g_ref, wg_ref, wu_ref, wo_ref, o_ref, *rest, emit_bf16):
    j = pl.program_id(1)
    xn_ref = rest[-1]

    @pl.when(j == 0)
    def _():
        h = h_ref[...]
        xn_ref[...] = _rms(h, g_ref[...]).astype(BF16)
        o_ref[...] = h

    if emit_bf16:
        wgb_ref, wub_ref, wob_ref = rest[:-1]
        wg = wg_ref[...].astype(BF16)
        wu = wu_ref[...].astype(BF16)
        wo = wo_ref[...].astype(BF16)
        wob_ref[...] = wo
    else:
        wg, wu, wo = wg_ref[...], wu_ref[...], wo_ref[...]

    xn = xn_ref[...]
    g = _dot(xn, wg)
    u = _dot(xn, wu)
    a = (g / (1.0 + jnp.exp(-g))) * (0.5 * u)
    o_ref[...] += _dot(a.astype(BF16), wo)

    if emit_bf16:
        for part in range(FFN_TF // FFN_CAST_TF):
            @pl.when(j % (FFN_TF // FFN_CAST_TF) == part)
            def _(part=part):
                wgb_ref[:, part * FFN_CAST_TF:(part + 1) * FFN_CAST_TF] = wg
                wub_ref[:, part * FFN_CAST_TF:(part + 1) * FFN_CAST_TF] = wu


def _ffn(h, gain, wg, wu, wo, tm):
    rows = h.shape[0]
    tf = FFN_TF
    n_ff = D_FF // tf
    return pl.pallas_call(
        functools.partial(_ffn_kernel, emit_bf16=False),
        grid=(rows // tm, n_ff),
        in_specs=[
            pl.BlockSpec((tm, D_MODEL), lambda i, j: (i, 0)),
            pl.BlockSpec((1, D_MODEL), lambda i, j: (0, 0)),
            pl.BlockSpec((None, D_MODEL, tf), lambda i, j: (j, 0, 0)),
            pl.BlockSpec((None, D_MODEL, tf), lambda i, j: (j, 0, 0)),
            pl.BlockSpec((tf, D_MODEL), lambda i, j: (j, 0)),
        ],
        out_specs=pl.BlockSpec((tm, D_MODEL), lambda i, j: (i, 0)),
        out_shape=jax.ShapeDtypeStruct((rows, D_MODEL), F32),
        scratch_shapes=[pltpu.VMEM((tm, D_MODEL), BF16)],
        compiler_params=_params(2, VMEM_FFN_MIB),
        name="ffn",
    )(h, gain, wg, wu, wo)


def _ffn_cast(h, gain, w_in, w_out, layer, which):
    rows = h.shape[0]
    tf = FFN_CAST_TF
    n_ff = D_FF // tf
    per = FFN_TF // tf
    return pl.pallas_call(
        functools.partial(_ffn_kernel, emit_bf16=True),
        grid=(1, n_ff),
        in_specs=[
            pl.BlockSpec((rows, D_MODEL), lambda i, j: (0, 0)),
            pl.BlockSpec((1, D_MODEL), lambda i, j: (0, 0)),
            pl.BlockSpec((None, None, D_MODEL, tf), lambda i, j: (layer, which, 0, j)),
            pl.BlockSpec((None, None, D_MODEL, tf), lambda i, j: (layer, which, 0, j + n_ff)),
            pl.BlockSpec((None, None, tf, D_MODEL), lambda i, j: (layer, which, j, 0)),
        ],
        out_specs=[
            pl.BlockSpec((rows, D_MODEL), lambda i, j: (0, 0)),
            pl.BlockSpec((None, D_MODEL, FFN_TF), lambda i, j: (j // per, 0, 0)),
            pl.BlockSpec((None, D_MODEL, FFN_TF), lambda i, j: (j // per, 0, 0)),
            pl.BlockSpec((tf, D_MODEL), lambda i, j: (j, 0)),
        ],
        out_shape=[
            jax.ShapeDtypeStruct((rows, D_MODEL), F32),
            jax.ShapeDtypeStruct((D_FF // FFN_TF, D_MODEL, FFN_TF), BF16),
            jax.ShapeDtypeStruct((D_FF // FFN_TF, D_MODEL, FFN_TF), BF16),
            jax.ShapeDtypeStruct((D_FF, D_MODEL), BF16),
        ],
        scratch_shapes=[pltpu.VMEM((rows, D_MODEL), BF16)],
        compiler_params=_params(2),
        name="ffn_cast",
    )(h, gain, w_in, w_in, w_out)


GATE_ROWS = 2 * A_HEADS


def _inproj_kernel(h_ref, g_ref, w_ref, wgate_ref, bgate_ref, p_ref, gates_ref, *rest, emit_bf16):
    j = pl.program_id(1)
    xn_ref = rest[-1]

    @pl.when(j == 0)
    def _():
        xn = _rms(h_ref[...], g_ref[...]).astype(BF16)
        xn_ref[...] = xn
        wgate = jnp.concatenate([wgate_ref[...], jnp.zeros((LANES - GATE_ROWS, D_MODEL), F32)], axis=0)
        pre = _dot_nt(xn, wgate.astype(BF16)) + bgate_ref[...]
        capped = A_GATE_CAP * jnp.tanh(pre / A_GATE_CAP)
        lane = lax.broadcasted_iota(jnp.int32, (1, LANES), 1)
        gates_ref[...] = jnp.where(lane < A_HEADS, capped, _log_sigmoid(capped))

    if emit_bf16:
        w = w_ref[...].astype(BF16)
        rest[0][...] = w
    else:
        w = w_ref[...]
    p_ref[...] = _dot_nt(xn_ref[...], w)


P_PLANES = A_QKVO // D_MODEL


def _inproj(h, gain, w_t, w_gate_t, bgate, tm, tn, emit_bf16=False):
    rows = h.shape[0]
    assert not emit_bf16 or rows == tm
    per = D_MODEL // tn
    out_specs = [
        pl.BlockSpec((None, tm, tn), lambda i, j: (j // per, i, j % per)),
        pl.BlockSpec((tm, LANES), lambda i, j: (i, 0)),
    ]
    out_shape = [jax.ShapeDtypeStruct((P_PLANES, rows, D_MODEL), F32),
                 jax.ShapeDtypeStruct((rows, LANES), F32)]
    if emit_bf16:
        out_specs.append(pl.BlockSpec((tn, D_MODEL), lambda i, j: (j, 0)))
        out_shape.append(jax.ShapeDtypeStruct((A_QKVO, D_MODEL), BF16))
    return pl.pallas_call(
        functools.partial(_inproj_kernel, emit_bf16=emit_bf16),
        grid=(rows // tm, A_QKVO // tn),
        in_specs=[
            pl.BlockSpec((tm, D_MODEL), lambda i, j: (i, 0)),
            pl.BlockSpec((1, D_MODEL), lambda i, j: (0, 0)),
            pl.BlockSpec((tn, D_MODEL), lambda i, j: (j, 0)),
            pl.BlockSpec((GATE_ROWS, D_MODEL), lambda i, j: (A_QKVO // GATE_ROWS, 0)),
            pl.BlockSpec((1, LANES), lambda i, j: (0, 0)),
        ],
        out_specs=out_specs,
        out_shape=out_shape,
        scratch_shapes=[pltpu.VMEM((tm, D_MODEL), BF16)],
        compiler_params=_params(2),
        name="mlstm_inproj",
    )(h, gain, w_t, w_gate_t, bgate)


def _mlstm_chunk_kernel(q_ref, k_ref, v_ref, g_ref, c0_ref, n0_ref, m0_ref,
                        h_ref, c_ref, n_ref, m_ref, qk_ref, qc_ref, sb_ref, *, n_valid):
    L = A_CHUNK
    heads = range(A_HEADS)

    @pl.when(pl.program_id(1) == 0)
    def _():
        c_ref[...] = c0_ref[...]
        n_ref[...] = n0_ref[...]
        m_ref[...] = m0_ref[...]

    for hd in heads:
        qb = q_ref[:, hd * A_DK:(hd + 1) * A_DK].astype(BF16)
        kb = (k_ref[:, hd * A_DK:(hd + 1) * A_DK] * (A_DK ** -0.5)).astype(BF16)
        qk_ref[hd] = _dot_nt(qb, kb)
        qc_ref[hd] = _dot(qb, c_ref[0, hd].astype(BF16))

    gates = g_ref[...]
    row = lax.broadcasted_iota(jnp.int32, (L, L), 0)
    col = lax.broadcasted_iota(jnp.int32, (L, L), 1)
    causal = col <= row
    eye = col == row
    masked = n_valid < L
    if masked:
        row_ok = lax.broadcasted_iota(jnp.int32, (L, 1), 0) < n_valid
        gates_lf = jnp.where(row_ok, gates, 0.0)
    else:
        gates_lf = gates
    csum = jnp.dot(causal.astype(F32), gates_lf, precision=lax.Precision.HIGHEST,
                   preferred_element_type=F32)

    def to_row(x_col):
        return jnp.sum(jnp.where(eye, x_col, 0.0), axis=0, keepdims=True)

    w_intra, w_inter, floor, w_state, decay, m_new = [], [], [], [], [], []
    for hd in heads:
        b_col = csum[:, A_HEADS + hd:A_HEADS + hd + 1]
        ig_col = gates[:, hd:hd + 1]
        if masked:
            ig_col = jnp.where(row_ok, ig_col, NEG_INF)
        b_row = to_row(b_col)
        ig_row = to_row(ig_col)
        m_prev = m_ref[0, hd][:, 0:1]
        d_log = jnp.where(causal, b_col - b_row + ig_row, NEG_INF)
        inter_log = b_col + m_prev
        m_t = jnp.maximum(inter_log, jnp.max(d_log, axis=-1, keepdims=True))
        w_intra.append(jnp.exp(d_log - m_t))
        w_inter.append(jnp.exp(inter_log - m_t))
        floor.append(jnp.exp(-m_t))
        b_last = b_col[L - 1:L, :]
        w_log = b_last - b_col + ig_col
        m_new.append(jnp.maximum(b_last + m_prev, jnp.max(w_log, axis=0, keepdims=True)))
        w_state.append(jnp.exp(w_log - m_new[hd]))
        decay.append(jnp.exp(b_last + m_prev - m_new[hd]))

    den = []
    for hd in heads:
        s = qk_ref[hd] * w_intra[hd]
        sb_ref[hd] = s.astype(BF16)
        q = q_ref[:, hd * A_DK:(hd + 1) * A_DK]
        d = jnp.sum(s, axis=-1, keepdims=True) + w_inter[hd] * jnp.sum(q * n_ref[0, hd], axis=-1, keepdims=True)
        den.append(jnp.maximum(jnp.abs(d), floor[hd]))

    for hd in heads:
        k = k_ref[:, hd * A_DK:(hd + 1) * A_DK] * (A_DK ** -0.5)
        vb = v_ref[:, hd * A_DV:(hd + 1) * A_DV].astype(BF16)
        num = _dot(sb_ref[hd], vb) + w_inter[hd] * qc_ref[hd]
        h_ref[:, hd * A_DV:(hd + 1) * A_DV] = num / den[hd]
        kw = k * w_state[hd]
        c_ref[0, hd] = decay[hd] * c_ref[0, hd] + _dot_tn(kw.astype(BF16), vb)
        n_ref[0, hd] = decay[hd] * n_ref[0, hd] + jnp.sum(kw, axis=0, keepdims=True)
        m_ref[0, hd] = jnp.broadcast_to(m_new[hd], (1, LANES))


def _mlstm_chunks(p, gates, c0, n0, m0, n_seq, n_chunks, row_block0, n_valid, shared_state):
    L = A_CHUNK
    rb = lambda b, c: row_block0 + b * n_chunks + c
    st = (lambda b, c: (0, 0, 0, 0)) if shared_state else (lambda b, c: (b, 0, 0, 0))
    return pl.pallas_call(
        functools.partial(_mlstm_chunk_kernel, n_valid=n_valid),
        grid=(n_seq, n_chunks),
        in_specs=[
            pl.BlockSpec((None, L, A_HEADS * A_DK), lambda b, c: (0, rb(b, c), 0)),
            pl.BlockSpec((None, L, A_HEADS * A_DK), lambda b, c: (0, rb(b, c), 1)),
            pl.BlockSpec((None, L, A_HEADS * A_DV), lambda b, c: (1, rb(b, c), 0)),
            pl.BlockSpec((L, LANES), lambda b, c: (rb(b, c), 0)),
            pl.BlockSpec((1, A_HEADS, A_DK, A_DV), st),
            pl.BlockSpec((1, A_HEADS, 1, A_DK), st),
            pl.BlockSpec((1, A_HEADS, 1, LANES), st),
        ],
        out_specs=[
            pl.BlockSpec((L, A_HEADS * A_DV), lambda b, c: (b * n_chunks + c, 0)),
            pl.BlockSpec((1, A_HEADS, A_DK, A_DV), lambda b, c: (b, 0, 0, 0)),
            pl.BlockSpec((1, A_HEADS, 1, A_DK), lambda b, c: (b, 0, 0, 0)),
            pl.BlockSpec((1, A_HEADS, 1, LANES), lambda b, c: (b, 0, 0, 0)),
        ],
        out_shape=[
            jax.ShapeDtypeStruct((n_seq * n_chunks * L, D_MODEL), F32),
            jax.ShapeDtypeStruct((n_seq, A_HEADS, A_DK, A_DV), F32),
            jax.ShapeDtypeStruct((n_seq, A_HEADS, 1, A_DK), F32),
            jax.ShapeDtypeStruct((n_seq, A_HEADS, 1, LANES), F32),
        ],
        scratch_shapes=[pltpu.VMEM((A_HEADS, L, L), F32),
                        pltpu.VMEM((A_HEADS, L, A_DV), F32),
                        pltpu.VMEM((A_HEADS, L, L), BF16)],
        compiler_params=_params(2),
        name="mlstm_chunks",
    )(p, p, p, gates, c0, n0, m0)


def _mlstm_sample_kernel(q_ref, k_ref, v_ref, g_ref, c0_ref, n0_ref, m0_ref,
                         h_ref, c_ref, n_ref, m_ref):
    R = 2 * DEC_SEQ
    PAD = A_CHUNK - R
    gates = g_ref[...]
    r_col = lax.broadcasted_iota(jnp.int32, (R, 1), 0)
    is_a = r_col < DEC_SEQ
    row = lax.broadcasted_iota(jnp.int32, (R, LANES), 0)
    lane = lax.broadcasted_iota(jnp.int32, (R, LANES), 1)
    same = ((lane < DEC_SEQ) & (row < DEC_SEQ)) | ((lane >= DEC_SEQ) & (lane < R) & (row >= DEC_SEQ))
    causal = same & (lane <= row)
    eye = lane == row

    def to_row(x_col):
        return jnp.sum(jnp.where(eye, x_col, 0.0), axis=0, keepdims=True)

    for hd in range(A_HEADS):
        lf_col = gates[:, A_HEADS + hd:A_HEADS + hd + 1]
        ig_col = gates[:, hd:hd + 1]
        lf_row = to_row(lf_col)
        ig_row = to_row(ig_col)
        b_col = jnp.sum(jnp.where(causal, lf_row, 0.0), axis=1, keepdims=True)
        b_row = to_row(b_col)
        m_a = m0_ref[0, hd][:, 0:1]
        m_b = m0_ref[1, hd][:, 0:1]
        m_prev = jnp.where(is_a, m_a, m_b)
        c_a = c0_ref[0, hd]
        c_b = c0_ref[1, hd]
        n_a = n0_ref[0, hd]
        n_b = n0_ref[1, hd]

        q = q_ref[:, hd * A_DK:(hd + 1) * A_DK]
        k = k_ref[:, hd * A_DK:(hd + 1) * A_DK] * (A_DK ** -0.5)
        v = v_ref[:, hd * A_DV:(hd + 1) * A_DV]
        qb = q.astype(BF16)
        k_pad = jnp.concatenate([k, jnp.zeros((PAD, A_DK), F32)], axis=0).astype(BF16)
        v_pad = jnp.concatenate([v, jnp.zeros((PAD, A_DV), F32)], axis=0).astype(BF16)

        d_log = jnp.where(causal, b_col - b_row + ig_row, NEG_INF)
        inter_log = b_col + m_prev
        m_t = jnp.maximum(inter_log, jnp.max(d_log, axis=-1, keepdims=True))
        w_intra = jnp.exp(d_log - m_t)
        w_inter = jnp.exp(inter_log - m_t)
        s = _dot_nt(qb, k_pad) * w_intra
        q_c = jnp.where(is_a, _dot(qb, c_a.astype(BF16)), _dot(qb, c_b.astype(BF16)))
        num = _dot(s.astype(BF16), v_pad) + w_inter * q_c
        q_n = jnp.sum(q * jnp.where(is_a, n_a, n_b), axis=-1, keepdims=True)
        den = jnp.sum(s, axis=-1, keepdims=True) + w_inter * q_n
        den = jnp.maximum(jnp.abs(den), jnp.exp(-m_t))
        h_ref[:, hd * A_DV:(hd + 1) * A_DV] = num / den

        for idx, sel, m_x, c_x, n_x in ((0, is_a, m_a, c_a, n_a),
                                        (1, jnp.logical_not(is_a), m_b, c_b, n_b)):
            last = (idx + 1) * DEC_SEQ - 1
            b_last = b_col[last:last + 1, :]
            w_log = jnp.where(sel, b_last - b_col + ig_col, NEG_INF)
            m_new = jnp.maximum(b_last + m_x, jnp.max(w_log, axis=0, keepdims=True))
            w_state = jnp.exp(w_log - m_new)
            decay = jnp.exp(b_last + m_x - m_new)
            kw = k * w_state
            kw_pad = jnp.concatenate([kw, jnp.zeros((PAD, A_DK), F32)], axis=0).astype(BF16)
            c_ref[idx, hd] = decay * c_x + _dot_tn(kw_pad, v_pad)
            n_ref[idx, hd] = decay * n_x + jnp.sum(kw, axis=0, keepdims=True)
            m_ref[idx, hd] = jnp.broadcast_to(m_new, (1, LANES))


def _mlstm_sample(p, gates, c0, n0, m0):
    R = 2 * DEC_SEQ
    n_pairs = DEC_BATCH // 2
    st = lambda i: (i, 0, 0, 0)
    return pl.pallas_call(
        _mlstm_sample_kernel,
        grid=(n_pairs,),
        in_specs=[
            pl.BlockSpec((None, R, A_HEADS * A_DK), lambda i: (0, i, 0)),
            pl.BlockSpec((None, R, A_HEADS * A_DK), lambda i: (0, i, 1)),
            pl.BlockSpec((None, R, A_HEADS * A_DV), lambda i: (1, i, 0)),
            pl.BlockSpec((R, LANES), lambda i: (i, 0)),
            pl.BlockSpec((2, A_HEADS, A_DK, A_DV), st),
            pl.BlockSpec((2, A_HEADS, 1, A_DK), st),
            pl.BlockSpec((2, A_HEADS, 1, LANES), st),
        ],
        out_specs=[
            pl.BlockSpec((R, A_HEADS * A_DV), lambda i: (i, 0)),
            pl.BlockSpec((2, A_HEADS, A_DK, A_DV), st),
            pl.BlockSpec((2, A_HEADS, 1, A_DK), st),
            pl.BlockSpec((2, A_HEADS, 1, LANES), st),
        ],
        out_shape=[
            jax.ShapeDtypeStruct((S_ROWS, D_MODEL), F32),
            jax.ShapeDtypeStruct((DEC_BATCH, A_HEADS, A_DK, A_DV), F32),
            jax.ShapeDtypeStruct((DEC_BATCH, A_HEADS, 1, A_DK), F32),
            jax.ShapeDtypeStruct((DEC_BATCH, A_HEADS, 1, LANES), F32),
        ],
        compiler_params=_params(1),
        name="mlstm_sample",
    )(p, p, p, gates, c0, n0, m0)


def _mlstm_out_kernel(hm_ref, o_ref, hg_ref, w_ref, res_ref, out_ref):
    cols = []
    for hd in range(A_HEADS):
        x = hm_ref[:, hd * A_DV:(hd + 1) * A_DV]
        cols.append(x * lax.rsqrt(jnp.mean(x * x, axis=-1, keepdims=True) + EPS))
    hn = jnp.concatenate(cols, axis=1) * hg_ref[...]
    o = o_ref[...]
    pre = (hn * (1.0 / (1.0 + jnp.exp(-o)))).astype(BF16)
    out_ref[...] = res_ref[...] + _dot(pre, w_ref[...])


def _mlstm_out(hm, p, head_gain, w, res, tm):
    rows = hm.shape[0]
    return pl.pallas_call(
        _mlstm_out_kernel,
        grid=(rows // tm,),
        in_specs=[
            pl.BlockSpec((tm, D_MODEL), lambda i: (i, 0)),
            pl.BlockSpec((None, tm, D_MODEL), lambda i: (2, i, 0)),
            pl.BlockSpec((1, D_MODEL), lambda i: (0, 0)),
            _resident((D_MODEL, D_MODEL)),
            pl.BlockSpec((tm, D_MODEL), lambda i: (i, 0)),
        ],
        out_specs=pl.BlockSpec((tm, D_MODEL), lambda i: (i, 0)),
        out_shape=jax.ShapeDtypeStruct((rows, D_MODEL), F32),
        compiler_params=_params(1),
        name="mlstm_out",
    )(hm, p, head_gain, w, res)


def _normproj_kernel(h_ref, g_ref, w_ref, hg_ref, o_ref, *, n_norm):
    xn = _rms(h_ref[...], g_ref[...]).astype(BF16)
    y = _dot(xn, w_ref[...])
    if n_norm == y.shape[1]:
        o_ref[...] = _head_norm64(y, hg_ref[...])
    else:
        o_ref[:, :n_norm] = _head_norm64(y[:, :n_norm], hg_ref[...])
        o_ref[:, n_norm:] = y[:, n_norm:]


def _normproj(h, gain, w, head_gain, tm):
    rows = h.shape[0]
    n = w.shape[1]
    n_norm = head_gain.shape[1]
    return pl.pallas_call(
        functools.partial(_normproj_kernel, n_norm=n_norm),
        grid=(rows // tm,),
        in_specs=[
            pl.BlockSpec((tm, D_MODEL), lambda i: (i, 0)),
            pl.BlockSpec((1, D_MODEL), lambda i: (0, 0)),
            _resident((D_MODEL, n)),
            pl.BlockSpec((1, n_norm), lambda i: (0, 0)),
        ],
        out_specs=pl.BlockSpec((tm, n), lambda i: (i, 0)),
        out_shape=jax.ShapeDtypeStruct((rows, n), F32),
        compiler_params=_params(1),
        name="normproj",
    )(h, gain, w, head_gain)


def _matres_kernel(x_ref, w_ref, res_ref, o_ref):
    o_ref[...] = res_ref[...] + _dot(x_ref[...].astype(BF16), w_ref[...])


def _matres(x, w, res, tm):
    rows = x.shape[0]
    return pl.pallas_call(
        _matres_kernel,
        grid=(rows // tm,),
        in_specs=[
            pl.BlockSpec((tm, D_MODEL), lambda i: (i, 0)),
            _resident((D_MODEL, D_MODEL)),
            pl.BlockSpec((tm, D_MODEL), lambda i: (i, 0)),
        ],
        out_specs=pl.BlockSpec((tm, D_MODEL), lambda i: (i, 0)),
        out_shape=jax.ShapeDtypeStruct((rows, D_MODEL), F32),
        compiler_params=_params(1),
        name="matres",
    )(x, w, res)


def _alibi_slope(head):
    return 2.0 ** (-8.0 * (head + 1) / B_HEADS)


PROMPT_SUB = WINDOW // 2
PROMPT_BAND = WINDOW + PROMPT_SUB
PROMPT_KEYS = 2 * LANES
HEAD_PAIRS = B_HEADS // 2
PAIRS_PER_GROUP = B_GROUP // 2


def _pair_rhs(x, c):
    col = x[:, c * LANES:(c + 1) * LANES]
    rol = pltpu.roll(col, B_DH, axis=1)
    lo = lax.broadcasted_iota(jnp.int32, (1, LANES), 1) < B_DH
    even = jnp.concatenate([jnp.where(lo, col, 0.0), jnp.where(lo, 0.0, rol)], axis=0)
    odd = jnp.concatenate([jnp.where(lo, rol, 0.0), jnp.where(lo, 0.0, col)], axis=0)
    return even.astype(BF16), odd.astype(BF16)


def _attn_prompt_kernel(q_ref, kvo_ref, kvp_ref, kvm_ref, sink_ref, o_ref, bias_ref, s_ref, p_ref):
    W = WINDOW
    SB = PROMPT_SUB
    NK = PROMPT_KEYS
    first = (pl.program_id(0) == 0) & (pl.program_id(1) == 0)
    j = pl.program_id(1)
    rows = PAIRS_PER_GROUP * SB
    lo = lax.broadcasted_iota(jnp.int32, (1, LANES), 1) < B_DH

    @pl.when(first)
    def _():
        c = lax.broadcasted_iota(jnp.int32, (SB, NK), 1)
        is_meta = c < N_META
        is_sink = c == NK - 1
        for sub in range(2):
            i = lax.broadcasted_iota(jnp.int32, (SB, NK), 0) + sub * SB
            pos = c - N_META + sub * SB
            rel = W + i - pos
            in_band = (c >= N_META) & (c < N_META + PROMPT_BAND) & (rel >= 0) & (rel < W)
            for variant in range(2):
                if variant == 0:
                    dist = jnp.where(is_meta, jnp.minimum(i + N_META - c, W), rel).astype(F32)
                    valid = is_meta | (in_band & (pos >= W))
                else:
                    dist = jnp.where(is_meta, W, rel).astype(F32)
                    valid = is_meta | in_band
                for hd in range(B_HEADS):
                    r0 = (hd // 2) * SB
                    c0 = (hd % 2) * NK
                    table = jnp.where(valid, -_alibi_slope(hd) * dist, NEG_INF)
                    bias_ref[variant, sub, r0:r0 + SB, c0:c0 + NK] = jnp.where(
                        is_sink, sink_ref[0:1, hd:hd + 1], table)

    variant = jnp.minimum(j, 1)
    kvm = kvm_ref[...]
    band = jnp.concatenate([kvp_ref[...], kvo_ref[...]], axis=0)
    pad = jnp.zeros((NK - N_META - PROMPT_BAND, 2 * B_KV), F32)
    ones_rhs = jnp.concatenate([jnp.broadcast_to(jnp.where(lo, 1.0, 0.0), (NK, LANES)),
                                jnp.broadcast_to(jnp.where(lo, 0.0, 1.0), (NK, LANES))], axis=0).astype(BF16)

    sub_rows = B_KV_HEADS * rows
    for sub in range(2):
        keys = jnp.concatenate([kvm, band[sub * SB:sub * SB + PROMPT_BAND], pad], axis=0)
        k_rhs = _pair_rhs(keys, 0) + _pair_rhs(keys, 1)
        v_rhs = _pair_rhs(keys, 2) + _pair_rhs(keys, 3)
        for g in range(B_KV_HEADS):
            p0 = g * PAIRS_PER_GROUP
            r0 = (sub * B_KV_HEADS + g) * rows
            qg = jnp.concatenate([q_ref[sub * SB:(sub + 1) * SB, (p0 + m) * LANES:(p0 + m + 1) * LANES]
                                  for m in range(PAIRS_PER_GROUP)], axis=0) * (B_DH ** -0.5)
            s_ref[r0:r0 + rows, :] = _dot_nt(qg.astype(BF16), k_rhs[g])

        s = s_ref[sub * sub_rows:(sub + 1) * sub_rows, :] + bias_ref[variant, sub]
        halves = []
        for half in range(2):
            sh = s[:, half * NK:(half + 1) * NK]
            halves.append(jnp.exp(sh - jnp.max(sh, axis=-1, keepdims=True)).astype(BF16))
        p_ref[sub * sub_rows:(sub + 1) * sub_rows, :] = jnp.concatenate(halves, axis=1)

        for g in range(B_KV_HEADS):
            p0 = g * PAIRS_PER_GROUP
            r0 = (sub * B_KV_HEADS + g) * rows
            od = _dot(p_ref[r0:r0 + rows, :], jnp.concatenate([v_rhs[g], ones_rhs], axis=1))
            o = od[:, :LANES] / od[:, LANES:]
            for m in range(PAIRS_PER_GROUP):
                o_ref[sub * SB:(sub + 1) * SB, (p0 + m) * LANES:(p0 + m + 1) * LANES] = o[m * SB:(m + 1) * SB, :]


def _attn_prompt(q, kv_p, kv_q, sinks_row):
    nb = SEQ // WINDOW
    blk = lambda b, j: b * nb + j
    return pl.pallas_call(
        _attn_prompt_kernel,
        grid=(BATCH, nb),
        in_specs=[
            pl.BlockSpec((WINDOW, D_MODEL), lambda b, j: (blk(b, j), 0)),
            pl.BlockSpec((WINDOW, 2 * B_KV), lambda b, j: (blk(b, j), 0)),
            pl.BlockSpec((WINDOW, 2 * B_KV), lambda b, j: (blk(b, jnp.maximum(j - 1, 0)), 0)),
            pl.BlockSpec((N_META, 2 * B_KV), lambda b, j: (S_ROWS // N_META, 0)),
            pl.BlockSpec((1, LANES), lambda b, j: (0, 0)),
        ],
        out_specs=pl.BlockSpec((WINDOW, D_MODEL), lambda b, j: (blk(b, j), 0)),
        out_shape=jax.ShapeDtypeStruct((P_ROWS, D_MODEL), F32),
        scratch_shapes=[pltpu.VMEM((2, 2, HEAD_PAIRS * PROMPT_SUB, 2 * PROMPT_KEYS), F32),
                        pltpu.VMEM((2 * HEAD_PAIRS * PROMPT_SUB, 2 * PROMPT_KEYS), F32),
                        pltpu.VMEM((2 * HEAD_PAIRS * PROMPT_SUB, 2 * PROMPT_KEYS), BF16)],
        compiler_params=_params(2),
        name="attn_prompt",
    )(q, kv_p, kv_p, kv_q, sinks_row)


GROUP_SHIFT = B_GROUP.bit_length() - 1
assert 1 << GROUP_SHIFT == B_GROUP
SAMPLE_OLD = SUBLANES
SAMPLE_KEYS = 2 * LANES
SAMPLE_SEQ_PER_STEP = 8
SAMPLE_ROWS = DEC_SEQ * B_GROUP
KV_PAIRS = B_KV_HEADS // 2


def _split_rhs(col):
    lo = lax.broadcasted_iota(jnp.int32, (1, LANES), 1) < B_DH
    return jnp.concatenate([jnp.where(lo, col, 0.0), jnp.where(lo, 0.0, col)], axis=0).astype(BF16)


def _attn_sample_kernel(q_ref, km_ref, vm_ref, ko_ref, vo_ref, kw_ref, vw_ref, sink_ref, o_ref,
                        bias_ref, s_ref, p_ref):
    W = WINDOW
    NK = SAMPLE_KEYS
    R = SAMPLE_ROWS
    off_w = N_META + SAMPLE_OLD
    lo = lax.broadcasted_iota(jnp.int32, (1, LANES), 1) < B_DH

    @pl.when(pl.program_id(0) == 0)
    def _():
        row = lax.broadcasted_iota(jnp.int32, (R, 1), 0)
        r_in_group = jnp.bitwise_and(row, B_GROUP - 1)
        t = jnp.right_shift(lax.broadcasted_iota(jnp.int32, (R, NK), 0), GROUP_SHIFT)
        c = lax.broadcasted_iota(jnp.int32, (R, NK), 1)
        is_meta = c < N_META
        is_old = (c >= N_META) & (c < off_w)
        is_win = (c >= off_w) & (c < off_w + W)
        j_old = c - N_META
        d_win = (W - DEC_SEQ) + t - (c - off_w)
        dist = jnp.where(is_meta, W, jnp.where(is_old, W + t - j_old, d_win)).astype(F32)
        valid = is_meta | (is_old & (j_old > t) & (j_old < DEC_SEQ)) | (is_win & (d_win >= 0))
        for pair in range(KV_PAIRS):
            for e in range(2):
                slope = jnp.zeros((R, 1), F32)
                sink = jnp.zeros((R, 1), F32)
                for r in range(B_GROUP):
                    hd = (2 * pair + e) * B_GROUP + r
                    slope = jnp.where(r_in_group == r, _alibi_slope(hd), slope)
                    sink = jnp.where(r_in_group == r, sink_ref[0:1, hd:hd + 1], sink)
                table = jnp.where(c == NK - 1, sink, jnp.where(valid, -slope * dist, NEG_INF))
                for sq in range(SAMPLE_SEQ_PER_STEP):
                    r0 = (sq * KV_PAIRS + pair) * R
                    bias_ref[r0:r0 + R, e * NK:(e + 1) * NK] = table

    pad = jnp.zeros((NK - off_w - W, B_KV), F32)
    ones_rhs = jnp.concatenate([jnp.broadcast_to(jnp.where(lo, 1.0, 0.0), (NK, LANES)),
                                jnp.broadcast_to(jnp.where(lo, 0.0, 1.0), (NK, LANES))], axis=0).astype(BF16)
    v_rhs = []
    for sq in range(SAMPLE_SEQ_PER_STEP):
        kcat = jnp.concatenate([km_ref[sq], ko_ref[sq], kw_ref[sq], pad], axis=0)
        vcat = jnp.concatenate([vm_ref[sq], vo_ref[sq], vw_ref[sq], pad], axis=0)
        for pair in range(KV_PAIRS):
            r0 = (sq * KV_PAIRS + pair) * R
            k_rhs = _split_rhs(kcat[:, pair * LANES:(pair + 1) * LANES])
            v_rhs.append(_split_rhs(vcat[:, pair * LANES:(pair + 1) * LANES]))
            q = (q_ref[sq, pair] * (B_DH ** -0.5)).astype(BF16)
            s_ref[r0:r0 + R, :] = _dot_nt(q, k_rhs)

    s = s_ref[...] + bias_ref[...]
    halves = []
    for e in range(2):
        sh = s[:, e * NK:(e + 1) * NK]
        halves.append(jnp.exp(sh - jnp.max(sh, axis=-1, keepdims=True)).astype(BF16))
    p_ref[...] = jnp.concatenate(halves, axis=1)

    for sq in range(SAMPLE_SEQ_PER_STEP):
        for pair in range(KV_PAIRS):
            b = sq * KV_PAIRS + pair
            od = _dot(p_ref[b * R:(b + 1) * R, :], jnp.concatenate([v_rhs[b], ones_rhs], axis=1))
            o_ref[sq, pair] = od[:, :LANES] / od[:, LANES:]


def _attn_sample(q4, k_meta, v_meta, k_old, v_old, k_win, v_win, sinks_row):
    nb = SAMPLE_SEQ_PER_STEP
    R = SAMPLE_ROWS
    n_rows = nb * KV_PAIRS * R
    seq3 = lambda rows: pl.BlockSpec((nb, rows, B_KV), lambda i: (i, 0, 0))
    qspec = pl.BlockSpec((nb, KV_PAIRS, R, LANES), lambda i: (i, 0, 0, 0))
    return pl.pallas_call(
        _attn_sample_kernel,
        grid=(DEC_BATCH // nb,),
        in_specs=[qspec, seq3(N_META), seq3(N_META), seq3(SAMPLE_OLD), seq3(SAMPLE_OLD),
                  seq3(WINDOW), seq3(WINDOW), pl.BlockSpec((1, LANES), lambda i: (0, 0))],
        out_specs=qspec,
        out_shape=jax.ShapeDtypeStruct((DEC_BATCH, KV_PAIRS, R, LANES), F32),
        scratch_shapes=[pltpu.VMEM((n_rows, 2 * SAMPLE_KEYS), F32),
                        pltpu.VMEM((n_rows, 2 * SAMPLE_KEYS), F32),
                        pltpu.VMEM((n_rows, 2 * SAMPLE_KEYS), BF16)],
        compiler_params=_params(1),
        name="attn_sample",
    )(q4, k_meta, v_meta, k_old, v_old, k_win, v_win, sinks_row)


def kernel(x_prompt, x_sample, state_C, state_n, state_m, cache_k_meta, cache_v_meta, cache_k_win, cache_v_win, meta_tokens, ffn_norm, w_ffn_in, w_ffn_out, mix_norm, w_a_in, b_a_gate, a_head_norm, w_a_out, kv_norm, w_kv, k_norm, w_q, q_norm, sinks, w_b_out):
    assert x_prompt.shape == (BATCH, SEQ, D_MODEL) and x_sample.shape == (DEC_BATCH, DEC_SEQ, D_MODEL)
    assert w_a_in.shape[0] == 1 and w_q.shape[0] == 1 and ffn_norm.shape[0] == 2

    wa_in_t = jnp.swapaxes(w_a_in[0], 0, 1)
    ba_gate = jnp.pad(b_a_gate[0].astype(F32), (0, LANES - 2 * A_HEADS)).reshape(1, LANES)
    wa_out = w_a_out[0].astype(BF16)
    wkv = w_kv.astype(BF16)
    wq = w_q[0].astype(BF16)
    wb_out = w_b_out[0].astype(BF16)
    row = lambda x: x.astype(F32).reshape(1, -1)
    k_gain = jnp.tile(row(k_norm), (1, B_KV_HEADS))
    q_gain = jnp.tile(row(q_norm[0]), (1, B_HEADS))
    sinks_row = jnp.pad(sinks[0].astype(F32), (0, LANES - B_HEADS)).reshape(1, LANES)

    h_p = x_prompt.reshape(P_ROWS, D_MODEL)
    h_q = jnp.concatenate([x_sample.reshape(S_ROWS, D_MODEL), meta_tokens.astype(F32),
                           jnp.zeros((A_CHUNK - N_META, D_MODEL), F32)], axis=0)
    TM_P, TM_W = 1024, 512

    h_q, *wf = _ffn_cast(h_q, row(ffn_norm[0, 0]), w_ffn_in, w_ffn_out, 0, 0)
    h_p = _ffn(h_p, row(ffn_norm[0, 0]), *wf, TM_P)
    p_q, g_q, wa_in = _inproj(h_q, row(mix_norm[0]), wa_in_t, wa_in_t, ba_gate, Q_ROWS, 1024, emit_bf16=True)
    p_p, g_p = _inproj(h_p, row(mix_norm[0]), wa_in, wa_in_t, ba_gate, TM_W, D_MODEL)

    zc = jnp.zeros((1, A_HEADS, A_DK, A_DV), F32)
    zn = jnp.zeros((1, A_HEADS, 1, A_DK), F32)
    zm = jnp.zeros((1, A_HEADS, 1, LANES), F32)
    hm_m, c_m, n_m, m_m = _mlstm_chunks(p_q, g_q, zc, zn, zm, 1, 1, META_BLOCK, N_META, True)
    hm_p, c_p, n_p, m_p = _mlstm_chunks(p_p, g_p, c_m, n_m, m_m, BATCH, SEQ // A_CHUNK, 0, A_CHUNK, True)
    m0_s = jnp.broadcast_to(state_m[0].astype(F32)[:, :, None, None], (DEC_BATCH, A_HEADS, 1, LANES))
    hm_s, c_s, n_s, m_s = _mlstm_sample(p_q, g_q, state_C[0].astype(F32),
                                        state_n[0].astype(F32)[:, :, None, :], m0_s)
    hm_q = jnp.concatenate([hm_s, hm_m], axis=0)

    h_p = _mlstm_out(hm_p, p_p, row(a_head_norm[0]), wa_out, h_p, TM_W)
    h_q = _mlstm_out(hm_q, p_q, row(a_head_norm[0]), wa_out, h_q, A_CHUNK)
    h_q, *wf = _ffn_cast(h_q, row(ffn_norm[0, 1]), w_ffn_in, w_ffn_out, 0, 1)
    h_p = _ffn(h_p, row(ffn_norm[0, 1]), *wf, TM_P)

    kv_p = _normproj(h_p, row(kv_norm), wkv, k_gain, TM_P)
    kv_q = _normproj(h_q, row(kv_norm), wkv, k_gain, Q_ROWS)
    k_s = kv_q[:S_ROWS, :B_KV].reshape(DEC_BATCH, DEC_SEQ, B_KV_HEADS, B_DH)
    v_s = kv_q[:S_ROWS, B_KV:].reshape(DEC_BATCH, DEC_SEQ, B_KV_HEADS, B_DH)
    k_win_s = jnp.concatenate([cache_k_win, k_s.astype(cache_k_win.dtype)], axis=1)[:, -WINDOW:]
    v_win_s = jnp.concatenate([cache_v_win, v_s.astype(cache_v_win.dtype)], axis=1)[:, -WINDOW:]

    h_s = h_q[:S_ROWS]
    h_s, *wf = _ffn_cast(h_s, row(ffn_norm[1, 0]), w_ffn_in, w_ffn_out, 1, 0)
    h_p = _ffn(h_p, row(ffn_norm[1, 0]), *wf, TM_P)
    q_p = _normproj(h_p, row(mix_norm[1]), wq, q_gain, TM_W)
    q_s = _normproj(h_s, row(mix_norm[1]), wq, q_gain, TM_W)

    o_p = _attn_prompt(q_p, kv_p, kv_q, sinks_row)
    q4 = q_s.reshape(DEC_BATCH, DEC_SEQ, KV_PAIRS, 2, B_GROUP, B_DH).transpose(0, 2, 1, 4, 3, 5)
    q4 = q4.reshape(DEC_BATCH, KV_PAIRS, SAMPLE_ROWS, LANES)
    seq3 = lambda x: x.astype(F32).reshape(DEC_BATCH, -1, B_KV)
    o4 = _attn_sample(q4, seq3(cache_k_meta), seq3(cache_v_meta),
                      seq3(cache_k_win[:, :SAMPLE_OLD]), seq3(cache_v_win[:, :SAMPLE_OLD]),
                      seq3(k_win_s), seq3(v_win_s), sinks_row)
    o_s = o4.reshape(DEC_BATCH, KV_PAIRS, DEC_SEQ, B_GROUP, 2, B_DH).transpose(0, 2, 1, 4, 3, 5)
    o_s = o_s.reshape(S_ROWS, D_MODEL)

    h_p = _matres(o_p, wb_out, h_p, TM_W)
    h_s = _matres(o_s, wb_out, h_s, TM_W)
    h_s, *wf = _ffn_cast(h_s, row(ffn_norm[1, 1]), w_ffn_in, w_ffn_out, 1, 1)
    h_p = _ffn(h_p, row(ffn_norm[1, 1]), *wf, TM_P)

    kv4 = lambda x: x.reshape(x.shape[:-1] + (B_KV_HEADS, B_DH))
    meta_rows = kv_q[S_ROWS:S_ROWS + N_META]
    kv_p3 = kv_p.reshape(BATCH, SEQ, 2 * B_KV)
    st = lambda x, dt: x[None].astype(dt)
    return (
        h_p.reshape(BATCH, SEQ, D_MODEL),
        h_s.reshape(DEC_BATCH, DEC_SEQ, D_MODEL),
        st(c_p, state_C.dtype), st(n_p[:, :, 0, :], state_n.dtype), st(m_p[:, :, 0, 0], state_m.dtype),
        jnp.broadcast_to(kv4(meta_rows[:, :B_KV])[None], (BATCH, N_META, B_KV_HEADS, B_DH)),
        jnp.broadcast_to(kv4(meta_rows[:, B_KV:])[None], (BATCH, N_META, B_KV_HEADS, B_DH)),
        kv4(kv_p3[:, -WINDOW:, :B_KV]), kv4(kv_p3[:, -WINDOW:, B_KV:]),
        st(c_s, state_C.dtype), st(n_s[:, :, 0, :], state_n.dtype), st(m_s[:, :, 0, 0], state_m.dtype),
        k_win_s, v_win_s,
    )
```

```python
import functools

import jax
import jax.numpy as jnp
from jax import lax
from jax.experimental import pallas as pl
from jax.experimental.pallas import tpu as pltpu

D_MODEL = 2048
BATCH = 8
SEQ = 2048
DEC_BATCH = 128
DEC_SEQ = 4
PAST_LEN = 8192
N_META = 16
A_HEADS = 4
A_DV = D_MODEL // A_HEADS
A_DK = A_DV // 2
A_CHUNK = 128
A_GATE_CAP = 15.0
A_QKVO = 2 * A_HEADS * A_DK + 2 * A_HEADS * A_DV
B_HEADS = 32
B_DH = D_MODEL // B_HEADS
B_KV_HEADS = 4
B_GROUP = B_HEADS // B_KV_HEADS
B_KV = B_KV_HEADS * B_DH
WINDOW = 128
D_FF = ((8 * D_MODEL // 3 + 255) // 256) * 256
EPS = 1e-6

LANES = 128
SUBLANES = 8
VMEM_DEFAULT_MIB = 48
VMEM_FFN_MIB = 60

P_ROWS = BATCH * SEQ
S_ROWS = DEC_BATCH * DEC_SEQ
Q_ROWS = S_ROWS + A_CHUNK
META_BLOCK = S_ROWS // A_CHUNK

F32 = jnp.float32
BF16 = jnp.bfloat16
NEG_INF = float("-inf")


def _params(n_axes, vmem_mib=VMEM_DEFAULT_MIB):
    return pltpu.CompilerParams(dimension_semantics=("arbitrary",) * n_axes,
                                vmem_limit_bytes=vmem_mib * 1024 * 1024)


def _resident(shape):
    return pl.BlockSpec(shape, lambda i: (0, 0), pipeline_mode=pl.Buffered(1))


def _rms(x, g):
    return x * lax.rsqrt(jnp.mean(x * x, axis=-1, keepdims=True) + EPS) * g


def _dot(a, b):
    return jnp.dot(a, b, preferred_element_type=F32)


def _dot_nt(a, b):
    return lax.dot_general(a, b, (((1,), (1,)), ((), ())), preferred_element_type=F32)


def _dot_tn(a, b):
    return lax.dot_general(a, b, (((0,), (0,)), ((), ())), preferred_element_type=F32)


def _log_sigmoid(x):
    return -(jnp.maximum(-x, 0.0) + jnp.log1p(jnp.exp(-jnp.abs(x))))


def _head_norm64(y, gain):
    lo = lax.broadcasted_iota(jnp.int32, (1, LANES), 1) < B_DH
    cols = []
    for c in range(y.shape[1] // LANES):
        x = y[:, c * LANES:(c + 1) * LANES]
        xx = x * x
        s_lo = jnp.sum(jnp.where(lo, xx, 0.0), axis=-1, keepdims=True)
        s_hi = jnp.sum(jnp.where(lo, 0.0, xx), axis=-1, keepdims=True)
        scale = jnp.where(lo, lax.rsqrt(s_lo / B_DH + EPS), lax.rsqrt(s_hi / B_DH + EPS))
        cols.append(x * scale * gain[:, c * LANES:(c + 1) * LANES])
    return jnp.concatenate(cols, axis=1)


FFN_TF = 512
FFN_CAST_TF = 256


def _ffn_kernel(h_ref, g_ref, wg_ref, wu_ref, wo_ref, o_ref, *rest, emit_bf16):
    j = pl.program_id(1)
    xn_ref = rest[-1]

    @pl.when(j == 0)
    def _():
        h = h_ref[...]
        xn_ref[...] = _rms(h, g_ref[...]).astype(BF16)
        o_ref[...] = h

    if emit_bf16:
        wgb_ref, wub_ref, wob_ref = rest[:-1]
        wg = wg_ref[...].astype(BF16)
        wu = wu_ref[...].astype(BF16)
        wo = wo_ref[...].astype(BF16)
        wgb_ref[...] = wg
        wub_ref[...] = wu
        wob_ref[...] = wo
    else:
        wg, wu, wo = wg_ref[...], wu_ref[...], wo_ref[...]

    xn = xn_ref[...]
    g = _dot(xn, wg)
    u = _dot(xn, wu)
    a = (g / (1.0 + jnp.exp(-g))) * (0.5 * u)
    o_ref[...] += _dot(a.astype(BF16), wo)


def _ffn(h, gain, wg, wu, wo, tm):
    rows = h.shape[0]
    tf = FFN_TF
    n_ff = D_FF // tf
    return pl.pallas_call(
        functools.partial(_ffn_kernel, emit_bf16=False),
        grid=(rows // tm, n_ff),
        in_specs=[
            pl.BlockSpec((tm, D_MODEL), lambda i, j: (i, 0)),
            pl.BlockSpec((1, D_MODEL), lambda i, j: (0, 0)),
            pl.BlockSpec((D_MODEL, tf), lambda i, j: (0, j)),
            pl.BlockSpec((D_MODEL, tf), lambda i, j: (0, j)),
            pl.BlockSpec((tf, D_MODEL), lambda i, j: (j, 0)),
        ],
        out_specs=pl.BlockSpec((tm, D_MODEL), lambda i, j: (i, 0)),
        out_shape=jax.ShapeDtypeStruct((rows, D_MODEL), F32),
        scratch_shapes=[pltpu.VMEM((tm, D_MODEL), BF16)],
        compiler_params=_params(2, VMEM_FFN_MIB),
        name="ffn",
    )(h, gain, wg, wu, wo)


def _ffn_cast(h, gain, w_in, w_out, layer, which):
    rows = h.shape[0]
    tf = FFN_CAST_TF
    n_ff = D_FF // tf
    return pl.pallas_call(
        functools.partial(_ffn_kernel, emit_bf16=True),
        grid=(1, n_ff),
        in_specs=[
            pl.BlockSpec((rows, D_MODEL), lambda i, j: (0, 0)),
            pl.BlockSpec((1, D_MODEL), lambda i, j: (0, 0)),
            pl.BlockSpec((None, None, D_MODEL, tf), lambda i, j: (layer, which, 0, j)),
            pl.BlockSpec((None, None, D_MODEL, tf), lambda i, j: (layer, which, 0, j + n_ff)),
            pl.BlockSpec((None, None, tf, D_MODEL), lambda i, j: (layer, which, j, 0)),
        ],
        out_specs=[
            pl.BlockSpec((rows, D_MODEL), lambda i, j: (0, 0)),
            pl.BlockSpec((D_MODEL, tf), lambda i, j: (0, j)),
            pl.BlockSpec((D_MODEL, tf), lambda i, j: (0, j)),
            pl.BlockSpec((tf, D_MODEL), lambda i, j: (j, 0)),
        ],
        out_shape=[
            jax.ShapeDtypeStruct((rows, D_MODEL), F32),
            jax.ShapeDtypeStruct((D_MODEL, D_FF), BF16),
            jax.ShapeDtypeStruct((D_MODEL, D_FF), BF16),
            jax.ShapeDtypeStruct((D_FF, D_MODEL), BF16),
        ],
        scratch_shapes=[pltpu.VMEM((rows, D_MODEL), BF16)],
        compiler_params=_params(2),
        name="ffn_cast",
    )(h, gain, w_in, w_in, w_out)


GATE_ROWS = 2 * A_HEADS


def _inproj_kernel(h_ref, g_ref, w_ref, wgate_ref, bgate_ref, p_ref, gates_ref, *rest, emit_bf16):
    j = pl.program_id(1)
    xn_ref = rest[-1]

    @pl.when(j == 0)
    def _():
        xn = _rms(h_ref[...], g_ref[...]).astype(BF16)
        xn_ref[...] = xn
        wgate = jnp.concatenate([wgate_ref[...], jnp.zeros((LANES - GATE_ROWS, D_MODEL), F32)], axis=0)
        pre = _dot_nt(xn, wgate.astype(BF16)) + bgate_ref[...]
        capped = A_GATE_CAP * jnp.tanh(pre / A_GATE_CAP)
        lane = lax.broadcasted_iota(jnp.int32, (1, LANES), 1)
        gates_ref[...] = jnp.where(lane < A_HEADS, capped, _log_sigmoid(capped))

    if emit_bf16:
        w = w_ref[...].astype(BF16)
        rest[0][...] = w
    else:
        w = w_ref[...]
    p_ref[...] = _dot_nt(xn_ref[...], w)


def _inproj(h, gain, w_t, w_gate_t, bgate, tm, emit_bf16=False, tn=1024):
    rows = h.shape[0]
    assert not emit_bf16 or rows == tm
    out_specs = [
        pl.BlockSpec((tm, tn), lambda i, j: (i, j)),
        pl.BlockSpec((tm, LANES), lambda i, j: (i, 0)),
    ]
    out_shape = [jax.ShapeDtypeStruct((rows, A_QKVO), F32),
                 jax.ShapeDtypeStruct((rows, LANES), F32)]
    if emit_bf16:
        out_specs.append(pl.BlockSpec((tn, D_MODEL), lambda i, j: (j, 0)))
        out_shape.append(jax.ShapeDtypeStruct((A_QKVO, D_MODEL), BF16))
    return pl.pallas_call(
        functools.partial(_inproj_kernel, emit_bf16=emit_bf16),
        grid=(rows // tm, A_QKVO // tn),
        in_specs=[
            pl.BlockSpec((tm, D_MODEL), lambda i, j: (i, 0)),
            pl.BlockSpec((1, D_MODEL), lambda i, j: (0, 0)),
            pl.BlockSpec((tn, D_MODEL), lambda i, j: (j, 0)),
            pl.BlockSpec((GATE_ROWS, D_MODEL), lambda i, j: (A_QKVO // GATE_ROWS, 0)),
            pl.BlockSpec((1, LANES), lambda i, j: (0, 0)),
        ],
        out_specs=out_specs,
        out_shape=out_shape,
        scratch_shapes=[pltpu.VMEM((tm, D_MODEL), BF16)],
        compiler_params=_params(2),
        name="mlstm_inproj",
    )(h, gain, w_t, w_gate_t, bgate)


def _mlstm_chunk_kernel(q_ref, k_ref, v_ref, g_ref, c0_ref, n0_ref, m0_ref,
                        h_ref, c_ref, n_ref, m_ref, qk_ref, qc_ref, sb_ref, *, n_valid):
    L = A_CHUNK
    heads = range(A_HEADS)

    @pl.when(pl.program_id(1) == 0)
    def _():
        c_ref[...] = c0_ref[...]
        n_ref[...] = n0_ref[...]
        m_ref[...] = m0_ref[...]

    for hd in heads:
        qb = q_ref[:, hd * A_DK:(hd + 1) * A_DK].astype(BF16)
        kb = (k_ref[:, hd * A_DK:(hd + 1) * A_DK] * (A_DK ** -0.5)).astype(BF16)
        qk_ref[hd] = _dot_nt(qb, kb)
        qc_ref[hd] = _dot(qb, c_ref[0, hd].astype(BF16))

    gates = g_ref[...]
    row = lax.broadcasted_iota(jnp.int32, (L, L), 0)
    col = lax.broadcasted_iota(jnp.int32, (L, L), 1)
    causal = col <= row
    eye = col == row
    masked = n_valid < L
    if masked:
        row_ok = lax.broadcasted_iota(jnp.int32, (L, 1), 0) < n_valid
        gates_lf = jnp.where(row_ok, gates, 0.0)
    else:
        gates_lf = gates
    csum = jnp.dot(causal.astype(F32), gates_lf, precision=lax.Precision.HIGHEST,
                   preferred_element_type=F32)

    def to_row(x_col):
        return jnp.sum(jnp.where(eye, x_col, 0.0), axis=0, keepdims=True)

    w_intra, w_inter, floor, w_state, decay, m_new = [], [], [], [], [], []
    for hd in heads:
        b_col = csum[:, A_HEADS + hd:A_HEADS + hd + 1]
        ig_col = gates[:, hd:hd + 1]
        if masked:
            ig_col = jnp.where(row_ok, ig_col, NEG_INF)
        b_row = to_row(b_col)
        ig_row = to_row(ig_col)
        m_prev = m_ref[0, hd][:, 0:1]
        d_log = jnp.where(causal, b_col - b_row + ig_row, NEG_INF)
        inter_log = b_col + m_prev
        m_t = jnp.maximum(inter_log, jnp.max(d_log, axis=-1, keepdims=True))
        w_intra.append(jnp.exp(d_log - m_t))
        w_inter.append(jnp.exp(inter_log - m_t))
        floor.append(jnp.exp(-m_t))
        b_last = b_col[L - 1:L, :]
        w_log = b_last - b_col + ig_col
        m_new.append(jnp.maximum(b_last + m_prev, jnp.max(w_log, axis=0, keepdims=True)))
        w_state.append(jnp.exp(w_log - m_new[hd]))
        decay.append(jnp.exp(b_last + m_prev - m_new[hd]))

    den = []
    for hd in heads:
        s = qk_ref[hd] * w_intra[hd]
        sb_ref[hd] = s.astype(BF16)
        q = q_ref[:, hd * A_DK:(hd + 1) * A_DK]
        d = jnp.sum(s, axis=-1, keepdims=True) + w_inter[hd] * jnp.sum(q * n_ref[0, hd], axis=-1, keepdims=True)
        den.append(jnp.maximum(jnp.abs(d), floor[hd]))

    for hd in heads:
        k = k_ref[:, hd * A_DK:(hd + 1) * A_DK] * (A_DK ** -0.5)
        vb = v_ref[:, hd * A_DV:(hd + 1) * A_DV].astype(BF16)
        num = _dot(sb_ref[hd], vb) + w_inter[hd] * qc_ref[hd]
        h_ref[:, hd * A_DV:(hd + 1) * A_DV] = num / den[hd]
        kw = k * w_state[hd]
        c_ref[0, hd] = decay[hd] * c_ref[0, hd] + _dot_tn(kw.astype(BF16), vb)
        n_ref[0, hd] = decay[hd] * n_ref[0, hd] + jnp.sum(kw, axis=0, keepdims=True)
        m_ref[0, hd] = jnp.broadcast_to(m_new[hd], (1, LANES))


def _mlstm_chunks(p, gates, c0, n0, m0, n_seq, n_chunks, row_block0, n_valid, shared_state):
    L = A_CHUNK
    rb = lambda b, c: row_block0 + b * n_chunks + c
    st = (lambda b, c: (0, 0, 0, 0)) if shared_state else (lambda b, c: (b, 0, 0, 0))
    return pl.pallas_call(
        functools.partial(_mlstm_chunk_kernel, n_valid=n_valid),
        grid=(n_seq, n_chunks),
        in_specs=[
            pl.BlockSpec((L, A_HEADS * A_DK), lambda b, c: (rb(b, c), 0)),
            pl.BlockSpec((L, A_HEADS * A_DK), lambda b, c: (rb(b, c), 1)),
            pl.BlockSpec((L, A_HEADS * A_DV), lambda b, c: (rb(b, c), 1)),
            pl.BlockSpec((L, LANES), lambda b, c: (rb(b, c), 0)),
            pl.BlockSpec((1, A_HEADS, A_DK, A_DV), st),
            pl.BlockSpec((1, A_HEADS, 1, A_DK), st),
            pl.BlockSpec((1, A_HEADS, 1, LANES), st),
        ],
        out_specs=[
            pl.BlockSpec((L, A_HEADS * A_DV), lambda b, c: (b * n_chunks + c, 0)),
            pl.BlockSpec((1, A_HEADS, A_DK, A_DV), lambda b, c: (b, 0, 0, 0)),
            pl.BlockSpec((1, A_HEADS, 1, A_DK), lambda b, c: (b, 0, 0, 0)),
            pl.BlockSpec((1, A_HEADS, 1, LANES), lambda b, c: (b, 0, 0, 0)),
        ],
        out_shape=[
            jax.ShapeDtypeStruct((n_seq * n_chunks * L, D_MODEL), F32),
            jax.ShapeDtypeStruct((n_seq, A_HEADS, A_DK, A_DV), F32),
            jax.ShapeDtypeStruct((n_seq, A_HEADS, 1, A_DK), F32),
            jax.ShapeDtypeStruct((n_seq, A_HEADS, 1, LANES), F32),
        ],
        scratch_shapes=[pltpu.VMEM((A_HEADS, L, L), F32),
                        pltpu.VMEM((A_HEADS, L, A_DV), F32),
                        pltpu.VMEM((A_HEADS, L, L), BF16)],
        compiler_params=_params(2),
        name="mlstm_chunks",
    )(p, p, p, gates, c0, n0, m0)


def _mlstm_sample_kernel(q_ref, k_ref, v_ref, g_ref, c0_ref, n0_ref, m0_ref,
                         h_ref, c_ref, n_ref, m_ref):
    R = 2 * DEC_SEQ
    PAD = A_CHUNK - R
    gates = g_ref[...]
    r_col = lax.broadcasted_iota(jnp.int32, (R, 1), 0)
    is_a = r_col < DEC_SEQ
    row = lax.broadcasted_iota(jnp.int32, (R, LANES), 0)
    lane = lax.broadcasted_iota(jnp.int32, (R, LANES), 1)
    same = ((lane < DEC_SEQ) & (row < DEC_SEQ)) | ((lane >= DEC_SEQ) & (lane < R) & (row >= DEC_SEQ))
    causal = same & (lane <= row)
    eye = lane == row

    def to_row(x_col):
        return jnp.sum(jnp.where(eye, x_col, 0.0), axis=0, keepdims=True)

    for hd in range(A_HEADS):
        lf_col = gates[:, A_HEADS + hd:A_HEADS + hd + 1]
        ig_col = gates[:, hd:hd + 1]
        lf_row = to_row(lf_col)
        ig_row = to_row(ig_col)
        b_col = jnp.sum(jnp.where(causal, lf_row, 0.0), axis=1, keepdims=True)
        b_row = to_row(b_col)
        m_a = m0_ref[0, hd][:, 0:1]
        m_b = m0_ref[1, hd][:, 0:1]
        m_prev = jnp.where(is_a, m_a, m_b)
        c_a = c0_ref[0, hd]
        c_b = c0_ref[1, hd]
        n_a = n0_ref[0, hd]
        n_b = n0_ref[1, hd]

        q = q_ref[:, hd * A_DK:(hd + 1) * A_DK]
        k = k_ref[:, hd * A_DK:(hd + 1) * A_DK] * (A_DK ** -0.5)
        v = v_ref[:, hd * A_DV:(hd + 1) * A_DV]
        qb = q.astype(BF16)
        k_pad = jnp.concatenate([k, jnp.zeros((PAD, A_DK), F32)], axis=0).astype(BF16)
        v_pad = jnp.concatenate([v, jnp.zeros((PAD, A_DV), F32)], axis=0).astype(BF16)

        d_log = jnp.where(causal, b_col - b_row + ig_row, NEG_INF)
        inter_log = b_col + m_prev
        m_t = jnp.maximum(inter_log, jnp.max(d_log, axis=-1, keepdims=True))
        w_intra = jnp.exp(d_log - m_t)
        w_inter = jnp.exp(inter_log - m_t)
        s = _dot_nt(qb, k_pad) * w_intra
        q_c = jnp.where(is_a, _dot(qb, c_a.astype(BF16)), _dot(qb, c_b.astype(BF16)))
        num = _dot(s.astype(BF16), v_pad) + w_inter * q_c
        q_n = jnp.sum(q * jnp.where(is_a, n_a, n_b), axis=-1, keepdims=True)
        den = jnp.sum(s, axis=-1, keepdims=True) + w_inter * q_n
        den = jnp.maximum(jnp.abs(den), jnp.exp(-m_t))
        h_ref[:, hd * A_DV:(hd + 1) * A_DV] = num / den

        for idx, sel, m_x, c_x, n_x in ((0, is_a, m_a, c_a, n_a),
                                        (1, jnp.logical_not(is_a), m_b, c_b, n_b)):
            last = (idx + 1) * DEC_SEQ - 1
            b_last = b_col[last:last + 1, :]
            w_log = jnp.where(sel, b_last - b_col + ig_col, NEG_INF)
            m_new = jnp.maximum(b_last + m_x, jnp.max(w_log, axis=0, keepdims=True))
            w_state = jnp.exp(w_log - m_new)
            decay = jnp.exp(b_last + m_x - m_new)
            kw = k * w_state
            kw_pad = jnp.concatenate([kw, jnp.zeros((PAD, A_DK), F32)], axis=0).astype(BF16)
            c_ref[idx, hd] = decay * c_x + _dot_tn(kw_pad, v_pad)
            n_ref[idx, hd] = decay * n_x + jnp.sum(kw, axis=0, keepdims=True)
            m_ref[idx, hd] = jnp.broadcast_to(m_new, (1, LANES))


def _mlstm_sample(p, gates, c0, n0, m0):
    R = 2 * DEC_SEQ
    n_pairs = DEC_BATCH // 2
    st = lambda i: (i, 0, 0, 0)
    return pl.pallas_call(
        _mlstm_sample_kernel,
        grid=(n_pairs,),
        in_specs=[
            pl.BlockSpec((R, A_HEADS * A_DK), lambda i: (i, 0)),
            pl.BlockSpec((R, A_HEADS * A_DK), lambda i: (i, 1)),
            pl.BlockSpec((R, A_HEADS * A_DV), lambda i: (i, 1)),
            pl.BlockSpec((R, LANES), lambda i: (i, 0)),
            pl.BlockSpec((2, A_HEADS, A_DK, A_DV), st),
            pl.BlockSpec((2, A_HEADS, 1, A_DK), st),
            pl.BlockSpec((2, A_HEADS, 1, LANES), st),
        ],
        out_specs=[
            pl.BlockSpec((R, A_HEADS * A_DV), lambda i: (i, 0)),
            pl.BlockSpec((2, A_HEADS, A_DK, A_DV), st),
            pl.BlockSpec((2, A_HEADS, 1, A_DK), st),
            pl.BlockSpec((2, A_HEADS, 1, LANES), st),
        ],
        out_shape=[
            jax.ShapeDtypeStruct((S_ROWS, D_MODEL), F32),
            jax.ShapeDtypeStruct((DEC_BATCH, A_HEADS, A_DK, A_DV), F32),
            jax.ShapeDtypeStruct((DEC_BATCH, A_HEADS, 1, A_DK), F32),
            jax.ShapeDtypeStruct((DEC_BATCH, A_HEADS, 1, LANES), F32),
        ],
        compiler_params=_params(1),
        name="mlstm_sample",
    )(p, p, p, gates, c0, n0, m0)


def _mlstm_out_kernel(hm_ref, o_ref, hg_ref, w_ref, res_ref, out_ref):
    cols = []
    for hd in range(A_HEADS):
        x = hm_ref[:, hd * A_DV:(hd + 1) * A_DV]
        cols.append(x * lax.rsqrt(jnp.mean(x * x, axis=-1, keepdims=True) + EPS))
    hn = jnp.concatenate(cols, axis=1) * hg_ref[...]
    o = o_ref[...]
    pre = (hn * (1.0 / (1.0 + jnp.exp(-o)))).astype(BF16)
    out_ref[...] = res_ref[...] + _dot(pre, w_ref[...])


def _mlstm_out(hm, p, head_gain, w, res, tm):
    rows = hm.shape[0]
    return pl.pallas_call(
        _mlstm_out_kernel,
        grid=(rows // tm,),
        in_specs=[
            pl.BlockSpec((tm, D_MODEL), lambda i: (i, 0)),
            pl.BlockSpec((tm, D_MODEL), lambda i: (i, 2)),
            pl.BlockSpec((1, D_MODEL), lambda i: (0, 0)),
            _resident((D_MODEL, D_MODEL)),
            pl.BlockSpec((tm, D_MODEL), lambda i: (i, 0)),
        ],
        out_specs=pl.BlockSpec((tm, D_MODEL), lambda i: (i, 0)),
        out_shape=jax.ShapeDtypeStruct((rows, D_MODEL), F32),
        compiler_params=_params(1),
        name="mlstm_out",
    )(hm, p, head_gain, w, res)


def _normproj_kernel(h_ref, g_ref, w_ref, hg_ref, o_ref, *, n_norm):
    xn = _rms(h_ref[...], g_ref[...]).astype(BF16)
    y = _dot(xn, w_ref[...])
    if n_norm == y.shape[1]:
        o_ref[...] = _head_norm64(y, hg_ref[...])
    else:
        o_ref[:, :n_norm] = _head_norm64(y[:, :n_norm], hg_ref[...])
        o_ref[:, n_norm:] = y[:, n_norm:]


def _normproj(h, gain, w, head_gain, tm):
    rows = h.shape[0]
    n = w.shape[1]
    n_norm = head_gain.shape[1]
    return pl.pallas_call(
        functools.partial(_normproj_kernel, n_norm=n_norm),
        grid=(rows // tm,),
        in_specs=[
            pl.BlockSpec((tm, D_MODEL), lambda i: (i, 0)),
            pl.BlockSpec((1, D_MODEL), lambda i: (0, 0)),
            _resident((D_MODEL, n)),
            pl.BlockSpec((1, n_norm), lambda i: (0, 0)),
        ],
        out_specs=pl.BlockSpec((tm, n), lambda i: (i, 0)),
        out_shape=jax.ShapeDtypeStruct((rows, n), F32),
        compiler_params=_params(1),
        name="normproj",
    )(h, gain, w, head_gain)


def _matres_kernel(x_ref, w_ref, res_ref, o_ref):
    o_ref[...] = res_ref[...] + _dot(x_ref[...].astype(BF16), w_ref[...])


def _matres(x, w, res, tm):
    rows = x.shape[0]
    return pl.pallas_call(
        _matres_kernel,
        grid=(rows // tm,),
        in_specs=[
            pl.BlockSpec((tm, D_MODEL), lambda i: (i, 0)),
            _resident((D_MODEL, D_MODEL)),
            pl.BlockSpec((tm, D_MODEL), lambda i: (i, 0)),
        ],
        out_specs=pl.BlockSpec((tm, D_MODEL), lambda i: (i, 0)),
        out_shape=jax.ShapeDtypeStruct((rows, D_MODEL), F32),
        compiler_params=_params(1),
        name="matres",
    )(x, w, res)


def _alibi_slope(head):
    return 2.0 ** (-8.0 * (head + 1) / B_HEADS)


PROMPT_SUB = WINDOW // 2
PROMPT_BAND = WINDOW + PROMPT_SUB
PROMPT_KEYS = 2 * LANES
HEAD_PAIRS = B_HEADS // 2
PAIRS_PER_GROUP = B_GROUP // 2


def _pair_rhs(x, c):
    col = x[:, c * LANES:(c + 1) * LANES]
    rol = pltpu.roll(col, B_DH, axis=1)
    lo = lax.broadcasted_iota(jnp.int32, (1, LANES), 1) < B_DH
    even = jnp.concatenate([jnp.where(lo, col, 0.0), jnp.where(lo, 0.0, rol)], axis=0)
    odd = jnp.concatenate([jnp.where(lo, rol, 0.0), jnp.where(lo, 0.0, col)], axis=0)
    return even.astype(BF16), odd.astype(BF16)


def _attn_prompt_kernel(q_ref, kvo_ref, kvp_ref, kvm_ref, sink_ref, o_ref, bias_ref, s_ref, p_ref):
    W = WINDOW
    SB = PROMPT_SUB
    NK = PROMPT_KEYS
    first = (pl.program_id(0) == 0) & (pl.program_id(1) == 0)
    j = pl.program_id(1)
    rows = PAIRS_PER_GROUP * SB
    lo = lax.broadcasted_iota(jnp.int32, (1, LANES), 1) < B_DH

    @pl.when(first)
    def _():
        c = lax.broadcasted_iota(jnp.int32, (SB, NK), 1)
        is_meta = c < N_META
        is_sink = c == NK - 1
        for sub in range(2):
            i = lax.broadcasted_iota(jnp.int32, (SB, NK), 0) + sub * SB
            pos = c - N_META + sub * SB
            rel = W + i - pos
            in_band = (c >= N_META) & (c < N_META + PROMPT_BAND) & (rel >= 0) & (rel < W)
            for variant in range(2):
                if variant == 0:
                    dist = jnp.where(is_meta, jnp.minimum(i + N_META - c, W), rel).astype(F32)
                    valid = is_meta | (in_band & (pos >= W))
                else:
                    dist = jnp.where(is_meta, W, rel).astype(F32)
                    valid = is_meta | in_band
                for hd in range(B_HEADS):
                    r0 = (hd // 2) * SB
                    c0 = (hd % 2) * NK
                    table = jnp.where(valid, -_alibi_slope(hd) * dist, NEG_INF)
                    bias_ref[variant, sub, r0:r0 + SB, c0:c0 + NK] = jnp.where(
                        is_sink, sink_ref[0:1, hd:hd + 1], table)

    variant = jnp.minimum(j, 1)
    kvm = kvm_ref[...]
    band = jnp.concatenate([kvp_ref[...], kvo_ref[...]], axis=0)
    pad = jnp.zeros((NK - N_META - PROMPT_BAND, 2 * B_KV), F32)
    ones_rhs = jnp.concatenate([jnp.broadcast_to(jnp.where(lo, 1.0, 0.0), (NK, LANES)),
                                jnp.broadcast_to(jnp.where(lo, 0.0, 1.0), (NK, LANES))], axis=0).astype(BF16)

    sub_rows = B_KV_HEADS * rows
    for sub in range(2):
        keys = jnp.concatenate([kvm, band[sub * SB:sub * SB + PROMPT_BAND], pad], axis=0)
        k_rhs = _pair_rhs(keys, 0) + _pair_rhs(keys, 1)
        v_rhs = _pair_rhs(keys, 2) + _pair_rhs(keys, 3)
        for g in range(B_KV_HEADS):
            p0 = g * PAIRS_PER_GROUP
            r0 = (sub * B_KV_HEADS + g) * rows
            qg = jnp.concatenate([q_ref[sub * SB:(sub + 1) * SB, (p0 + m) * LANES:(p0 + m + 1) * LANES]
                                  for m in range(PAIRS_PER_GROUP)], axis=0) * (B_DH ** -0.5)
            s_ref[r0:r0 + rows, :] = _dot_nt(qg.astype(BF16), k_rhs[g])

        s = s_ref[sub * sub_rows:(sub + 1) * sub_rows, :] + bias_ref[variant, sub]
        halves = []
        for half in range(2):
            sh = s[:, half * NK:(half + 1) * NK]
            halves.append(jnp.exp(sh - jnp.max(sh, axis=-1, keepdims=True)).astype(BF16))
        p_ref[sub * sub_rows:(sub + 1) * sub_rows, :] = jnp.concatenate(halves, axis=1)

        for g in range(B_KV_HEADS):
            p0 = g * PAIRS_PER_GROUP
            r0 = (sub * B_KV_HEADS + g) * rows
            od = _dot(p_ref[r0:r0 + rows, :], jnp.concatenate([v_rhs[g], ones_rhs], axis=1))
            o = od[:, :LANES] / od[:, LANES:]
            for m in range(PAIRS_PER_GROUP):
                o_ref[sub * SB:(sub + 1) * SB, (p0 + m) * LANES:(p0 + m + 1) * LANES] = o[m * SB:(m + 1) * SB, :]


def _attn_prompt(q, kv_p, kv_q, sinks_row):
    nb = SEQ // WINDOW
    blk = lambda b, j: b * nb + j
    return pl.pallas_call(
        _attn_prompt_kernel,
        grid=(BATCH, nb),
        in_specs=[
            pl.BlockSpec((WINDOW, D_MODEL), lambda b, j: (blk(b, j), 0)),
            pl.BlockSpec((WINDOW, 2 * B_KV), lambda b, j: (blk(b, j), 0)),
            pl.BlockSpec((WINDOW, 2 * B_KV), lambda b, j: (blk(b, jnp.maximum(j - 1, 0)), 0)),
            pl.BlockSpec((N_META, 2 * B_KV), lambda b, j: (S_ROWS // N_META, 0)),
            pl.BlockSpec((1, LANES), lambda b, j: (0, 0)),
        ],
        out_specs=pl.BlockSpec((WINDOW, D_MODEL), lambda b, j: (blk(b, j), 0)),
        out_shape=jax.ShapeDtypeStruct((P_ROWS, D_MODEL), F32),
        scratch_shapes=[pltpu.VMEM((2, 2, HEAD_PAIRS * PROMPT_SUB, 2 * PROMPT_KEYS), F32),
                        pltpu.VMEM((2 * HEAD_PAIRS * PROMPT_SUB, 2 * PROMPT_KEYS), F32),
                        pltpu.VMEM((2 * HEAD_PAIRS * PROMPT_SUB, 2 * PROMPT_KEYS), BF16)],
        compiler_params=_params(2),
        name="attn_prompt",
    )(q, kv_p, kv_p, kv_q, sinks_row)


GROUP_SHIFT = B_GROUP.bit_length() - 1
assert 1 << GROUP_SHIFT == B_GROUP
SAMPLE_OLD = SUBLANES
SAMPLE_KEYS = 2 * LANES
SAMPLE_SEQ_PER_STEP = 8
SAMPLE_ROWS = DEC_SEQ * B_GROUP
KV_PAIRS = B_KV_HEADS // 2


def _split_rhs(col):
    lo = lax.broadcasted_iota(jnp.int32, (1, LANES), 1) < B_DH
    return jnp.concatenate([jnp.where(lo, col, 0.0), jnp.where(lo, 0.0, col)], axis=0).astype(BF16)


def _attn_sample_kernel(q_ref, km_ref, vm_ref, ko_ref, vo_ref, kw_ref, vw_ref, sink_ref, o_ref,
                        bias_ref, s_ref, p_ref):
    W = WINDOW
    NK = SAMPLE_KEYS
    R = SAMPLE_ROWS
    off_w = N_META + SAMPLE_OLD
    lo = lax.broadcasted_iota(jnp.int32, (1, LANES), 1) < B_DH

    @pl.when(pl.program_id(0) == 0)
    def _():
        row = lax.broadcasted_iota(jnp.int32, (R, 1), 0)
        r_in_group = jnp.bitwise_and(row, B_GROUP - 1)
        t = jnp.right_shift(lax.broadcasted_iota(jnp.int32, (R, NK), 0), GROUP_SHIFT)
        c = lax.broadcasted_iota(jnp.int32, (R, NK), 1)
        is_meta = c < N_META
        is_old = (c >= N_META) & (c < off_w)
        is_win = (c >= off_w) & (c < off_w + W)
        j_old = c - N_META
        d_win = (W - DEC_SEQ) + t - (c - off_w)
        dist = jnp.where(is_meta, W, jnp.where(is_old, W + t - j_old, d_win)).astype(F32)
        valid = is_meta | (is_old & (j_old > t) & (j_old < DEC_SEQ)) | (is_win & (d_win >= 0))
        for pair in range(KV_PAIRS):
            for e in range(2):
                slope = jnp.zeros((R, 1), F32)
                sink = jnp.zeros((R, 1), F32)
                for r in range(B_GROUP):
                    hd = (2 * pair + e) * B_GROUP + r
                    slope = jnp.where(r_in_group == r, _alibi_slope(hd), slope)
                    sink = jnp.where(r_in_group == r, sink_ref[0:1, hd:hd + 1], sink)
                table = jnp.where(c == NK - 1, sink, jnp.where(valid, -slope * dist, NEG_INF))
                for sq in range(SAMPLE_SEQ_PER_STEP):
                    r0 = (sq * KV_PAIRS + pair) * R
                    bias_ref[r0:r0 + R, e * NK:(e + 1) * NK] = table

    pad = jnp.zeros((NK - off_w - W, B_KV), F32)
    ones_rhs = jnp.concatenate([jnp.broadcast_to(jnp.where(lo, 1.0, 0.0), (NK, LANES)),
                                jnp.broadcast_to(jnp.where(lo, 0.0, 1.0), (NK, LANES))], axis=0).astype(BF16)
    v_rhs = []
    for sq in range(SAMPLE_SEQ_PER_STEP):
        kcat = jnp.concatenate([km_ref[sq], ko_ref[sq], kw_ref[sq], pad], axis=0)
        vcat = jnp.concatenate([vm_ref[sq], vo_ref[sq], vw_ref[sq], pad], axis=0)
        for pair in range(KV_PAIRS):
            r0 = (sq * KV_PAIRS + pair) * R
            k_rhs = _split_rhs(kcat[:, pair * LANES:(pair + 1) * LANES])
            v_rhs.append(_split_rhs(vcat[:, pair * LANES:(pair + 1) * LANES]))
            q = (q_ref[sq, pair] * (B_DH ** -0.5)).astype(BF16)
            s_ref[r0:r0 + R, :] = _dot_nt(q, k_rhs)

    s = s_ref[...] + bias_ref[...]
    halves = []
    for e in range(2):
        sh = s[:, e * NK:(e + 1) * NK]
        halves.append(jnp.exp(sh - jnp.max(sh, axis=-1, keepdims=True)).astype(BF16))
    p_ref[...] = jnp.concatenate(halves, axis=1)

    for sq in range(SAMPLE_SEQ_PER_STEP):
        for pair in range(KV_PAIRS):
            b = sq * KV_PAIRS + pair
            od = _dot(p_ref[b * R:(b + 1) * R, :], jnp.concatenate([v_rhs[b], ones_rhs], axis=1))
            o_ref[sq, pair] = od[:, :LANES] / od[:, LANES:]


def _attn_sample(q4, k_meta, v_meta, k_old, v_old, k_win, v_win, sinks_row):
    nb = SAMPLE_SEQ_PER_STEP
    R = SAMPLE_ROWS
    n_rows = nb * KV_PAIRS * R
    seq3 = lambda rows: pl.BlockSpec((nb, rows, B_KV), lambda i: (i, 0, 0))
    qspec = pl.BlockSpec((nb, KV_PAIRS, R, LANES), lambda i: (i, 0, 0, 0))
    return pl.pallas_call(
        _attn_sample_kernel,
        grid=(DEC_BATCH // nb,),
        in_specs=[qspec, seq3(N_META), seq3(N_META), seq3(SAMPLE_OLD), seq3(SAMPLE_OLD),
                  seq3(WINDOW), seq3(WINDOW), pl.BlockSpec((1, LANES), lambda i: (0, 0))],
        out_specs=qspec,
        out_shape=jax.ShapeDtypeStruct((DEC_BATCH, KV_PAIRS, R, LANES), F32),
        scratch_shapes=[pltpu.VMEM((n_rows, 2 * SAMPLE_KEYS), F32),
                        pltpu.VMEM((n_rows, 2 * SAMPLE_KEYS), F32),
                        pltpu.VMEM((n_rows, 2 * SAMPLE_KEYS), BF16)],
        compiler_params=_params(1),
        name="attn_sample",
    )(q4, k_meta, v_meta, k_old, v_old, k_win, v_win, sinks_row)


def kernel(x_prompt, x_sample, state_C, state_n, state_m, cache_k_meta, cache_v_meta, cache_k_win, cache_v_win, meta_tokens, ffn_norm, w_ffn_in, w_ffn_out, mix_norm, w_a_in, b_a_gate, a_head_norm, w_a_out, kv_norm, w_kv, k_norm, w_q, q_norm, sinks, w_b_out):
    assert x_prompt.shape == (BATCH, SEQ, D_MODEL) and x_sample.shape == (DEC_BATCH, DEC_SEQ, D_MODEL)
    assert w_a_in.shape[0] == 1 and w_q.shape[0] == 1 and ffn_norm.shape[0] == 2

    wa_in_t = jnp.swapaxes(w_a_in[0], 0, 1)
    ba_gate = jnp.pad(b_a_gate[0].astype(F32), (0, LANES - 2 * A_HEADS)).reshape(1, LANES)
    wa_out = w_a_out[0].astype(BF16)
    wkv = w_kv.astype(BF16)
    wq = w_q[0].astype(BF16)
    wb_out = w_b_out[0].astype(BF16)
    row = lambda x: x.astype(F32).reshape(1, -1)
    k_gain = jnp.tile(row(k_norm), (1, B_KV_HEADS))
    q_gain = jnp.tile(row(q_norm[0]), (1, B_HEADS))
    sinks_row = jnp.pad(sinks[0].astype(F32), (0, LANES - B_HEADS)).reshape(1, LANES)

    h_p = x_prompt.reshape(P_ROWS, D_MODEL)
    h_q = jnp.concatenate([x_sample.reshape(S_ROWS, D_MODEL), meta_tokens.astype(F32),
                           jnp.zeros((A_CHUNK - N_META, D_MODEL), F32)], axis=0)
    TM_P, TM_W = 1024, 512

    h_q, *wf = _ffn_cast(h_q, row(ffn_norm[0, 0]), w_ffn_in, w_ffn_out, 0, 0)
    h_p = _ffn(h_p, row(ffn_norm[0, 0]), *wf, TM_P)
    p_q, g_q, wa_in = _inproj(h_q, row(mix_norm[0]), wa_in_t, wa_in_t, ba_gate, Q_ROWS, emit_bf16=True)
    p_p, g_p = _inproj(h_p, row(mix_norm[0]), wa_in, wa_in_t, ba_gate, TM_P)

    zc = jnp.zeros((1, A_HEADS, A_DK, A_DV), F32)
    zn = jnp.zeros((1, A_HEADS, 1, A_DK), F32)
    zm = jnp.zeros((1, A_HEADS, 1, LANES), F32)
    hm_m, c_m, n_m, m_m = _mlstm_chunks(p_q, g_q, zc, zn, zm, 1, 1, META_BLOCK, N_META, True)
    hm_p, c_p, n_p, m_p = _mlstm_chunks(p_p, g_p, c_m, n_m, m_m, BATCH, SEQ // A_CHUNK, 0, A_CHUNK, True)
    m0_s = jnp.broadcast_to(state_m[0].astype(F32)[:, :, None, None], (DEC_BATCH, A_HEADS, 1, LANES))
    hm_s, c_s, n_s, m_s = _mlstm_sample(p_q, g_q, state_C[0].astype(F32),
                                        state_n[0].astype(F32)[:, :, None, :], m0_s)
    hm_q = jnp.concatenate([hm_s, hm_m], axis=0)

    h_p = _mlstm_out(hm_p, p_p, row(a_head_norm[0]), wa_out, h_p, TM_W)
    h_q = _mlstm_out(hm_q, p_q, row(a_head_norm[0]), wa_out, h_q, A_CHUNK)
    h_q, *wf = _ffn_cast(h_q, row(ffn_norm[0, 1]), w_ffn_in, w_ffn_out, 0, 1)
    h_p = _ffn(h_p, row(ffn_norm[0, 1]), *wf, TM_P)

    kv_p = _normproj(h_p, row(kv_norm), wkv, k_gain, TM_P)
    kv_q = _normproj(h_q, row(kv_norm), wkv, k_gain, Q_ROWS)
    k_s = kv_q[:S_ROWS, :B_KV].reshape(DEC_BATCH, DEC_SEQ, B_KV_HEADS, B_DH)
    v_s = kv_q[:S_ROWS, B_KV:].reshape(DEC_BATCH, DEC_SEQ, B_KV_HEADS, B_DH)
    k_win_s = jnp.concatenate([cache_k_win, k_s.astype(cache_k_win.dtype)], axis=1)[:, -WINDOW:]
    v_win_s = jnp.concatenate([cache_v_win, v_s.astype(cache_v_win.dtype)], axis=1)[:, -WINDOW:]

    h_s = h_q[:S_ROWS]
    h_s, *wf = _ffn_cast(h_s, row(ffn_norm[1, 0]), w_ffn_in, w_ffn_out, 1, 0)
    h_p = _ffn(h_p, row(ffn_norm[1, 0]), *wf, TM_P)
    q_p = _normproj(h_p, row(mix_norm[1]), wq, q_gain, TM_W)
    q_s = _normproj(h_s, row(mix_norm[1]), wq, q_gain, TM_W)

    o_p = _attn_prompt(q_p, kv_p, kv_q, sinks_row)
    q4 = q_s.reshape(DEC_BATCH, DEC_SEQ, KV_PAIRS, 2, B_GROUP, B_DH).transpose(0, 2, 1, 4, 3, 5)
    q4 = q4.reshape(DEC_BATCH, KV_PAIRS, SAMPLE_ROWS, LANES)
    seq3 = lambda x: x.astype(F32).reshape(DEC_BATCH, -1, B_KV)
    o4 = _attn_sample(q4, seq3(cache_k_meta), seq3(cache_v_meta),
                      seq3(cache_k_win[:, :SAMPLE_OLD]), seq3(cache_v_win[:, :SAMPLE_OLD]),
                      seq3(k_win_s), seq3(v_win_s), sinks_row)
    o_s = o4.reshape(DEC_BATCH, KV_PAIRS, DEC_SEQ, B_GROUP, 2, B_DH).transpose(0, 2, 1, 4, 3, 5)
    o_s = o_s.reshape(S_ROWS, D_MODEL)

    h_p = _matres(o_p, wb_out, h_p, TM_W)
    h_s = _matres(o_s, wb_out, h_s, TM_W)
    h_s, *wf = _ffn_cast(h_s, row(ffn_norm[1, 1]), w_ffn_in, w_ffn_out, 1, 1)
    h_p = _ffn(h_p, row(ffn_norm[1, 1]), *wf, TM_P)

    kv4 = lambda x: x.reshape(x.shape[:-1] + (B_KV_HEADS, B_DH))
    meta_rows = kv_q[S_ROWS:S_ROWS + N_META]
    kv_p3 = kv_p.reshape(BATCH, SEQ, 2 * B_KV)
    st = lambda x, dt: x[None].astype(dt)
    return (
        h_p.reshape(BATCH, SEQ, D_MODEL),
        h_s.reshape(DEC_BATCH, DEC_SEQ, D_MODEL),
        st(c_p, state_C.dtype), st(n_p[:, :, 0, :], state_n.dtype), st(m_p[:, :, 0, 0], state_m.dtype),
        jnp.broadcast_to(kv4(meta_rows[:, :B_KV])[None], (BATCH, N_META, B_KV_HEADS, B_DH)),
        jnp.broadcast_to(kv4(meta_rows[:, B_KV:])[None], (BATCH, N_META, B_KV_HEADS, B_DH)),
        kv4(kv_p3[:, -WINDOW:, :B_KV]), kv4(kv_p3[:, -WINDOW:, B_KV:]),
        st(c_s, state_C.dtype), st(n_s[:, :, 0, :], state_n.dtype), st(m_s[:, :, 0, 0], state_m.dtype),
        k_win_s, v_win_s,
    )
```

```python
import functools

import jax
import jax.numpy as jnp
from jax import lax
from jax.experimental import pallas as pl
from jax.experimental.pallas import tpu as pltpu

D_MODEL = 2048
BATCH = 8
SEQ = 2048
DEC_BATCH = 128
DEC_SEQ = 4
PAST_LEN = 8192
N_META = 16
A_HEADS = 4
A_DV = D_MODEL // A_HEADS
A_DK = A_DV // 2
A_CHUNK = 128
A_GATE_CAP = 15.0
A_QKVO = 2 * A_HEADS * A_DK + 2 * A_HEADS * A_DV
B_HEADS = 32
B_DH = D_MODEL // B_HEADS
B_KV_HEADS = 4
B_GROUP = B_HEADS // B_KV_HEADS
B_KV = B_KV_HEADS * B_DH
WINDOW = 128
D_FF = ((8 * D_MODEL // 3 + 255) // 256) * 256
EPS = 1e-6

LANES = 128
SUBLANES = 8
VMEM_DEFAULT_MIB = 48
VMEM_FFN_MIB = 60

P_ROWS = BATCH * SEQ
S_ROWS = DEC_BATCH * DEC_SEQ
Q_ROWS = S_ROWS + A_CHUNK
META_BLOCK = S_ROWS // A_CHUNK

F32 = jnp.float32
BF16 = jnp.bfloat16
NEG_INF = float("-inf")


def _params(n_axes, vmem_mib=VMEM_DEFAULT_MIB):
    return pltpu.CompilerParams(dimension_semantics=("arbitrary",) * n_axes,
                                vmem_limit_bytes=vmem_mib * 1024 * 1024)


def _resident(shape):
    return pl.BlockSpec(shape, lambda i: (0, 0), pipeline_mode=pl.Buffered(1))


def _rms(x, g):
    return x * lax.rsqrt(jnp.mean(x * x, axis=-1, keepdims=True) + EPS) * g


def _dot(a, b):
    return jnp.dot(a, b, preferred_element_type=F32)


def _dot_nt(a, b):
    return lax.dot_general(a, b, (((1,), (1,)), ((), ())), preferred_element_type=F32)


def _dot_tn(a, b):
    return lax.dot_general(a, b, (((0,), (0,)), ((), ())), preferred_element_type=F32)


def _log_sigmoid(x):
    return -(jnp.maximum(-x, 0.0) + jnp.log1p(jnp.exp(-jnp.abs(x))))


def _head_norm64(y, gain):
    lo = lax.broadcasted_iota(jnp.int32, (1, LANES), 1) < B_DH
    cols = []
    for c in range(y.shape[1] // LANES):
        x = y[:, c * LANES:(c + 1) * LANES]
        xx = x * x
        s_lo = jnp.sum(jnp.where(lo, xx, 0.0), axis=-1, keepdims=True)
        s_hi = jnp.sum(jnp.where(lo, 0.0, xx), axis=-1, keepdims=True)
        scale = jnp.where(lo, lax.rsqrt(s_lo / B_DH + EPS), lax.rsqrt(s_hi / B_DH + EPS))
        cols.append(x * scale * gain[:, c * LANES:(c + 1) * LANES])
    return jnp.concatenate(cols, axis=1)


FFN_TF = 512
FFN_CAST_TF = 256


def _ffn_kernel(h_ref, g_ref, wg_ref, wu_ref, wo_ref, o_ref, *rest, emit_bf16):
    j = pl.program_id(1)
    xn_ref = rest[-1]

    @pl.when(j == 0)
    def _():
        h = h_ref[...]
        xn_ref[...] = _rms(h, g_ref[...]).astype(BF16)
        o_ref[...] = h

    if emit_bf16:
        wgb_ref, wub_ref, wob_ref = rest[:-1]
        wg = wg_ref[...].astype(BF16)
        wu = wu_ref[...].astype(BF16)
        wo = wo_ref[...].astype(BF16)
        wgb_ref[...] = wg
        wub_ref[...] = wu
        wob_ref[...] = wo
    else:
        wg, wu, wo = wg_ref[...], wu_ref[...], wo_ref[...]

    xn = xn_ref[...]
    g = _dot(xn, wg)
    u = _dot(xn, wu)
    a = (g / (1.0 + jnp.exp(-g))) * (0.5 * u)
    o_ref[...] += _dot(a.astype(BF16), wo)


def _ffn(h, gain, wg, wu, wo, tm):
    rows = h.shape[0]
    tf = FFN_TF
    n_ff = D_FF // tf
    return pl.pallas_call(
        functools.partial(_ffn_kernel, emit_bf16=False),
        grid=(rows // tm, n_ff),
        in_specs=[
            pl.BlockSpec((tm, D_MODEL), lambda i, j: (i, 0)),
            pl.BlockSpec((1, D_MODEL), lambda i, j: (0, 0)),
            pl.BlockSpec((D_MODEL, tf), lambda i, j: (0, j)),
            pl.BlockSpec((D_MODEL, tf), lambda i, j: (0, j)),
            pl.BlockSpec((tf, D_MODEL), lambda i, j: (j, 0)),
        ],
        out_specs=pl.BlockSpec((tm, D_MODEL), lambda i, j: (i, 0)),
        out_shape=jax.ShapeDtypeStruct((rows, D_MODEL), F32),
        scratch_shapes=[pltpu.VMEM((tm, D_MODEL), BF16)],
        compiler_params=_params(2, VMEM_FFN_MIB),
        name="ffn",
    )(h, gain, wg, wu, wo)


def _ffn_cast(h, gain, w_in, w_out, layer, which):
    rows = h.shape[0]
    tf = FFN_CAST_TF
    n_ff = D_FF // tf
    return pl.pallas_call(
        functools.partial(_ffn_kernel, emit_bf16=True),
        grid=(1, n_ff),
        in_specs=[
            pl.BlockSpec((rows, D_MODEL), lambda i, j: (0, 0)),
            pl.BlockSpec((1, D_MODEL), lambda i, j: (0, 0)),
            pl.BlockSpec((None, None, D_MODEL, tf), lambda i, j: (layer, which, 0, j)),
            pl.BlockSpec((None, None, D_MODEL, tf), lambda i, j: (layer, which, 0, j + n_ff)),
            pl.BlockSpec((None, None, tf, D_MODEL), lambda i, j: (layer, which, j, 0)),
        ],
        out_specs=[
            pl.BlockSpec((rows, D_MODEL), lambda i, j: (0, 0)),
            pl.BlockSpec((D_MODEL, tf), lambda i, j: (0, j)),
            pl.BlockSpec((D_MODEL, tf), lambda i, j: (0, j)),
            pl.BlockSpec((tf, D_MODEL), lambda i, j: (j, 0)),
        ],
        out_shape=[
            jax.ShapeDtypeStruct((rows, D_MODEL), F32),
            jax.ShapeDtypeStruct((D_MODEL, D_FF), BF16),
            jax.ShapeDtypeStruct((D_MODEL, D_FF), BF16),
            jax.ShapeDtypeStruct((D_FF, D_MODEL), BF16),
        ],
        scratch_shapes=[pltpu.VMEM((rows, D_MODEL), BF16)],
        compiler_params=_params(2),
        name="ffn_cast",
    )(h, gain, w_in, w_in, w_out)


GATE_ROWS = 2 * A_HEADS


def _inproj_kernel(h_ref, g_ref, w_ref, wgate_ref, bgate_ref, p_ref, gates_ref, *rest, emit_bf16):
    j = pl.program_id(1)
    xn_ref = rest[-1]

    @pl.when(j == 0)
    def _():
        xn = _rms(h_ref[...], g_ref[...]).astype(BF16)
        xn_ref[...] = xn
        wgate = jnp.concatenate([wgate_ref[...], jnp.zeros((LANES - GATE_ROWS, D_MODEL), F32)], axis=0)
        pre = _dot_nt(xn, wgate.astype(BF16)) + bgate_ref[...]
        capped = A_GATE_CAP * jnp.tanh(pre / A_GATE_CAP)
        lane = lax.broadcasted_iota(jnp.int32, (1, LANES), 1)
        gates_ref[...] = jnp.where(lane < A_HEADS, capped, _log_sigmoid(capped))

    if emit_bf16:
        w = w_ref[...].astype(BF16)
        rest[0][...] = w
    else:
        w = w_ref[...]
    p_ref[...] = _dot_nt(xn_ref[...], w)


def _inproj(h, gain, w_t, w_gate_t, bgate, tm, emit_bf16=False, tn=1024):
    rows = h.shape[0]
    assert not emit_bf16 or rows == tm
    out_specs = [
        pl.BlockSpec((tm, tn), lambda i, j: (i, j)),
        pl.BlockSpec((tm, LANES), lambda i, j: (i, 0)),
    ]
    out_shape = [jax.ShapeDtypeStruct((rows, A_QKVO), F32),
                 jax.ShapeDtypeStruct((rows, LANES), F32)]
    if emit_bf16:
        out_specs.append(pl.BlockSpec((tn, D_MODEL), lambda i, j: (j, 0)))
        out_shape.append(jax.ShapeDtypeStruct((A_QKVO, D_MODEL), BF16))
    return pl.pallas_call(
        functools.partial(_inproj_kernel, emit_bf16=emit_bf16),
        grid=(rows // tm, A_QKVO // tn),
        in_specs=[
            pl.BlockSpec((tm, D_MODEL), lambda i, j: (i, 0)),
            pl.BlockSpec((1, D_MODEL), lambda i, j: (0, 0)),
            pl.BlockSpec((tn, D_MODEL), lambda i, j: (j, 0)),
            pl.BlockSpec((GATE_ROWS, D_MODEL), lambda i, j: (A_QKVO // GATE_ROWS, 0)),
            pl.BlockSpec((1, LANES), lambda i, j: (0, 0)),
        ],
        out_specs=out_specs,
        out_shape=out_shape,
        scratch_shapes=[pltpu.VMEM((tm, D_MODEL), BF16)],
        compiler_params=_params(2),
        name="mlstm_inproj",
    )(h, gain, w_t, w_gate_t, bgate)


def _mlstm_chunk_kernel(q_ref, k_ref, v_ref, g_ref, c0_ref, n0_ref, m0_ref,
                        h_ref, c_ref, n_ref, m_ref, qk_ref, qc_ref, sb_ref, *, n_valid):
    L = A_CHUNK
    heads = range(A_HEADS)

    @pl.when(pl.program_id(1) == 0)
    def _():
        c_ref[...] = c0_ref[...]
        n_ref[...] = n0_ref[...]
        m_ref[...] = m0_ref[...]

    for hd in heads:
        qb = q_ref[:, hd * A_DK:(hd + 1) * A_DK].astype(BF16)
        kb = (k_ref[:, hd * A_DK:(hd + 1) * A_DK] * (A_DK ** -0.5)).astype(BF16)
        qk_ref[hd] = _dot_nt(qb, kb)
        qc_ref[hd] = _dot(qb, c_ref[0, hd].astype(BF16))

    gates = g_ref[...]
    row = lax.broadcasted_iota(jnp.int32, (L, L), 0)
    col = lax.broadcasted_iota(jnp.int32, (L, L), 1)
    causal = col <= row
    eye = col == row
    masked = n_valid < L
    if masked:
        row_ok = lax.broadcasted_iota(jnp.int32, (L, 1), 0) < n_valid
        gates_lf = jnp.where(row_ok, gates, 0.0)
    else:
        gates_lf = gates
    csum = jnp.dot(causal.astype(F32), gates_lf, precision=lax.Precision.HIGHEST,
                   preferred_element_type=F32)

    def to_row(x_col):
        return jnp.sum(jnp.where(eye, x_col, 0.0), axis=0, keepdims=True)

    w_intra, w_inter, floor, w_state, decay, m_new = [], [], [], [], [], []
    for hd in heads:
        b_col = csum[:, A_HEADS + hd:A_HEADS + hd + 1]
        ig_col = gates[:, hd:hd + 1]
        if masked:
            ig_col = jnp.where(row_ok, ig_col, NEG_INF)
        b_row = to_row(b_col)
        ig_row = to_row(ig_col)
        m_prev = m_ref[0, hd][:, 0:1]
        d_log = jnp.where(causal, b_col - b_row + ig_row, NEG_INF)
        inter_log = b_col + m_prev
        m_t = jnp.maximum(inter_log, jnp.max(d_log, axis=-1, keepdims=True))
        w_intra.append(jnp.exp(d_log - m_t))
        w_inter.append(jnp.exp(inter_log - m_t))
        floor.append(jnp.exp(-m_t))
        b_last = b_col[L - 1:L, :]
        w_log = b_last - b_col + ig_col
        m_new.append(jnp.maximum(b_last + m_prev, jnp.max(w_log, axis=0, keepdims=True)))
        w_state.append(jnp.exp(w_log - m_new[hd]))
        decay.append(jnp.exp(b_last + m_prev - m_new[hd]))

    den = []
    for hd in heads:
        s = qk_ref[hd] * w_intra[hd]
        sb_ref[hd] = s.astype(BF16)
        q = q_ref[:, hd * A_DK:(hd + 1) * A_DK]
        d = jnp.sum(s, axis=-1, keepdims=True) + w_inter[hd] * jnp.sum(q * n_ref[0, hd], axis=-1, keepdims=True)
        den.append(jnp.maximum(jnp.abs(d), floor[hd]))

    for hd in heads:
        k = k_ref[:, hd * A_DK:(hd + 1) * A_DK] * (A_DK ** -0.5)
        vb = v_ref[:, hd * A_DV:(hd + 1) * A_DV].astype(BF16)
        num = _dot(sb_ref[hd], vb) + w_inter[hd] * qc_ref[hd]
        h_ref[:, hd * A_DV:(hd + 1) * A_DV] = num / den[hd]
        kw = k * w_state[hd]
        c_ref[0, hd] = decay[hd] * c_ref[0, hd] + _dot_tn(kw.astype(BF16), vb)
        n_ref[0, hd] = decay[hd] * n_ref[0, hd] + jnp.sum(kw, axis=0, keepdims=True)
        m_ref[0, hd] = jnp.broadcast_to(m_new[hd], (1, LANES))


def _mlstm_chunks(p, gates, c0, n0, m0, n_seq, n_chunks, row_block0, n_valid, shared_state):
    L = A_CHUNK
    rb = lambda b, c: row_block0 + b * n_chunks + c
    st = (lambda b, c: (0, 0, 0, 0)) if shared_state else (lambda b, c: (b, 0, 0, 0))
    return pl.pallas_call(
        functools.partial(_mlstm_chunk_kernel, n_valid=n_valid),
        grid=(n_seq, n_chunks),
        in_specs=[
            pl.BlockSpec((L, A_HEADS * A_DK), lambda b, c: (rb(b, c), 0)),
            pl.BlockSpec((L, A_HEADS * A_DK), lambda b, c: (rb(b, c), 1)),
            pl.BlockSpec((L, A_HEADS * A_DV), lambda b, c: (rb(b, c), 1)),
            pl.BlockSpec((L, LANES), lambda b, c: (rb(b, c), 0)),
            pl.BlockSpec((1, A_HEADS, A_DK, A_DV), st),
            pl.BlockSpec((1, A_HEADS, 1, A_DK), st),
            pl.BlockSpec((1, A_HEADS, 1, LANES), st),
        ],
        out_specs=[
            pl.BlockSpec((L, A_HEADS * A_DV), lambda b, c: (b * n_chunks + c, 0)),
            pl.BlockSpec((1, A_HEADS, A_DK, A_DV), lambda b, c: (b, 0, 0, 0)),
            pl.BlockSpec((1, A_HEADS, 1, A_DK), lambda b, c: (b, 0, 0, 0)),
            pl.BlockSpec((1, A_HEADS, 1, LANES), lambda b, c: (b, 0, 0, 0)),
        ],
        out_shape=[
            jax.ShapeDtypeStruct((n_seq * n_chunks * L, D_MODEL), F32),
            jax.ShapeDtypeStruct((n_seq, A_HEADS, A_DK, A_DV), F32),
            jax.ShapeDtypeStruct((n_seq, A_HEADS, 1, A_DK), F32),
            jax.ShapeDtypeStruct((n_seq, A_HEADS, 1, LANES), F32),
        ],
        scratch_shapes=[pltpu.VMEM((A_HEADS, L, L), F32),
                        pltpu.VMEM((A_HEADS, L, A_DV), F32),
                        pltpu.VMEM((A_HEADS, L, L), BF16)],
        compiler_params=_params(2),
        name="mlstm_chunks",
    )(p, p, p, gates, c0, n0, m0)


def _mlstm_sample_kernel(q_ref, k_ref, v_ref, g_ref, c0_ref, n0_ref, m0_ref,
                         h_ref, c_ref, n_ref, m_ref):
    R = 2 * DEC_SEQ
    PAD = A_CHUNK - R
    gates = g_ref[...]
    r_col = lax.broadcasted_iota(jnp.int32, (R, 1), 0)
    is_a = r_col < DEC_SEQ
    row = lax.broadcasted_iota(jnp.int32, (R, LANES), 0)
    lane = lax.broadcasted_iota(jnp.int32, (R, LANES), 1)
    same = ((lane < DEC_SEQ) & (row < DEC_SEQ)) | ((lane >= DEC_SEQ) & (lane < R) & (row >= DEC_SEQ))
    causal = same & (lane <= row)
    eye = lane == row

    def to_row(x_col):
        return jnp.sum(jnp.where(eye, x_col, 0.0), axis=0, keepdims=True)

    for hd in range(A_HEADS):
        lf_col = gates[:, A_HEADS + hd:A_HEADS + hd + 1]
        ig_col = gates[:, hd:hd + 1]
        lf_row = to_row(lf_col)
        ig_row = to_row(ig_col)
        b_col = jnp.sum(jnp.where(causal, lf_row, 0.0), axis=1, keepdims=True)
        b_row = to_row(b_col)
        m_a = m0_ref[0, hd][:, 0:1]
        m_b = m0_ref[1, hd][:, 0:1]
        m_prev = jnp.where(is_a, m_a, m_b)
        c_a = c0_ref[0, hd]
        c_b = c0_ref[1, hd]
        n_a = n0_ref[0, hd]
        n_b = n0_ref[1, hd]

        q = q_ref[:, hd * A_DK:(hd + 1) * A_DK]
        k = k_ref[:, hd * A_DK:(hd + 1) * A_DK] * (A_DK ** -0.5)
        v = v_ref[:, hd * A_DV:(hd + 1) * A_DV]
        qb = q.astype(BF16)
        k_pad = jnp.concatenate([k, jnp.zeros((PAD, A_DK), F32)], axis=0).astype(BF16)
        v_pad = jnp.concatenate([v, jnp.zeros((PAD, A_DV), F32)], axis=0).astype(BF16)

        d_log = jnp.where(causal, b_col - b_row + ig_row, NEG_INF)
        inter_log = b_col + m_prev
        m_t = jnp.maximum(inter_log, jnp.max(d_log, axis=-1, keepdims=True))
        w_intra = jnp.exp(d_log - m_t)
        w_inter = jnp.exp(inter_log - m_t)
        s = _dot_nt(qb, k_pad) * w_intra
        q_c = jnp.where(is_a, _dot(qb, c_a.astype(BF16)), _dot(qb, c_b.astype(BF16)))
        num = _dot(s.astype(BF16), v_pad) + w_inter * q_c
        q_n = jnp.sum(q * jnp.where(is_a, n_a, n_b), axis=-1, keepdims=True)
        den = jnp.sum(s, axis=-1, keepdims=True) + w_inter * q_n
        den = jnp.maximum(jnp.abs(den), jnp.exp(-m_t))
        h_ref[:, hd * A_DV:(hd + 1) * A_DV] = num / den

        for idx, sel, m_x, c_x, n_x in ((0, is_a, m_a, c_a, n_a),
                                        (1, jnp.logical_not(is_a), m_b, c_b, n_b)):
            last = (idx + 1) * DEC_SEQ - 1
            b_last = b_col[last:last + 1, :]
            w_log = jnp.where(sel, b_last - b_col + ig_col, NEG_INF)
            m_new = jnp.maximum(b_last + m_x, jnp.max(w_log, axis=0, keepdims=True))
            w_state = jnp.exp(w_log - m_new)
            decay = jnp.exp(b_last + m_x - m_new)
            kw = k * w_state
            kw_pad = jnp.concatenate([kw, jnp.zeros((PAD, A_DK), F32)], axis=0).astype(BF16)
            c_ref[idx, hd] = decay * c_x + _dot_tn(kw_pad, v_pad)
            n_ref[idx, hd] = decay * n_x + jnp.sum(kw, axis=0, keepdims=True)
            m_ref[idx, hd] = jnp.broadcast_to(m_new, (1, LANES))


def _mlstm_sample(p, gates, c0, n0, m0):
    R = 2 * DEC_SEQ
    n_pairs = DEC_BATCH // 2
    st = lambda i: (i, 0, 0, 0)
    return pl.pallas_call(
        _mlstm_sample_kernel,
        grid=(n_pairs,),
        in_specs=[
            pl.BlockSpec((R, A_HEADS * A_DK), lambda i: (i, 0)),
            pl.BlockSpec((R, A_HEADS * A_DK), lambda i: (i, 1)),
            pl.BlockSpec((R, A_HEADS * A_DV), lambda i: (i, 1)),
            pl.BlockSpec((R, LANES), lambda i: (i, 0)),
            pl.BlockSpec((2, A_HEADS, A_DK, A_DV), st),
            pl.BlockSpec((2, A_HEADS, 1, A_DK), st),
            pl.BlockSpec((2, A_HEADS, 1, LANES), st),
        ],
        out_specs=[
            pl.BlockSpec((R, A_HEADS * A_DV), lambda i: (i, 0)),
            pl.BlockSpec((2, A_HEADS, A_DK, A_DV), st),
            pl.BlockSpec((2, A_HEADS, 1, A_DK), st),
            pl.BlockSpec((2, A_HEADS, 1, LANES), st),
        ],
        out_shape=[
            jax.ShapeDtypeStruct((S_ROWS, D_MODEL), F32),
            jax.ShapeDtypeStruct((DEC_BATCH, A_HEADS, A_DK, A_DV), F32),
            jax.ShapeDtypeStruct((DEC_BATCH, A_HEADS, 1, A_DK), F32),
            jax.ShapeDtypeStruct((DEC_BATCH, A_HEADS, 1, LANES), F32),
        ],
        compiler_params=_params(1),
        name="mlstm_sample",
    )(p, p, p, gates, c0, n0, m0)


def _mlstm_out_kernel(hm_ref, o_ref, hg_ref, w_ref, res_ref, out_ref):
    cols = []
    for hd in range(A_HEADS):
        x = hm_ref[:, hd * A_DV:(hd + 1) * A_DV]
        cols.append(x * lax.rsqrt(jnp.mean(x * x, axis=-1, keepdims=True) + EPS))
    hn = jnp.concatenate(cols, axis=1) * hg_ref[...]
    o = o_ref[...]
    pre = (hn * (1.0 / (1.0 + jnp.exp(-o)))).astype(BF16)
    out_ref[...] = res_ref[...] + _dot(pre, w_ref[...])


def _mlstm_out(hm, p, head_gain, w, res, tm):
    rows = hm.shape[0]
    return pl.pallas_call(
        _mlstm_out_kernel,
        grid=(rows // tm,),
        in_specs=[
            pl.BlockSpec((tm, D_MODEL), lambda i: (i, 0)),
            pl.BlockSpec((tm, D_MODEL), lambda i: (i, 2)),
            pl.BlockSpec((1, D_MODEL), lambda i: (0, 0)),
            _resident((D_MODEL, D_MODEL)),
            pl.BlockSpec((tm, D_MODEL), lambda i: (i, 0)),
        ],
        out_specs=pl.BlockSpec((tm, D_MODEL), lambda i: (i, 0)),
        out_shape=jax.ShapeDtypeStruct((rows, D_MODEL), F32),
        compiler_params=_params(1),
        name="mlstm_out",
    )(hm, p, head_gain, w, res)


def _normproj_kernel(h_ref, g_ref, w_ref, hg_ref, o_ref, *, n_norm):
    xn = _rms(h_ref[...], g_ref[...]).astype(BF16)
    y = _dot(xn, w_ref[...])
    if n_norm == y.shape[1]:
        o_ref[...] = _head_norm64(y, hg_ref[...])
    else:
        o_ref[:, :n_norm] = _head_norm64(y[:, :n_norm], hg_ref[...])
        o_ref[:, n_norm:] = y[:, n_norm:]


def _normproj(h, gain, w, head_gain, tm):
    rows = h.shape[0]
    n = w.shape[1]
    n_norm = head_gain.shape[1]
    return pl.pallas_call(
        functools.partial(_normproj_kernel, n_norm=n_norm),
        grid=(rows // tm,),
        in_specs=[
            pl.BlockSpec((tm, D_MODEL), lambda i: (i, 0)),
            pl.BlockSpec((1, D_MODEL), lambda i: (0, 0)),
            _resident((D_MODEL, n)),
            pl.BlockSpec((1, n_norm), lambda i: (0, 0)),
        ],
        out_specs=pl.BlockSpec((tm, n), lambda i: (i, 0)),
        out_shape=jax.ShapeDtypeStruct((rows, n), F32),
        compiler_params=_params(1),
        name="normproj",
    )(h, gain, w, head_gain)


def _matres_kernel(x_ref, w_ref, res_ref, o_ref):
    o_ref[...] = res_ref[...] + _dot(x_ref[...].astype(BF16), w_ref[...])


def _matres(x, w, res, tm):
    rows = x.shape[0]
    return pl.pallas_call(
        _matres_kernel,
        grid=(rows // tm,),
        in_specs=[
            pl.BlockSpec((tm, D_MODEL), lambda i: (i, 0)),
            _resident((D_MODEL, D_MODEL)),
            pl.BlockSpec((tm, D_MODEL), lambda i: (i, 0)),
        ],
        out_specs=pl.BlockSpec((tm, D_MODEL), lambda i: (i, 0)),
        out_shape=jax.ShapeDtypeStruct((rows, D_MODEL), F32),
        compiler_params=_params(1),
        name="matres",
    )(x, w, res)


def _alibi_slope(head):
    return 2.0 ** (-8.0 * (head + 1) / B_HEADS)


PROMPT_SUB = WINDOW // 2
PROMPT_BAND = WINDOW + PROMPT_SUB
PROMPT_KEYS = 2 * LANES
HEAD_PAIRS = B_HEADS // 2
PAIRS_PER_GROUP = B_GROUP // 2


def _pair_rhs(x, c):
    col = x[:, c * LANES:(c + 1) * LANES]
    rol = pltpu.roll(col, B_DH, axis=1)
    lo = lax.broadcasted_iota(jnp.int32, (1, LANES), 1) < B_DH
    even = jnp.concatenate([jnp.where(lo, col, 0.0), jnp.where(lo, 0.0, rol)], axis=0)
    odd = jnp.concatenate([jnp.where(lo, rol, 0.0), jnp.where(lo, 0.0, col)], axis=0)
    return even.astype(BF16), odd.astype(BF16)


def _attn_prompt_kernel(q_ref, kvo_ref, kvp_ref, kvm_ref, sink_ref, o_ref, bias_ref, s_ref, p_ref):
    W = WINDOW
    SB = PROMPT_SUB
    NK = PROMPT_KEYS
    first = (pl.program_id(0) == 0) & (pl.program_id(1) == 0)
    j = pl.program_id(1)
    rows = PAIRS_PER_GROUP * SB
    lo = lax.broadcasted_iota(jnp.int32, (1, LANES), 1) < B_DH

    @pl.when(first)
    def _():
        c = lax.broadcasted_iota(jnp.int32, (SB, NK), 1)
        is_meta = c < N_META
        is_sink = c == NK - 1
        for sub in range(2):
            i = lax.broadcasted_iota(jnp.int32, (SB, NK), 0) + sub * SB
            pos = c - N_META + sub * SB
            rel = W + i - pos
            in_band = (c >= N_META) & (c < N_META + PROMPT_BAND) & (rel >= 0) & (rel < W)
            for variant in range(2):
                if variant == 0:
                    dist = jnp.where(is_meta, jnp.minimum(i + N_META - c, W), rel).astype(F32)
                    valid = is_meta | (in_band & (pos >= W))
                else:
                    dist = jnp.where(is_meta, W, rel).astype(F32)
                    valid = is_meta | in_band
                for hd in range(B_HEADS):
                    r0 = (hd // 2) * SB
                    c0 = (hd % 2) * NK
                    table = jnp.where(valid, -_alibi_slope(hd) * dist, NEG_INF)
                    bias_ref[variant, sub, r0:r0 + SB, c0:c0 + NK] = jnp.where(
                        is_sink, sink_ref[0:1, hd:hd + 1], table)

    variant = jnp.minimum(j, 1)
    kvm = kvm_ref[...]
    band = jnp.concatenate([kvp_ref[...], kvo_ref[...]], axis=0)
    pad = jnp.zeros((NK - N_META - PROMPT_BAND, 2 * B_KV), F32)
    ones_rhs = jnp.concatenate([jnp.broadcast_to(jnp.where(lo, 1.0, 0.0), (NK, LANES)),
                                jnp.broadcast_to(jnp.where(lo, 0.0, 1.0), (NK, LANES))], axis=0).astype(BF16)

    sub_rows = B_KV_HEADS * rows
    for sub in range(2):
        keys = jnp.concatenate([kvm, band[sub * SB:sub * SB + PROMPT_BAND], pad], axis=0)
        k_rhs = _pair_rhs(keys, 0) + _pair_rhs(keys, 1)
        v_rhs = _pair_rhs(keys, 2) + _pair_rhs(keys, 3)
        for g in range(B_KV_HEADS):
            p0 = g * PAIRS_PER_GROUP
            r0 = (sub * B_KV_HEADS + g) * rows
            qg = jnp.concatenate([q_ref[sub * SB:(sub + 1) * SB, (p0 + m) * LANES:(p0 + m + 1) * LANES]
                                  for m in range(PAIRS_PER_GROUP)], axis=0) * (B_DH ** -0.5)
            s_ref[r0:r0 + rows, :] = _dot_nt(qg.astype(BF16), k_rhs[g])

        s = s_ref[sub * sub_rows:(sub + 1) * sub_rows, :] + bias_ref[variant, sub]
        halves = []
        for half in range(2):
            sh = s[:, half * NK:(half + 1) * NK]
            halves.append(jnp.exp(sh - jnp.max(sh, axis=-1, keepdims=True)).astype(BF16))
        p_ref[sub * sub_rows:(sub + 1) * sub_rows, :] = jnp.concatenate(halves, axis=1)

        for g in range(B_KV_HEADS):
            p0 = g * PAIRS_PER_GROUP
            r0 = (sub * B_KV_HEADS + g) * rows
            od = _dot(p_ref[r0:r0 + rows, :], jnp.concatenate([v_rhs[g], ones_rhs], axis=1))
            o = od[:, :LANES] / od[:, LANES:]
            for m in range(PAIRS_PER_GROUP):
                o_ref[sub * SB:(sub + 1) * SB, (p0 + m) * LANES:(p0 + m + 1) * LANES] = o[m * SB:(m + 1) * SB, :]


def _attn_prompt(q, kv_p, kv_q, sinks_row):
    nb = SEQ // WINDOW
    blk = lambda b, j: b * nb + j
    return pl.pallas_call(
        _attn_prompt_kernel,
        grid=(BATCH, nb),
        in_specs=[
            pl.BlockSpec((WINDOW, D_MODEL), lambda b, j: (blk(b, j), 0)),
            pl.BlockSpec((WINDOW, 2 * B_KV), lambda b, j: (blk(b, j), 0)),
            pl.BlockSpec((WINDOW, 2 * B_KV), lambda b, j: (blk(b, jnp.maximum(j - 1, 0)), 0)),
            pl.BlockSpec((N_META, 2 * B_KV), lambda b, j: (S_ROWS // N_META, 0)),
            pl.BlockSpec((1, LANES), lambda b, j: (0, 0)),
        ],
        out_specs=pl.BlockSpec((WINDOW, D_MODEL), lambda b, j: (blk(b, j), 0)),
        out_shape=jax.ShapeDtypeStruct((P_ROWS, D_MODEL), F32),
        scratch_shapes=[pltpu.VMEM((2, 2, HEAD_PAIRS * PROMPT_SUB, 2 * PROMPT_KEYS), F32),
                        pltpu.VMEM((2 * HEAD_PAIRS * PROMPT_SUB, 2 * PROMPT_KEYS), F32),
                        pltpu.VMEM((2 * HEAD_PAIRS * PROMPT_SUB, 2 * PROMPT_KEYS), BF16)],
        compiler_params=_params(2),
        name="attn_prompt",
    )(q, kv_p, kv_p, kv_q, sinks_row)


GROUP_SHIFT = B_GROUP.bit_length() - 1
assert 1 << GROUP_SHIFT == B_GROUP
SAMPLE_NEW = SUBLANES
SAMPLE_SMALL = LANES
SAMPLE_KEYS = WINDOW + SAMPLE_SMALL
SAMPLE_SEQ_PER_STEP = 8
SAMPLE_ROWS = DEC_SEQ * B_GROUP
KV_PAIRS = B_KV_HEADS // 2


def _split_rhs(col):
    lo = lax.broadcasted_iota(jnp.int32, (1, LANES), 1) < B_DH
    return jnp.concatenate([jnp.where(lo, col, 0.0), jnp.where(lo, 0.0, col)], axis=0)


def _split_cols(mat_t):
    top = lax.broadcasted_iota(jnp.int32, (LANES, 1), 0) < B_DH
    return jnp.concatenate([jnp.where(top, mat_t, 0.0), jnp.where(top, 0.0, mat_t)], axis=1)


def _attn_sample_kernel(q_ref, km_ref, vm_ref, kvn_ref, kt_ref, vt_ref, sink_ref,
                        o_ref, kt_out_ref, vt_out_ref, bias_ref, s_ref, p_ref):
    W = WINDOW
    NK = SAMPLE_KEYS
    R = SAMPLE_ROWS
    lo = lax.broadcasted_iota(jnp.int32, (1, LANES), 1) < B_DH
    lane = lax.broadcasted_iota(jnp.int32, (1, LANES), 1)

    @pl.when(pl.program_id(0) == 0)
    def _():
        row = lax.broadcasted_iota(jnp.int32, (R, 1), 0)
        r_in_group = jnp.bitwise_and(row, B_GROUP - 1)
        t = jnp.right_shift(lax.broadcasted_iota(jnp.int32, (R, NK), 0), GROUP_SHIFT)
        c = lax.broadcasted_iota(jnp.int32, (R, NK), 1)
        is_win = c < W
        is_meta = (c >= W) & (c < W + N_META)
        is_new = (c >= W + N_META) & (c < W + N_META + DEC_SEQ)
        t_new = c - (W + N_META)
        dist = jnp.where(is_win, W + t - c, jnp.where(is_meta, W, t - t_new)).astype(F32)
        valid = (is_win & (c > t)) | is_meta | (is_new & (t_new <= t))
        for pair in range(KV_PAIRS):
            for e in range(2):
                slope = jnp.zeros((R, 1), F32)
                sink = jnp.zeros((R, 1), F32)
                for r in range(B_GROUP):
                    hd = (2 * pair + e) * B_GROUP + r
                    slope = jnp.where(r_in_group == r, _alibi_slope(hd), slope)
                    sink = jnp.where(r_in_group == r, sink_ref[0:1, hd:hd + 1], sink)
                table = jnp.where(c == NK - 1, sink, jnp.where(valid, -slope * dist, NEG_INF))
                for sq in range(SAMPLE_SEQ_PER_STEP):
                    r0 = (sq * KV_PAIRS + pair) * R
                    bias_ref[r0:r0 + R, e * NK:(e + 1) * NK] = table

    sel_r = lax.broadcasted_iota(jnp.int32, (SAMPLE_SMALL, W), 0)
    sel_c = lax.broadcasted_iota(jnp.int32, (SAMPLE_SMALL, W), 1)
    place_new = ((sel_r >= N_META) & (sel_r < N_META + DEC_SEQ)
                 & (sel_c == sel_r - N_META + W - DEC_SEQ)).astype(F32)
    pad = jnp.zeros((SAMPLE_SMALL - N_META - SAMPLE_NEW, B_KV), F32)
    top = lax.broadcasted_iota(jnp.int32, (LANES, 1), 0) < B_DH
    ones_rhs = jnp.concatenate([jnp.broadcast_to(jnp.where(lo, 1.0, 0.0), (SAMPLE_SMALL, LANES)),
                                jnp.broadcast_to(jnp.where(lo, 0.0, 1.0), (SAMPLE_SMALL, LANES))], axis=0)
    ones_t = jnp.concatenate([jnp.broadcast_to(jnp.where(top, 1.0, 0.0), (LANES, W)),
                              jnp.broadcast_to(jnp.where(top, 0.0, 1.0), (LANES, W))], axis=1)

    v_small, v_win = [], []
    for sq in range(SAMPLE_SEQ_PER_STEP):
        kvn = kvn_ref[sq]
        k_small = jnp.concatenate([km_ref[sq], kvn[:, :B_KV], pad], axis=0)
        v_small_sq = jnp.concatenate([vm_ref[sq], kvn[:, B_KV:], pad], axis=0)
        kt = kt_ref[sq]
        vt = vt_ref[sq]
        for src, small, dst in ((kt, k_small, kt_out_ref), (vt, v_small_sq, vt_out_ref)):
            new_cols = lax.dot_general(small, place_new, (((0,), (0,)), ((), ())),
                                       precision=lax.Precision.HIGHEST, preferred_element_type=F32)
            dst[sq] = jnp.where(lane < W - DEC_SEQ, pltpu.roll(src, W - DEC_SEQ, axis=1), new_cols)
        for pair in range(KV_PAIRS):
            r0 = (sq * KV_PAIRS + pair) * R
            q = (q_ref[sq, pair] * (B_DH ** -0.5)).astype(BF16)
            s_win = _dot(q, _split_cols(kt[pair * LANES:(pair + 1) * LANES, :]).astype(BF16))
            s_small = _dot_nt(q, _split_rhs(k_small[:, pair * LANES:(pair + 1) * LANES]).astype(BF16))
            for e in range(2):
                s_ref[r0:r0 + R, e * NK:e * NK + W] = s_win[:, e * W:(e + 1) * W]
                s_ref[r0:r0 + R, e * NK + W:(e + 1) * NK] = s_small[:, e * SAMPLE_SMALL:(e + 1) * SAMPLE_SMALL]
            v_win.append(jnp.concatenate([_split_cols(vt[pair * LANES:(pair + 1) * LANES, :]), ones_t],
                                         axis=0).astype(BF16))
            v_small.append(jnp.concatenate([_split_rhs(v_small_sq[:, pair * LANES:(pair + 1) * LANES]),
                                            ones_rhs], axis=1).astype(BF16))

    s = s_ref[...] + bias_ref[...]
    halves = []
    for e in range(2):
        sh = s[:, e * NK:(e + 1) * NK]
        halves.append(jnp.exp(sh - jnp.max(sh, axis=-1, keepdims=True)).astype(BF16))
    p_ref[...] = jnp.concatenate(halves, axis=1)

    for sq in range(SAMPLE_SEQ_PER_STEP):
        for pair in range(KV_PAIRS):
            b = sq * KV_PAIRS + pair
            p = p_ref[b * R:(b + 1) * R, :]
            p_win = jnp.concatenate([p[:, 0:W], p[:, NK:NK + W]], axis=1)
            p_small = jnp.concatenate([p[:, W:NK], p[:, NK + W:2 * NK]], axis=1)
            od = _dot_nt(p_win, v_win[b]) + _dot(p_small, v_small[b])
            o_ref[sq, pair] = od[:, :LANES] / od[:, LANES:]


def _attn_sample(q4, k_meta, v_meta, kv_new, k_win_t, v_win_t, sinks_row):
    nb = SAMPLE_SEQ_PER_STEP
    R = SAMPLE_ROWS
    n_rows = nb * KV_PAIRS * R
    seq3 = lambda rows, cols: pl.BlockSpec((nb, rows, cols), lambda i: (i, 0, 0))
    qspec = pl.BlockSpec((nb, KV_PAIRS, R, LANES), lambda i: (i, 0, 0, 0))
    win_shape = jax.ShapeDtypeStruct((DEC_BATCH, B_KV, WINDOW), F32)
    return pl.pallas_call(
        _attn_sample_kernel,
        grid=(DEC_BATCH // nb,),
        in_specs=[qspec, seq3(N_META, B_KV), seq3(N_META, B_KV), seq3(SAMPLE_NEW, 2 * B_KV),
                  seq3(B_KV, WINDOW), seq3(B_KV, WINDOW), pl.BlockSpec((1, LANES), lambda i: (0, 0))],
        out_specs=[qspec, seq3(B_KV, WINDOW), seq3(B_KV, WINDOW)],
        out_shape=[jax.ShapeDtypeStruct((DEC_BATCH, KV_PAIRS, R, LANES), F32), win_shape, win_shape],
        scratch_shapes=[pltpu.VMEM((n_rows, 2 * SAMPLE_KEYS), F32),
                        pltpu.VMEM((n_rows, 2 * SAMPLE_KEYS), F32),
                        pltpu.VMEM((n_rows, 2 * SAMPLE_KEYS), BF16)],
        compiler_params=_params(1),
        name="attn_sample",
    )(q4, k_meta, v_meta, kv_new, k_win_t, v_win_t, sinks_row)


def kernel(x_prompt, x_sample, state_C, state_n, state_m, cache_k_meta, cache_v_meta, cache_k_win, cache_v_win, meta_tokens, ffn_norm, w_ffn_in, w_ffn_out, mix_norm, w_a_in, b_a_gate, a_head_norm, w_a_out, kv_norm, w_kv, k_norm, w_q, q_norm, sinks, w_b_out):
    assert x_prompt.shape == (BATCH, SEQ, D_MODEL) and x_sample.shape == (DEC_BATCH, DEC_SEQ, D_MODEL)
    assert w_a_in.shape[0] == 1 and w_q.shape[0] == 1 and ffn_norm.shape[0] == 2

    wa_in_t = jnp.swapaxes(w_a_in[0], 0, 1)
    ba_gate = jnp.pad(b_a_gate[0].astype(F32), (0, LANES - 2 * A_HEADS)).reshape(1, LANES)
    wa_out = w_a_out[0].astype(BF16)
    wkv = w_kv.astype(BF16)
    wq = w_q[0].astype(BF16)
    wb_out = w_b_out[0].astype(BF16)
    row = lambda x: x.astype(F32).reshape(1, -1)
    k_gain = jnp.tile(row(k_norm), (1, B_KV_HEADS))
    q_gain = jnp.tile(row(q_norm[0]), (1, B_HEADS))
    sinks_row = jnp.pad(sinks[0].astype(F32), (0, LANES - B_HEADS)).reshape(1, LANES)

    h_p = x_prompt.reshape(P_ROWS, D_MODEL)
    h_q = jnp.concatenate([x_sample.reshape(S_ROWS, D_MODEL), meta_tokens.astype(F32),
                           jnp.zeros((A_CHUNK - N_META, D_MODEL), F32)], axis=0)
    TM_P, TM_W = 1024, 512

    h_q, *wf = _ffn_cast(h_q, row(ffn_norm[0, 0]), w_ffn_in, w_ffn_out, 0, 0)
    h_p = _ffn(h_p, row(ffn_norm[0, 0]), *wf, TM_P)
    p_q, g_q, wa_in = _inproj(h_q, row(mix_norm[0]), wa_in_t, wa_in_t, ba_gate, Q_ROWS, emit_bf16=True)
    p_p, g_p = _inproj(h_p, row(mix_norm[0]), wa_in, wa_in_t, ba_gate, TM_P)

    zc = jnp.zeros((1, A_HEADS, A_DK, A_DV), F32)
    zn = jnp.zeros((1, A_HEADS, 1, A_DK), F32)
    zm = jnp.zeros((1, A_HEADS, 1, LANES), F32)
    hm_m, c_m, n_m, m_m = _mlstm_chunks(p_q, g_q, zc, zn, zm, 1, 1, META_BLOCK, N_META, True)
    hm_p, c_p, n_p, m_p = _mlstm_chunks(p_p, g_p, c_m, n_m, m_m, BATCH, SEQ // A_CHUNK, 0, A_CHUNK, True)
    m0_s = jnp.broadcast_to(state_m[0].astype(F32)[:, :, None, None], (DEC_BATCH, A_HEADS, 1, LANES))
    hm_s, c_s, n_s, m_s = _mlstm_sample(p_q, g_q, state_C[0].astype(F32),
                                        state_n[0].astype(F32)[:, :, None, :], m0_s)
    hm_q = jnp.concatenate([hm_s, hm_m], axis=0)

    h_p = _mlstm_out(hm_p, p_p, row(a_head_norm[0]), wa_out, h_p, TM_W)
    h_q = _mlstm_out(hm_q, p_q, row(a_head_norm[0]), wa_out, h_q, A_CHUNK)
    h_q, *wf = _ffn_cast(h_q, row(ffn_norm[0, 1]), w_ffn_in, w_ffn_out, 0, 1)
    h_p = _ffn(h_p, row(ffn_norm[0, 1]), *wf, TM_P)

    kv_p = _normproj(h_p, row(kv_norm), wkv, k_gain, TM_P)
    kv_q = _normproj(h_q, row(kv_norm), wkv, k_gain, Q_ROWS)
    h_s = h_q[:S_ROWS]
    h_s, *wf = _ffn_cast(h_s, row(ffn_norm[1, 0]), w_ffn_in, w_ffn_out, 1, 0)
    h_p = _ffn(h_p, row(ffn_norm[1, 0]), *wf, TM_P)
    q_p = _normproj(h_p, row(mix_norm[1]), wq, q_gain, TM_W)
    q_s = _normproj(h_s, row(mix_norm[1]), wq, q_gain, TM_W)

    o_p = _attn_prompt(q_p, kv_p, kv_q, sinks_row)
    q4 = q_s.reshape(DEC_BATCH, DEC_SEQ, KV_PAIRS, 2, B_GROUP, B_DH).transpose(0, 2, 1, 4, 3, 5)
    q4 = q4.reshape(DEC_BATCH, KV_PAIRS, SAMPLE_ROWS, LANES)
    seq3 = lambda x: x.astype(F32).reshape(DEC_BATCH, -1, B_KV)
    win_t = lambda x: x.astype(F32).transpose(0, 2, 3, 1).reshape(DEC_BATCH, B_KV, WINDOW)
    kv_new = jnp.pad(kv_q[:S_ROWS].reshape(DEC_BATCH, DEC_SEQ, 2 * B_KV), ((0, 0), (0, SAMPLE_NEW - DEC_SEQ), (0, 0)))
    o4, k_win_t, v_win_t = _attn_sample(q4, seq3(cache_k_meta), seq3(cache_v_meta), kv_new,
                                        win_t(cache_k_win), win_t(cache_v_win), sinks_row)
    from_t = lambda x, like: x.reshape(DEC_BATCH, B_KV_HEADS, B_DH, WINDOW).transpose(0, 3, 1, 2).astype(like.dtype)
    k_win_s = from_t(k_win_t, cache_k_win)
    v_win_s = from_t(v_win_t, cache_v_win)
    o_s = o4.reshape(DEC_BATCH, KV_PAIRS, DEC_SEQ, B_GROUP, 2, B_DH).transpose(0, 2, 1, 4, 3, 5)
    o_s = o_s.reshape(S_ROWS, D_MODEL)

    h_p = _matres(o_p, wb_out, h_p, TM_W)
    h_s = _matres(o_s, wb_out, h_s, TM_W)
    h_s, *wf = _ffn_cast(h_s, row(ffn_norm[1, 1]), w_ffn_in, w_ffn_out, 1, 1)
    h_p = _ffn(h_p, row(ffn_norm[1, 1]), *wf, TM_P)

    kv4 = lambda x: x.reshape(x.shape[:-1] + (B_KV_HEADS, B_DH))
    meta_rows = kv_q[S_ROWS:S_ROWS + N_META]
    kv_p3 = kv_p.reshape(BATCH, SEQ, 2 * B_KV)
    st = lambda x, dt: x[None].astype(dt)
    return (
        h_p.reshape(BATCH, SEQ, D_MODEL),
        h_s.reshape(DEC_BATCH, DEC_SEQ, D_MODEL),
        st(c_p, state_C.dtype), st(n_p[:, :, 0, :], state_n.dtype), st(m_p[:, :, 0, 0], state_m.dtype),
        jnp.broadcast_to(kv4(meta_rows[:, :B_KV])[None], (BATCH, N_META, B_KV_HEADS, B_DH)),
        jnp.broadcast_to(kv4(meta_rows[:, B_KV:])[None], (BATCH, N_META, B_KV_HEADS, B_DH)),
        kv4(kv_p3[:, -WINDOW:, :B_KV]), kv4(kv_p3[:, -WINDOW:, B_KV:]),
        st(c_s, state_C.dtype), st(n_s[:, :, 0, :], state_n.dtype), st(m_s[:, :, 0, 0], state_m.dtype),
        k_win_s, v_win_s,
    )
```

```python
import functools

import jax
import jax.numpy as jnp
from jax import lax
from jax.experimental import pallas as pl
from jax.experimental.pallas import tpu as pltpu

D_MODEL = 2048
BATCH = 8
SEQ = 2048
DEC_BATCH = 128
DEC_SEQ = 4
PAST_LEN = 8192
N_META = 16
A_HEADS = 4
A_DV = D_MODEL // A_HEADS
A_DK = A_DV // 2
A_CHUNK = 128
A_GATE_CAP = 15.0
A_QKVO = 2 * A_HEADS * A_DK + 2 * A_HEADS * A_DV
B_HEADS = 32
B_DH = D_MODEL // B_HEADS
B_KV_HEADS = 4
B_GROUP = B_HEADS // B_KV_HEADS
B_KV = B_KV_HEADS * B_DH
WINDOW = 128
D_FF = ((8 * D_MODEL // 3 + 255) // 256) * 256
EPS = 1e-6

LANES = 128
SUBLANES = 8
VMEM_DEFAULT_MIB = 48
VMEM_FFN_MIB = 60

P_ROWS = BATCH * SEQ
S_ROWS = DEC_BATCH * DEC_SEQ
Q_ROWS = S_ROWS + A_CHUNK
META_BLOCK = S_ROWS // A_CHUNK

F32 = jnp.float32
BF16 = jnp.bfloat16
NEG_INF = float("-inf")


def _params(n_axes, vmem_mib=VMEM_DEFAULT_MIB):
    return pltpu.CompilerParams(dimension_semantics=("arbitrary",) * n_axes,
                                vmem_limit_bytes=vmem_mib * 1024 * 1024)


def _resident(shape):
    return pl.BlockSpec(shape, lambda i: (0, 0), pipeline_mode=pl.Buffered(1))


def _rms(x, g):
    return x * lax.rsqrt(jnp.mean(x * x, axis=-1, keepdims=True) + EPS) * g


def _dot(a, b):
    return jnp.dot(a, b, preferred_element_type=F32)


def _dot_nt(a, b):
    return lax.dot_general(a, b, (((1,), (1,)), ((), ())), preferred_element_type=F32)


def _dot_tn(a, b):
    return lax.dot_general(a, b, (((0,), (0,)), ((), ())), preferred_element_type=F32)


def _log_sigmoid(x):
    return -(jnp.maximum(-x, 0.0) + jnp.log1p(jnp.exp(-jnp.abs(x))))


def _head_norm64(y, gain):
    lo = lax.broadcasted_iota(jnp.int32, (1, LANES), 1) < B_DH
    cols = []
    for c in range(y.shape[1] // LANES):
        x = y[:, c * LANES:(c + 1) * LANES]
        xx = x * x
        s_lo = jnp.sum(jnp.where(lo, xx, 0.0), axis=-1, keepdims=True)
        s_hi = jnp.sum(jnp.where(lo, 0.0, xx), axis=-1, keepdims=True)
        scale = jnp.where(lo, lax.rsqrt(s_lo / B_DH + EPS), lax.rsqrt(s_hi / B_DH + EPS))
        cols.append(x * scale * gain[:, c * LANES:(c + 1) * LANES])
    return jnp.concatenate(cols, axis=1)


FFN_TF = 512
FFN_CAST_TF = 256


def _ffn_kernel(h_ref, g_ref, wg_ref, wu_ref, wo_ref, o_ref, *rest, emit_bf16):
    j = pl.program_id(1)
    xn_ref = rest[-1]

    def half_ffn(xn, wg, wu, wo):
        g = _dot(xn, wg)
        u = _dot(xn, wu)
        a = (g / (1.0 + jnp.exp(-g))) * (0.5 * u)
        return _dot(a.astype(BF16), wo)

    def normed(h):
        xn = _rms(h, g_ref[...]).astype(BF16)
        xn_ref[...] = xn
        return xn

    if emit_bf16:
        @pl.when(j == 0)
        def _():
            h = h_ref[...]
            normed(h)
            o_ref[...] = h

        wgb_ref, wub_ref, wob_ref = rest[:-1]
        wg = wg_ref[...].astype(BF16)
        wu = wu_ref[...].astype(BF16)
        wo = wo_ref[...].astype(BF16)
        wgb_ref[...] = wg
        wub_ref[...] = wu
        wob_ref[...] = wo
        o_ref[...] += half_ffn(xn_ref[...], wg, wu, wo)
    else:
        @pl.when(j == 0)
        def _():
            h = h_ref[...]
            o_ref[...] = h + half_ffn(normed(h), wg_ref[...], wu_ref[...], wo_ref[...])

        @pl.when(j > 0)
        def _():
            o_ref[...] += half_ffn(xn_ref[...], wg_ref[...], wu_ref[...], wo_ref[...])


def _ffn(h, gain, wg, wu, wo, tm):
    rows = h.shape[0]
    tf = FFN_TF
    n_ff = D_FF // tf
    return pl.pallas_call(
        functools.partial(_ffn_kernel, emit_bf16=False),
        grid=(rows // tm, n_ff),
        in_specs=[
            pl.BlockSpec((tm, D_MODEL), lambda i, j: (i, 0)),
            pl.BlockSpec((1, D_MODEL), lambda i, j: (0, 0)),
            pl.BlockSpec((D_MODEL, tf), lambda i, j: (0, j)),
            pl.BlockSpec((D_MODEL, tf), lambda i, j: (0, j)),
            pl.BlockSpec((tf, D_MODEL), lambda i, j: (j, 0)),
        ],
        out_specs=pl.BlockSpec((tm, D_MODEL), lambda i, j: (i, 0)),
        out_shape=jax.ShapeDtypeStruct((rows, D_MODEL), F32),
        scratch_shapes=[pltpu.VMEM((tm, D_MODEL), BF16)],
        compiler_params=_params(2, VMEM_FFN_MIB),
        name="ffn",
    )(h, gain, wg, wu, wo)


def _ffn_cast(h, gain, w_in, w_out, layer, which):
    rows = h.shape[0]
    tf = FFN_CAST_TF
    n_ff = D_FF // tf
    return pl.pallas_call(
        functools.partial(_ffn_kernel, emit_bf16=True),
        grid=(1, n_ff),
        in_specs=[
            pl.BlockSpec((rows, D_MODEL), lambda i, j: (0, 0)),
            pl.BlockSpec((1, D_MODEL), lambda i, j: (0, 0)),
            pl.BlockSpec((None, None, D_MODEL, tf), lambda i, j: (layer, which, 0, j)),
            pl.BlockSpec((None, None, D_MODEL, tf), lambda i, j: (layer, which, 0, j + n_ff)),
            pl.BlockSpec((None, None, tf, D_MODEL), lambda i, j: (layer, which, j, 0)),
        ],
        out_specs=[
            pl.BlockSpec((rows, D_MODEL), lambda i, j: (0, 0)),
            pl.BlockSpec((D_MODEL, tf), lambda i, j: (0, j)),
            pl.BlockSpec((D_MODEL, tf), lambda i, j: (0, j)),
            pl.BlockSpec((tf, D_MODEL), lambda i, j: (j, 0)),
        ],
        out_shape=[
            jax.ShapeDtypeStruct((rows, D_MODEL), F32),
            jax.ShapeDtypeStruct((D_MODEL, D_FF), BF16),
            jax.ShapeDtypeStruct((D_MODEL, D_FF), BF16),
            jax.ShapeDtypeStruct((D_FF, D_MODEL), BF16),
        ],
        scratch_shapes=[pltpu.VMEM((rows, D_MODEL), BF16)],
        compiler_params=_params(2),
        name="ffn_cast",
    )(h, gain, w_in, w_in, w_out)


GATE_ROWS = 2 * A_HEADS


def _inproj_kernel(h_ref, g_ref, w_ref, wgate_ref, bgate_ref, p_ref, gates_ref, *rest, emit_bf16):
    j = pl.program_id(1)
    xn_ref = rest[-1]

    @pl.when(j == 0)
    def _():
        xn = _rms(h_ref[...], g_ref[...]).astype(BF16)
        xn_ref[...] = xn
        wgate = jnp.concatenate([wgate_ref[...], jnp.zeros((LANES - GATE_ROWS, D_MODEL), F32)], axis=0)
        pre = _dot_nt(xn, wgate.astype(BF16)) + bgate_ref[...]
        capped = A_GATE_CAP * jnp.tanh(pre / A_GATE_CAP)
        lane = lax.broadcasted_iota(jnp.int32, (1, LANES), 1)
        gates_ref[...] = jnp.where(lane < A_HEADS, capped, _log_sigmoid(capped))

    if emit_bf16:
        w = w_ref[...].astype(BF16)
        rest[0][...] = w
    else:
        w = w_ref[...]
    p_ref[...] = _dot_nt(xn_ref[...], w)


def _inproj(h, gain, w_t, w_gate_t, bgate, tm, emit_bf16=False, tn=1024):
    rows = h.shape[0]
    assert not emit_bf16 or rows == tm
    out_specs = [
        pl.BlockSpec((tm, tn), lambda i, j: (i, j)),
        pl.BlockSpec((tm, LANES), lambda i, j: (i, 0)),
    ]
    out_shape = [jax.ShapeDtypeStruct((rows, A_QKVO), F32),
                 jax.ShapeDtypeStruct((rows, LANES), F32)]
    if emit_bf16:
        out_specs.append(pl.BlockSpec((tn, D_MODEL), lambda i, j: (j, 0)))
        out_shape.append(jax.ShapeDtypeStruct((A_QKVO, D_MODEL), BF16))
    return pl.pallas_call(
        functools.partial(_inproj_kernel, emit_bf16=emit_bf16),
        grid=(rows // tm, A_QKVO // tn),
        in_specs=[
            pl.BlockSpec((tm, D_MODEL), lambda i, j: (i, 0)),
            pl.BlockSpec((1, D_MODEL), lambda i, j: (0, 0)),
            pl.BlockSpec((tn, D_MODEL), lambda i, j: (j, 0)),
            pl.BlockSpec((GATE_ROWS, D_MODEL), lambda i, j: (A_QKVO // GATE_ROWS, 0)),
            pl.BlockSpec((1, LANES), lambda i, j: (0, 0)),
        ],
        out_specs=out_specs,
        out_shape=out_shape,
        scratch_shapes=[pltpu.VMEM((tm, D_MODEL), BF16)],
        compiler_params=_params(2),
        name="mlstm_inproj",
    )(h, gain, w_t, w_gate_t, bgate)


def _mlstm_chunk_kernel(q_ref, k_ref, v_ref, g_ref, c0_ref, n0_ref, m0_ref,
                        h_ref, c_ref, n_ref, m_ref, qk_ref, qc_ref, sb_ref, *, n_valid):
    L = A_CHUNK
    heads = range(A_HEADS)

    @pl.when(pl.program_id(1) == 0)
    def _():
        c_ref[...] = c0_ref[...]
        n_ref[...] = n0_ref[...]
        m_ref[...] = m0_ref[...]

    for hd in heads:
        qb = q_ref[:, hd * A_DK:(hd + 1) * A_DK].astype(BF16)
        kb = (k_ref[:, hd * A_DK:(hd + 1) * A_DK] * (A_DK ** -0.5)).astype(BF16)
        qk_ref[hd] = _dot_nt(qb, kb)
        qc_ref[hd] = _dot(qb, c_ref[0, hd].astype(BF16))

    gates = g_ref[...]
    row = lax.broadcasted_iota(jnp.int32, (L, L), 0)
    col = lax.broadcasted_iota(jnp.int32, (L, L), 1)
    causal = col <= row
    eye = col == row
    masked = n_valid < L
    if masked:
        row_ok = lax.broadcasted_iota(jnp.int32, (L, 1), 0) < n_valid
        gates_lf = jnp.where(row_ok, gates, 0.0)
    else:
        gates_lf = gates
    csum = jnp.dot(causal.astype(F32), gates_lf, precision=lax.Precision.HIGHEST,
                   preferred_element_type=F32)

    def to_row(x_col):
        return jnp.sum(jnp.where(eye, x_col, 0.0), axis=0, keepdims=True)

    w_intra, w_inter, floor, w_state, decay, m_new = [], [], [], [], [], []
    for hd in heads:
        b_col = csum[:, A_HEADS + hd:A_HEADS + hd + 1]
        ig_col = gates[:, hd:hd + 1]
        if masked:
            ig_col = jnp.where(row_ok, ig_col, NEG_INF)
        b_row = to_row(b_col)
        ig_row = to_row(ig_col)
        m_prev = m_ref[0, hd][:, 0:1]
        d_log = jnp.where(causal, b_col - b_row + ig_row, NEG_INF)
        inter_log = b_col + m_prev
        m_t = jnp.maximum(inter_log, jnp.max(d_log, axis=-1, keepdims=True))
        w_intra.append(jnp.exp(d_log - m_t))
        w_inter.append(jnp.exp(inter_log - m_t))
        floor.append(jnp.exp(-m_t))
        b_last = b_col[L - 1:L, :]
        w_log = b_last - b_col + ig_col
        m_new.append(jnp.maximum(b_last + m_prev, jnp.max(w_log, axis=0, keepdims=True)))
        w_state.append(jnp.exp(w_log - m_new[hd]))
        decay.append(jnp.exp(b_last + m_prev - m_new[hd]))

    den = []
    for hd in heads:
        s = qk_ref[hd] * w_intra[hd]
        sb_ref[hd] = s.astype(BF16)
        q = q_ref[:, hd * A_DK:(hd + 1) * A_DK]
        d = jnp.sum(s, axis=-1, keepdims=True) + w_inter[hd] * jnp.sum(q * n_ref[0, hd], axis=-1, keepdims=True)
        den.append(jnp.maximum(jnp.abs(d), floor[hd]))

    for hd in heads:
        k = k_ref[:, hd * A_DK:(hd + 1) * A_DK] * (A_DK ** -0.5)
        vb = v_ref[:, hd * A_DV:(hd + 1) * A_DV].astype(BF16)
        num = _dot(sb_ref[hd], vb) + w_inter[hd] * qc_ref[hd]
        h_ref[:, hd * A_DV:(hd + 1) * A_DV] = num / den[hd]
        kw = k * w_state[hd]
        c_ref[0, hd] = decay[hd] * c_ref[0, hd] + _dot_tn(kw.astype(BF16), vb)
        n_ref[0, hd] = decay[hd] * n_ref[0, hd] + jnp.sum(kw, axis=0, keepdims=True)
        m_ref[0, hd] = jnp.broadcast_to(m_new[hd], (1, LANES))


def _mlstm_chunks(p, gates, c0, n0, m0, n_seq, n_chunks, row_block0, n_valid, shared_state):
    L = A_CHUNK
    rb = lambda b, c: row_block0 + b * n_chunks + c
    st = (lambda b, c: (0, 0, 0, 0)) if shared_state else (lambda b, c: (b, 0, 0, 0))
    return pl.pallas_call(
        functools.partial(_mlstm_chunk_kernel, n_valid=n_valid),
        grid=(n_seq, n_chunks),
        in_specs=[
            pl.BlockSpec((L, A_HEADS * A_DK), lambda b, c: (rb(b, c), 0)),
            pl.BlockSpec((L, A_HEADS * A_DK), lambda b, c: (rb(b, c), 1)),
            pl.BlockSpec((L, A_HEADS * A_DV), lambda b, c: (rb(b, c), 1)),
            pl.BlockSpec((L, LANES), lambda b, c: (rb(b, c), 0)),
            pl.BlockSpec((1, A_HEADS, A_DK, A_DV), st),
            pl.BlockSpec((1, A_HEADS, 1, A_DK), st),
            pl.BlockSpec((1, A_HEADS, 1, LANES), st),
        ],
        out_specs=[
            pl.BlockSpec((L, A_HEADS * A_DV), lambda b, c: (b * n_chunks + c, 0)),
            pl.BlockSpec((1, A_HEADS, A_DK, A_DV), lambda b, c: (b, 0, 0, 0)),
            pl.BlockSpec((1, A_HEADS, 1, A_DK), lambda b, c: (b, 0, 0, 0)),
            pl.BlockSpec((1, A_HEADS, 1, LANES), lambda b, c: (b, 0, 0, 0)),
        ],
        out_shape=[
            jax.ShapeDtypeStruct((n_seq * n_chunks * L, D_MODEL), F32),
            jax.ShapeDtypeStruct((n_seq, A_HEADS, A_DK, A_DV), F32),
            jax.ShapeDtypeStruct((n_seq, A_HEADS, 1, A_DK), F32),
            jax.ShapeDtypeStruct((n_seq, A_HEADS, 1, LANES), F32),
        ],
        scratch_shapes=[pltpu.VMEM((A_HEADS, L, L), F32),
                        pltpu.VMEM((A_HEADS, L, A_DV), F32),
                        pltpu.VMEM((A_HEADS, L, L), BF16)],
        compiler_params=_params(2),
        name="mlstm_chunks",
    )(p, p, p, gates, c0, n0, m0)


def _mlstm_sample_kernel(q_ref, k_ref, v_ref, g_ref, c0_ref, n0_ref, m0_ref,
                         h_ref, c_ref, n_ref, m_ref):
    R = 2 * DEC_SEQ
    PAD = A_CHUNK - R
    gates = g_ref[...]
    r_col = lax.broadcasted_iota(jnp.int32, (R, 1), 0)
    is_a = r_col < DEC_SEQ
    row = lax.broadcasted_iota(jnp.int32, (R, LANES), 0)
    lane = lax.broadcasted_iota(jnp.int32, (R, LANES), 1)
    same = ((lane < DEC_SEQ) & (row < DEC_SEQ)) | ((lane >= DEC_SEQ) & (lane < R) & (row >= DEC_SEQ))
    causal = same & (lane <= row)
    eye = lane == row

    def to_row(x_col):
        return jnp.sum(jnp.where(eye, x_col, 0.0), axis=0, keepdims=True)

    for hd in range(A_HEADS):
        lf_col = gates[:, A_HEADS + hd:A_HEADS + hd + 1]
        ig_col = gates[:, hd:hd + 1]
        lf_row = to_row(lf_col)
        ig_row = to_row(ig_col)
        b_col = jnp.sum(jnp.where(causal, lf_row, 0.0), axis=1, keepdims=True)
        b_row = to_row(b_col)
        m_a = m0_ref[0, hd][:, 0:1]
        m_b = m0_ref[1, hd][:, 0:1]
        m_prev = jnp.where(is_a, m_a, m_b)
        c_a = c0_ref[0, hd]
        c_b = c0_ref[1, hd]
        n_a = n0_ref[0, hd]
        n_b = n0_ref[1, hd]

        q = q_ref[:, hd * A_DK:(hd + 1) * A_DK]
        k = k_ref[:, hd * A_DK:(hd + 1) * A_DK] * (A_DK ** -0.5)
        v = v_ref[:, hd * A_DV:(hd + 1) * A_DV]
        qb = q.astype(BF16)
        k_pad = jnp.concatenate([k, jnp.zeros((PAD, A_DK), F32)], axis=0).astype(BF16)
        v_pad = jnp.concatenate([v, jnp.zeros((PAD, A_DV), F32)], axis=0).astype(BF16)

        d_log = jnp.where(causal, b_col - b_row + ig_row, NEG_INF)
        inter_log = b_col + m_prev
        m_t = jnp.maximum(inter_log, jnp.max(d_log, axis=-1, keepdims=True))
        w_intra = jnp.exp(d_log - m_t)
        w_inter = jnp.exp(inter_log - m_t)
        s = _dot_nt(qb, k_pad) * w_intra
        q_c = jnp.where(is_a, _dot(qb, c_a.astype(BF16)), _dot(qb, c_b.astype(BF16)))
        num = _dot(s.astype(BF16), v_pad) + w_inter * q_c
        q_n = jnp.sum(q * jnp.where(is_a, n_a, n_b), axis=-1, keepdims=True)
        den = jnp.sum(s, axis=-1, keepdims=True) + w_inter * q_n
        den = jnp.maximum(jnp.abs(den), jnp.exp(-m_t))
        h_ref[:, hd * A_DV:(hd + 1) * A_DV] = num / den

        for idx, sel, m_x, c_x, n_x in ((0, is_a, m_a, c_a, n_a),
                                        (1, jnp.logical_not(is_a), m_b, c_b, n_b)):
            last = (idx + 1) * DEC_SEQ - 1
            b_last = b_col[last:last + 1, :]
            w_log = jnp.where(sel, b_last - b_col + ig_col, NEG_INF)
            m_new = jnp.maximum(b_last + m_x, jnp.max(w_log, axis=0, keepdims=True))
            w_state = jnp.exp(w_log - m_new)
            decay = jnp.exp(b_last + m_x - m_new)
            kw = k * w_state
            kw_pad = jnp.concatenate([kw, jnp.zeros((PAD, A_DK), F32)], axis=0).astype(BF16)
            c_ref[idx, hd] = decay * c_x + _dot_tn(kw_pad, v_pad)
            n_ref[idx, hd] = decay * n_x + jnp.sum(kw, axis=0, keepdims=True)
            m_ref[idx, hd] = jnp.broadcast_to(m_new, (1, LANES))


def _mlstm_sample(p, gates, c0, n0, m0):
    R = 2 * DEC_SEQ
    n_pairs = DEC_BATCH // 2
    st = lambda i: (i, 0, 0, 0)
    return pl.pallas_call(
        _mlstm_sample_kernel,
        grid=(n_pairs,),
        in_specs=[
            pl.BlockSpec((R, A_HEADS * A_DK), lambda i: (i, 0)),
            pl.BlockSpec((R, A_HEADS * A_DK), lambda i: (i, 1)),
            pl.BlockSpec((R, A_HEADS * A_DV), lambda i: (i, 1)),
            pl.BlockSpec((R, LANES), lambda i: (i, 0)),
            pl.BlockSpec((2, A_HEADS, A_DK, A_DV), st),
            pl.BlockSpec((2, A_HEADS, 1, A_DK), st),
            pl.BlockSpec((2, A_HEADS, 1, LANES), st),
        ],
        out_specs=[
            pl.BlockSpec((R, A_HEADS * A_DV), lambda i: (i, 0)),
            pl.BlockSpec((2, A_HEADS, A_DK, A_DV), st),
            pl.BlockSpec((2, A_HEADS, 1, A_DK), st),
            pl.BlockSpec((2, A_HEADS, 1, LANES), st),
        ],
        out_shape=[
            jax.ShapeDtypeStruct((S_ROWS, D_MODEL), F32),
            jax.ShapeDtypeStruct((DEC_BATCH, A_HEADS, A_DK, A_DV), F32),
            jax.ShapeDtypeStruct((DEC_BATCH, A_HEADS, 1, A_DK), F32),
            jax.ShapeDtypeStruct((DEC_BATCH, A_HEADS, 1, LANES), F32),
        ],
        compiler_params=_params(1),
        name="mlstm_sample",
    )(p, p, p, gates, c0, n0, m0)


def _mlstm_out_kernel(hm_ref, o_ref, hg_ref, w_ref, res_ref, out_ref):
    cols = []
    for hd in range(A_HEADS):
        x = hm_ref[:, hd * A_DV:(hd + 1) * A_DV]
        cols.append(x * lax.rsqrt(jnp.mean(x * x, axis=-1, keepdims=True) + EPS))
    hn = jnp.concatenate(cols, axis=1) * hg_ref[...]
    o = o_ref[...]
    pre = (hn * (1.0 / (1.0 + jnp.exp(-o)))).astype(BF16)
    out_ref[...] = res_ref[...] + _dot(pre, w_ref[...])


def _mlstm_out(hm, p, head_gain, w, res, tm):
    rows = hm.shape[0]
    return pl.pallas_call(
        _mlstm_out_kernel,
        grid=(rows // tm,),
        in_specs=[
            pl.BlockSpec((tm, D_MODEL), lambda i: (i, 0)),
            pl.BlockSpec((tm, D_MODEL), lambda i: (i, 2)),
            pl.BlockSpec((1, D_MODEL), lambda i: (0, 0)),
            _resident((D_MODEL, D_MODEL)),
            pl.BlockSpec((tm, D_MODEL), lambda i: (i, 0)),
        ],
        out_specs=pl.BlockSpec((tm, D_MODEL), lambda i: (i, 0)),
        out_shape=jax.ShapeDtypeStruct((rows, D_MODEL), F32),
        compiler_params=_params(1),
        name="mlstm_out",
    )(hm, p, head_gain, w, res)


def _normproj_kernel(h_ref, g_ref, w_ref, hg_ref, o_ref, *, n_norm):
    xn = _rms(h_ref[...], g_ref[...]).astype(BF16)
    y = _dot(xn, w_ref[...])
    if n_norm == y.shape[1]:
        o_ref[...] = _head_norm64(y, hg_ref[...])
    else:
        o_ref[:, :n_norm] = _head_norm64(y[:, :n_norm], hg_ref[...])
        o_ref[:, n_norm:] = y[:, n_norm:]


def _normproj(h, gain, w, head_gain, tm):
    rows = h.shape[0]
    n = w.shape[1]
    n_norm = head_gain.shape[1]
    return pl.pallas_call(
        functools.partial(_normproj_kernel, n_norm=n_norm),
        grid=(rows // tm,),
        in_specs=[
            pl.BlockSpec((tm, D_MODEL), lambda i: (i, 0)),
            pl.BlockSpec((1, D_MODEL), lambda i: (0, 0)),
            _resident((D_MODEL, n)),
            pl.BlockSpec((1, n_norm), lambda i: (0, 0)),
        ],
        out_specs=pl.BlockSpec((tm, n), lambda i: (i, 0)),
        out_shape=jax.ShapeDtypeStruct((rows, n), F32),
        compiler_params=_params(1),
        name="normproj",
    )(h, gain, w, head_gain)


def _matres_kernel(x_ref, w_ref, res_ref, o_ref):
    o_ref[...] = res_ref[...] + _dot(x_ref[...].astype(BF16), w_ref[...])


def _matres(x, w, res, tm):
    rows = x.shape[0]
    return pl.pallas_call(
        _matres_kernel,
        grid=(rows // tm,),
        in_specs=[
            pl.BlockSpec((tm, D_MODEL), lambda i: (i, 0)),
            _resident((D_MODEL, D_MODEL)),
            pl.BlockSpec((tm, D_MODEL), lambda i: (i, 0)),
        ],
        out_specs=pl.BlockSpec((tm, D_MODEL), lambda i: (i, 0)),
        out_shape=jax.ShapeDtypeStruct((rows, D_MODEL), F32),
        compiler_params=_params(1),
        name="matres",
    )(x, w, res)


def _alibi_slope(head):
    return 2.0 ** (-8.0 * (head + 1) / B_HEADS)


PROMPT_SUB = WINDOW // 2
PROMPT_BAND = WINDOW + PROMPT_SUB
PROMPT_KEYS = 2 * LANES
HEAD_PAIRS = B_HEADS // 2
PAIRS_PER_GROUP = B_GROUP // 2


def _pair_rhs(x, c):
    col = x[:, c * LANES:(c + 1) * LANES]
    rol = pltpu.roll(col, B_DH, axis=1)
    lo = lax.broadcasted_iota(jnp.int32, (1, LANES), 1) < B_DH
    even = jnp.concatenate([jnp.where(lo, col, 0.0), jnp.where(lo, 0.0, rol)], axis=0)
    odd = jnp.concatenate([jnp.where(lo, rol, 0.0), jnp.where(lo, 0.0, col)], axis=0)
    return even.astype(BF16), odd.astype(BF16)


def _attn_prompt_kernel(q_ref, kvo_ref, kvp_ref, kvm_ref, sink_ref, o_ref, bias_ref, s_ref, p_ref):
    W = WINDOW
    SB = PROMPT_SUB
    NK = PROMPT_KEYS
    first = (pl.program_id(0) == 0) & (pl.program_id(1) == 0)
    j = pl.program_id(1)
    rows = PAIRS_PER_GROUP * SB
    lo = lax.broadcasted_iota(jnp.int32, (1, LANES), 1) < B_DH

    @pl.when(first)
    def _():
        c = lax.broadcasted_iota(jnp.int32, (SB, NK), 1)
        is_meta = c < N_META
        is_sink = c == NK - 1
        for sub in range(2):
            i = lax.broadcasted_iota(jnp.int32, (SB, NK), 0) + sub * SB
            pos = c - N_META + sub * SB
            rel = W + i - pos
            in_band = (c >= N_META) & (c < N_META + PROMPT_BAND) & (rel >= 0) & (rel < W)
            for variant in range(2):
                if variant == 0:
                    dist = jnp.where(is_meta, jnp.minimum(i + N_META - c, W), rel).astype(F32)
                    valid = is_meta | (in_band & (pos >= W))
                else:
                    dist = jnp.where(is_meta, W, rel).astype(F32)
                    valid = is_meta | in_band
                for hd in range(B_HEADS):
                    r0 = (hd // 2) * SB
                    c0 = (hd % 2) * NK
                    table = jnp.where(valid, -_alibi_slope(hd) * dist, NEG_INF)
                    bias_ref[variant, sub, r0:r0 + SB, c0:c0 + NK] = jnp.where(
                        is_sink, sink_ref[0:1, hd:hd + 1], table)

    variant = jnp.minimum(j, 1)
    kvm = kvm_ref[...]
    band = jnp.concatenate([kvp_ref[...], kvo_ref[...]], axis=0)
    pad = jnp.zeros((NK - N_META - PROMPT_BAND, 2 * B_KV), F32)
    ones_rhs = jnp.concatenate([jnp.broadcast_to(jnp.where(lo, 1.0, 0.0), (NK, LANES)),
                                jnp.broadcast_to(jnp.where(lo, 0.0, 1.0), (NK, LANES))], axis=0).astype(BF16)

    sub_rows = B_KV_HEADS * rows
    for sub in range(2):
        keys = jnp.concatenate([kvm, band[sub * SB:sub * SB + PROMPT_BAND], pad], axis=0)
        k_rhs = _pair_rhs(keys, 0) + _pair_rhs(keys, 1)
        v_rhs = _pair_rhs(keys, 2) + _pair_rhs(keys, 3)
        for g in range(B_KV_HEADS):
            p0 = g * PAIRS_PER_GROUP
            r0 = (sub * B_KV_HEADS + g) * rows
            qg = jnp.concatenate([q_ref[sub * SB:(sub + 1) * SB, (p0 + m) * LANES:(p0 + m + 1) * LANES]
                                  for m in range(PAIRS_PER_GROUP)], axis=0) * (B_DH ** -0.5)
            s_ref[r0:r0 + rows, :] = _dot_nt(qg.astype(BF16), k_rhs[g])

        s = s_ref[sub * sub_rows:(sub + 1) * sub_rows, :] + bias_ref[variant, sub]
        halves = []
        for half in range(2):
            sh = s[:, half * NK:(half + 1) * NK]
            halves.append(jnp.exp(sh - jnp.max(sh, axis=-1, keepdims=True)).astype(BF16))
        p_ref[sub * sub_rows:(sub + 1) * sub_rows, :] = jnp.concatenate(halves, axis=1)

        for g in range(B_KV_HEADS):
            p0 = g * PAIRS_PER_GROUP
            r0 = (sub * B_KV_HEADS + g) * rows
            od = _dot(p_ref[r0:r0 + rows, :], jnp.concatenate([v_rhs[g], ones_rhs], axis=1))
            o = od[:, :LANES] / od[:, LANES:]
            for m in range(PAIRS_PER_GROUP):
                o_ref[sub * SB:(sub + 1) * SB, (p0 + m) * LANES:(p0 + m + 1) * LANES] = o[m * SB:(m + 1) * SB, :]


def _attn_prompt(q, kv_p, kv_q, sinks_row):
    nb = SEQ // WINDOW
    blk = lambda b, j: b * nb + j
    return pl.pallas_call(
        _attn_prompt_kernel,
        grid=(BATCH, nb),
        in_specs=[
            pl.BlockSpec((WINDOW, D_MODEL), lambda b, j: (blk(b, j), 0)),
            pl.BlockSpec((WINDOW, 2 * B_KV), lambda b, j: (blk(b, j), 0)),
            pl.BlockSpec((WINDOW, 2 * B_KV), lambda b, j: (blk(b, jnp.maximum(j - 1, 0)), 0)),
            pl.BlockSpec((N_META, 2 * B_KV), lambda b, j: (S_ROWS // N_META, 0)),
            pl.BlockSpec((1, LANES), lambda b, j: (0, 0)),
        ],
        out_specs=pl.BlockSpec((WINDOW, D_MODEL), lambda b, j: (blk(b, j), 0)),
        out_shape=jax.ShapeDtypeStruct((P_ROWS, D_MODEL), F32),
        scratch_shapes=[pltpu.VMEM((2, 2, HEAD_PAIRS * PROMPT_SUB, 2 * PROMPT_KEYS), F32),
                        pltpu.VMEM((2 * HEAD_PAIRS * PROMPT_SUB, 2 * PROMPT_KEYS), F32),
                        pltpu.VMEM((2 * HEAD_PAIRS * PROMPT_SUB, 2 * PROMPT_KEYS), BF16)],
        compiler_params=_params(2),
        name="attn_prompt",
    )(q, kv_p, kv_p, kv_q, sinks_row)


GROUP_SHIFT = B_GROUP.bit_length() - 1
assert 1 << GROUP_SHIFT == B_GROUP
SAMPLE_NEW = SUBLANES
SAMPLE_SMALL = LANES
SAMPLE_KEYS = WINDOW + SAMPLE_SMALL
SAMPLE_SEQ_PER_STEP = 8
SAMPLE_ROWS = DEC_SEQ * B_GROUP
KV_PAIRS = B_KV_HEADS // 2


def _split_rhs(col):
    lo = lax.broadcasted_iota(jnp.int32, (1, LANES), 1) < B_DH
    return jnp.concatenate([jnp.where(lo, col, 0.0), jnp.where(lo, 0.0, col)], axis=0)


def _split_cols(mat_t):
    top = lax.broadcasted_iota(jnp.int32, (LANES, 1), 0) < B_DH
    return jnp.concatenate([jnp.where(top, mat_t, 0.0), jnp.where(top, 0.0, mat_t)], axis=1)


def _attn_sample_kernel(q_ref, km_ref, vm_ref, kvn_ref, kt_ref, vt_ref, sink_ref,
                        o_ref, kt_out_ref, vt_out_ref, bias_ref, s_ref, p_ref):
    W = WINDOW
    NK = SAMPLE_KEYS
    R = SAMPLE_ROWS
    lo = lax.broadcasted_iota(jnp.int32, (1, LANES), 1) < B_DH
    lane = lax.broadcasted_iota(jnp.int32, (1, LANES), 1)

    @pl.when(pl.program_id(0) == 0)
    def _():
        row = lax.broadcasted_iota(jnp.int32, (R, 1), 0)
        r_in_group = jnp.bitwise_and(row, B_GROUP - 1)
        t = jnp.right_shift(lax.broadcasted_iota(jnp.int32, (R, NK), 0), GROUP_SHIFT)
        c = lax.broadcasted_iota(jnp.int32, (R, NK), 1)
        is_win = c < W
        is_meta = (c >= W) & (c < W + N_META)
        is_new = (c >= W + N_META) & (c < W + N_META + DEC_SEQ)
        t_new = c - (W + N_META)
        dist = jnp.where(is_win, W + t - c, jnp.where(is_meta, W, t - t_new)).astype(F32)
        valid = (is_win & (c > t)) | is_meta | (is_new & (t_new <= t))
        for pair in range(KV_PAIRS):
            for e in range(2):
                slope = jnp.zeros((R, 1), F32)
                sink = jnp.zeros((R, 1), F32)
                for r in range(B_GROUP):
                    hd = (2 * pair + e) * B_GROUP + r
                    slope = jnp.where(r_in_group == r, _alibi_slope(hd), slope)
                    sink = jnp.where(r_in_group == r, sink_ref[0:1, hd:hd + 1], sink)
                table = jnp.where(c == NK - 1, sink, jnp.where(valid, -slope * dist, NEG_INF))
                for sq in range(SAMPLE_SEQ_PER_STEP):
                    r0 = (sq * KV_PAIRS + pair) * R
                    bias_ref[r0:r0 + R, e * NK:(e + 1) * NK] = table

    sel_r = lax.broadcasted_iota(jnp.int32, (SAMPLE_SMALL, W), 0)
    sel_c = lax.broadcasted_iota(jnp.int32, (SAMPLE_SMALL, W), 1)
    place_new = ((sel_r >= N_META) & (sel_r < N_META + DEC_SEQ)
                 & (sel_c == sel_r - N_META + W - DEC_SEQ)).astype(F32)
    pad = jnp.zeros((SAMPLE_SMALL - N_META - SAMPLE_NEW, B_KV), F32)
    top = lax.broadcasted_iota(jnp.int32, (LANES, 1), 0) < B_DH
    ones_rhs = jnp.concatenate([jnp.broadcast_to(jnp.where(lo, 1.0, 0.0), (SAMPLE_SMALL, LANES)),
                                jnp.broadcast_to(jnp.where(lo, 0.0, 1.0), (SAMPLE_SMALL, LANES))], axis=0)
    ones_t = jnp.concatenate([jnp.broadcast_to(jnp.where(top, 1.0, 0.0), (LANES, W)),
                              jnp.broadcast_to(jnp.where(top, 0.0, 1.0), (LANES, W))], axis=1)

    v_small, v_win = [], []
    for sq in range(SAMPLE_SEQ_PER_STEP):
        kvn = kvn_ref[sq]
        k_small = jnp.concatenate([km_ref[sq], kvn[:, :B_KV], pad], axis=0)
        v_small_sq = jnp.concatenate([vm_ref[sq], kvn[:, B_KV:], pad], axis=0)
        kt = kt_ref[sq]
        vt = vt_ref[sq]
        for src, small, dst in ((kt, k_small, kt_out_ref), (vt, v_small_sq, vt_out_ref)):
            new_cols = lax.dot_general(small, place_new, (((0,), (0,)), ((), ())),
                                       precision=lax.Precision.HIGHEST, preferred_element_type=F32)
            dst[sq] = jnp.where(lane < W - DEC_SEQ, pltpu.roll(src, W - DEC_SEQ, axis=1), new_cols)
        for pair in range(KV_PAIRS):
            r0 = (sq * KV_PAIRS + pair) * R
            q = (q_ref[sq, pair] * (B_DH ** -0.5)).astype(BF16)
            s_win = _dot(q, _split_cols(kt[pair * LANES:(pair + 1) * LANES, :]).astype(BF16))
            s_small = _dot_nt(q, _split_rhs(k_small[:, pair * LANES:(pair + 1) * LANES]).astype(BF16))
            for e in range(2):
                s_ref[r0:r0 + R, e * NK:e * NK + W] = s_win[:, e * W:(e + 1) * W]
                s_ref[r0:r0 + R, e * NK + W:(e + 1) * NK] = s_small[:, e * SAMPLE_SMALL:(e + 1) * SAMPLE_SMALL]
            v_win.append(jnp.concatenate([_split_cols(vt[pair * LANES:(pair + 1) * LANES, :]), ones_t],
                                         axis=0).astype(BF16))
            v_small.append(jnp.concatenate([_split_rhs(v_small_sq[:, pair * LANES:(pair + 1) * LANES]),
                                            ones_rhs], axis=1).astype(BF16))

    s = s_ref[...] + bias_ref[...]
    halves = []
    for e in range(2):
        sh = s[:, e * NK:(e + 1) * NK]
        halves.append(jnp.exp(sh - jnp.max(sh, axis=-1, keepdims=True)).astype(BF16))
    p_ref[...] = jnp.concatenate(halves, axis=1)

    for sq in range(SAMPLE_SEQ_PER_STEP):
        for pair in range(KV_PAIRS):
            b = sq * KV_PAIRS + pair
            p = p_ref[b * R:(b + 1) * R, :]
            p_win = jnp.concatenate([p[:, 0:W], p[:, NK:NK + W]], axis=1)
            p_small = jnp.concatenate([p[:, W:NK], p[:, NK + W:2 * NK]], axis=1)
            od = _dot_nt(p_win, v_win[b]) + _dot(p_small, v_small[b])
            o_ref[sq, pair] = od[:, :LANES] / od[:, LANES:]


def _attn_sample(q4, k_meta, v_meta, kv_new, k_win_t, v_win_t, sinks_row):
    nb = SAMPLE_SEQ_PER_STEP
    R = SAMPLE_ROWS
    n_rows = nb * KV_PAIRS * R
    seq3 = lambda rows, cols: pl.BlockSpec((nb, rows, cols), lambda i: (i, 0, 0))
    qspec = pl.BlockSpec((nb, KV_PAIRS, R, LANES), lambda i: (i, 0, 0, 0))
    win_shape = jax.ShapeDtypeStruct((DEC_BATCH, B_KV, WINDOW), F32)
    return pl.pallas_call(
        _attn_sample_kernel,
        grid=(DEC_BATCH // nb,),
        in_specs=[qspec, seq3(N_META, B_KV), seq3(N_META, B_KV), seq3(SAMPLE_NEW, 2 * B_KV),
                  seq3(B_KV, WINDOW), seq3(B_KV, WINDOW), pl.BlockSpec((1, LANES), lambda i: (0, 0))],
        out_specs=[qspec, seq3(B_KV, WINDOW), seq3(B_KV, WINDOW)],
        out_shape=[jax.ShapeDtypeStruct((DEC_BATCH, KV_PAIRS, R, LANES), F32), win_shape, win_shape],
        scratch_shapes=[pltpu.VMEM((n_rows, 2 * SAMPLE_KEYS), F32),
                        pltpu.VMEM((n_rows, 2 * SAMPLE_KEYS), F32),
                        pltpu.VMEM((n_rows, 2 * SAMPLE_KEYS), BF16)],
        compiler_params=_params(1),
        name="attn_sample",
    )(q4, k_meta, v_meta, kv_new, k_win_t, v_win_t, sinks_row)


def kernel(x_prompt, x_sample, state_C, state_n, state_m, cache_k_meta, cache_v_meta, cache_k_win, cache_v_win, meta_tokens, ffn_norm, w_ffn_in, w_ffn_out, mix_norm, w_a_in, b_a_gate, a_head_norm, w_a_out, kv_norm, w_kv, k_norm, w_q, q_norm, sinks, w_b_out):
    assert x_prompt.shape == (BATCH, SEQ, D_MODEL) and x_sample.shape == (DEC_BATCH, DEC_SEQ, D_MODEL)
    assert w_a_in.shape[0] == 1 and w_q.shape[0] == 1 and ffn_norm.shape[0] == 2

    wa_in_t = jnp.swapaxes(w_a_in[0], 0, 1)
    ba_gate = jnp.pad(b_a_gate[0].astype(F32), (0, LANES - 2 * A_HEADS)).reshape(1, LANES)
    wa_out = w_a_out[0].astype(BF16)
    wkv = w_kv.astype(BF16)
    wq = w_q[0].astype(BF16)
    wb_out = w_b_out[0].astype(BF16)
    row = lambda x: x.astype(F32).reshape(1, -1)
    k_gain = jnp.tile(row(k_norm), (1, B_KV_HEADS))
    q_gain = jnp.tile(row(q_norm[0]), (1, B_HEADS))
    sinks_row = jnp.pad(sinks[0].astype(F32), (0, LANES - B_HEADS)).reshape(1, LANES)

    h_p = x_prompt.reshape(P_ROWS, D_MODEL)
    h_q = jnp.concatenate([x_sample.reshape(S_ROWS, D_MODEL), meta_tokens.astype(F32),
                           jnp.zeros((A_CHUNK - N_META, D_MODEL), F32)], axis=0)
    TM_P, TM_W = 1024, 512

    h_q, *wf = _ffn_cast(h_q, row(ffn_norm[0, 0]), w_ffn_in, w_ffn_out, 0, 0)
    h_p = _ffn(h_p, row(ffn_norm[0, 0]), *wf, TM_P)
    p_q, g_q, wa_in = _inproj(h_q, row(mix_norm[0]), wa_in_t, wa_in_t, ba_gate, Q_ROWS, emit_bf16=True)
    p_p, g_p = _inproj(h_p, row(mix_norm[0]), wa_in, wa_in_t, ba_gate, TM_P)

    zc = jnp.zeros((1, A_HEADS, A_DK, A_DV), F32)
    zn = jnp.zeros((1, A_HEADS, 1, A_DK), F32)
    zm = jnp.zeros((1, A_HEADS, 1, LANES), F32)
    hm_m, c_m, n_m, m_m = _mlstm_chunks(p_q, g_q, zc, zn, zm, 1, 1, META_BLOCK, N_META, True)
    hm_p, c_p, n_p, m_p = _mlstm_chunks(p_p, g_p, c_m, n_m, m_m, BATCH, SEQ // A_CHUNK, 0, A_CHUNK, True)
    m0_s = jnp.broadcast_to(state_m[0].astype(F32)[:, :, None, None], (DEC_BATCH, A_HEADS, 1, LANES))
    hm_s, c_s, n_s, m_s = _mlstm_sample(p_q, g_q, state_C[0].astype(F32),
                                        state_n[0].astype(F32)[:, :, None, :], m0_s)
    hm_q = jnp.concatenate([hm_s, hm_m], axis=0)

    h_p = _mlstm_out(hm_p, p_p, row(a_head_norm[0]), wa_out, h_p, TM_W)
    h_q = _mlstm_out(hm_q, p_q, row(a_head_norm[0]), wa_out, h_q, A_CHUNK)
    h_q, *wf = _ffn_cast(h_q, row(ffn_norm[0, 1]), w_ffn_in, w_ffn_out, 0, 1)
    h_p = _ffn(h_p, row(ffn_norm[0, 1]), *wf, TM_P)

    kv_p = _normproj(h_p, row(kv_norm), wkv, k_gain, TM_P)
    kv_q = _normproj(h_q, row(kv_norm), wkv, k_gain, Q_ROWS)
    h_s = h_q[:S_ROWS]
    h_s, *wf = _ffn_cast(h_s, row(ffn_norm[1, 0]), w_ffn_in, w_ffn_out, 1, 0)
    h_p = _ffn(h_p, row(ffn_norm[1, 0]), *wf, TM_P)
    q_p = _normproj(h_p, row(mix_norm[1]), wq, q_gain, TM_W)
    q_s = _normproj(h_s, row(mix_norm[1]), wq, q_gain, TM_W)

    o_p = _attn_prompt(q_p, kv_p, kv_q, sinks_row)
    q4 = q_s.reshape(DEC_BATCH, DEC_SEQ, KV_PAIRS, 2, B_GROUP, B_DH).transpose(0, 2, 1, 4, 3, 5)
    q4 = q4.reshape(DEC_BATCH, KV_PAIRS, SAMPLE_ROWS, LANES)
    seq3 = lambda x: x.astype(F32).reshape(DEC_BATCH, -1, B_KV)
    win_t = lambda x: x.astype(F32).transpose(0, 2, 3, 1).reshape(DEC_BATCH, B_KV, WINDOW)
    kv_new = jnp.pad(kv_q[:S_ROWS].reshape(DEC_BATCH, DEC_SEQ, 2 * B_KV), ((0, 0), (0, SAMPLE_NEW - DEC_SEQ), (0, 0)))
    o4, k_win_t, v_win_t = _attn_sample(q4, seq3(cache_k_meta), seq3(cache_v_meta), kv_new,
                                        win_t(cache_k_win), win_t(cache_v_win), sinks_row)
    from_t = lambda x, like: x.reshape(DEC_BATCH, B_KV_HEADS, B_DH, WINDOW).transpose(0, 3, 1, 2).astype(like.dtype)
    k_win_s = from_t(k_win_t, cache_k_win)
    v_win_s = from_t(v_win_t, cache_v_win)
    o_s = o4.reshape(DEC_BATCH, KV_PAIRS, DEC_SEQ, B_GROUP, 2, B_DH).transpose(0, 2, 1, 4, 3, 5)
    o_s = o_s.reshape(S_ROWS, D_MODEL)

    h_p = _matres(o_p, wb_out, h_p, TM_W)
    h_s = _matres(o_s, wb_out, h_s, TM_W)
    h_s, *wf = _ffn_cast(h_s, row(ffn_norm[1, 1]), w_ffn_in, w_ffn_out, 1, 1)
    h_p = _ffn(h_p, row(ffn_norm[1, 1]), *wf, TM_P)

    kv4 = lambda x: x.reshape(x.shape[:-1] + (B_KV_HEADS, B_DH))
    meta_rows = kv_q[S_ROWS:S_ROWS + N_META]
    kv_p3 = kv_p.reshape(BATCH, SEQ, 2 * B_KV)
    st = lambda x, dt: x[None].astype(dt)
    return (
        h_p.reshape(BATCH, SEQ, D_MODEL),
        h_s.reshape(DEC_BATCH, DEC_SEQ, D_MODEL),
        st(c_p, state_C.dtype), st(n_p[:, :, 0, :], state_n.dtype), st(m_p[:, :, 0, 0], state_m.dtype),
        jnp.broadcast_to(kv4(meta_rows[:, :B_KV])[None], (BATCH, N_META, B_KV_HEADS, B_DH)),
        jnp.broadcast_to(kv4(meta_rows[:, B_KV:])[None], (BATCH, N_META, B_KV_HEADS, B_DH)),
        kv4(kv_p3[:, -WINDOW:, :B_KV]), kv4(kv_p3[:, -WINDOW:, B_KV:]),
        st(c_s, state_C.dtype), st(n_s[:, :, 0, :], state_n.dtype), st(m_s[:, :, 0, 0], state_m.dtype),
        k_win_s, v_win_s,
    )
```

```python
import functools

import jax
import jax.numpy as jnp
from jax import lax
from jax.experimental import pallas as pl
from jax.experimental.pallas import tpu as pltpu

D_MODEL = 2048
BATCH = 8
SEQ = 2048
DEC_BATCH = 128
DEC_SEQ = 4
PAST_LEN = 8192
N_META = 16
A_HEADS = 4
A_DV = D_MODEL // A_HEADS
A_DK = A_DV // 2
A_CHUNK = 128
A_GATE_CAP = 15.0
A_QKVO = 2 * A_HEADS * A_DK + 2 * A_HEADS * A_DV
B_HEADS = 32
B_DH = D_MODEL // B_HEADS
B_KV_HEADS = 4
B_GROUP = B_HEADS // B_KV_HEADS
B_KV = B_KV_HEADS * B_DH
WINDOW = 128
D_FF = ((8 * D_MODEL // 3 + 255) // 256) * 256
EPS = 1e-6

LANES = 128
SUBLANES = 8
VMEM_DEFAULT_MIB = 48
VMEM_FFN_MIB = 60

P_ROWS = BATCH * SEQ
S_ROWS = DEC_BATCH * DEC_SEQ
Q_ROWS = S_ROWS + A_CHUNK
META_BLOCK = S_ROWS // A_CHUNK

F32 = jnp.float32
BF16 = jnp.bfloat16
NEG_INF = float("-inf")


def _params(n_axes, vmem_mib=VMEM_DEFAULT_MIB):
    return pltpu.CompilerParams(dimension_semantics=("arbitrary",) * n_axes,
                                vmem_limit_bytes=vmem_mib * 1024 * 1024)


def _resident(shape):
    return pl.BlockSpec(shape, lambda i: (0, 0), pipeline_mode=pl.Buffered(1))


def _rms(x, g):
    return x * lax.rsqrt(jnp.mean(x * x, axis=-1, keepdims=True) + EPS) * g


def _dot(a, b):
    return jnp.dot(a, b, preferred_element_type=F32)


def _dot_nt(a, b):
    return lax.dot_general(a, b, (((1,), (1,)), ((), ())), preferred_element_type=F32)


def _dot_tn(a, b):
    return lax.dot_general(a, b, (((0,), (0,)), ((), ())), preferred_element_type=F32)


def _log_sigmoid(x):
    return -(jnp.maximum(-x, 0.0) + jnp.log1p(jnp.exp(-jnp.abs(x))))


def _head_norm64(y, gain):
    lo = lax.broadcasted_iota(jnp.int32, (1, LANES), 1) < B_DH
    cols = []
    for c in range(y.shape[1] // LANES):
        x = y[:, c * LANES:(c + 1) * LANES]
        xx = x * x
        s_lo = jnp.sum(jnp.where(lo, xx, 0.0), axis=-1, keepdims=True)
        s_hi = jnp.sum(jnp.where(lo, 0.0, xx), axis=-1, keepdims=True)
        scale = jnp.where(lo, lax.rsqrt(s_lo / B_DH + EPS), lax.rsqrt(s_hi / B_DH + EPS))
        cols.append(x * scale * gain[:, c * LANES:(c + 1) * LANES])
    return jnp.concatenate(cols, axis=1)


FFN_TF = 512
FFN_CAST_TF = 256


def _ffn_kernel(h_ref, g_ref, wg_ref, wu_ref, wo_ref, o_ref, *rest, emit_bf16):
    j = pl.program_id(1)
    xn_ref = rest[-1]

    def half_ffn(xn, wg, wu, wo):
        g = _dot(xn, wg)
        u = _dot(xn, wu)
        a = (g / (1.0 + jnp.exp(-g))) * (0.5 * u)
        return _dot(a.astype(BF16), wo)

    def normed(h):
        xn = _rms(h, g_ref[...]).astype(BF16)
        xn_ref[...] = xn
        return xn

    if emit_bf16:
        @pl.when(j == 0)
        def _():
            h = h_ref[...]
            normed(h)
            o_ref[...] = h

        wgb_ref, wub_ref, wob_ref = rest[:-1]
        wg = wg_ref[...].astype(BF16)
        wu = wu_ref[...].astype(BF16)
        wo = wo_ref[...].astype(BF16)
        wgb_ref[...] = wg
        wub_ref[...] = wu
        wob_ref[...] = wo
        o_ref[...] += half_ffn(xn_ref[...], wg, wu, wo)
    else:
        @pl.when(j == 0)
        def _():
            h = h_ref[...]
            o_ref[...] = h + half_ffn(normed(h), wg_ref[...], wu_ref[...], wo_ref[...])

        @pl.when(j > 0)
        def _():
            o_ref[...] += half_ffn(xn_ref[...], wg_ref[...], wu_ref[...], wo_ref[...])


def _ffn(h, gain, wg, wu, wo, tm):
    rows = h.shape[0]
    tf = FFN_TF
    n_ff = D_FF // tf
    return pl.pallas_call(
        functools.partial(_ffn_kernel, emit_bf16=False),
        grid=(rows // tm, n_ff),
        in_specs=[
            pl.BlockSpec((tm, D_MODEL), lambda i, j: (i, 0)),
            pl.BlockSpec((1, D_MODEL), lambda i, j: (0, 0)),
            pl.BlockSpec((D_MODEL, tf), lambda i, j: (0, j)),
            pl.BlockSpec((D_MODEL, tf), lambda i, j: (0, j)),
            pl.BlockSpec((tf, D_MODEL), lambda i, j: (j, 0)),
        ],
        out_specs=pl.BlockSpec((tm, D_MODEL), lambda i, j: (i, 0)),
        out_shape=jax.ShapeDtypeStruct((rows, D_MODEL), F32),
        scratch_shapes=[pltpu.VMEM((tm, D_MODEL), BF16)],
        compiler_params=_params(2, VMEM_FFN_MIB),
        name="ffn",
    )(h, gain, wg, wu, wo)


def _ffn_cast(h, gain, w_in, w_out, layer, which):
    rows = h.shape[0]
    tf = FFN_CAST_TF
    n_ff = D_FF // tf
    return pl.pallas_call(
        functools.partial(_ffn_kernel, emit_bf16=True),
        grid=(1, n_ff),
        in_specs=[
            pl.BlockSpec((rows, D_MODEL), lambda i, j: (0, 0)),
            pl.BlockSpec((1, D_MODEL), lambda i, j: (0, 0)),
            pl.BlockSpec((None, None, D_MODEL, tf), lambda i, j: (layer, which, 0, j)),
            pl.BlockSpec((None, None, D_MODEL, tf), lambda i, j: (layer, which, 0, j + n_ff)),
            pl.BlockSpec((None, None, tf, D_MODEL), lambda i, j: (layer, which, j, 0)),
        ],
        out_specs=[
            pl.BlockSpec((rows, D_MODEL), lambda i, j: (0, 0)),
            pl.BlockSpec((D_MODEL, tf), lambda i, j: (0, j)),
            pl.BlockSpec((D_MODEL, tf), lambda i, j: (0, j)),
            pl.BlockSpec((tf, D_MODEL), lambda i, j: (j, 0)),
        ],
        out_shape=[
            jax.ShapeDtypeStruct((rows, D_MODEL), F32),
            jax.ShapeDtypeStruct((D_MODEL, D_FF), BF16),
            jax.ShapeDtypeStruct((D_MODEL, D_FF), BF16),
            jax.ShapeDtypeStruct((D_FF, D_MODEL), BF16),
        ],
        scratch_shapes=[pltpu.VMEM((rows, D_MODEL), BF16)],
        compiler_params=_params(2),
        name="ffn_cast",
    )(h, gain, w_in, w_in, w_out)


GATE_ROWS = 2 * A_HEADS


def _inproj_kernel(h_ref, g_ref, w_ref, wgate_ref, bgate_ref, p_ref, gates_ref, *rest, emit_bf16):
    j = pl.program_id(1)
    xn_ref = rest[-1]

    @pl.when(j == 0)
    def _():
        xn = _rms(h_ref[...], g_ref[...]).astype(BF16)
        xn_ref[...] = xn
        wgate = jnp.concatenate([wgate_ref[...], jnp.zeros((LANES - GATE_ROWS, D_MODEL), F32)], axis=0)
        pre = _dot_nt(xn, wgate.astype(BF16)) + bgate_ref[...]
        capped = A_GATE_CAP * jnp.tanh(pre / A_GATE_CAP)
        lane = lax.broadcasted_iota(jnp.int32, (1, LANES), 1)
        gates_ref[...] = jnp.where(lane < A_HEADS, capped, _log_sigmoid(capped))

    if emit_bf16:
        w = w_ref[...].astype(BF16)
        rest[0][...] = w
    else:
        w = w_ref[...]
    p_ref[...] = _dot_nt(xn_ref[...], w)


def _inproj(h, gain, w_t, w_gate_t, bgate, tm, emit_bf16=False, tn=1024):
    rows = h.shape[0]
    assert not emit_bf16 or rows == tm
    out_specs = [
        pl.BlockSpec((tm, tn), lambda i, j: (i, j)),
        pl.BlockSpec((tm, LANES), lambda i, j: (i, 0)),
    ]
    out_shape = [jax.ShapeDtypeStruct((rows, A_QKVO), F32),
                 jax.ShapeDtypeStruct((rows, LANES), F32)]
    if emit_bf16:
        out_specs.append(pl.BlockSpec((tn, D_MODEL), lambda i, j: (j, 0)))
        out_shape.append(jax.ShapeDtypeStruct((A_QKVO, D_MODEL), BF16))
    return pl.pallas_call(
        functools.partial(_inproj_kernel, emit_bf16=emit_bf16),
        grid=(rows // tm, A_QKVO // tn),
        in_specs=[
            pl.BlockSpec((tm, D_MODEL), lambda i, j: (i, 0)),
            pl.BlockSpec((1, D_MODEL), lambda i, j: (0, 0)),
            pl.BlockSpec((tn, D_MODEL), lambda i, j: (j, 0)),
            pl.BlockSpec((GATE_ROWS, D_MODEL), lambda i, j: (A_QKVO // GATE_ROWS, 0)),
            pl.BlockSpec((1, LANES), lambda i, j: (0, 0)),
        ],
        out_specs=out_specs,
        out_shape=out_shape,
        scratch_shapes=[pltpu.VMEM((tm, D_MODEL), BF16)],
        compiler_params=_params(2, VMEM_FFN_MIB if tn > 1024 else VMEM_DEFAULT_MIB),
        name="mlstm_inproj",
    )(h, gain, w_t, w_gate_t, bgate)


def _mlstm_chunk_kernel(q_ref, k_ref, v_ref, g_ref, c0_ref, n0_ref, m0_ref,
                        h_ref, c_ref, n_ref, m_ref, qk_ref, qc_ref, sb_ref, *, n_valid):
    L = A_CHUNK
    heads = range(A_HEADS)

    @pl.when(pl.program_id(1) == 0)
    def _():
        c_ref[...] = c0_ref[...]
        n_ref[...] = n0_ref[...]
        m_ref[...] = m0_ref[...]

    for hd in heads:
        qb = q_ref[:, hd * A_DK:(hd + 1) * A_DK].astype(BF16)
        kb = (k_ref[:, hd * A_DK:(hd + 1) * A_DK] * (A_DK ** -0.5)).astype(BF16)
        qk_ref[hd] = _dot_nt(qb, kb)
        qc_ref[hd] = _dot(qb, c_ref[0, hd].astype(BF16))

    gates = g_ref[...]
    row = lax.broadcasted_iota(jnp.int32, (L, L), 0)
    col = lax.broadcasted_iota(jnp.int32, (L, L), 1)
    causal = col <= row
    eye = col == row
    masked = n_valid < L
    if masked:
        row_ok = lax.broadcasted_iota(jnp.int32, (L, 1), 0) < n_valid
        gates_lf = jnp.where(row_ok, gates, 0.0)
    else:
        gates_lf = gates
    csum = jnp.dot(causal.astype(F32), gates_lf, precision=lax.Precision.HIGHEST,
                   preferred_element_type=F32)

    def to_row(x_col):
        return jnp.sum(jnp.where(eye, x_col, 0.0), axis=0, keepdims=True)

    w_intra, w_inter, floor, w_state, decay, m_new = [], [], [], [], [], []
    for hd in heads:
        b_col = csum[:, A_HEADS + hd:A_HEADS + hd + 1]
        ig_col = gates[:, hd:hd + 1]
        if masked:
            ig_col = jnp.where(row_ok, ig_col, NEG_INF)
        b_row = to_row(b_col)
        ig_row = to_row(ig_col)
        m_prev = m_ref[0, hd][:, 0:1]
        d_log = jnp.where(causal, b_col - b_row + ig_row, NEG_INF)
        inter_log = b_col + m_prev
        m_t = jnp.maximum(inter_log, jnp.max(d_log, axis=-1, keepdims=True))
        w_intra.append(jnp.exp(d_log - m_t))
        w_inter.append(jnp.exp(inter_log - m_t))
        floor.append(jnp.exp(-m_t))
        b_last = b_col[L - 1:L, :]
        w_log = b_last - b_col + ig_col
        m_new.append(jnp.maximum(b_last + m_prev, jnp.max(w_log, axis=0, keepdims=True)))
        w_state.append(jnp.exp(w_log - m_new[hd]))
        decay.append(jnp.exp(b_last + m_prev - m_new[hd]))

    den = []
    for hd in heads:
        s = qk_ref[hd] * w_intra[hd]
        sb_ref[hd] = s.astype(BF16)
        q = q_ref[:, hd * A_DK:(hd + 1) * A_DK]
        d = jnp.sum(s, axis=-1, keepdims=True) + w_inter[hd] * jnp.sum(q * n_ref[0, hd], axis=-1, keepdims=True)
        den.append(jnp.maximum(jnp.abs(d), floor[hd]))

    for hd in heads:
        k = k_ref[:, hd * A_DK:(hd + 1) * A_DK] * (A_DK ** -0.5)
        vb = v_ref[:, hd * A_DV:(hd + 1) * A_DV].astype(BF16)
        num = _dot(sb_ref[hd], vb) + w_inter[hd] * qc_ref[hd]
        h_ref[:, hd * A_DV:(hd + 1) * A_DV] = num / den[hd]
        kw = k * w_state[hd]
        c_ref[0, hd] = decay[hd] * c_ref[0, hd] + _dot_tn(kw.astype(BF16), vb)
        n_ref[0, hd] = decay[hd] * n_ref[0, hd] + jnp.sum(kw, axis=0, keepdims=True)
        m_ref[0, hd] = jnp.broadcast_to(m_new[hd], (1, LANES))


def _mlstm_chunks(p, gates, c0, n0, m0, n_seq, n_chunks, row_block0, n_valid, shared_state):
    L = A_CHUNK
    rb = lambda b, c: row_block0 + b * n_chunks + c
    st = (lambda b, c: (0, 0, 0, 0)) if shared_state else (lambda b, c: (b, 0, 0, 0))
    return pl.pallas_call(
        functools.partial(_mlstm_chunk_kernel, n_valid=n_valid),
        grid=(n_seq, n_chunks),
        in_specs=[
            pl.BlockSpec((L, A_HEADS * A_DK), lambda b, c: (rb(b, c), 0)),
            pl.BlockSpec((L, A_HEADS * A_DK), lambda b, c: (rb(b, c), 1)),
            pl.BlockSpec((L, A_HEADS * A_DV), lambda b, c: (rb(b, c), 1)),
            pl.BlockSpec((L, LANES), lambda b, c: (rb(b, c), 0)),
            pl.BlockSpec((1, A_HEADS, A_DK, A_DV), st),
            pl.BlockSpec((1, A_HEADS, 1, A_DK), st),
            pl.BlockSpec((1, A_HEADS, 1, LANES), st),
        ],
        out_specs=[
            pl.BlockSpec((L, A_HEADS * A_DV), lambda b, c: (b * n_chunks + c, 0)),
            pl.BlockSpec((1, A_HEADS, A_DK, A_DV), lambda b, c: (b, 0, 0, 0)),
            pl.BlockSpec((1, A_HEADS, 1, A_DK), lambda b, c: (b, 0, 0, 0)),
            pl.BlockSpec((1, A_HEADS, 1, LANES), lambda b, c: (b, 0, 0, 0)),
        ],
        out_shape=[
            jax.ShapeDtypeStruct((n_seq * n_chunks * L, D_MODEL), F32),
            jax.ShapeDtypeStruct((n_seq, A_HEADS, A_DK, A_DV), F32),
            jax.ShapeDtypeStruct((n_seq, A_HEADS, 1, A_DK), F32),
            jax.ShapeDtypeStruct((n_seq, A_HEADS, 1, LANES), F32),
        ],
        scratch_shapes=[pltpu.VMEM((A_HEADS, L, L), F32),
                        pltpu.VMEM((A_HEADS, L, A_DV), F32),
                        pltpu.VMEM((A_HEADS, L, L), BF16)],
        compiler_params=_params(2),
        name="mlstm_chunks",
    )(p, p, p, gates, c0, n0, m0)


def _mlstm_sample_kernel(q_ref, k_ref, v_ref, g_ref, c0_ref, n0_ref, m0_ref,
                         h_ref, c_ref, n_ref, m_ref):
    R = 2 * DEC_SEQ
    PAD = A_CHUNK - R
    gates = g_ref[...]
    r_col = lax.broadcasted_iota(jnp.int32, (R, 1), 0)
    is_a = r_col < DEC_SEQ
    row = lax.broadcasted_iota(jnp.int32, (R, LANES), 0)
    lane = lax.broadcasted_iota(jnp.int32, (R, LANES), 1)
    same = ((lane < DEC_SEQ) & (row < DEC_SEQ)) | ((lane >= DEC_SEQ) & (lane < R) & (row >= DEC_SEQ))
    causal = same & (lane <= row)
    eye = lane == row

    def to_row(x_col):
        return jnp.sum(jnp.where(eye, x_col, 0.0), axis=0, keepdims=True)

    for hd in range(A_HEADS):
        lf_col = gates[:, A_HEADS + hd:A_HEADS + hd + 1]
        ig_col = gates[:, hd:hd + 1]
        lf_row = to_row(lf_col)
        ig_row = to_row(ig_col)
        b_col = jnp.sum(jnp.where(causal, lf_row, 0.0), axis=1, keepdims=True)
        b_row = to_row(b_col)
        m_a = m0_ref[0, hd][:, 0:1]
        m_b = m0_ref[1, hd][:, 0:1]
        m_prev = jnp.where(is_a, m_a, m_b)
        c_a = c0_ref[0, hd]
        c_b = c0_ref[1, hd]
        n_a = n0_ref[0, hd]
        n_b = n0_ref[1, hd]

        q = q_ref[:, hd * A_DK:(hd + 1) * A_DK]
        k = k_ref[:, hd * A_DK:(hd + 1) * A_DK] * (A_DK ** -0.5)
        v = v_ref[:, hd * A_DV:(hd + 1) * A_DV]
        qb = q.astype(BF16)
        k_pad = jnp.concatenate([k, jnp.zeros((PAD, A_DK), F32)], axis=0).astype(BF16)
        v_pad = jnp.concatenate([v, jnp.zeros((PAD, A_DV), F32)], axis=0).astype(BF16)

        d_log = jnp.where(causal, b_col - b_row + ig_row, NEG_INF)
        inter_log = b_col + m_prev
        m_t = jnp.maximum(inter_log, jnp.max(d_log, axis=-1, keepdims=True))
        w_intra = jnp.exp(d_log - m_t)
        w_inter = jnp.exp(inter_log - m_t)
        s = _dot_nt(qb, k_pad) * w_intra
        q_c = jnp.where(is_a, _dot(qb, c_a.astype(BF16)), _dot(qb, c_b.astype(BF16)))
        num = _dot(s.astype(BF16), v_pad) + w_inter * q_c
        q_n = jnp.sum(q * jnp.where(is_a, n_a, n_b), axis=-1, keepdims=True)
        den = jnp.sum(s, axis=-1, keepdims=True) + w_inter * q_n
        den = jnp.maximum(jnp.abs(den), jnp.exp(-m_t))
        h_ref[:, hd * A_DV:(hd + 1) * A_DV] = num / den

        for idx, sel, m_x, c_x, n_x in ((0, is_a, m_a, c_a, n_a),
                                        (1, jnp.logical_not(is_a), m_b, c_b, n_b)):
            last = (idx + 1) * DEC_SEQ - 1
            b_last = b_col[last:last + 1, :]
            w_log = jnp.where(sel, b_last - b_col + ig_col, NEG_INF)
            m_new = jnp.maximum(b_last + m_x, jnp.max(w_log, axis=0, keepdims=True))
            w_state = jnp.exp(w_log - m_new)
            decay = jnp.exp(b_last + m_x - m_new)
            kw = k * w_state
            kw_pad = jnp.concatenate([kw, jnp.zeros((PAD, A_DK), F32)], axis=0).astype(BF16)
            c_ref[idx, hd] = decay * c_x + _dot_tn(kw_pad, v_pad)
            n_ref[idx, hd] = decay * n_x + jnp.sum(kw, axis=0, keepdims=True)
            m_ref[idx, hd] = jnp.broadcast_to(m_new, (1, LANES))


def _mlstm_sample(p, gates, c0, n0, m0):
    R = 2 * DEC_SEQ
    n_pairs = DEC_BATCH // 2
    st = lambda i: (i, 0, 0, 0)
    return pl.pallas_call(
        _mlstm_sample_kernel,
        grid=(n_pairs,),
        in_specs=[
            pl.BlockSpec((R, A_HEADS * A_DK), lambda i: (i, 0)),
            pl.BlockSpec((R, A_HEADS * A_DK), lambda i: (i, 1)),
            pl.BlockSpec((R, A_HEADS * A_DV), lambda i: (i, 1)),
            pl.BlockSpec((R, LANES), lambda i: (i, 0)),
            pl.BlockSpec((2, A_HEADS, A_DK, A_DV), st),
            pl.BlockSpec((2, A_HEADS, 1, A_DK), st),
            pl.BlockSpec((2, A_HEADS, 1, LANES), st),
        ],
        out_specs=[
            pl.BlockSpec((R, A_HEADS * A_DV), lambda i: (i, 0)),
            pl.BlockSpec((2, A_HEADS, A_DK, A_DV), st),
            pl.BlockSpec((2, A_HEADS, 1, A_DK), st),
            pl.BlockSpec((2, A_HEADS, 1, LANES), st),
        ],
        out_shape=[
            jax.ShapeDtypeStruct((S_ROWS, D_MODEL), F32),
            jax.ShapeDtypeStruct((DEC_BATCH, A_HEADS, A_DK, A_DV), F32),
            jax.ShapeDtypeStruct((DEC_BATCH, A_HEADS, 1, A_DK), F32),
            jax.ShapeDtypeStruct((DEC_BATCH, A_HEADS, 1, LANES), F32),
        ],
        compiler_params=_params(1),
        name="mlstm_sample",
    )(p, p, p, gates, c0, n0, m0)


def _mlstm_out_kernel(hm_ref, o_ref, hg_ref, w_ref, res_ref, out_ref):
    cols = []
    for hd in range(A_HEADS):
        x = hm_ref[:, hd * A_DV:(hd + 1) * A_DV]
        cols.append(x * lax.rsqrt(jnp.mean(x * x, axis=-1, keepdims=True) + EPS))
    hn = jnp.concatenate(cols, axis=1) * hg_ref[...]
    o = o_ref[...]
    pre = (hn * (1.0 / (1.0 + jnp.exp(-o)))).astype(BF16)
    out_ref[...] = res_ref[...] + _dot(pre, w_ref[...])


def _mlstm_out(hm, p, head_gain, w, res, tm):
    rows = hm.shape[0]
    return pl.pallas_call(
        _mlstm_out_kernel,
        grid=(rows // tm,),
        in_specs=[
            pl.BlockSpec((tm, D_MODEL), lambda i: (i, 0)),
            pl.BlockSpec((tm, D_MODEL), lambda i: (i, 2)),
            pl.BlockSpec((1, D_MODEL), lambda i: (0, 0)),
            _resident((D_MODEL, D_MODEL)),
            pl.BlockSpec((tm, D_MODEL), lambda i: (i, 0)),
        ],
        out_specs=pl.BlockSpec((tm, D_MODEL), lambda i: (i, 0)),
        out_shape=jax.ShapeDtypeStruct((rows, D_MODEL), F32),
        compiler_params=_params(1),
        name="mlstm_out",
    )(hm, p, head_gain, w, res)


def _normproj_kernel(h_ref, g_ref, w_ref, hg_ref, o_ref, *, n_norm):
    xn = _rms(h_ref[...], g_ref[...]).astype(BF16)
    y = _dot(xn, w_ref[...])
    if n_norm == y.shape[1]:
        o_ref[...] = _head_norm64(y, hg_ref[...]).astype(o_ref.dtype)
    else:
        o_ref[:, :n_norm] = _head_norm64(y[:, :n_norm], hg_ref[...])
        o_ref[:, n_norm:] = y[:, n_norm:]


def _normproj(h, gain, w, head_gain, tm, out_dtype=F32):
    rows = h.shape[0]
    n = w.shape[1]
    n_norm = head_gain.shape[1]
    return pl.pallas_call(
        functools.partial(_normproj_kernel, n_norm=n_norm),
        grid=(rows // tm,),
        in_specs=[
            pl.BlockSpec((tm, D_MODEL), lambda i: (i, 0)),
            pl.BlockSpec((1, D_MODEL), lambda i: (0, 0)),
            _resident((D_MODEL, n)),
            pl.BlockSpec((1, n_norm), lambda i: (0, 0)),
        ],
        out_specs=pl.BlockSpec((tm, n), lambda i: (i, 0)),
        out_shape=jax.ShapeDtypeStruct((rows, n), out_dtype),
        compiler_params=_params(1),
        name="normproj",
    )(h, gain, w, head_gain)


def _matres_kernel(x_ref, w_ref, res_ref, o_ref):
    o_ref[...] = res_ref[...] + _dot(x_ref[...].astype(BF16), w_ref[...])


def _matres(x, w, res, tm):
    rows = x.shape[0]
    return pl.pallas_call(
        _matres_kernel,
        grid=(rows // tm,),
        in_specs=[
            pl.BlockSpec((tm, D_MODEL), lambda i: (i, 0)),
            _resident((D_MODEL, D_MODEL)),
            pl.BlockSpec((tm, D_MODEL), lambda i: (i, 0)),
        ],
        out_specs=pl.BlockSpec((tm, D_MODEL), lambda i: (i, 0)),
        out_shape=jax.ShapeDtypeStruct((rows, D_MODEL), F32),
        compiler_params=_params(1),
        name="matres",
    )(x, w, res)


def _alibi_slope(head):
    return 2.0 ** (-8.0 * (head + 1) / B_HEADS)


PROMPT_SUB = WINDOW // 2
PROMPT_BAND = WINDOW + PROMPT_SUB
PROMPT_KEYS = 2 * LANES
HEAD_PAIRS = B_HEADS // 2
PAIRS_PER_GROUP = B_GROUP // 2


def _pair_rhs(x, c):
    col = x[:, c * LANES:(c + 1) * LANES]
    rol = pltpu.roll(col, B_DH, axis=1)
    lo = lax.broadcasted_iota(jnp.int32, (1, LANES), 1) < B_DH
    even = jnp.concatenate([jnp.where(lo, col, 0.0), jnp.where(lo, 0.0, rol)], axis=0)
    odd = jnp.concatenate([jnp.where(lo, rol, 0.0), jnp.where(lo, 0.0, col)], axis=0)
    return even.astype(BF16), odd.astype(BF16)


def _attn_prompt_kernel(q_ref, kvo_ref, kvp_ref, kvm_ref, sink_ref, o_ref, bias_ref, s_ref, p_ref):
    W = WINDOW
    SB = PROMPT_SUB
    NK = PROMPT_KEYS
    first = (pl.program_id(0) == 0) & (pl.program_id(1) == 0)
    j = pl.program_id(1)
    rows = PAIRS_PER_GROUP * SB
    lo = lax.broadcasted_iota(jnp.int32, (1, LANES), 1) < B_DH

    @pl.when(first)
    def _():
        c = lax.broadcasted_iota(jnp.int32, (SB, NK), 1)
        is_meta = c < N_META
        is_sink = c == NK - 1
        for sub in range(2):
            i = lax.broadcasted_iota(jnp.int32, (SB, NK), 0) + sub * SB
            pos = c - N_META + sub * SB
            rel = W + i - pos
            in_band = (c >= N_META) & (c < N_META + PROMPT_BAND) & (rel >= 0) & (rel < W)
            for variant in range(2):
                if variant == 0:
                    dist = jnp.where(is_meta, jnp.minimum(i + N_META - c, W), rel).astype(F32)
                    valid = is_meta | (in_band & (pos >= W))
                else:
                    dist = jnp.where(is_meta, W, rel).astype(F32)
                    valid = is_meta | in_band
                for hd in range(B_HEADS):
                    r0 = (hd // 2) * SB
                    c0 = (hd % 2) * NK
                    table = jnp.where(valid, -_alibi_slope(hd) * dist, NEG_INF)
                    bias_ref[variant, sub, r0:r0 + SB, c0:c0 + NK] = jnp.where(
                        is_sink, sink_ref[0:1, hd:hd + 1], table)

    variant = jnp.minimum(j, 1)
    kvm = kvm_ref[...]
    band = jnp.concatenate([kvp_ref[...], kvo_ref[...]], axis=0)
    pad = jnp.zeros((NK - N_META - PROMPT_BAND, 2 * B_KV), F32)
    ones_rhs = jnp.concatenate([jnp.broadcast_to(jnp.where(lo, 1.0, 0.0), (NK, LANES)),
                                jnp.broadcast_to(jnp.where(lo, 0.0, 1.0), (NK, LANES))], axis=0).astype(BF16)

    sub_rows = B_KV_HEADS * rows
    for sub in range(2):
        keys = jnp.concatenate([kvm, band[sub * SB:sub * SB + PROMPT_BAND], pad], axis=0)
        k_rhs = _pair_rhs(keys, 0) + _pair_rhs(keys, 1)
        v_rhs = _pair_rhs(keys, 2) + _pair_rhs(keys, 3)
        for g in range(B_KV_HEADS):
            p0 = g * PAIRS_PER_GROUP
            r0 = (sub * B_KV_HEADS + g) * rows
            qg = jnp.concatenate([q_ref[sub * SB:(sub + 1) * SB, (p0 + m) * LANES:(p0 + m + 1) * LANES]
                                  for m in range(PAIRS_PER_GROUP)], axis=0) * (B_DH ** -0.5)
            s_ref[r0:r0 + rows, :] = _dot_nt(qg.astype(BF16), k_rhs[g])

        s = s_ref[sub * sub_rows:(sub + 1) * sub_rows, :] + bias_ref[variant, sub]
        halves = []
        for half in range(2):
            sh = s[:, half * NK:(half + 1) * NK]
            halves.append(jnp.exp(sh - jnp.max(sh, axis=-1, keepdims=True)).astype(BF16))
        p_ref[sub * sub_rows:(sub + 1) * sub_rows, :] = jnp.concatenate(halves, axis=1)

        for g in range(B_KV_HEADS):
            p0 = g * PAIRS_PER_GROUP
            r0 = (sub * B_KV_HEADS + g) * rows
            od = _dot(p_ref[r0:r0 + rows, :], jnp.concatenate([v_rhs[g], ones_rhs], axis=1))
            o = (od[:, :LANES] / od[:, LANES:]).astype(o_ref.dtype)
            for m in range(PAIRS_PER_GROUP):
                o_ref[sub * SB:(sub + 1) * SB, (p0 + m) * LANES:(p0 + m + 1) * LANES] = o[m * SB:(m + 1) * SB, :]


def _attn_prompt(q, kv_p, kv_q, sinks_row):
    nb = SEQ // WINDOW
    blk = lambda b, j: b * nb + j
    return pl.pallas_call(
        _attn_prompt_kernel,
        grid=(BATCH, nb),
        in_specs=[
            pl.BlockSpec((WINDOW, D_MODEL), lambda b, j: (blk(b, j), 0)),
            pl.BlockSpec((WINDOW, 2 * B_KV), lambda b, j: (blk(b, j), 0)),
            pl.BlockSpec((WINDOW, 2 * B_KV), lambda b, j: (blk(b, jnp.maximum(j - 1, 0)), 0)),
            pl.BlockSpec((N_META, 2 * B_KV), lambda b, j: (S_ROWS // N_META, 0)),
            pl.BlockSpec((1, LANES), lambda b, j: (0, 0)),
        ],
        out_specs=pl.BlockSpec((WINDOW, D_MODEL), lambda b, j: (blk(b, j), 0)),
        out_shape=jax.ShapeDtypeStruct((P_ROWS, D_MODEL), BF16),
        scratch_shapes=[pltpu.VMEM((2, 2, HEAD_PAIRS * PROMPT_SUB, 2 * PROMPT_KEYS), F32),
                        pltpu.VMEM((2 * HEAD_PAIRS * PROMPT_SUB, 2 * PROMPT_KEYS), F32),
                        pltpu.VMEM((2 * HEAD_PAIRS * PROMPT_SUB, 2 * PROMPT_KEYS), BF16)],
        compiler_params=_params(2),
        name="attn_prompt",
    )(q, kv_p, kv_p, kv_q, sinks_row)


GROUP_SHIFT = B_GROUP.bit_length() - 1
assert 1 << GROUP_SHIFT == B_GROUP
SAMPLE_NEW = SUBLANES
SAMPLE_SMALL = LANES
SAMPLE_KEYS = WINDOW + SAMPLE_SMALL
SAMPLE_SEQ_PER_STEP = 8
SAMPLE_ROWS = DEC_SEQ * B_GROUP
KV_PAIRS = B_KV_HEADS // 2


def _split_rhs(col):
    lo = lax.broadcasted_iota(jnp.int32, (1, LANES), 1) < B_DH
    return jnp.concatenate([jnp.where(lo, col, 0.0), jnp.where(lo, 0.0, col)], axis=0)


def _split_cols(mat_t):
    top = lax.broadcasted_iota(jnp.int32, (LANES, 1), 0) < B_DH
    return jnp.concatenate([jnp.where(top, mat_t, 0.0), jnp.where(top, 0.0, mat_t)], axis=1)


def _attn_sample_kernel(q_ref, km_ref, vm_ref, kvn_ref, kt_ref, vt_ref, sink_ref,
                        o_ref, kt_out_ref, vt_out_ref, bias_ref, s_ref, p_ref):
    W = WINDOW
    NK = SAMPLE_KEYS
    R = SAMPLE_ROWS
    lo = lax.broadcasted_iota(jnp.int32, (1, LANES), 1) < B_DH
    lane = lax.broadcasted_iota(jnp.int32, (1, LANES), 1)

    @pl.when(pl.program_id(0) == 0)
    def _():
        row = lax.broadcasted_iota(jnp.int32, (R, 1), 0)
        r_in_group = jnp.bitwise_and(row, B_GROUP - 1)
        t = jnp.right_shift(lax.broadcasted_iota(jnp.int32, (R, NK), 0), GROUP_SHIFT)
        c = lax.broadcasted_iota(jnp.int32, (R, NK), 1)
        is_win = c < W
        is_meta = (c >= W) & (c < W + N_META)
        is_new = (c >= W + N_META) & (c < W + N_META + DEC_SEQ)
        t_new = c - (W + N_META)
        dist = jnp.where(is_win, W + t - c, jnp.where(is_meta, W, t - t_new)).astype(F32)
        valid = (is_win & (c > t)) | is_meta | (is_new & (t_new <= t))
        for pair in range(KV_PAIRS):
            for e in range(2):
                slope = jnp.zeros((R, 1), F32)
                sink = jnp.zeros((R, 1), F32)
                for r in range(B_GROUP):
                    hd = (2 * pair + e) * B_GROUP + r
                    slope = jnp.where(r_in_group == r, _alibi_slope(hd), slope)
                    sink = jnp.where(r_in_group == r, sink_ref[0:1, hd:hd + 1], sink)
                table = jnp.where(c == NK - 1, sink, jnp.where(valid, -slope * dist, NEG_INF))
                for sq in range(SAMPLE_SEQ_PER_STEP):
                    r0 = (sq * KV_PAIRS + pair) * R
                    bias_ref[r0:r0 + R, e * NK:(e + 1) * NK] = table

    sel_r = lax.broadcasted_iota(jnp.int32, (SAMPLE_SMALL, W), 0)
    sel_c = lax.broadcasted_iota(jnp.int32, (SAMPLE_SMALL, W), 1)
    place_new = ((sel_r >= N_META) & (sel_r < N_META + DEC_SEQ)
                 & (sel_c == sel_r - N_META + W - DEC_SEQ)).astype(F32)
    pad = jnp.zeros((SAMPLE_SMALL - N_META - SAMPLE_NEW, B_KV), F32)
    top = lax.broadcasted_iota(jnp.int32, (LANES, 1), 0) < B_DH
    ones_rhs = jnp.concatenate([jnp.broadcast_to(jnp.where(lo, 1.0, 0.0), (SAMPLE_SMALL, LANES)),
                                jnp.broadcast_to(jnp.where(lo, 0.0, 1.0), (SAMPLE_SMALL, LANES))], axis=0)
    ones_t = jnp.concatenate([jnp.broadcast_to(jnp.where(top, 1.0, 0.0), (LANES, W)),
                              jnp.broadcast_to(jnp.where(top, 0.0, 1.0), (LANES, W))], axis=1)

    v_small, v_win = [], []
    for sq in range(SAMPLE_SEQ_PER_STEP):
        kvn = kvn_ref[sq]
        k_small = jnp.concatenate([km_ref[sq], kvn[:, :B_KV], pad], axis=0)
        v_small_sq = jnp.concatenate([vm_ref[sq], kvn[:, B_KV:], pad], axis=0)
        kt = kt_ref[sq]
        vt = vt_ref[sq]
        for src, small, dst in ((kt, k_small, kt_out_ref), (vt, v_small_sq, vt_out_ref)):
            new_cols = lax.dot_general(small, place_new, (((0,), (0,)), ((), ())),
                                       precision=lax.Precision.HIGHEST, preferred_element_type=F32)
            dst[sq] = jnp.where(lane < W - DEC_SEQ, pltpu.roll(src, W - DEC_SEQ, axis=1), new_cols)
        for pair in range(KV_PAIRS):
            r0 = (sq * KV_PAIRS + pair) * R
            q = (q_ref[sq, pair] * (B_DH ** -0.5)).astype(BF16)
            s_win = _dot(q, _split_cols(kt[pair * LANES:(pair + 1) * LANES, :]).astype(BF16))
            s_small = _dot_nt(q, _split_rhs(k_small[:, pair * LANES:(pair + 1) * LANES]).astype(BF16))
            for e in range(2):
                s_ref[r0:r0 + R, e * NK:e * NK + W] = s_win[:, e * W:(e + 1) * W]
                s_ref[r0:r0 + R, e * NK + W:(e + 1) * NK] = s_small[:, e * SAMPLE_SMALL:(e + 1) * SAMPLE_SMALL]
            v_win.append(jnp.concatenate([_split_cols(vt[pair * LANES:(pair + 1) * LANES, :]), ones_t],
                                         axis=0).astype(BF16))
            v_small.append(jnp.concatenate([_split_rhs(v_small_sq[:, pair * LANES:(pair + 1) * LANES]),
                                            ones_rhs], axis=1).astype(BF16))

    s = s_ref[...] + bias_ref[...]
    halves = []
    for e in range(2):
        sh = s[:, e * NK:(e + 1) * NK]
        halves.append(jnp.exp(sh - jnp.max(sh, axis=-1, keepdims=True)).astype(BF16))
    p_ref[...] = jnp.concatenate(halves, axis=1)

    for sq in range(SAMPLE_SEQ_PER_STEP):
        for pair in range(KV_PAIRS):
            b = sq * KV_PAIRS + pair
            p = p_ref[b * R:(b + 1) * R, :]
            p_win = jnp.concatenate([p[:, 0:W], p[:, NK:NK + W]], axis=1)
            p_small = jnp.concatenate([p[:, W:NK], p[:, NK + W:2 * NK]], axis=1)
            od = _dot_nt(p_win, v_win[b]) + _dot(p_small, v_small[b])
            o_ref[sq, pair] = od[:, :LANES] / od[:, LANES:]


def _attn_sample(q4, k_meta, v_meta, kv_new, k_win_t, v_win_t, sinks_row):
    nb = SAMPLE_SEQ_PER_STEP
    R = SAMPLE_ROWS
    n_rows = nb * KV_PAIRS * R
    seq3 = lambda rows, cols: pl.BlockSpec((nb, rows, cols), lambda i: (i, 0, 0))
    qspec = pl.BlockSpec((nb, KV_PAIRS, R, LANES), lambda i: (i, 0, 0, 0))
    win_shape = jax.ShapeDtypeStruct((DEC_BATCH, B_KV, WINDOW), F32)
    return pl.pallas_call(
        _attn_sample_kernel,
        grid=(DEC_BATCH // nb,),
        in_specs=[qspec, seq3(N_META, B_KV), seq3(N_META, B_KV), seq3(SAMPLE_NEW, 2 * B_KV),
                  seq3(B_KV, WINDOW), seq3(B_KV, WINDOW), pl.BlockSpec((1, LANES), lambda i: (0, 0))],
        out_specs=[qspec, seq3(B_KV, WINDOW), seq3(B_KV, WINDOW)],
        out_shape=[jax.ShapeDtypeStruct((DEC_BATCH, KV_PAIRS, R, LANES), F32), win_shape, win_shape],
        scratch_shapes=[pltpu.VMEM((n_rows, 2 * SAMPLE_KEYS), F32),
                        pltpu.VMEM((n_rows, 2 * SAMPLE_KEYS), F32),
                        pltpu.VMEM((n_rows, 2 * SAMPLE_KEYS), BF16)],
        compiler_params=_params(1),
        name="attn_sample",
    )(q4, k_meta, v_meta, kv_new, k_win_t, v_win_t, sinks_row)


def kernel(x_prompt, x_sample, state_C, state_n, state_m, cache_k_meta, cache_v_meta, cache_k_win, cache_v_win, meta_tokens, ffn_norm, w_ffn_in, w_ffn_out, mix_norm, w_a_in, b_a_gate, a_head_norm, w_a_out, kv_norm, w_kv, k_norm, w_q, q_norm, sinks, w_b_out):
    assert x_prompt.shape == (BATCH, SEQ, D_MODEL) and x_sample.shape == (DEC_BATCH, DEC_SEQ, D_MODEL)
    assert w_a_in.shape[0] == 1 and w_q.shape[0] == 1 and ffn_norm.shape[0] == 2

    wa_in_t = jnp.swapaxes(w_a_in[0], 0, 1)
    ba_gate = jnp.pad(b_a_gate[0].astype(F32), (0, LANES - 2 * A_HEADS)).reshape(1, LANES)
    wa_out = w_a_out[0].astype(BF16)
    wkv = w_kv.astype(BF16)
    wq = w_q[0].astype(BF16)
    wb_out = w_b_out[0].astype(BF16)
    row = lambda x: x.astype(F32).reshape(1, -1)
    k_gain = jnp.tile(row(k_norm), (1, B_KV_HEADS))
    q_gain = jnp.tile(row(q_norm[0]), (1, B_HEADS))
    sinks_row = jnp.pad(sinks[0].astype(F32), (0, LANES - B_HEADS)).reshape(1, LANES)

    h_p = x_prompt.reshape(P_ROWS, D_MODEL)
    h_q = jnp.concatenate([x_sample.reshape(S_ROWS, D_MODEL), meta_tokens.astype(F32),
                           jnp.zeros((A_CHUNK - N_META, D_MODEL), F32)], axis=0)
    TM_P, TM_W = 1024, 512

    h_q, *wf = _ffn_cast(h_q, row(ffn_norm[0, 0]), w_ffn_in, w_ffn_out, 0, 0)
    h_p = _ffn(h_p, row(ffn_norm[0, 0]), *wf, TM_P)
    p_q, g_q, wa_in = _inproj(h_q, row(mix_norm[0]), wa_in_t, wa_in_t, ba_gate, Q_ROWS, emit_bf16=True)
    p_p, g_p = _inproj(h_p, row(mix_norm[0]), wa_in, wa_in_t, ba_gate, TM_P, tn=D_MODEL)

    zc = jnp.zeros((1, A_HEADS, A_DK, A_DV), F32)
    zn = jnp.zeros((1, A_HEADS, 1, A_DK), F32)
    zm = jnp.zeros((1, A_HEADS, 1, LANES), F32)
    hm_m, c_m, n_m, m_m = _mlstm_chunks(p_q, g_q, zc, zn, zm, 1, 1, META_BLOCK, N_META, True)
    hm_p, c_p, n_p, m_p = _mlstm_chunks(p_p, g_p, c_m, n_m, m_m, BATCH, SEQ // A_CHUNK, 0, A_CHUNK, True)
    m0_s = jnp.broadcast_to(state_m[0].astype(F32)[:, :, None, None], (DEC_BATCH, A_HEADS, 1, LANES))
    hm_s, c_s, n_s, m_s = _mlstm_sample(p_q, g_q, state_C[0].astype(F32),
                                        state_n[0].astype(F32)[:, :, None, :], m0_s)
    hm_q = jnp.concatenate([hm_s, hm_m], axis=0)

    h_p = _mlstm_out(hm_p, p_p, row(a_head_norm[0]), wa_out, h_p, TM_W)
    h_q = _mlstm_out(hm_q, p_q, row(a_head_norm[0]), wa_out, h_q, A_CHUNK)
    h_q, *wf = _ffn_cast(h_q, row(ffn_norm[0, 1]), w_ffn_in, w_ffn_out, 0, 1)
    h_p = _ffn(h_p, row(ffn_norm[0, 1]), *wf, TM_P)

    kv_p = _normproj(h_p, row(kv_norm), wkv, k_gain, TM_P)
    kv_q = _normproj(h_q, row(kv_norm), wkv, k_gain, Q_ROWS)
    h_s = h_q[:S_ROWS]
    h_s, *wf = _ffn_cast(h_s, row(ffn_norm[1, 0]), w_ffn_in, w_ffn_out, 1, 0)
    h_p = _ffn(h_p, row(ffn_norm[1, 0]), *wf, TM_P)
    q_p = _normproj(h_p, row(mix_norm[1]), wq, q_gain, TM_W, BF16)
    q_s = _normproj(h_s, row(mix_norm[1]), wq, q_gain, TM_W, BF16)

    o_p = _attn_prompt(q_p, kv_p, kv_q, sinks_row)
    q4 = q_s.reshape(DEC_BATCH, DEC_SEQ, KV_PAIRS, 2, B_GROUP, B_DH).transpose(0, 2, 1, 4, 3, 5)
    q4 = q4.reshape(DEC_BATCH, KV_PAIRS, SAMPLE_ROWS, LANES)
    seq3 = lambda x: x.astype(F32).reshape(DEC_BATCH, -1, B_KV)
    win_t = lambda x: x.astype(F32).transpose(0, 2, 3, 1).reshape(DEC_BATCH, B_KV, WINDOW)
    kv_new = jnp.pad(kv_q[:S_ROWS].reshape(DEC_BATCH, DEC_SEQ, 2 * B_KV), ((0, 0), (0, SAMPLE_NEW - DEC_SEQ), (0, 0)))
    o4, k_win_t, v_win_t = _attn_sample(q4, seq3(cache_k_meta), seq3(cache_v_meta), kv_new,
                                        win_t(cache_k_win), win_t(cache_v_win), sinks_row)
    from_t = lambda x, like: x.reshape(DEC_BATCH, B_KV_HEADS, B_DH, WINDOW).transpose(0, 3, 1, 2).astype(like.dtype)
    k_win_s = from_t(k_win_t, cache_k_win)
    v_win_s = from_t(v_win_t, cache_v_win)
    o_s = o4.reshape(DEC_BATCH, KV_PAIRS, DEC_SEQ, B_GROUP, 2, B_DH).transpose(0, 2, 1, 4, 3, 5)
    o_s = o_s.reshape(S_ROWS, D_MODEL)

    h_p = _matres(o_p, wb_out, h_p, TM_W)
    h_s = _matres(o_s, wb_out, h_s, TM_W)
    h_s, *wf = _ffn_cast(h_s, row(ffn_norm[1, 1]), w_ffn_in, w_ffn_out, 1, 1)
    h_p = _ffn(h_p, row(ffn_norm[1, 1]), *wf, TM_P)

    kv4 = lambda x: x.reshape(x.shape[:-1] + (B_KV_HEADS, B_DH))
    meta_rows = kv_q[S_ROWS:S_ROWS + N_META]
    kv_p3 = kv_p.reshape(BATCH, SEQ, 2 * B_KV)
    st = lambda x, dt: x[None].astype(dt)
    return (
        h_p.reshape(BATCH, SEQ, D_MODEL),
        h_s.reshape(DEC_BATCH, DEC_SEQ, D_MODEL),
        st(c_p, state_C.dtype), st(n_p[:, :, 0, :], state_n.dtype), st(m_p[:, :, 0, 0], state_m.dtype),
        jnp.broadcast_to(kv4(meta_rows[:, :B_KV])[None], (BATCH, N_META, B_KV_HEADS, B_DH)),
        jnp.broadcast_to(kv4(meta_rows[:, B_KV:])[None], (BATCH, N_META, B_KV_HEADS, B_DH)),
        kv4(kv_p3[:, -WINDOW:, :B_KV]), kv4(kv_p3[:, -WINDOW:, B_KV:]),
        st(c_s, state_C.dtype), st(n_s[:, :, 0, :], state_n.dtype), st(m_s[:, :, 0, 0], state_m.dtype),
        k_win_s, v_win_s,
    )
```

```python
import functools

import jax
import jax.numpy as jnp
from jax import lax
from jax.experimental import pallas as pl
from jax.experimental.pallas import tpu as pltpu

D_MODEL = 2048
BATCH = 8
SEQ = 2048
DEC_BATCH = 128
DEC_SEQ = 4
PAST_LEN = 8192
N_META = 16
A_HEADS = 4
A_DV = D_MODEL // A_HEADS
A_DK = A_DV // 2
A_CHUNK = 128
A_GATE_CAP = 15.0
A_QKVO = 2 * A_HEADS * A_DK + 2 * A_HEADS * A_DV
B_HEADS = 32
B_DH = D_MODEL // B_HEADS
B_KV_HEADS = 4
B_GROUP = B_HEADS // B_KV_HEADS
B_KV = B_KV_HEADS * B_DH
WINDOW = 128
D_FF = ((8 * D_MODEL // 3 + 255) // 256) * 256
EPS = 1e-6

LANES = 128
SUBLANES = 8
VMEM_DEFAULT_MIB = 48
VMEM_FFN_MIB = 60

P_ROWS = BATCH * SEQ
S_ROWS = DEC_BATCH * DEC_SEQ
Q_ROWS = S_ROWS + A_CHUNK
META_BLOCK = S_ROWS // A_CHUNK

F32 = jnp.float32
BF16 = jnp.bfloat16
NEG_INF = float("-inf")


def _params(n_axes, vmem_mib=VMEM_DEFAULT_MIB):
    return pltpu.CompilerParams(dimension_semantics=("arbitrary",) * n_axes,
                                vmem_limit_bytes=vmem_mib * 1024 * 1024)


def _resident(shape):
    return pl.BlockSpec(shape, lambda i: (0, 0), pipeline_mode=pl.Buffered(1))


def _rms(x, g):
    return x * lax.rsqrt(jnp.mean(x * x, axis=-1, keepdims=True) + EPS) * g


def _dot(a, b):
    return jnp.dot(a, b, preferred_element_type=F32)


def _dot_nt(a, b):
    return lax.dot_general(a, b, (((1,), (1,)), ((), ())), preferred_element_type=F32)


def _dot_tn(a, b):
    return lax.dot_general(a, b, (((0,), (0,)), ((), ())), preferred_element_type=F32)


def _log_sigmoid(x):
    return -(jnp.maximum(-x, 0.0) + jnp.log1p(jnp.exp(-jnp.abs(x))))


def _head_norm64(y, gain):
    lo = lax.broadcasted_iota(jnp.int32, (1, LANES), 1) < B_DH
    cols = []
    for c in range(y.shape[1] // LANES):
        x = y[:, c * LANES:(c + 1) * LANES]
        xx = x * x
        s_lo = jnp.sum(jnp.where(lo, xx, 0.0), axis=-1, keepdims=True)
        s_hi = jnp.sum(jnp.where(lo, 0.0, xx), axis=-1, keepdims=True)
        scale = jnp.where(lo, lax.rsqrt(s_lo / B_DH + EPS), lax.rsqrt(s_hi / B_DH + EPS))
        cols.append(x * scale * gain[:, c * LANES:(c + 1) * LANES])
    return jnp.concatenate(cols, axis=1)


FFN_TF = 512
FFN_CAST_TF = 256


def _ffn_kernel(h_ref, g_ref, wg_ref, wu_ref, wo_ref, o_ref, *rest, emit_bf16):
    j = pl.program_id(1)
    xn_ref = rest[-1]

    def half_ffn(xn, wg, wu, wo):
        g = _dot(xn, wg)
        u = _dot(xn, wu)
        a = (g / (1.0 + jnp.exp(-g))) * (0.5 * u)
        return _dot(a.astype(BF16), wo)

    def normed(h):
        xn = _rms(h, g_ref[...]).astype(BF16)
        xn_ref[...] = xn
        return xn

    if emit_bf16:
        @pl.when(j == 0)
        def _():
            h = h_ref[...]
            normed(h)
            o_ref[...] = h

        wgb_ref, wub_ref, wob_ref = rest[:-1]
        wg = wg_ref[...].astype(BF16)
        wu = wu_ref[...].astype(BF16)
        wo = wo_ref[...].astype(BF16)
        wgb_ref[...] = wg
        wub_ref[...] = wu
        wob_ref[...] = wo
        o_ref[...] += half_ffn(xn_ref[...], wg, wu, wo)
    else:
        @pl.when(j == 0)
        def _():
            h = h_ref[...]
            o_ref[...] = h + half_ffn(normed(h), wg_ref[...], wu_ref[...], wo_ref[...])

        @pl.when(j > 0)
        def _():
            o_ref[...] += half_ffn(xn_ref[...], wg_ref[...], wu_ref[...], wo_ref[...])


def _ffn(h, gain, wg, wu, wo, tm):
    rows = h.shape[0]
    tf = FFN_TF
    n_ff = D_FF // tf
    return pl.pallas_call(
        functools.partial(_ffn_kernel, emit_bf16=False),
        grid=(rows // tm, n_ff),
        in_specs=[
            pl.BlockSpec((tm, D_MODEL), lambda i, j: (i, 0)),
            pl.BlockSpec((1, D_MODEL), lambda i, j: (0, 0)),
            pl.BlockSpec((D_MODEL, tf), lambda i, j: (0, j)),
            pl.BlockSpec((D_MODEL, tf), lambda i, j: (0, j)),
            pl.BlockSpec((tf, D_MODEL), lambda i, j: (j, 0)),
        ],
        out_specs=pl.BlockSpec((tm, D_MODEL), lambda i, j: (i, 0)),
        out_shape=jax.ShapeDtypeStruct((rows, D_MODEL), F32),
        scratch_shapes=[pltpu.VMEM((tm, D_MODEL), BF16)],
        compiler_params=_params(2, VMEM_FFN_MIB),
        name="ffn",
    )(h, gain, wg, wu, wo)


def _ffn_cast(h, gain, w_in, w_out, layer, which):
    rows = h.shape[0]
    tf = FFN_CAST_TF
    n_ff = D_FF // tf
    return pl.pallas_call(
        functools.partial(_ffn_kernel, emit_bf16=True),
        grid=(1, n_ff),
        in_specs=[
            pl.BlockSpec((rows, D_MODEL), lambda i, j: (0, 0)),
            pl.BlockSpec((1, D_MODEL), lambda i, j: (0, 0)),
            pl.BlockSpec((None, None, D_MODEL, tf), lambda i, j: (layer, which, 0, j)),
            pl.BlockSpec((None, None, D_MODEL, tf), lambda i, j: (layer, which, 0, j + n_ff)),
            pl.BlockSpec((None, None, tf, D_MODEL), lambda i, j: (layer, which, j, 0)),
        ],
        out_specs=[
            pl.BlockSpec((rows, D_MODEL), lambda i, j: (0, 0)),
            pl.BlockSpec((D_MODEL, tf), lambda i, j: (0, j)),
            pl.BlockSpec((D_MODEL, tf), lambda i, j: (0, j)),
            pl.BlockSpec((tf, D_MODEL), lambda i, j: (j, 0)),
        ],
        out_shape=[
            jax.ShapeDtypeStruct((rows, D_MODEL), F32),
            jax.ShapeDtypeStruct((D_MODEL, D_FF), BF16),
            jax.ShapeDtypeStruct((D_MODEL, D_FF), BF16),
            jax.ShapeDtypeStruct((D_FF, D_MODEL), BF16),
        ],
        scratch_shapes=[pltpu.VMEM((rows, D_MODEL), BF16)],
        compiler_params=_params(2),
        name="ffn_cast",
    )(h, gain, w_in, w_in, w_out)


GATE_ROWS = 2 * A_HEADS


def _inproj_kernel(h_ref, g_ref, w_ref, wgate_ref, bgate_ref, p_ref, gates_ref, *rest, emit_bf16):
    j = pl.program_id(1)
    xn_ref = rest[-1]

    @pl.when(j == 0)
    def _():
        xn = _rms(h_ref[...], g_ref[...]).astype(BF16)
        xn_ref[...] = xn
        wgate = jnp.concatenate([wgate_ref[...], jnp.zeros((LANES - GATE_ROWS, D_MODEL), F32)], axis=0)
        pre = _dot_nt(xn, wgate.astype(BF16)) + bgate_ref[...]
        capped = A_GATE_CAP * jnp.tanh(pre / A_GATE_CAP)
        lane = lax.broadcasted_iota(jnp.int32, (1, LANES), 1)
        gates_ref[...] = jnp.where(lane < A_HEADS, capped, _log_sigmoid(capped))

    if emit_bf16:
        w = w_ref[...].astype(BF16)
        rest[0][...] = w
    else:
        w = w_ref[...]
    p_ref[...] = _dot_nt(xn_ref[...], w)


def _inproj(h, gain, w_t, w_gate_t, bgate, tm, emit_bf16=False, tn=1024):
    rows = h.shape[0]
    assert not emit_bf16 or rows == tm
    out_specs = [
        pl.BlockSpec((tm, tn), lambda i, j: (i, j)),
        pl.BlockSpec((tm, LANES), lambda i, j: (i, 0)),
    ]
    out_shape = [jax.ShapeDtypeStruct((rows, A_QKVO), F32),
                 jax.ShapeDtypeStruct((rows, LANES), F32)]
    if emit_bf16:
        out_specs.append(pl.BlockSpec((tn, D_MODEL), lambda i, j: (j, 0)))
        out_shape.append(jax.ShapeDtypeStruct((A_QKVO, D_MODEL), BF16))
    return pl.pallas_call(
        functools.partial(_inproj_kernel, emit_bf16=emit_bf16),
        grid=(rows // tm, A_QKVO // tn),
        in_specs=[
            pl.BlockSpec((tm, D_MODEL), lambda i, j: (i, 0)),
            pl.BlockSpec((1, D_MODEL), lambda i, j: (0, 0)),
            pl.BlockSpec((tn, D_MODEL), lambda i, j: (j, 0)),
            pl.BlockSpec((GATE_ROWS, D_MODEL), lambda i, j: (A_QKVO // GATE_ROWS, 0)),
            pl.BlockSpec((1, LANES), lambda i, j: (0, 0)),
        ],
        out_specs=out_specs,
        out_shape=out_shape,
        scratch_shapes=[pltpu.VMEM((tm, D_MODEL), BF16)],
        compiler_params=_params(2, VMEM_FFN_MIB if tn > 1024 else VMEM_DEFAULT_MIB),
        name="mlstm_inproj",
    )(h, gain, w_t, w_gate_t, bgate)


def _mlstm_chunk_kernel(q_ref, k_ref, v_ref, g_ref, c0_ref, n0_ref, m0_ref,
                        h_ref, c_ref, n_ref, m_ref, qk_ref, qc_ref, sb_ref, *, n_valid):
    L = A_CHUNK
    heads = range(A_HEADS)

    @pl.when(pl.program_id(1) == 0)
    def _():
        c_ref[...] = c0_ref[...]
        n_ref[...] = n0_ref[...]
        m_ref[...] = m0_ref[...]

    for hd in heads:
        qb = q_ref[:, hd * A_DK:(hd + 1) * A_DK].astype(BF16)
        kb = (k_ref[:, hd * A_DK:(hd + 1) * A_DK] * (A_DK ** -0.5)).astype(BF16)
        qk_ref[hd] = _dot_nt(qb, kb)
        qc_ref[hd] = _dot(qb, c_ref[0, hd].astype(BF16))

    gates = g_ref[...]
    row = lax.broadcasted_iota(jnp.int32, (L, L), 0)
    col = lax.broadcasted_iota(jnp.int32, (L, L), 1)
    causal = col <= row
    eye = col == row
    masked = n_valid < L
    if masked:
        row_ok = lax.broadcasted_iota(jnp.int32, (L, 1), 0) < n_valid
        gates_lf = jnp.where(row_ok, gates, 0.0)
    else:
        gates_lf = gates
    csum = jnp.dot(causal.astype(F32), gates_lf, precision=lax.Precision.HIGHEST,
                   preferred_element_type=F32)

    def to_row(x_col):
        return jnp.sum(jnp.where(eye, x_col, 0.0), axis=0, keepdims=True)

    w_intra, w_inter, floor, w_state, decay, m_new = [], [], [], [], [], []
    for hd in heads:
        b_col = csum[:, A_HEADS + hd:A_HEADS + hd + 1]
        ig_col = gates[:, hd:hd + 1]
        if masked:
            ig_col = jnp.where(row_ok, ig_col, NEG_INF)
        b_row = to_row(b_col)
        ig_row = to_row(ig_col)
        m_prev = m_ref[0, hd][:, 0:1]
        d_log = jnp.where(causal, b_col - b_row + ig_row, NEG_INF)
        inter_log = b_col + m_prev
        m_t = jnp.maximum(inter_log, jnp.max(d_log, axis=-1, keepdims=True))
        w_intra.append(jnp.exp(d_log - m_t))
        w_inter.append(jnp.exp(inter_log - m_t))
        floor.append(jnp.exp(-m_t))
        b_last = b_col[L - 1:L, :]
        w_log = b_last - b_col + ig_col
        m_new.append(jnp.maximum(b_last + m_prev, jnp.max(w_log, axis=0, keepdims=True)))
        w_state.append(jnp.exp(w_log - m_new[hd]))
        decay.append(jnp.exp(b_last + m_prev - m_new[hd]))

    den = []
    for hd in heads:
        s = qk_ref[hd] * w_intra[hd]
        sb_ref[hd] = s.astype(BF16)
        q = q_ref[:, hd * A_DK:(hd + 1) * A_DK]
        d = jnp.sum(s, axis=-1, keepdims=True) + w_inter[hd] * jnp.sum(q * n_ref[0, hd], axis=-1, keepdims=True)
        den.append(jnp.maximum(jnp.abs(d), floor[hd]))

    for hd in heads:
        k = k_ref[:, hd * A_DK:(hd + 1) * A_DK] * (A_DK ** -0.5)
        vb = v_ref[:, hd * A_DV:(hd + 1) * A_DV].astype(BF16)
        num = _dot(sb_ref[hd], vb) + w_inter[hd] * qc_ref[hd]
        h_ref[:, hd * A_DV:(hd + 1) * A_DV] = num / den[hd]
        kw = k * w_state[hd]
        c_ref[0, hd] = decay[hd] * c_ref[0, hd] + _dot_tn(kw.astype(BF16), vb)
        n_ref[0, hd] = decay[hd] * n_ref[0, hd] + jnp.sum(kw, axis=0, keepdims=True)
        m_ref[0, hd] = jnp.broadcast_to(m_new[hd], (1, LANES))


def _mlstm_chunks(p, gates, c0, n0, m0, n_seq, n_chunks, row_block0, n_valid, shared_state):
    L = A_CHUNK
    rb = lambda b, c: row_block0 + b * n_chunks + c
    st = (lambda b, c: (0, 0, 0, 0)) if shared_state else (lambda b, c: (b, 0, 0, 0))
    return pl.pallas_call(
        functools.partial(_mlstm_chunk_kernel, n_valid=n_valid),
        grid=(n_seq, n_chunks),
        in_specs=[
            pl.BlockSpec((L, A_HEADS * A_DK), lambda b, c: (rb(b, c), 0)),
            pl.BlockSpec((L, A_HEADS * A_DK), lambda b, c: (rb(b, c), 1)),
            pl.BlockSpec((L, A_HEADS * A_DV), lambda b, c: (rb(b, c), 1)),
            pl.BlockSpec((L, LANES), lambda b, c: (rb(b, c), 0)),
            pl.BlockSpec((1, A_HEADS, A_DK, A_DV), st),
            pl.BlockSpec((1, A_HEADS, 1, A_DK), st),
            pl.BlockSpec((1, A_HEADS, 1, LANES), st),
        ],
        out_specs=[
            pl.BlockSpec((L, A_HEADS * A_DV), lambda b, c: (b * n_chunks + c, 0)),
            pl.BlockSpec((1, A_HEADS, A_DK, A_DV), lambda b, c: (b, 0, 0, 0)),
            pl.BlockSpec((1, A_HEADS, 1, A_DK), lambda b, c: (b, 0, 0, 0)),
            pl.BlockSpec((1, A_HEADS, 1, LANES), lambda b, c: (b, 0, 0, 0)),
        ],
        out_shape=[
            jax.ShapeDtypeStruct((n_seq * n_chunks * L, D_MODEL), F32),
            jax.ShapeDtypeStruct((n_seq, A_HEADS, A_DK, A_DV), F32),
            jax.ShapeDtypeStruct((n_seq, A_HEADS, 1, A_DK), F32),
            jax.ShapeDtypeStruct((n_seq, A_HEADS, 1, LANES), F32),
        ],
        scratch_shapes=[pltpu.VMEM((A_HEADS, L, L), F32),
                        pltpu.VMEM((A_HEADS, L, A_DV), F32),
                        pltpu.VMEM((A_HEADS, L, L), BF16)],
        compiler_params=_params(2),
        name="mlstm_chunks",
    )(p, p, p, gates, c0, n0, m0)


def _mlstm_sample_kernel(q_ref, k_ref, v_ref, g_ref, c0_ref, n0_ref, m0_ref,
                         h_ref, c_ref, n_ref, m_ref):
    R = 2 * DEC_SEQ
    PAD = A_CHUNK - R
    gates = g_ref[...]
    r_col = lax.broadcasted_iota(jnp.int32, (R, 1), 0)
    is_a = r_col < DEC_SEQ
    row = lax.broadcasted_iota(jnp.int32, (R, LANES), 0)
    lane = lax.broadcasted_iota(jnp.int32, (R, LANES), 1)
    same = ((lane < DEC_SEQ) & (row < DEC_SEQ)) | ((lane >= DEC_SEQ) & (lane < R) & (row >= DEC_SEQ))
    causal = same & (lane <= row)
    eye = lane == row

    def to_row(x_col):
        return jnp.sum(jnp.where(eye, x_col, 0.0), axis=0, keepdims=True)

    for hd in range(A_HEADS):
        lf_col = gates[:, A_HEADS + hd:A_HEADS + hd + 1]
        ig_col = gates[:, hd:hd + 1]
        lf_row = to_row(lf_col)
        ig_row = to_row(ig_col)
        b_col = jnp.sum(jnp.where(causal, lf_row, 0.0), axis=1, keepdims=True)
        b_row = to_row(b_col)
        m_a = m0_ref[0, hd][:, 0:1]
        m_b = m0_ref[1, hd][:, 0:1]
        m_prev = jnp.where(is_a, m_a, m_b)
        c_a = c0_ref[0, hd]
        c_b = c0_ref[1, hd]
        n_a = n0_ref[0, hd]
        n_b = n0_ref[1, hd]

        q = q_ref[:, hd * A_DK:(hd + 1) * A_DK]
        k = k_ref[:, hd * A_DK:(hd + 1) * A_DK] * (A_DK ** -0.5)
        v = v_ref[:, hd * A_DV:(hd + 1) * A_DV]
        qb = q.astype(BF16)
        k_pad = jnp.concatenate([k, jnp.zeros((PAD, A_DK), F32)], axis=0).astype(BF16)
        v_pad = jnp.concatenate([v, jnp.zeros((PAD, A_DV), F32)], axis=0).astype(BF16)

        d_log = jnp.where(causal, b_col - b_row + ig_row, NEG_INF)
        inter_log = b_col + m_prev
        m_t = jnp.maximum(inter_log, jnp.max(d_log, axis=-1, keepdims=True))
        w_intra = jnp.exp(d_log - m_t)
        w_inter = jnp.exp(inter_log - m_t)
        s = _dot_nt(qb, k_pad) * w_intra
        q_c = jnp.where(is_a, _dot(qb, c_a.astype(BF16)), _dot(qb, c_b.astype(BF16)))
        num = _dot(s.astype(BF16), v_pad) + w_inter * q_c
        q_n = jnp.sum(q * jnp.where(is_a, n_a, n_b), axis=-1, keepdims=True)
        den = jnp.sum(s, axis=-1, keepdims=True) + w_inter * q_n
        den = jnp.maximum(jnp.abs(den), jnp.exp(-m_t))
        h_ref[:, hd * A_DV:(hd + 1) * A_DV] = num / den

        for idx, sel, m_x, c_x, n_x in ((0, is_a, m_a, c_a, n_a),
                                        (1, jnp.logical_not(is_a), m_b, c_b, n_b)):
            last = (idx + 1) * DEC_SEQ - 1
            b_last = b_col[last:last + 1, :]
            w_log = jnp.where(sel, b_last - b_col + ig_col, NEG_INF)
            m_new = jnp.maximum(b_last + m_x, jnp.max(w_log, axis=0, keepdims=True))
            w_state = jnp.exp(w_log - m_new)
            decay = jnp.exp(b_last + m_x - m_new)
            kw = k * w_state
            kw_pad = jnp.concatenate([kw, jnp.zeros((PAD, A_DK), F32)], axis=0).astype(BF16)
            c_ref[idx, hd] = decay * c_x + _dot_tn(kw_pad, v_pad)
            n_ref[idx, hd] = decay * n_x + jnp.sum(kw, axis=0, keepdims=True)
            m_ref[idx, hd] = jnp.broadcast_to(m_new, (1, LANES))


def _mlstm_sample(p, gates, c0, n0, m0):
    R = 2 * DEC_SEQ
    n_pairs = DEC_BATCH // 2
    st = lambda i: (i, 0, 0, 0)
    return pl.pallas_call(
        _mlstm_sample_kernel,
        grid=(n_pairs,),
        in_specs=[
            pl.BlockSpec((R, A_HEADS * A_DK), lambda i: (i, 0)),
            pl.BlockSpec((R, A_HEADS * A_DK), lambda i: (i, 1)),
            pl.BlockSpec((R, A_HEADS * A_DV), lambda i: (i, 1)),
            pl.BlockSpec((R, LANES), lambda i: (i, 0)),
            pl.BlockSpec((2, A_HEADS, A_DK, A_DV), st),
            pl.BlockSpec((2, A_HEADS, 1, A_DK), st),
            pl.BlockSpec((2, A_HEADS, 1, LANES), st),
        ],
        out_specs=[
            pl.BlockSpec((R, A_HEADS * A_DV), lambda i: (i, 0)),
            pl.BlockSpec((2, A_HEADS, A_DK, A_DV), st),
            pl.BlockSpec((2, A_HEADS, 1, A_DK), st),
            pl.BlockSpec((2, A_HEADS, 1, LANES), st),
        ],
        out_shape=[
            jax.ShapeDtypeStruct((S_ROWS, D_MODEL), F32),
            jax.ShapeDtypeStruct((DEC_BATCH, A_HEADS, A_DK, A_DV), F32),
            jax.ShapeDtypeStruct((DEC_BATCH, A_HEADS, 1, A_DK), F32),
            jax.ShapeDtypeStruct((DEC_BATCH, A_HEADS, 1, LANES), F32),
        ],
        compiler_params=_params(1),
        name="mlstm_sample",
    )(p, p, p, gates, c0, n0, m0)


def _mlstm_out_kernel(hm_ref, o_ref, hg_ref, w_ref, res_ref, out_ref):
    cols = []
    for hd in range(A_HEADS):
        x = hm_ref[:, hd * A_DV:(hd + 1) * A_DV]
        cols.append(x * lax.rsqrt(jnp.mean(x * x, axis=-1, keepdims=True) + EPS))
    hn = jnp.concatenate(cols, axis=1) * hg_ref[...]
    o = o_ref[...]
    pre = (hn * (1.0 / (1.0 + jnp.exp(-o)))).astype(BF16)
    out_ref[...] = res_ref[...] + _dot(pre, w_ref[...])


def _mlstm_out(hm, p, head_gain, w, res, tm):
    rows = hm.shape[0]
    return pl.pallas_call(
        _mlstm_out_kernel,
        grid=(rows // tm,),
        in_specs=[
            pl.BlockSpec((tm, D_MODEL), lambda i: (i, 0)),
            pl.BlockSpec((tm, D_MODEL), lambda i: (i, 2)),
            pl.BlockSpec((1, D_MODEL), lambda i: (0, 0)),
            _resident((D_MODEL, D_MODEL)),
            pl.BlockSpec((tm, D_MODEL), lambda i: (i, 0)),
        ],
        out_specs=pl.BlockSpec((tm, D_MODEL), lambda i: (i, 0)),
        out_shape=jax.ShapeDtypeStruct((rows, D_MODEL), F32),
        compiler_params=_params(1),
        name="mlstm_out",
    )(hm, p, head_gain, w, res)


def _normproj_kernel(h_ref, g_ref, w_ref, hg_ref, o_ref, *, n_norm):
    xn = _rms(h_ref[...], g_ref[...]).astype(BF16)
    y = _dot(xn, w_ref[...])
    if n_norm == y.shape[1]:
        o_ref[...] = _head_norm64(y, hg_ref[...]).astype(o_ref.dtype)
    else:
        o_ref[:, :n_norm] = _head_norm64(y[:, :n_norm], hg_ref[...])
        o_ref[:, n_norm:] = y[:, n_norm:]


def _normproj(h, gain, w, head_gain, tm, out_dtype=F32):
    rows = h.shape[0]
    n = w.shape[1]
    n_norm = head_gain.shape[1]
    return pl.pallas_call(
        functools.partial(_normproj_kernel, n_norm=n_norm),
        grid=(rows // tm,),
        in_specs=[
            pl.BlockSpec((tm, D_MODEL), lambda i: (i, 0)),
            pl.BlockSpec((1, D_MODEL), lambda i: (0, 0)),
            _resident((D_MODEL, n)),
            pl.BlockSpec((1, n_norm), lambda i: (0, 0)),
        ],
        out_specs=pl.BlockSpec((tm, n), lambda i: (i, 0)),
        out_shape=jax.ShapeDtypeStruct((rows, n), out_dtype),
        compiler_params=_params(1, VMEM_FFN_MIB if tm * n > 512 * D_MODEL else VMEM_DEFAULT_MIB),
        name="normproj",
    )(h, gain, w, head_gain)


def _matres_kernel(x_ref, w_ref, res_ref, o_ref):
    o_ref[...] = res_ref[...] + _dot(x_ref[...].astype(BF16), w_ref[...])


def _matres(x, w, res, tm):
    rows = x.shape[0]
    return pl.pallas_call(
        _matres_kernel,
        grid=(rows // tm,),
        in_specs=[
            pl.BlockSpec((tm, D_MODEL), lambda i: (i, 0)),
            _resident((D_MODEL, D_MODEL)),
            pl.BlockSpec((tm, D_MODEL), lambda i: (i, 0)),
        ],
        out_specs=pl.BlockSpec((tm, D_MODEL), lambda i: (i, 0)),
        out_shape=jax.ShapeDtypeStruct((rows, D_MODEL), F32),
        compiler_params=_params(1, VMEM_FFN_MIB if tm > 512 else VMEM_DEFAULT_MIB),
        name="matres",
    )(x, w, res)


def _alibi_slope(head):
    return 2.0 ** (-8.0 * (head + 1) / B_HEADS)


PROMPT_SUB = WINDOW // 2
PROMPT_BAND = WINDOW + PROMPT_SUB
PROMPT_KEYS = 2 * LANES
HEAD_PAIRS = B_HEADS // 2
PAIRS_PER_GROUP = B_GROUP // 2


def _pair_rhs(x, c):
    col = x[:, c * LANES:(c + 1) * LANES]
    rol = pltpu.roll(col, B_DH, axis=1)
    lo = lax.broadcasted_iota(jnp.int32, (1, LANES), 1) < B_DH
    even = jnp.concatenate([jnp.where(lo, col, 0.0), jnp.where(lo, 0.0, rol)], axis=0)
    odd = jnp.concatenate([jnp.where(lo, rol, 0.0), jnp.where(lo, 0.0, col)], axis=0)
    return even.astype(BF16), odd.astype(BF16)


def _attn_prompt_kernel(q_ref, kvo_ref, kvp_ref, kvm_ref, sink_ref, o_ref, bias_ref, s_ref, p_ref):
    W = WINDOW
    SB = PROMPT_SUB
    NK = PROMPT_KEYS
    first = (pl.program_id(0) == 0) & (pl.program_id(1) == 0)
    j = pl.program_id(1)
    rows = PAIRS_PER_GROUP * SB
    lo = lax.broadcasted_iota(jnp.int32, (1, LANES), 1) < B_DH

    @pl.when(first)
    def _():
        c = lax.broadcasted_iota(jnp.int32, (SB, NK), 1)
        is_meta = c < N_META
        is_sink = c == NK - 1
        for sub in range(2):
            i = lax.broadcasted_iota(jnp.int32, (SB, NK), 0) + sub * SB
            pos = c - N_META + sub * SB
            rel = W + i - pos
            in_band = (c >= N_META) & (c < N_META + PROMPT_BAND) & (rel >= 0) & (rel < W)
            for variant in range(2):
                if variant == 0:
                    dist = jnp.where(is_meta, jnp.minimum(i + N_META - c, W), rel).astype(F32)
                    valid = is_meta | (in_band & (pos >= W))
                else:
                    dist = jnp.where(is_meta, W, rel).astype(F32)
                    valid = is_meta | in_band
                for hd in range(B_HEADS):
                    r0 = (hd // 2) * SB
                    c0 = (hd % 2) * NK
                    table = jnp.where(valid, -_alibi_slope(hd) * dist, NEG_INF)
                    bias_ref[variant, sub, r0:r0 + SB, c0:c0 + NK] = jnp.where(
                        is_sink, sink_ref[0:1, hd:hd + 1], table)

    variant = jnp.minimum(j, 1)
    kvm = kvm_ref[...]
    band = jnp.concatenate([kvp_ref[...], kvo_ref[...]], axis=0)
    pad = jnp.zeros((NK - N_META - PROMPT_BAND, 2 * B_KV), F32)
    ones_rhs = jnp.concatenate([jnp.broadcast_to(jnp.where(lo, 1.0, 0.0), (NK, LANES)),
                                jnp.broadcast_to(jnp.where(lo, 0.0, 1.0), (NK, LANES))], axis=0).astype(BF16)

    sub_rows = B_KV_HEADS * rows
    for sub in range(2):
        keys = jnp.concatenate([kvm, band[sub * SB:sub * SB + PROMPT_BAND], pad], axis=0)
        k_rhs = _pair_rhs(keys, 0) + _pair_rhs(keys, 1)
        v_rhs = _pair_rhs(keys, 2) + _pair_rhs(keys, 3)
        for g in range(B_KV_HEADS):
            p0 = g * PAIRS_PER_GROUP
            r0 = (sub * B_KV_HEADS + g) * rows
            qg = jnp.concatenate([q_ref[sub * SB:(sub + 1) * SB, (p0 + m) * LANES:(p0 + m + 1) * LANES]
                                  for m in range(PAIRS_PER_GROUP)], axis=0) * (B_DH ** -0.5)
            s_ref[r0:r0 + rows, :] = _dot_nt(qg.astype(BF16), k_rhs[g])

        s = s_ref[sub * sub_rows:(sub + 1) * sub_rows, :] + bias_ref[variant, sub]
        halves = []
        for half in range(2):
            sh = s[:, half * NK:(half + 1) * NK]
            halves.append(jnp.exp(sh - jnp.max(sh, axis=-1, keepdims=True)).astype(BF16))
        p_ref[sub * sub_rows:(sub + 1) * sub_rows, :] = jnp.concatenate(halves, axis=1)

        for g in range(B_KV_HEADS):
            p0 = g * PAIRS_PER_GROUP
            r0 = (sub * B_KV_HEADS + g) * rows
            od = _dot(p_ref[r0:r0 + rows, :], jnp.concatenate([v_rhs[g], ones_rhs], axis=1))
            o = (od[:, :LANES] / od[:, LANES:]).astype(o_ref.dtype)
            for m in range(PAIRS_PER_GROUP):
                o_ref[sub * SB:(sub + 1) * SB, (p0 + m) * LANES:(p0 + m + 1) * LANES] = o[m * SB:(m + 1) * SB, :]


def _attn_prompt(q, kv_p, kv_q, sinks_row):
    nb = SEQ // WINDOW
    blk = lambda b, j: b * nb + j
    return pl.pallas_call(
        _attn_prompt_kernel,
        grid=(BATCH, nb),
        in_specs=[
            pl.BlockSpec((WINDOW, D_MODEL), lambda b, j: (blk(b, j), 0)),
            pl.BlockSpec((WINDOW, 2 * B_KV), lambda b, j: (blk(b, j), 0)),
            pl.BlockSpec((WINDOW, 2 * B_KV), lambda b, j: (blk(b, jnp.maximum(j - 1, 0)), 0)),
            pl.BlockSpec((N_META, 2 * B_KV), lambda b, j: (S_ROWS // N_META, 0)),
            pl.BlockSpec((1, LANES), lambda b, j: (0, 0)),
        ],
        out_specs=pl.BlockSpec((WINDOW, D_MODEL), lambda b, j: (blk(b, j), 0)),
        out_shape=jax.ShapeDtypeStruct((P_ROWS, D_MODEL), BF16),
        scratch_shapes=[pltpu.VMEM((2, 2, HEAD_PAIRS * PROMPT_SUB, 2 * PROMPT_KEYS), F32),
                        pltpu.VMEM((2 * HEAD_PAIRS * PROMPT_SUB, 2 * PROMPT_KEYS), F32),
                        pltpu.VMEM((2 * HEAD_PAIRS * PROMPT_SUB, 2 * PROMPT_KEYS), BF16)],
        compiler_params=_params(2),
        name="attn_prompt",
    )(q, kv_p, kv_p, kv_q, sinks_row)


GROUP_SHIFT = B_GROUP.bit_length() - 1
assert 1 << GROUP_SHIFT == B_GROUP
SAMPLE_NEW = SUBLANES
SAMPLE_SMALL = LANES
SAMPLE_KEYS = WINDOW + SAMPLE_SMALL
SAMPLE_SEQ_PER_STEP = 8
SAMPLE_ROWS = DEC_SEQ * B_GROUP
KV_PAIRS = B_KV_HEADS // 2


def _split_rhs(col):
    lo = lax.broadcasted_iota(jnp.int32, (1, LANES), 1) < B_DH
    return jnp.concatenate([jnp.where(lo, col, 0.0), jnp.where(lo, 0.0, col)], axis=0)


def _split_cols(mat_t):
    top = lax.broadcasted_iota(jnp.int32, (LANES, 1), 0) < B_DH
    return jnp.concatenate([jnp.where(top, mat_t, 0.0), jnp.where(top, 0.0, mat_t)], axis=1)


def _attn_sample_kernel(q_ref, km_ref, vm_ref, kvn_ref, kt_ref, vt_ref, sink_ref,
                        o_ref, kt_out_ref, vt_out_ref, bias_ref, s_ref, p_ref):
    W = WINDOW
    NK = SAMPLE_KEYS
    R = SAMPLE_ROWS
    lo = lax.broadcasted_iota(jnp.int32, (1, LANES), 1) < B_DH
    lane = lax.broadcasted_iota(jnp.int32, (1, LANES), 1)

    @pl.when(pl.program_id(0) == 0)
    def _():
        row = lax.broadcasted_iota(jnp.int32, (R, 1), 0)
        r_in_group = jnp.bitwise_and(row, B_GROUP - 1)
        t = jnp.right_shift(lax.broadcasted_iota(jnp.int32, (R, NK), 0), GROUP_SHIFT)
        c = lax.broadcasted_iota(jnp.int32, (R, NK), 1)
        is_win = c < W
        is_meta = (c >= W) & (c < W + N_META)
        is_new = (c >= W + N_META) & (c < W + N_META + DEC_SEQ)
        t_new = c - (W + N_META)
        dist = jnp.where(is_win, W + t - c, jnp.where(is_meta, W, t - t_new)).astype(F32)
        valid = (is_win & (c > t)) | is_meta | (is_new & (t_new <= t))
        for pair in range(KV_PAIRS):
            for e in range(2):
                slope = jnp.zeros((R, 1), F32)
                sink = jnp.zeros((R, 1), F32)
                for r in range(B_GROUP):
                    hd = (2 * pair + e) * B_GROUP + r
                    slope = jnp.where(r_in_group == r, _alibi_slope(hd), slope)
                    sink = jnp.where(r_in_group == r, sink_ref[0:1, hd:hd + 1], sink)
                table = jnp.where(c == NK - 1, sink, jnp.where(valid, -slope * dist, NEG_INF))
                for sq in range(SAMPLE_SEQ_PER_STEP):
                    r0 = (sq * KV_PAIRS + pair) * R
                    bias_ref[r0:r0 + R, e * NK:(e + 1) * NK] = table

    sel_r = lax.broadcasted_iota(jnp.int32, (SAMPLE_SMALL, W), 0)
    sel_c = lax.broadcasted_iota(jnp.int32, (SAMPLE_SMALL, W), 1)
    place_new = ((sel_r >= N_META) & (sel_r < N_META + DEC_SEQ)
                 & (sel_c == sel_r - N_META + W - DEC_SEQ)).astype(F32)
    pad = jnp.zeros((SAMPLE_SMALL - N_META - SAMPLE_NEW, B_KV), F32)
    top = lax.broadcasted_iota(jnp.int32, (LANES, 1), 0) < B_DH
    ones_rhs = jnp.concatenate([jnp.broadcast_to(jnp.where(lo, 1.0, 0.0), (SAMPLE_SMALL, LANES)),
                                jnp.broadcast_to(jnp.where(lo, 0.0, 1.0), (SAMPLE_SMALL, LANES))], axis=0)
    ones_t = jnp.concatenate([jnp.broadcast_to(jnp.where(top, 1.0, 0.0), (LANES, W)),
                              jnp.broadcast_to(jnp.where(top, 0.0, 1.0), (LANES, W))], axis=1)

    v_small, v_win = [], []
    for sq in range(SAMPLE_SEQ_PER_STEP):
        kvn = kvn_ref[sq]
        k_small = jnp.concatenate([km_ref[sq], kvn[:, :B_KV], pad], axis=0)
        v_small_sq = jnp.concatenate([vm_ref[sq], kvn[:, B_KV:], pad], axis=0)
        kt = kt_ref[sq]
        vt = vt_ref[sq]
        for src, small, dst in ((kt, k_small, kt_out_ref), (vt, v_small_sq, vt_out_ref)):
            new_cols = lax.dot_general(small, place_new, (((0,), (0,)), ((), ())),
                                       precision=lax.Precision.HIGHEST, preferred_element_type=F32)
            dst[sq] = jnp.where(lane < W - DEC_SEQ, pltpu.roll(src, W - DEC_SEQ, axis=1), new_cols)
        for pair in range(KV_PAIRS):
            r0 = (sq * KV_PAIRS + pair) * R
            q = (q_ref[sq, pair] * (B_DH ** -0.5)).astype(BF16)
            s_win = _dot(q, _split_cols(kt[pair * LANES:(pair + 1) * LANES, :]).astype(BF16))
            s_small = _dot_nt(q, _split_rhs(k_small[:, pair * LANES:(pair + 1) * LANES]).astype(BF16))
            for e in range(2):
                s_ref[r0:r0 + R, e * NK:e * NK + W] = s_win[:, e * W:(e + 1) * W]
                s_ref[r0:r0 + R, e * NK + W:(e + 1) * NK] = s_small[:, e * SAMPLE_SMALL:(e + 1) * SAMPLE_SMALL]
            v_win.append(jnp.concatenate([_split_cols(vt[pair * LANES:(pair + 1) * LANES, :]), ones_t],
                                         axis=0).astype(BF16))
            v_small.append(jnp.concatenate([_split_rhs(v_small_sq[:, pair * LANES:(pair + 1) * LANES]),
                                            ones_rhs], axis=1).astype(BF16))

    s = s_ref[...] + bias_ref[...]
    halves = []
    for e in range(2):
        sh = s[:, e * NK:(e + 1) * NK]
        halves.append(jnp.exp(sh - jnp.max(sh, axis=-1, keepdims=True)).astype(BF16))
    p_ref[...] = jnp.concatenate(halves, axis=1)

    for sq in range(SAMPLE_SEQ_PER_STEP):
        for pair in range(KV_PAIRS):
            b = sq * KV_PAIRS + pair
            p = p_ref[b * R:(b + 1) * R, :]
            p_win = jnp.concatenate([p[:, 0:W], p[:, NK:NK + W]], axis=1)
            p_small = jnp.concatenate([p[:, W:NK], p[:, NK + W:2 * NK]], axis=1)
            od = _dot_nt(p_win, v_win[b]) + _dot(p_small, v_small[b])
            o_ref[sq, pair] = od[:, :LANES] / od[:, LANES:]


def _attn_sample(q4, k_meta, v_meta, kv_new, k_win_t, v_win_t, sinks_row):
    nb = SAMPLE_SEQ_PER_STEP
    R = SAMPLE_ROWS
    n_rows = nb * KV_PAIRS * R
    seq3 = lambda rows, cols: pl.BlockSpec((nb, rows, cols), lambda i: (i, 0, 0))
    qspec = pl.BlockSpec((nb, KV_PAIRS, R, LANES), lambda i: (i, 0, 0, 0))
    win_shape = jax.ShapeDtypeStruct((DEC_BATCH, B_KV, WINDOW), F32)
    return pl.pallas_call(
        _attn_sample_kernel,
        grid=(DEC_BATCH // nb,),
        in_specs=[qspec, seq3(N_META, B_KV), seq3(N_META, B_KV), seq3(SAMPLE_NEW, 2 * B_KV),
                  seq3(B_KV, WINDOW), seq3(B_KV, WINDOW), pl.BlockSpec((1, LANES), lambda i: (0, 0))],
        out_specs=[qspec, seq3(B_KV, WINDOW), seq3(B_KV, WINDOW)],
        out_shape=[jax.ShapeDtypeStruct((DEC_BATCH, KV_PAIRS, R, LANES), F32), win_shape, win_shape],
        scratch_shapes=[pltpu.VMEM((n_rows, 2 * SAMPLE_KEYS), F32),
                        pltpu.VMEM((n_rows, 2 * SAMPLE_KEYS), F32),
                        pltpu.VMEM((n_rows, 2 * SAMPLE_KEYS), BF16)],
        compiler_params=_params(1),
        name="attn_sample",
    )(q4, k_meta, v_meta, kv_new, k_win_t, v_win_t, sinks_row)


def kernel(x_prompt, x_sample, state_C, state_n, state_m, cache_k_meta, cache_v_meta, cache_k_win, cache_v_win, meta_tokens, ffn_norm, w_ffn_in, w_ffn_out, mix_norm, w_a_in, b_a_gate, a_head_norm, w_a_out, kv_norm, w_kv, k_norm, w_q, q_norm, sinks, w_b_out):
    assert x_prompt.shape == (BATCH, SEQ, D_MODEL) and x_sample.shape == (DEC_BATCH, DEC_SEQ, D_MODEL)
    assert w_a_in.shape[0] == 1 and w_q.shape[0] == 1 and ffn_norm.shape[0] == 2

    wa_in_t = jnp.swapaxes(w_a_in[0], 0, 1)
    ba_gate = jnp.pad(b_a_gate[0].astype(F32), (0, LANES - 2 * A_HEADS)).reshape(1, LANES)
    wa_out = w_a_out[0].astype(BF16)
    wkv = w_kv.astype(BF16)
    wq = w_q[0].astype(BF16)
    wb_out = w_b_out[0].astype(BF16)
    row = lambda x: x.astype(F32).reshape(1, -1)
    k_gain = jnp.tile(row(k_norm), (1, B_KV_HEADS))
    q_gain = jnp.tile(row(q_norm[0]), (1, B_HEADS))
    sinks_row = jnp.pad(sinks[0].astype(F32), (0, LANES - B_HEADS)).reshape(1, LANES)

    h_p = x_prompt.reshape(P_ROWS, D_MODEL)
    h_q = jnp.concatenate([x_sample.reshape(S_ROWS, D_MODEL), meta_tokens.astype(F32),
                           jnp.zeros((A_CHUNK - N_META, D_MODEL), F32)], axis=0)
    TM_P, TM_W = 1024, 512

    h_q, *wf = _ffn_cast(h_q, row(ffn_norm[0, 0]), w_ffn_in, w_ffn_out, 0, 0)
    h_p = _ffn(h_p, row(ffn_norm[0, 0]), *wf, TM_P)
    p_q, g_q, wa_in = _inproj(h_q, row(mix_norm[0]), wa_in_t, wa_in_t, ba_gate, Q_ROWS, emit_bf16=True)
    p_p, g_p = _inproj(h_p, row(mix_norm[0]), wa_in, wa_in_t, ba_gate, TM_P, tn=D_MODEL)

    zc = jnp.zeros((1, A_HEADS, A_DK, A_DV), F32)
    zn = jnp.zeros((1, A_HEADS, 1, A_DK), F32)
    zm = jnp.zeros((1, A_HEADS, 1, LANES), F32)
    hm_m, c_m, n_m, m_m = _mlstm_chunks(p_q, g_q, zc, zn, zm, 1, 1, META_BLOCK, N_META, True)
    hm_p, c_p, n_p, m_p = _mlstm_chunks(p_p, g_p, c_m, n_m, m_m, BATCH, SEQ // A_CHUNK, 0, A_CHUNK, True)
    m0_s = jnp.broadcast_to(state_m[0].astype(F32)[:, :, None, None], (DEC_BATCH, A_HEADS, 1, LANES))
    hm_s, c_s, n_s, m_s = _mlstm_sample(p_q, g_q, state_C[0].astype(F32),
                                        state_n[0].astype(F32)[:, :, None, :], m0_s)
    hm_q = jnp.concatenate([hm_s, hm_m], axis=0)

    h_p = _mlstm_out(hm_p, p_p, row(a_head_norm[0]), wa_out, h_p, TM_W)
    h_q = _mlstm_out(hm_q, p_q, row(a_head_norm[0]), wa_out, h_q, A_CHUNK)
    h_q, *wf = _ffn_cast(h_q, row(ffn_norm[0, 1]), w_ffn_in, w_ffn_out, 0, 1)
    h_p = _ffn(h_p, row(ffn_norm[0, 1]), *wf, TM_P)

    kv_p = _normproj(h_p, row(kv_norm), wkv, k_gain, TM_P)
    kv_q = _normproj(h_q, row(kv_norm), wkv, k_gain, Q_ROWS)
    h_s = h_q[:S_ROWS]
    h_s, *wf = _ffn_cast(h_s, row(ffn_norm[1, 0]), w_ffn_in, w_ffn_out, 1, 0)
    h_p = _ffn(h_p, row(ffn_norm[1, 0]), *wf, TM_P)
    q_p = _normproj(h_p, row(mix_norm[1]), wq, q_gain, TM_P, BF16)
    q_s = _normproj(h_s, row(mix_norm[1]), wq, q_gain, TM_W, BF16)

    o_p = _attn_prompt(q_p, kv_p, kv_q, sinks_row)
    q4 = q_s.reshape(DEC_BATCH, DEC_SEQ, KV_PAIRS, 2, B_GROUP, B_DH).transpose(0, 2, 1, 4, 3, 5)
    q4 = q4.reshape(DEC_BATCH, KV_PAIRS, SAMPLE_ROWS, LANES)
    seq3 = lambda x: x.astype(F32).reshape(DEC_BATCH, -1, B_KV)
    win_t = lambda x: x.astype(F32).transpose(0, 2, 3, 1).reshape(DEC_BATCH, B_KV, WINDOW)
    kv_new = jnp.pad(kv_q[:S_ROWS].reshape(DEC_BATCH, DEC_SEQ, 2 * B_KV), ((0, 0), (0, SAMPLE_NEW - DEC_SEQ), (0, 0)))
    o4, k_win_t, v_win_t = _attn_sample(q4, seq3(cache_k_meta), seq3(cache_v_meta), kv_new,
                                        win_t(cache_k_win), win_t(cache_v_win), sinks_row)
    from_t = lambda x, like: x.reshape(DEC_BATCH, B_KV_HEADS, B_DH, WINDOW).transpose(0, 3, 1, 2).astype(like.dtype)
    k_win_s = from_t(k_win_t, cache_k_win)
    v_win_s = from_t(v_win_t, cache_v_win)
    o_s = o4.reshape(DEC_BATCH, KV_PAIRS, DEC_SEQ, B_GROUP, 2, B_DH).transpose(0, 2, 1, 4, 3, 5)
    o_s = o_s.reshape(S_ROWS, D_MODEL)

    h_p = _matres(o_p, wb_out, h_p, TM_P)
    h_s = _matres(o_s, wb_out, h_s, TM_W)
    h_s, *wf = _ffn_cast(h_s, row(ffn_norm[1, 1]), w_ffn_in, w_ffn_out, 1, 1)
    h_p = _ffn(h_p, row(ffn_norm[1, 1]), *wf, TM_P)

    kv4 = lambda x: x.reshape(x.shape[:-1] + (B_KV_HEADS, B_DH))
    meta_rows = kv_q[S_ROWS:S_ROWS + N_META]
    kv_p3 = kv_p.reshape(BATCH, SEQ, 2 * B_KV)
    st = lambda x, dt: x[None].astype(dt)
    return (
        h_p.reshape(BATCH, SEQ, D_MODEL),
        h_s.reshape(DEC_BATCH, DEC_SEQ, D_MODEL),
        st(c_p, state_C.dtype), st(n_p[:, :, 0, :], state_n.dtype), st(m_p[:, :, 0, 0], state_m.dtype),
        jnp.broadcast_to(kv4(meta_rows[:, :B_KV])[None], (BATCH, N_META, B_KV_HEADS, B_DH)),
        jnp.broadcast_to(kv4(meta_rows[:, B_KV:])[None], (BATCH, N_META, B_KV_HEADS, B_DH)),
        kv4(kv_p3[:, -WINDOW:, :B_KV]), kv4(kv_p3[:, -WINDOW:, B_KV:]),
        st(c_s, state_C.dtype), st(n_s[:, :, 0, :], state_n.dtype), st(m_s[:, :, 0, 0], state_m.dtype),
        k_win_s, v_win_s,
    )
```

```python
import functools

import jax
import jax.numpy as jnp
from jax import lax
from jax.experimental import pallas as pl
from jax.experimental.pallas import tpu as pltpu

D_MODEL = 2048
BATCH = 8
SEQ = 2048
DEC_BATCH = 128
DEC_SEQ = 4
PAST_LEN = 8192
N_META = 16
A_HEADS = 4
A_DV = D_MODEL // A_HEADS
A_DK = A_DV // 2
A_CHUNK = 128
A_GATE_CAP = 15.0
A_QKVO = 2 * A_HEADS * A_DK + 2 * A_HEADS * A_DV
B_HEADS = 32
B_DH = D_MODEL // B_HEADS
B_KV_HEADS = 4
B_GROUP = B_HEADS // B_KV_HEADS
B_KV = B_KV_HEADS * B_DH
WINDOW = 128
D_FF = ((8 * D_MODEL // 3 + 255) // 256) * 256
EPS = 1e-6

LANES = 128
SUBLANES = 8
VMEM_DEFAULT_MIB = 48
VMEM_FFN_MIB = 60

P_ROWS = BATCH * SEQ
S_ROWS = DEC_BATCH * DEC_SEQ
Q_ROWS = S_ROWS + A_CHUNK
META_BLOCK = S_ROWS // A_CHUNK

F32 = jnp.float32
BF16 = jnp.bfloat16
NEG_INF = float("-inf")


def _params(n_axes, vmem_mib=VMEM_DEFAULT_MIB):
    return pltpu.CompilerParams(dimension_semantics=("arbitrary",) * n_axes,
                                vmem_limit_bytes=vmem_mib * 1024 * 1024)


def _resident(shape):
    return pl.BlockSpec(shape, lambda i: (0, 0), pipeline_mode=pl.Buffered(1))


def _rms(x, g):
    return x * lax.rsqrt(jnp.mean(x * x, axis=-1, keepdims=True) + EPS) * g


def _dot(a, b):
    return jnp.dot(a, b, preferred_element_type=F32)


def _dot_nt(a, b):
    return lax.dot_general(a, b, (((1,), (1,)), ((), ())), preferred_element_type=F32)


def _dot_tn(a, b):
    return lax.dot_general(a, b, (((0,), (0,)), ((), ())), preferred_element_type=F32)


def _log_sigmoid(x):
    return -(jnp.maximum(-x, 0.0) + jnp.log1p(jnp.exp(-jnp.abs(x))))


def _head_norm64(y, gain):
    lo = lax.broadcasted_iota(jnp.int32, (1, LANES), 1) < B_DH
    cols = []
    for c in range(y.shape[1] // LANES):
        x = y[:, c * LANES:(c + 1) * LANES]
        xx = x * x
        s_lo = jnp.sum(jnp.where(lo, xx, 0.0), axis=-1, keepdims=True)
        s_hi = jnp.sum(jnp.where(lo, 0.0, xx), axis=-1, keepdims=True)
        scale = jnp.where(lo, lax.rsqrt(s_lo / B_DH + EPS), lax.rsqrt(s_hi / B_DH + EPS))
        cols.append(x * scale * gain[:, c * LANES:(c + 1) * LANES])
    return jnp.concatenate(cols, axis=1)


FFN_TF = 512
FFN_CAST_TF = 256


def _ffn_kernel(h_ref, g_ref, wg_ref, wu_ref, wo_ref, o_ref, *rest, emit_bf16):
    j = pl.program_id(1)
    xn_ref = rest[-1]

    def half_ffn(xn, wg, wu, wo):
        g = _dot(xn, wg)
        u = _dot(xn, wu)
        a = (g / (1.0 + jnp.exp(-g))) * (0.5 * u)
        return _dot(a.astype(BF16), wo)

    def normed(h):
        xn = _rms(h, g_ref[...]).astype(BF16)
        xn_ref[...] = xn
        return xn

    if emit_bf16:
        @pl.when(j == 0)
        def _():
            h = h_ref[...]
            normed(h)
            o_ref[...] = h

        wgb_ref, wub_ref, wob_ref = rest[:-1]
        wg = wg_ref[...].astype(BF16)
        wu = wu_ref[...].astype(BF16)
        wo = wo_ref[...].astype(BF16)
        wgb_ref[...] = wg
        wub_ref[...] = wu
        wob_ref[...] = wo
        o_ref[...] += half_ffn(xn_ref[...], wg, wu, wo)
    else:
        @pl.when(j == 0)
        def _():
            h = h_ref[...]
            o_ref[...] = h + half_ffn(normed(h), wg_ref[...], wu_ref[...], wo_ref[...])

        @pl.when(j > 0)
        def _():
            o_ref[...] += half_ffn(xn_ref[...], wg_ref[...], wu_ref[...], wo_ref[...])


def _ffn(h, gain, wg, wu, wo, tm):
    rows = h.shape[0]
    tf = FFN_TF
    n_ff = D_FF // tf
    return pl.pallas_call(
        functools.partial(_ffn_kernel, emit_bf16=False),
        grid=(rows // tm, n_ff),
        in_specs=[
            pl.BlockSpec((tm, D_MODEL), lambda i, j: (i, 0)),
            pl.BlockSpec((1, D_MODEL), lambda i, j: (0, 0)),
            pl.BlockSpec((D_MODEL, tf), lambda i, j: (0, j)),
            pl.BlockSpec((D_MODEL, tf), lambda i, j: (0, j)),
            pl.BlockSpec((tf, D_MODEL), lambda i, j: (j, 0)),
        ],
        out_specs=pl.BlockSpec((tm, D_MODEL), lambda i, j: (i, 0)),
        out_shape=jax.ShapeDtypeStruct((rows, D_MODEL), F32),
        scratch_shapes=[pltpu.VMEM((tm, D_MODEL), BF16)],
        compiler_params=_params(2, VMEM_FFN_MIB),
        name="ffn",
    )(h, gain, wg, wu, wo)


def _ffn_cast(h, gain, w_in, w_out, layer, which):
    rows = h.shape[0]
    tf = FFN_CAST_TF
    n_ff = D_FF // tf
    return pl.pallas_call(
        functools.partial(_ffn_kernel, emit_bf16=True),
        grid=(1, n_ff),
        in_specs=[
            pl.BlockSpec((rows, D_MODEL), lambda i, j: (0, 0)),
            pl.BlockSpec((1, D_MODEL), lambda i, j: (0, 0)),
            pl.BlockSpec((None, None, D_MODEL, tf), lambda i, j: (layer, which, 0, j)),
            pl.BlockSpec((None, None, D_MODEL, tf), lambda i, j: (layer, which, 0, j + n_ff)),
            pl.BlockSpec((None, None, tf, D_MODEL), lambda i, j: (layer, which, j, 0)),
        ],
        out_specs=[
            pl.BlockSpec((rows, D_MODEL), lambda i, j: (0, 0)),
            pl.BlockSpec((D_MODEL, tf), lambda i, j: (0, j)),
            pl.BlockSpec((D_MODEL, tf), lambda i, j: (0, j)),
            pl.BlockSpec((tf, D_MODEL), lambda i, j: (j, 0)),
        ],
        out_shape=[
            jax.ShapeDtypeStruct((rows, D_MODEL), F32),
            jax.ShapeDtypeStruct((D_MODEL, D_FF), BF16),
            jax.ShapeDtypeStruct((D_MODEL, D_FF), BF16),
            jax.ShapeDtypeStruct((D_FF, D_MODEL), BF16),
        ],
        scratch_shapes=[pltpu.VMEM((rows, D_MODEL), BF16)],
        compiler_params=_params(2),
        name="ffn_cast",
    )(h, gain, w_in, w_in, w_out)


GATE_ROWS = 2 * A_HEADS


def _inproj_kernel(h_ref, g_ref, w_ref, wgate_ref, bgate_ref, p_ref, gates_ref, *rest, emit_bf16):
    j = pl.program_id(1)
    xn_ref = rest[-1]

    @pl.when(j == 0)
    def _():
        xn = _rms(h_ref[...], g_ref[...]).astype(BF16)
        xn_ref[...] = xn
        wgate = jnp.concatenate([wgate_ref[...], jnp.zeros((LANES - GATE_ROWS, D_MODEL), F32)], axis=0)
        pre = _dot_nt(xn, wgate.astype(BF16)) + bgate_ref[...]
        capped = A_GATE_CAP * jnp.tanh(pre / A_GATE_CAP)
        lane = lax.broadcasted_iota(jnp.int32, (1, LANES), 1)
        gates_ref[...] = jnp.where(lane < A_HEADS, capped, _log_sigmoid(capped))

    if emit_bf16:
        w = w_ref[...].astype(BF16)
        rest[0][...] = w
    else:
        w = w_ref[...]
    p_ref[...] = _dot_nt(xn_ref[...], w)


def _inproj(h, gain, w_t, w_gate_t, bgate, tm, emit_bf16=False, tn=1024):
    rows = h.shape[0]
    assert not emit_bf16 or rows == tm
    out_specs = [
        pl.BlockSpec((tm, tn), lambda i, j: (i, j)),
        pl.BlockSpec((tm, LANES), lambda i, j: (i, 0)),
    ]
    out_shape = [jax.ShapeDtypeStruct((rows, A_QKVO), F32),
                 jax.ShapeDtypeStruct((rows, LANES), F32)]
    if emit_bf16:
        out_specs.append(pl.BlockSpec((tn, D_MODEL), lambda i, j: (j, 0)))
        out_shape.append(jax.ShapeDtypeStruct((A_QKVO, D_MODEL), BF16))
    return pl.pallas_call(
        functools.partial(_inproj_kernel, emit_bf16=emit_bf16),
        grid=(rows // tm, A_QKVO // tn),
        in_specs=[
            pl.BlockSpec((tm, D_MODEL), lambda i, j: (i, 0)),
            pl.BlockSpec((1, D_MODEL), lambda i, j: (0, 0)),
            pl.BlockSpec((tn, D_MODEL), lambda i, j: (j, 0)),
            pl.BlockSpec((GATE_ROWS, D_MODEL), lambda i, j: (A_QKVO // GATE_ROWS, 0)),
            pl.BlockSpec((1, LANES), lambda i, j: (0, 0)),
        ],
        out_specs=out_specs,
        out_shape=out_shape,
        scratch_shapes=[pltpu.VMEM((tm, D_MODEL), BF16)],
        compiler_params=_params(2, VMEM_FFN_MIB if tn > 1024 else VMEM_DEFAULT_MIB),
        name="mlstm_inproj",
    )(h, gain, w_t, w_gate_t, bgate)


def _mlstm_chunk_kernel(q_ref, k_ref, v_ref, g_ref, c0_ref, n0_ref, m0_ref,
                        h_ref, c_ref, n_ref, m_ref, qk_ref, qc_ref, sb_ref, *, n_valid, n_seq_step, shared_state):
    L = A_CHUNK
    heads = range(A_HEADS)

    @pl.when(pl.program_id(1) == 0)
    def _():
        for sq in range(n_seq_step):
            s0 = 0 if shared_state else sq
            c_ref[sq] = c0_ref[s0]
            n_ref[sq] = n0_ref[s0]
            m_ref[sq] = m0_ref[s0]

    row = lax.broadcasted_iota(jnp.int32, (L, L), 0)
    col = lax.broadcasted_iota(jnp.int32, (L, L), 1)
    causal = col <= row
    eye = col == row
    masked = n_valid < L
    if masked:
        row_ok = lax.broadcasted_iota(jnp.int32, (L, 1), 0) < n_valid

    def to_row(x_col):
        return jnp.sum(jnp.where(eye, x_col, 0.0), axis=0, keepdims=True)

    for sq in range(n_seq_step):
        for hd in heads:
            qb = q_ref[sq, :, hd * A_DK:(hd + 1) * A_DK].astype(BF16)
            kb = (k_ref[sq, :, hd * A_DK:(hd + 1) * A_DK] * (A_DK ** -0.5)).astype(BF16)
            qk_ref[hd] = _dot_nt(qb, kb)
            qc_ref[hd] = _dot(qb, c_ref[sq, hd].astype(BF16))

        gates = g_ref[sq]
        gates_lf = jnp.where(row_ok, gates, 0.0) if masked else gates
        csum = jnp.dot(causal.astype(F32), gates_lf, precision=lax.Precision.HIGHEST,
                       preferred_element_type=F32)

        w_intra, w_inter, floor, w_state, decay, m_new = [], [], [], [], [], []
        for hd in heads:
            b_col = csum[:, A_HEADS + hd:A_HEADS + hd + 1]
            ig_col = gates[:, hd:hd + 1]
            if masked:
                ig_col = jnp.where(row_ok, ig_col, NEG_INF)
            b_row = to_row(b_col)
            ig_row = to_row(ig_col)
            m_prev = m_ref[sq, hd][:, 0:1]
            d_log = jnp.where(causal, b_col - b_row + ig_row, NEG_INF)
            inter_log = b_col + m_prev
            m_t = jnp.maximum(inter_log, jnp.max(d_log, axis=-1, keepdims=True))
            w_intra.append(jnp.exp(d_log - m_t))
            w_inter.append(jnp.exp(inter_log - m_t))
            floor.append(jnp.exp(-m_t))
            b_last = b_col[L - 1:L, :]
            w_log = b_last - b_col + ig_col
            m_new.append(jnp.maximum(b_last + m_prev, jnp.max(w_log, axis=0, keepdims=True)))
            w_state.append(jnp.exp(w_log - m_new[hd]))
            decay.append(jnp.exp(b_last + m_prev - m_new[hd]))

        den = []
        for hd in heads:
            s = qk_ref[hd] * w_intra[hd]
            sb_ref[hd] = s.astype(BF16)
            q = q_ref[sq, :, hd * A_DK:(hd + 1) * A_DK]
            d = jnp.sum(s, axis=-1, keepdims=True) + w_inter[hd] * jnp.sum(q * n_ref[sq, hd], axis=-1, keepdims=True)
            den.append(jnp.maximum(jnp.abs(d), floor[hd]))

        for hd in heads:
            k = k_ref[sq, :, hd * A_DK:(hd + 1) * A_DK] * (A_DK ** -0.5)
            vb = v_ref[sq, :, hd * A_DV:(hd + 1) * A_DV].astype(BF16)
            num = _dot(sb_ref[hd], vb) + w_inter[hd] * qc_ref[hd]
            h_ref[sq, :, hd * A_DV:(hd + 1) * A_DV] = num / den[hd]
            kw = k * w_state[hd]
            c_ref[sq, hd] = decay[hd] * c_ref[sq, hd] + _dot_tn(kw.astype(BF16), vb)
            n_ref[sq, hd] = decay[hd] * n_ref[sq, hd] + jnp.sum(kw, axis=0, keepdims=True)
            m_ref[sq, hd] = jnp.broadcast_to(m_new[hd], (1, LANES))


def _mlstm_chunks(p, gates, c0, n0, m0, n_seq, n_chunks, row_block0, n_valid, shared_state, n_seq_step=1):
    L = A_CHUNK
    nb = n_seq_step
    if row_block0:
        assert n_seq == 1 and n_chunks == 1
    seq_rows = p.shape[0] // n_seq if not row_block0 else p.shape[0]
    p3 = p.reshape(n_seq, seq_rows, p.shape[1])
    g3 = gates.reshape(n_seq, seq_rows, gates.shape[1])
    st_in = (lambda b, c: (0, 0, 0, 0)) if shared_state else (lambda b, c: (b, 0, 0, 0))
    st_rows = 1 if shared_state else nb
    h3, c_out, n_out, m_out = pl.pallas_call(
        functools.partial(_mlstm_chunk_kernel, n_valid=n_valid, n_seq_step=nb, shared_state=shared_state),
        grid=(n_seq // nb, n_chunks),
        in_specs=[
            pl.BlockSpec((nb, L, A_HEADS * A_DK), lambda b, c: (b, row_block0 + c, 0)),
            pl.BlockSpec((nb, L, A_HEADS * A_DK), lambda b, c: (b, row_block0 + c, 1)),
            pl.BlockSpec((nb, L, A_HEADS * A_DV), lambda b, c: (b, row_block0 + c, 1)),
            pl.BlockSpec((nb, L, LANES), lambda b, c: (b, row_block0 + c, 0)),
            pl.BlockSpec((st_rows, A_HEADS, A_DK, A_DV), st_in),
            pl.BlockSpec((st_rows, A_HEADS, 1, A_DK), st_in),
            pl.BlockSpec((st_rows, A_HEADS, 1, LANES), st_in),
        ],
        out_specs=[
            pl.BlockSpec((nb, L, A_HEADS * A_DV), lambda b, c: (b, c, 0)),
            pl.BlockSpec((nb, A_HEADS, A_DK, A_DV), lambda b, c: (b, 0, 0, 0)),
            pl.BlockSpec((nb, A_HEADS, 1, A_DK), lambda b, c: (b, 0, 0, 0)),
            pl.BlockSpec((nb, A_HEADS, 1, LANES), lambda b, c: (b, 0, 0, 0)),
        ],
        out_shape=[
            jax.ShapeDtypeStruct((n_seq, n_chunks * L, D_MODEL), F32),
            jax.ShapeDtypeStruct((n_seq, A_HEADS, A_DK, A_DV), F32),
            jax.ShapeDtypeStruct((n_seq, A_HEADS, 1, A_DK), F32),
            jax.ShapeDtypeStruct((n_seq, A_HEADS, 1, LANES), F32),
        ],
        scratch_shapes=[pltpu.VMEM((A_HEADS, L, L), F32),
                        pltpu.VMEM((A_HEADS, L, A_DV), F32),
                        pltpu.VMEM((A_HEADS, L, L), BF16)],
        compiler_params=_params(2),
        name="mlstm_chunks",
    )(p3, p3, p3, g3, c0, n0, m0)
    return h3.reshape(n_seq * n_chunks * L, D_MODEL), c_out, n_out, m_out


def _mlstm_sample_kernel(q_ref, k_ref, v_ref, g_ref, c0_ref, n0_ref, m0_ref,
                         h_ref, c_ref, n_ref, m_ref):
    R = 2 * DEC_SEQ
    PAD = A_CHUNK - R
    gates = g_ref[...]
    r_col = lax.broadcasted_iota(jnp.int32, (R, 1), 0)
    is_a = r_col < DEC_SEQ
    row = lax.broadcasted_iota(jnp.int32, (R, LANES), 0)
    lane = lax.broadcasted_iota(jnp.int32, (R, LANES), 1)
    same = ((lane < DEC_SEQ) & (row < DEC_SEQ)) | ((lane >= DEC_SEQ) & (lane < R) & (row >= DEC_SEQ))
    causal = same & (lane <= row)
    eye = lane == row

    def to_row(x_col):
        return jnp.sum(jnp.where(eye, x_col, 0.0), axis=0, keepdims=True)

    for hd in range(A_HEADS):
        lf_col = gates[:, A_HEADS + hd:A_HEADS + hd + 1]
        ig_col = gates[:, hd:hd + 1]
        lf_row = to_row(lf_col)
        ig_row = to_row(ig_col)
        b_col = jnp.sum(jnp.where(causal, lf_row, 0.0), axis=1, keepdims=True)
        b_row = to_row(b_col)
        m_a = m0_ref[0, hd][:, 0:1]
        m_b = m0_ref[1, hd][:, 0:1]
        m_prev = jnp.where(is_a, m_a, m_b)
        c_a = c0_ref[0, hd]
        c_b = c0_ref[1, hd]
        n_a = n0_ref[0, hd]
        n_b = n0_ref[1, hd]

        q = q_ref[:, hd * A_DK:(hd + 1) * A_DK]
        k = k_ref[:, hd * A_DK:(hd + 1) * A_DK] * (A_DK ** -0.5)
        v = v_ref[:, hd * A_DV:(hd + 1) * A_DV]
        qb = q.astype(BF16)
        k_pad = jnp.concatenate([k, jnp.zeros((PAD, A_DK), F32)], axis=0).astype(BF16)
        v_pad = jnp.concatenate([v, jnp.zeros((PAD, A_DV), F32)], axis=0).astype(BF16)

        d_log = jnp.where(causal, b_col - b_row + ig_row, NEG_INF)
        inter_log = b_col + m_prev
        m_t = jnp.maximum(inter_log, jnp.max(d_log, axis=-1, keepdims=True))
        w_intra = jnp.exp(d_log - m_t)
        w_inter = jnp.exp(inter_log - m_t)
        s = _dot_nt(qb, k_pad) * w_intra
        q_c = jnp.where(is_a, _dot(qb, c_a.astype(BF16)), _dot(qb, c_b.astype(BF16)))
        num = _dot(s.astype(BF16), v_pad) + w_inter * q_c
        q_n = jnp.sum(q * jnp.where(is_a, n_a, n_b), axis=-1, keepdims=True)
        den = jnp.sum(s, axis=-1, keepdims=True) + w_inter * q_n
        den = jnp.maximum(jnp.abs(den), jnp.exp(-m_t))
        h_ref[:, hd * A_DV:(hd + 1) * A_DV] = num / den

        for idx, sel, m_x, c_x, n_x in ((0, is_a, m_a, c_a, n_a),
                                        (1, jnp.logical_not(is_a), m_b, c_b, n_b)):
            last = (idx + 1) * DEC_SEQ - 1
            b_last = b_col[last:last + 1, :]
            w_log = jnp.where(sel, b_last - b_col + ig_col, NEG_INF)
            m_new = jnp.maximum(b_last + m_x, jnp.max(w_log, axis=0, keepdims=True))
            w_state = jnp.exp(w_log - m_new)
            decay = jnp.exp(b_last + m_x - m_new)
            kw = k * w_state
            kw_pad = jnp.concatenate([kw, jnp.zeros((PAD, A_DK), F32)], axis=0).astype(BF16)
            c_ref[idx, hd] = decay * c_x + _dot_tn(kw_pad, v_pad)
            n_ref[idx, hd] = decay * n_x + jnp.sum(kw, axis=0, keepdims=True)
            m_ref[idx, hd] = jnp.broadcast_to(m_new, (1, LANES))


def _mlstm_sample(p, gates, c0, n0, m0):
    R = 2 * DEC_SEQ
    n_pairs = DEC_BATCH // 2
    st = lambda i: (i, 0, 0, 0)
    return pl.pallas_call(
        _mlstm_sample_kernel,
        grid=(n_pairs,),
        in_specs=[
            pl.BlockSpec((R, A_HEADS * A_DK), lambda i: (i, 0)),
            pl.BlockSpec((R, A_HEADS * A_DK), lambda i: (i, 1)),
            pl.BlockSpec((R, A_HEADS * A_DV), lambda i: (i, 1)),
            pl.BlockSpec((R, LANES), lambda i: (i, 0)),
            pl.BlockSpec((2, A_HEADS, A_DK, A_DV), st),
            pl.BlockSpec((2, A_HEADS, 1, A_DK), st),
            pl.BlockSpec((2, A_HEADS, 1, LANES), st),
        ],
        out_specs=[
            pl.BlockSpec((R, A_HEADS * A_DV), lambda i: (i, 0)),
            pl.BlockSpec((2, A_HEADS, A_DK, A_DV), st),
            pl.BlockSpec((2, A_HEADS, 1, A_DK), st),
            pl.BlockSpec((2, A_HEADS, 1, LANES), st),
        ],
        out_shape=[
            jax.ShapeDtypeStruct((S_ROWS, D_MODEL), F32),
            jax.ShapeDtypeStruct((DEC_BATCH, A_HEADS, A_DK, A_DV), F32),
            jax.ShapeDtypeStruct((DEC_BATCH, A_HEADS, 1, A_DK), F32),
            jax.ShapeDtypeStruct((DEC_BATCH, A_HEADS, 1, LANES), F32),
        ],
        compiler_params=_params(1),
        name="mlstm_sample",
    )(p, p, p, gates, c0, n0, m0)


def _mlstm_out_kernel(hm_ref, o_ref, hg_ref, w_ref, res_ref, out_ref):
    cols = []
    for hd in range(A_HEADS):
        x = hm_ref[:, hd * A_DV:(hd + 1) * A_DV]
        cols.append(x * lax.rsqrt(jnp.mean(x * x, axis=-1, keepdims=True) + EPS))
    hn = jnp.concatenate(cols, axis=1) * hg_ref[...]
    o = o_ref[...]
    pre = (hn * (1.0 / (1.0 + jnp.exp(-o)))).astype(BF16)
    out_ref[...] = res_ref[...] + _dot(pre, w_ref[...])


def _mlstm_out(hm, p, head_gain, w, res, tm):
    rows = hm.shape[0]
    return pl.pallas_call(
        _mlstm_out_kernel,
        grid=(rows // tm,),
        in_specs=[
            pl.BlockSpec((tm, D_MODEL), lambda i: (i, 0)),
            pl.BlockSpec((tm, D_MODEL), lambda i: (i, 2)),
            pl.BlockSpec((1, D_MODEL), lambda i: (0, 0)),
            _resident((D_MODEL, D_MODEL)),
            pl.BlockSpec((tm, D_MODEL), lambda i: (i, 0)),
        ],
        out_specs=pl.BlockSpec((tm, D_MODEL), lambda i: (i, 0)),
        out_shape=jax.ShapeDtypeStruct((rows, D_MODEL), F32),
        compiler_params=_params(1),
        name="mlstm_out",
    )(hm, p, head_gain, w, res)


def _normproj_kernel(h_ref, g_ref, w_ref, hg_ref, o_ref, *, n_norm):
    xn = _rms(h_ref[...], g_ref[...]).astype(BF16)
    y = _dot(xn, w_ref[...])
    if n_norm == y.shape[1]:
        o_ref[...] = _head_norm64(y, hg_ref[...]).astype(o_ref.dtype)
    else:
        o_ref[:, :n_norm] = _head_norm64(y[:, :n_norm], hg_ref[...])
        o_ref[:, n_norm:] = y[:, n_norm:]


def _normproj(h, gain, w, head_gain, tm, out_dtype=F32):
    rows = h.shape[0]
    n = w.shape[1]
    n_norm = head_gain.shape[1]
    return pl.pallas_call(
        functools.partial(_normproj_kernel, n_norm=n_norm),
        grid=(rows // tm,),
        in_specs=[
            pl.BlockSpec((tm, D_MODEL), lambda i: (i, 0)),
            pl.BlockSpec((1, D_MODEL), lambda i: (0, 0)),
            _resident((D_MODEL, n)),
            pl.BlockSpec((1, n_norm), lambda i: (0, 0)),
        ],
        out_specs=pl.BlockSpec((tm, n), lambda i: (i, 0)),
        out_shape=jax.ShapeDtypeStruct((rows, n), out_dtype),
        compiler_params=_params(1),
        name="normproj",
    )(h, gain, w, head_gain)


def _matres_kernel(x_ref, w_ref, res_ref, o_ref):
    o_ref[...] = res_ref[...] + _dot(x_ref[...].astype(BF16), w_ref[...])


def _matres(x, w, res, tm):
    rows = x.shape[0]
    return pl.pallas_call(
        _matres_kernel,
        grid=(rows // tm,),
        in_specs=[
            pl.BlockSpec((tm, D_MODEL), lambda i: (i, 0)),
            _resident((D_MODEL, D_MODEL)),
            pl.BlockSpec((tm, D_MODEL), lambda i: (i, 0)),
        ],
        out_specs=pl.BlockSpec((tm, D_MODEL), lambda i: (i, 0)),
        out_shape=jax.ShapeDtypeStruct((rows, D_MODEL), F32),
        compiler_params=_params(1),
        name="matres",
    )(x, w, res)


def _alibi_slope(head):
    return 2.0 ** (-8.0 * (head + 1) / B_HEADS)


PROMPT_SUB = WINDOW // 2
PROMPT_BAND = WINDOW + PROMPT_SUB
PROMPT_KEYS = 2 * LANES
HEAD_PAIRS = B_HEADS // 2
PAIRS_PER_GROUP = B_GROUP // 2


def _pair_rhs(x, c):
    col = x[:, c * LANES:(c + 1) * LANES]
    rol = pltpu.roll(col, B_DH, axis=1)
    lo = lax.broadcasted_iota(jnp.int32, (1, LANES), 1) < B_DH
    even = jnp.concatenate([jnp.where(lo, col, 0.0), jnp.where(lo, 0.0, rol)], axis=0)
    odd = jnp.concatenate([jnp.where(lo, rol, 0.0), jnp.where(lo, 0.0, col)], axis=0)
    return even.astype(BF16), odd.astype(BF16)


def _attn_prompt_kernel(q_ref, kvo_ref, kvp_ref, kvm_ref, sink_ref, o_ref, bias_ref, s_ref, p_ref):
    W = WINDOW
    SB = PROMPT_SUB
    NK = PROMPT_KEYS
    first = (pl.program_id(0) == 0) & (pl.program_id(1) == 0)
    j = pl.program_id(1)
    rows = PAIRS_PER_GROUP * SB
    lo = lax.broadcasted_iota(jnp.int32, (1, LANES), 1) < B_DH

    @pl.when(first)
    def _():
        c = lax.broadcasted_iota(jnp.int32, (SB, NK), 1)
        is_meta = c < N_META
        is_sink = c == NK - 1
        for sub in range(2):
            i = lax.broadcasted_iota(jnp.int32, (SB, NK), 0) + sub * SB
            pos = c - N_META + sub * SB
            rel = W + i - pos
            in_band = (c >= N_META) & (c < N_META + PROMPT_BAND) & (rel >= 0) & (rel < W)
            for variant in range(2):
                if variant == 0:
                    dist = jnp.where(is_meta, jnp.minimum(i + N_META - c, W), rel).astype(F32)
                    valid = is_meta | (in_band & (pos >= W))
                else:
                    dist = jnp.where(is_meta, W, rel).astype(F32)
                    valid = is_meta | in_band
                for hd in range(B_HEADS):
                    r0 = (hd // 2) * SB
                    c0 = (hd % 2) * NK
                    table = jnp.where(valid, -_alibi_slope(hd) * dist, NEG_INF)
                    bias_ref[variant, sub, r0:r0 + SB, c0:c0 + NK] = jnp.where(
                        is_sink, sink_ref[0:1, hd:hd + 1], table)

    variant = jnp.minimum(j, 1)
    kvm = kvm_ref[...]
    band = jnp.concatenate([kvp_ref[...], kvo_ref[...]], axis=0)
    pad = jnp.zeros((NK - N_META - PROMPT_BAND, 2 * B_KV), F32)
    ones_rhs = jnp.concatenate([jnp.broadcast_to(jnp.where(lo, 1.0, 0.0), (NK, LANES)),
                                jnp.broadcast_to(jnp.where(lo, 0.0, 1.0), (NK, LANES))], axis=0).astype(BF16)

    sub_rows = B_KV_HEADS * rows
    for sub in range(2):
        keys = jnp.concatenate([kvm, band[sub * SB:sub * SB + PROMPT_BAND], pad], axis=0)
        k_rhs = _pair_rhs(keys, 0) + _pair_rhs(keys, 1)
        v_rhs = _pair_rhs(keys, 2) + _pair_rhs(keys, 3)
        for g in range(B_KV_HEADS):
            p0 = g * PAIRS_PER_GROUP
            r0 = (sub * B_KV_HEADS + g) * rows
            qg = jnp.concatenate([q_ref[sub * SB:(sub + 1) * SB, (p0 + m) * LANES:(p0 + m + 1) * LANES]
                                  for m in range(PAIRS_PER_GROUP)], axis=0) * (B_DH ** -0.5)
            s_ref[r0:r0 + rows, :] = _dot_nt(qg.astype(BF16), k_rhs[g])

        s = s_ref[sub * sub_rows:(sub + 1) * sub_rows, :] + bias_ref[variant, sub]
        halves = []
        for half in range(2):
            sh = s[:, half * NK:(half + 1) * NK]
            halves.append(jnp.exp(sh - jnp.max(sh, axis=-1, keepdims=True)).astype(BF16))
        p_ref[sub * sub_rows:(sub + 1) * sub_rows, :] = jnp.concatenate(halves, axis=1)

        for g in range(B_KV_HEADS):
            p0 = g * PAIRS_PER_GROUP
            r0 = (sub * B_KV_HEADS + g) * rows
            od = _dot(p_ref[r0:r0 + rows, :], jnp.concatenate([v_rhs[g], ones_rhs], axis=1))
            o = (od[:, :LANES] / od[:, LANES:]).astype(o_ref.dtype)
            for m in range(PAIRS_PER_GROUP):
                o_ref[sub * SB:(sub + 1) * SB, (p0 + m) * LANES:(p0 + m + 1) * LANES] = o[m * SB:(m + 1) * SB, :]


def _attn_prompt(q, kv_p, kv_q, sinks_row):
    nb = SEQ // WINDOW
    blk = lambda b, j: b * nb + j
    return pl.pallas_call(
        _attn_prompt_kernel,
        grid=(BATCH, nb),
        in_specs=[
            pl.BlockSpec((WINDOW, D_MODEL), lambda b, j: (blk(b, j), 0)),
            pl.BlockSpec((WINDOW, 2 * B_KV), lambda b, j: (blk(b, j), 0)),
            pl.BlockSpec((WINDOW, 2 * B_KV), lambda b, j: (blk(b, jnp.maximum(j - 1, 0)), 0)),
            pl.BlockSpec((N_META, 2 * B_KV), lambda b, j: (S_ROWS // N_META, 0)),
            pl.BlockSpec((1, LANES), lambda b, j: (0, 0)),
        ],
        out_specs=pl.BlockSpec((WINDOW, D_MODEL), lambda b, j: (blk(b, j), 0)),
        out_shape=jax.ShapeDtypeStruct((P_ROWS, D_MODEL), BF16),
        scratch_shapes=[pltpu.VMEM((2, 2, HEAD_PAIRS * PROMPT_SUB, 2 * PROMPT_KEYS), F32),
                        pltpu.VMEM((2 * HEAD_PAIRS * PROMPT_SUB, 2 * PROMPT_KEYS), F32),
                        pltpu.VMEM((2 * HEAD_PAIRS * PROMPT_SUB, 2 * PROMPT_KEYS), BF16)],
        compiler_params=_params(2),
        name="attn_prompt",
    )(q, kv_p, kv_p, kv_q, sinks_row)


GROUP_SHIFT = B_GROUP.bit_length() - 1
assert 1 << GROUP_SHIFT == B_GROUP
SAMPLE_NEW = SUBLANES
SAMPLE_SMALL = LANES
SAMPLE_KEYS = WINDOW + SAMPLE_SMALL
SAMPLE_SEQ_PER_STEP = 8
SAMPLE_ROWS = DEC_SEQ * B_GROUP
KV_PAIRS = B_KV_HEADS // 2


def _split_rhs(col):
    lo = lax.broadcasted_iota(jnp.int32, (1, LANES), 1) < B_DH
    return jnp.concatenate([jnp.where(lo, col, 0.0), jnp.where(lo, 0.0, col)], axis=0)


def _split_cols(mat_t):
    top = lax.broadcasted_iota(jnp.int32, (LANES, 1), 0) < B_DH
    return jnp.concatenate([jnp.where(top, mat_t, 0.0), jnp.where(top, 0.0, mat_t)], axis=1)


def _attn_sample_kernel(q_ref, km_ref, vm_ref, kvn_ref, kt_ref, vt_ref, sink_ref,
                        o_ref, kt_out_ref, vt_out_ref, bias_ref, s_ref, p_ref):
    W = WINDOW
    NK = SAMPLE_KEYS
    R = SAMPLE_ROWS
    lo = lax.broadcasted_iota(jnp.int32, (1, LANES), 1) < B_DH
    lane = lax.broadcasted_iota(jnp.int32, (1, LANES), 1)

    @pl.when(pl.program_id(0) == 0)
    def _():
        row = lax.broadcasted_iota(jnp.int32, (R, 1), 0)
        r_in_group = jnp.bitwise_and(row, B_GROUP - 1)
        t = jnp.right_shift(lax.broadcasted_iota(jnp.int32, (R, NK), 0), GROUP_SHIFT)
        c = lax.broadcasted_iota(jnp.int32, (R, NK), 1)
        is_win = c < W
        is_meta = (c >= W) & (c < W + N_META)
        is_new = (c >= W + N_META) & (c < W + N_META + DEC_SEQ)
        t_new = c - (W + N_META)
        dist = jnp.where(is_win, W + t - c, jnp.where(is_meta, W, t - t_new)).astype(F32)
        valid = (is_win & (c > t)) | is_meta | (is_new & (t_new <= t))
        for pair in range(KV_PAIRS):
            for e in range(2):
                slope = jnp.zeros((R, 1), F32)
                sink = jnp.zeros((R, 1), F32)
                for r in range(B_GROUP):
                    hd = (2 * pair + e) * B_GROUP + r
                    slope = jnp.where(r_in_group == r, _alibi_slope(hd), slope)
                    sink = jnp.where(r_in_group == r, sink_ref[0:1, hd:hd + 1], sink)
                table = jnp.where(c == NK - 1, sink, jnp.where(valid, -slope * dist, NEG_INF))
                for sq in range(SAMPLE_SEQ_PER_STEP):
                    r0 = (sq * KV_PAIRS + pair) * R
                    bias_ref[r0:r0 + R, e * NK:(e + 1) * NK] = table

    sel_r = lax.broadcasted_iota(jnp.int32, (SAMPLE_SMALL, W), 0)
    sel_c = lax.broadcasted_iota(jnp.int32, (SAMPLE_SMALL, W), 1)
    place_new = ((sel_r >= N_META) & (sel_r < N_META + DEC_SEQ)
                 & (sel_c == sel_r - N_META + W - DEC_SEQ)).astype(F32)
    pad = jnp.zeros((SAMPLE_SMALL - N_META - SAMPLE_NEW, B_KV), F32)
    top = lax.broadcasted_iota(jnp.int32, (LANES, 1), 0) < B_DH
    ones_rhs = jnp.concatenate([jnp.broadcast_to(jnp.where(lo, 1.0, 0.0), (SAMPLE_SMALL, LANES)),
                                jnp.broadcast_to(jnp.where(lo, 0.0, 1.0), (SAMPLE_SMALL, LANES))], axis=0)
    ones_t = jnp.concatenate([jnp.broadcast_to(jnp.where(top, 1.0, 0.0), (LANES, W)),
                              jnp.broadcast_to(jnp.where(top, 0.0, 1.0), (LANES, W))], axis=1)

    v_small, v_win = [], []
    for sq in range(SAMPLE_SEQ_PER_STEP):
        kvn = kvn_ref[sq]
        k_small = jnp.concatenate([km_ref[sq], kvn[:, :B_KV], pad], axis=0)
        v_small_sq = jnp.concatenate([vm_ref[sq], kvn[:, B_KV:], pad], axis=0)
        kt = kt_ref[sq]
        vt = vt_ref[sq]
        for src, small, dst in ((kt, k_small, kt_out_ref), (vt, v_small_sq, vt_out_ref)):
            new_cols = lax.dot_general(small, place_new, (((0,), (0,)), ((), ())),
                                       precision=lax.Precision.HIGHEST, preferred_element_type=F32)
            dst[sq] = jnp.where(lane < W - DEC_SEQ, pltpu.roll(src, W - DEC_SEQ, axis=1), new_cols)
        for pair in range(KV_PAIRS):
            r0 = (sq * KV_PAIRS + pair) * R
            q = (q_ref[sq, pair] * (B_DH ** -0.5)).astype(BF16)
            s_win = _dot(q, _split_cols(kt[pair * LANES:(pair + 1) * LANES, :]).astype(BF16))
            s_small = _dot_nt(q, _split_rhs(k_small[:, pair * LANES:(pair + 1) * LANES]).astype(BF16))
            for e in range(2):
                s_ref[r0:r0 + R, e * NK:e * NK + W] = s_win[:, e * W:(e + 1) * W]
                s_ref[r0:r0 + R, e * NK + W:(e + 1) * NK] = s_small[:, e * SAMPLE_SMALL:(e + 1) * SAMPLE_SMALL]
            v_win.append(jnp.concatenate([_split_cols(vt[pair * LANES:(pair + 1) * LANES, :]), ones_t],
                                         axis=0).astype(BF16))
            v_small.append(jnp.concatenate([_split_rhs(v_small_sq[:, pair * LANES:(pair + 1) * LANES]),
                                            ones_rhs], axis=1).astype(BF16))

    s = s_ref[...] + bias_ref[...]
    halves = []
    for e in range(2):
        sh = s[:, e * NK:(e + 1) * NK]
        halves.append(jnp.exp(sh - jnp.max(sh, axis=-1, keepdims=True)).astype(BF16))
    p_ref[...] = jnp.concatenate(halves, axis=1)

    for sq in range(SAMPLE_SEQ_PER_STEP):
        for pair in range(KV_PAIRS):
            b = sq * KV_PAIRS + pair
            p = p_ref[b * R:(b + 1) * R, :]
            p_win = jnp.concatenate([p[:, 0:W], p[:, NK:NK + W]], axis=1)
            p_small = jnp.concatenate([p[:, W:NK], p[:, NK + W:2 * NK]], axis=1)
            od = _dot_nt(p_win, v_win[b]) + _dot(p_small, v_small[b])
            o_ref[sq, pair] = od[:, :LANES] / od[:, LANES:]


def _attn_sample(q4, k_meta, v_meta, kv_new, k_win_t, v_win_t, sinks_row):
    nb = SAMPLE_SEQ_PER_STEP
    R = SAMPLE_ROWS
    n_rows = nb * KV_PAIRS * R
    seq3 = lambda rows, cols: pl.BlockSpec((nb, rows, cols), lambda i: (i, 0, 0))
    qspec = pl.BlockSpec((nb, KV_PAIRS, R, LANES), lambda i: (i, 0, 0, 0))
    win_shape = jax.ShapeDtypeStruct((DEC_BATCH, B_KV, WINDOW), F32)
    return pl.pallas_call(
        _attn_sample_kernel,
        grid=(DEC_BATCH // nb,),
        in_specs=[qspec, seq3(N_META, B_KV), seq3(N_META, B_KV), seq3(SAMPLE_NEW, 2 * B_KV),
                  seq3(B_KV, WINDOW), seq3(B_KV, WINDOW), pl.BlockSpec((1, LANES), lambda i: (0, 0))],
        out_specs=[qspec, seq3(B_KV, WINDOW), seq3(B_KV, WINDOW)],
        out_shape=[jax.ShapeDtypeStruct((DEC_BATCH, KV_PAIRS, R, LANES), F32), win_shape, win_shape],
        scratch_shapes=[pltpu.VMEM((n_rows, 2 * SAMPLE_KEYS), F32),
                        pltpu.VMEM((n_rows, 2 * SAMPLE_KEYS), F32),
                        pltpu.VMEM((n_rows, 2 * SAMPLE_KEYS), BF16)],
        compiler_params=_params(1),
        name="attn_sample",
    )(q4, k_meta, v_meta, kv_new, k_win_t, v_win_t, sinks_row)


def kernel(x_prompt, x_sample, state_C, state_n, state_m, cache_k_meta, cache_v_meta, cache_k_win, cache_v_win, meta_tokens, ffn_norm, w_ffn_in, w_ffn_out, mix_norm, w_a_in, b_a_gate, a_head_norm, w_a_out, kv_norm, w_kv, k_norm, w_q, q_norm, sinks, w_b_out):
    assert x_prompt.shape == (BATCH, SEQ, D_MODEL) and x_sample.shape == (DEC_BATCH, DEC_SEQ, D_MODEL)
    assert w_a_in.shape[0] == 1 and w_q.shape[0] == 1 and ffn_norm.shape[0] == 2

    wa_in_t = jnp.swapaxes(w_a_in[0], 0, 1)
    ba_gate = jnp.pad(b_a_gate[0].astype(F32), (0, LANES - 2 * A_HEADS)).reshape(1, LANES)
    wa_out = w_a_out[0].astype(BF16)
    wkv = w_kv.astype(BF16)
    wq = w_q[0].astype(BF16)
    wb_out = w_b_out[0].astype(BF16)
    row = lambda x: x.astype(F32).reshape(1, -1)
    k_gain = jnp.tile(row(k_norm), (1, B_KV_HEADS))
    q_gain = jnp.tile(row(q_norm[0]), (1, B_HEADS))
    sinks_row = jnp.pad(sinks[0].astype(F32), (0, LANES - B_HEADS)).reshape(1, LANES)

    h_p = x_prompt.reshape(P_ROWS, D_MODEL)
    h_q = jnp.concatenate([x_sample.reshape(S_ROWS, D_MODEL), meta_tokens.astype(F32),
                           jnp.zeros((A_CHUNK - N_META, D_MODEL), F32)], axis=0)
    TM_P, TM_W = 1024, 512

    h_q, *wf = _ffn_cast(h_q, row(ffn_norm[0, 0]), w_ffn_in, w_ffn_out, 0, 0)
    h_p = _ffn(h_p, row(ffn_norm[0, 0]), *wf, TM_P)
    p_q, g_q, wa_in = _inproj(h_q, row(mix_norm[0]), wa_in_t, wa_in_t, ba_gate, Q_ROWS, emit_bf16=True)
    p_p, g_p = _inproj(h_p, row(mix_norm[0]), wa_in, wa_in_t, ba_gate, TM_P, tn=D_MODEL)

    zc = jnp.zeros((1, A_HEADS, A_DK, A_DV), F32)
    zn = jnp.zeros((1, A_HEADS, 1, A_DK), F32)
    zm = jnp.zeros((1, A_HEADS, 1, LANES), F32)
    hm_m, c_m, n_m, m_m = _mlstm_chunks(p_q, g_q, zc, zn, zm, 1, 1, META_BLOCK, N_META, True)
    hm_p, c_p, n_p, m_p = _mlstm_chunks(p_p, g_p, c_m, n_m, m_m, BATCH, SEQ // A_CHUNK, 0, A_CHUNK, True,
                                         n_seq_step=2)
    m0_s = jnp.broadcast_to(state_m[0].astype(F32)[:, :, None, None], (DEC_BATCH, A_HEADS, 1, LANES))
    hm_s, c_s, n_s, m_s = _mlstm_sample(p_q, g_q, state_C[0].astype(F32),
                                        state_n[0].astype(F32)[:, :, None, :], m0_s)
    hm_q = jnp.concatenate([hm_s, hm_m], axis=0)

    h_p = _mlstm_out(hm_p, p_p, row(a_head_norm[0]), wa_out, h_p, TM_W)
    h_q = _mlstm_out(hm_q, p_q, row(a_head_norm[0]), wa_out, h_q, A_CHUNK)
    h_q, *wf = _ffn_cast(h_q, row(ffn_norm[0, 1]), w_ffn_in, w_ffn_out, 0, 1)
    h_p = _ffn(h_p, row(ffn_norm[0, 1]), *wf, TM_P)

    kv_p = _normproj(h_p, row(kv_norm), wkv, k_gain, TM_P)
    kv_q = _normproj(h_q, row(kv_norm), wkv, k_gain, Q_ROWS)
    h_s = h_q[:S_ROWS]
    h_s, *wf = _ffn_cast(h_s, row(ffn_norm[1, 0]), w_ffn_in, w_ffn_out, 1, 0)
    h_p = _ffn(h_p, row(ffn_norm[1, 0]), *wf, TM_P)
    q_p = _normproj(h_p, row(mix_norm[1]), wq, q_gain, TM_W, BF16)
    q_s = _normproj(h_s, row(mix_norm[1]), wq, q_gain, TM_W, BF16)

    o_p = _attn_prompt(q_p, kv_p, kv_q, sinks_row)
    q4 = q_s.reshape(DEC_BATCH, DEC_SEQ, KV_PAIRS, 2, B_GROUP, B_DH).transpose(0, 2, 1, 4, 3, 5)
    q4 = q4.reshape(DEC_BATCH, KV_PAIRS, SAMPLE_ROWS, LANES)
    seq3 = lambda x: x.astype(F32).reshape(DEC_BATCH, -1, B_KV)
    win_t = lambda x: x.astype(F32).transpose(0, 2, 3, 1).reshape(DEC_BATCH, B_KV, WINDOW)
    kv_new = jnp.pad(kv_q[:S_ROWS].reshape(DEC_BATCH, DEC_SEQ, 2 * B_KV), ((0, 0), (0, SAMPLE_NEW - DEC_SEQ), (0, 0)))
    o4, k_win_t, v_win_t = _attn_sample(q4, seq3(cache_k_meta), seq3(cache_v_meta), kv_new,
                                        win_t(cache_k_win), win_t(cache_v_win), sinks_row)
    from_t = lambda x, like: x.reshape(DEC_BATCH, B_KV_HEADS, B_DH, WINDOW).transpose(0, 3, 1, 2).astype(like.dtype)
    k_win_s = from_t(k_win_t, cache_k_win)
    v_win_s = from_t(v_win_t, cache_v_win)
    o_s = o4.reshape(DEC_BATCH, KV_PAIRS, DEC_SEQ, B_GROUP, 2, B_DH).transpose(0, 2, 1, 4, 3, 5)
    o_s = o_s.reshape(S_ROWS, D_MODEL)

    h_p = _matres(o_p, wb_out, h_p, TM_W)
    h_s = _matres(o_s, wb_out, h_s, TM_W)
    h_s, *wf = _ffn_cast(h_s, row(ffn_norm[1, 1]), w_ffn_in, w_ffn_out, 1, 1)
    h_p = _ffn(h_p, row(ffn_norm[1, 1]), *wf, TM_P)

    kv4 = lambda x: x.reshape(x.shape[:-1] + (B_KV_HEADS, B_DH))
    meta_rows = kv_q[S_ROWS:S_ROWS + N_META]
    kv_p3 = kv_p.reshape(BATCH, SEQ, 2 * B_KV)
    st = lambda x, dt: x[None].astype(dt)
    return (
        h_p.reshape(BATCH, SEQ, D_MODEL),
        h_s.reshape(DEC_BATCH, DEC_SEQ, D_MODEL),
        st(c_p, state_C.dtype), st(n_p[:, :, 0, :], state_n.dtype), st(m_p[:, :, 0, 0], state_m.dtype),
        jnp.broadcast_to(kv4(meta_rows[:, :B_KV])[None], (BATCH, N_META, B_KV_HEADS, B_DH)),
        jnp.broadcast_to(kv4(meta_rows[:, B_KV:])[None], (BATCH, N_META, B_KV_HEADS, B_DH)),
        kv4(kv_p3[:, -WINDOW:, :B_KV]), kv4(kv_p3[:, -WINDOW:, B_KV:]),
        st(c_s, state_C.dtype), st(n_s[:, :, 0, :], state_n.dtype), st(m_s[:, :, 0, 0], state_m.dtype),
        k_win_s, v_win_s,
    )
```

```python
import functools

import jax
import jax.numpy as jnp
from jax import lax
from jax.experimental import pallas as pl
from jax.experimental.pallas import tpu as pltpu

D_MODEL = 2048
BATCH = 8
SEQ = 2048
DEC_BATCH = 128
DEC_SEQ = 4
PAST_LEN = 8192
N_META = 16
A_HEADS = 4
A_DV = D_MODEL // A_HEADS
A_DK = A_DV // 2
A_CHUNK = 128
A_GATE_CAP = 15.0
A_QKVO = 2 * A_HEADS * A_DK + 2 * A_HEADS * A_DV
B_HEADS = 32
B_DH = D_MODEL // B_HEADS
B_KV_HEADS = 4
B_GROUP = B_HEADS // B_KV_HEADS
B_KV = B_KV_HEADS * B_DH
WINDOW = 128
D_FF = ((8 * D_MODEL // 3 + 255) // 256) * 256
EPS = 1e-6

LANES = 128
SUBLANES = 8
VMEM_DEFAULT_MIB = 48
VMEM_FFN_MIB = 60

P_ROWS = BATCH * SEQ
S_ROWS = DEC_BATCH * DEC_SEQ
Q_ROWS = S_ROWS + A_CHUNK
META_BLOCK = S_ROWS // A_CHUNK

F32 = jnp.float32
BF16 = jnp.bfloat16
NEG_INF = float("-inf")


def _params(n_axes, vmem_mib=VMEM_DEFAULT_MIB):
    return pltpu.CompilerParams(dimension_semantics=("arbitrary",) * n_axes,
                                vmem_limit_bytes=vmem_mib * 1024 * 1024)


def _resident(shape):
    return pl.BlockSpec(shape, lambda i: (0, 0), pipeline_mode=pl.Buffered(1))


def _rms(x, g):
    return x * lax.rsqrt(jnp.mean(x * x, axis=-1, keepdims=True) + EPS) * g


def _dot(a, b):
    return jnp.dot(a, b, preferred_element_type=F32)


def _dot_nt(a, b):
    return lax.dot_general(a, b, (((1,), (1,)), ((), ())), preferred_element_type=F32)


def _dot_tn(a, b):
    return lax.dot_general(a, b, (((0,), (0,)), ((), ())), preferred_element_type=F32)


def _log_sigmoid(x):
    return -(jnp.maximum(-x, 0.0) + jnp.log1p(jnp.exp(-jnp.abs(x))))


def _head_norm64(y, gain):
    lo = lax.broadcasted_iota(jnp.int32, (1, LANES), 1) < B_DH
    cols = []
    for c in range(y.shape[1] // LANES):
        x = y[:, c * LANES:(c + 1) * LANES]
        xx = x * x
        s_lo = jnp.sum(jnp.where(lo, xx, 0.0), axis=-1, keepdims=True)
        s_hi = jnp.sum(jnp.where(lo, 0.0, xx), axis=-1, keepdims=True)
        scale = jnp.where(lo, lax.rsqrt(s_lo / B_DH + EPS), lax.rsqrt(s_hi / B_DH + EPS))
        cols.append(x * scale * gain[:, c * LANES:(c + 1) * LANES])
    return jnp.concatenate(cols, axis=1)


FFN_TF = 512
FFN_CAST_TF = 256


def _ffn_kernel(h_ref, g_ref, wg_ref, wu_ref, wo_ref, o_ref, *rest, emit_bf16):
    j = pl.program_id(1)
    xn_ref = rest[-1]

    def half_ffn(xn, wg, wu, wo):
        g = _dot(xn, wg)
        u = _dot(xn, wu)
        a = (g / (1.0 + jnp.exp(-g))) * (0.5 * u)
        return _dot(a.astype(BF16), wo)

    def normed(h):
        xn = _rms(h, g_ref[...]).astype(BF16)
        xn_ref[...] = xn
        return xn

    if emit_bf16:
        @pl.when(j == 0)
        def _():
            h = h_ref[...]
            normed(h)
            o_ref[...] = h

        wgb_ref, wub_ref, wob_ref = rest[:-1]
        wg = wg_ref[...].astype(BF16)
        wu = wu_ref[...].astype(BF16)
        wo = wo_ref[...].astype(BF16)
        wgb_ref[...] = wg
        wub_ref[...] = wu
        wob_ref[...] = wo
        o_ref[...] += half_ffn(xn_ref[...], wg, wu, wo)
    else:
        @pl.when(j == 0)
        def _():
            h = h_ref[...]
            o_ref[...] = h + half_ffn(normed(h), wg_ref[...], wu_ref[...], wo_ref[...])

        @pl.when(j > 0)
        def _():
            o_ref[...] += half_ffn(xn_ref[...], wg_ref[...], wu_ref[...], wo_ref[...])


def _ffn(h, gain, wg, wu, wo, tm):
    rows = h.shape[0]
    tf = FFN_TF
    n_ff = D_FF // tf
    return pl.pallas_call(
        functools.partial(_ffn_kernel, emit_bf16=False),
        grid=(rows // tm, n_ff),
        in_specs=[
            pl.BlockSpec((tm, D_MODEL), lambda i, j: (i, 0)),
            pl.BlockSpec((1, D_MODEL), lambda i, j: (0, 0)),
            pl.BlockSpec((D_MODEL, tf), lambda i, j: (0, j)),
            pl.BlockSpec((D_MODEL, tf), lambda i, j: (0, j)),
            pl.BlockSpec((tf, D_MODEL), lambda i, j: (j, 0)),
        ],
        out_specs=pl.BlockSpec((tm, D_MODEL), lambda i, j: (i, 0)),
        out_shape=jax.ShapeDtypeStruct((rows, D_MODEL), F32),
        scratch_shapes=[pltpu.VMEM((tm, D_MODEL), BF16)],
        compiler_params=_params(2, VMEM_FFN_MIB),
        name="ffn",
    )(h, gain, wg, wu, wo)


def _ffn_cast(h, gain, w_in, w_out, layer, which):
    rows = h.shape[0]
    tf = FFN_CAST_TF
    n_ff = D_FF // tf
    return pl.pallas_call(
        functools.partial(_ffn_kernel, emit_bf16=True),
        grid=(1, n_ff),
        in_specs=[
            pl.BlockSpec((rows, D_MODEL), lambda i, j: (0, 0)),
            pl.BlockSpec((1, D_MODEL), lambda i, j: (0, 0)),
            pl.BlockSpec((None, None, D_MODEL, tf), lambda i, j: (layer, which, 0, j)),
            pl.BlockSpec((None, None, D_MODEL, tf), lambda i, j: (layer, which, 0, j + n_ff)),
            pl.BlockSpec((None, None, tf, D_MODEL), lambda i, j: (layer, which, j, 0)),
        ],
        out_specs=[
            pl.BlockSpec((rows, D_MODEL), lambda i, j: (0, 0)),
            pl.BlockSpec((D_MODEL, tf), lambda i, j: (0, j)),
            pl.BlockSpec((D_MODEL, tf), lambda i, j: (0, j)),
            pl.BlockSpec((tf, D_MODEL), lambda i, j: (j, 0)),
        ],
        out_shape=[
            jax.ShapeDtypeStruct((rows, D_MODEL), F32),
            jax.ShapeDtypeStruct((D_MODEL, D_FF), BF16),
            jax.ShapeDtypeStruct((D_MODEL, D_FF), BF16),
            jax.ShapeDtypeStruct((D_FF, D_MODEL), BF16),
        ],
        scratch_shapes=[pltpu.VMEM((rows, D_MODEL), BF16)],
        compiler_params=_params(2),
        name="ffn_cast",
    )(h, gain, w_in, w_in, w_out)


GATE_ROWS = 2 * A_HEADS


def _inproj_kernel(h_ref, g_ref, w_ref, wgate_ref, bgate_ref, p_ref, gates_ref, *rest, emit_bf16):
    j = pl.program_id(1)
    xn_ref = rest[-1]

    @pl.when(j == 0)
    def _():
        xn = _rms(h_ref[...], g_ref[...]).astype(BF16)
        xn_ref[...] = xn
        wgate = jnp.concatenate([wgate_ref[...], jnp.zeros((LANES - GATE_ROWS, D_MODEL), F32)], axis=0)
        pre = _dot_nt(xn, wgate.astype(BF16)) + bgate_ref[...]
        capped = A_GATE_CAP * jnp.tanh(pre / A_GATE_CAP)
        lane = lax.broadcasted_iota(jnp.int32, (1, LANES), 1)
        gates_ref[...] = jnp.where(lane < A_HEADS, capped, _log_sigmoid(capped))

    if emit_bf16:
        w = w_ref[...].astype(BF16)
        rest[0][...] = w
    else:
        w = w_ref[...]
    p_ref[...] = _dot_nt(xn_ref[...], w)


def _inproj(h, gain, w_t, w_gate_t, bgate, tm, emit_bf16=False, tn=1024):
    rows = h.shape[0]
    assert not emit_bf16 or rows == tm
    out_specs = [
        pl.BlockSpec((tm, tn), lambda i, j: (i, j)),
        pl.BlockSpec((tm, LANES), lambda i, j: (i, 0)),
    ]
    out_shape = [jax.ShapeDtypeStruct((rows, A_QKVO), F32),
                 jax.ShapeDtypeStruct((rows, LANES), F32)]
    if emit_bf16:
        out_specs.append(pl.BlockSpec((tn, D_MODEL), lambda i, j: (j, 0)))
        out_shape.append(jax.ShapeDtypeStruct((A_QKVO, D_MODEL), BF16))
    return pl.pallas_call(
        functools.partial(_inproj_kernel, emit_bf16=emit_bf16),
        grid=(rows // tm, A_QKVO // tn),
        in_specs=[
            pl.BlockSpec((tm, D_MODEL), lambda i, j: (i, 0)),
            pl.BlockSpec((1, D_MODEL), lambda i, j: (0, 0)),
            pl.BlockSpec((tn, D_MODEL), lambda i, j: (j, 0)),
            pl.BlockSpec((GATE_ROWS, D_MODEL), lambda i, j: (A_QKVO // GATE_ROWS, 0)),
            pl.BlockSpec((1, LANES), lambda i, j: (0, 0)),
        ],
        out_specs=out_specs,
        out_shape=out_shape,
        scratch_shapes=[pltpu.VMEM((tm, D_MODEL), BF16)],
        compiler_params=_params(2, VMEM_FFN_MIB if tn > 1024 else VMEM_DEFAULT_MIB),
        name="mlstm_inproj",
    )(h, gain, w_t, w_gate_t, bgate)


def _mlstm_chunk_kernel(q_ref, k_ref, v_ref, g_ref, c0_ref, n0_ref, m0_ref,
                        h_ref, c_ref, n_ref, m_ref, qk_ref, qc_ref, sb_ref, *, n_valid, n_seq_step, shared_state):
    L = A_CHUNK
    heads = range(A_HEADS)

    @pl.when(pl.program_id(1) == 0)
    def _():
        for sq in range(n_seq_step):
            s0 = 0 if shared_state else sq
            c_ref[sq] = c0_ref[s0]
            n_ref[sq] = n0_ref[s0]
            m_ref[sq] = m0_ref[s0]

    row = lax.broadcasted_iota(jnp.int32, (L, L), 0)
    col = lax.broadcasted_iota(jnp.int32, (L, L), 1)
    causal = col <= row
    eye = col == row
    masked = n_valid < L
    if masked:
        row_ok = lax.broadcasted_iota(jnp.int32, (L, 1), 0) < n_valid

    def to_row(x_col):
        return jnp.sum(jnp.where(eye, x_col, 0.0), axis=0, keepdims=True)

    for sq in range(n_seq_step):
        for hd in heads:
            qb = q_ref[sq, :, hd * A_DK:(hd + 1) * A_DK].astype(BF16)
            kb = (k_ref[sq, :, hd * A_DK:(hd + 1) * A_DK] * (A_DK ** -0.5)).astype(BF16)
            qk_ref[hd] = _dot_nt(qb, kb)
            qc_ref[hd] = _dot(qb, c_ref[sq, hd].astype(BF16))

        gates = g_ref[sq]
        gates_lf = jnp.where(row_ok, gates, 0.0) if masked else gates
        csum = jnp.dot(causal.astype(F32), gates_lf, precision=lax.Precision.HIGHEST,
                       preferred_element_type=F32)

        w_intra, w_inter, floor, w_state, decay, m_new = [], [], [], [], [], []
        for hd in heads:
            b_col = csum[:, A_HEADS + hd:A_HEADS + hd + 1]
            ig_col = gates[:, hd:hd + 1]
            if masked:
                ig_col = jnp.where(row_ok, ig_col, NEG_INF)
            b_row = to_row(b_col)
            ig_row = to_row(ig_col)
            m_prev = m_ref[sq, hd][:, 0:1]
            d_log = jnp.where(causal, b_col - b_row + ig_row, NEG_INF)
            inter_log = b_col + m_prev
            m_t = jnp.maximum(inter_log, jnp.max(d_log, axis=-1, keepdims=True))
            w_intra.append(jnp.exp(d_log - m_t))
            w_inter.append(jnp.exp(inter_log - m_t))
            floor.append(jnp.exp(-m_t))
            b_last = b_col[L - 1:L, :]
            w_log = b_last - b_col + ig_col
            m_new.append(jnp.maximum(b_last + m_prev, jnp.max(w_log, axis=0, keepdims=True)))
            w_state.append(jnp.exp(w_log - m_new[hd]))
            decay.append(jnp.exp(b_last + m_prev - m_new[hd]))

        den = []
        for hd in heads:
            s = qk_ref[hd] * w_intra[hd]
            sb_ref[hd] = s.astype(BF16)
            q = q_ref[sq, :, hd * A_DK:(hd + 1) * A_DK]
            d = jnp.sum(s, axis=-1, keepdims=True) + w_inter[hd] * jnp.sum(q * n_ref[sq, hd], axis=-1, keepdims=True)
            den.append(jnp.maximum(jnp.abs(d), floor[hd]))

        for hd in heads:
            k = k_ref[sq, :, hd * A_DK:(hd + 1) * A_DK] * (A_DK ** -0.5)
            vb = v_ref[sq, :, hd * A_DV:(hd + 1) * A_DV].astype(BF16)
            num = _dot(sb_ref[hd], vb) + w_inter[hd] * qc_ref[hd]
            h_ref[sq, :, hd * A_DV:(hd + 1) * A_DV] = num / den[hd]
            kw = k * w_state[hd]
            c_ref[sq, hd] = decay[hd] * c_ref[sq, hd] + _dot_tn(kw.astype(BF16), vb)
            n_ref[sq, hd] = decay[hd] * n_ref[sq, hd] + jnp.sum(kw, axis=0, keepdims=True)
            m_ref[sq, hd] = jnp.broadcast_to(m_new[hd], (1, LANES))


def _mlstm_chunks(p, gates, c0, n0, m0, n_seq, n_chunks, row_block0, n_valid, shared_state, n_seq_step=1):
    L = A_CHUNK
    nb = n_seq_step
    if row_block0:
        assert n_seq == 1 and n_chunks == 1
    seq_rows = p.shape[0] // n_seq if not row_block0 else p.shape[0]
    p3 = p.reshape(n_seq, seq_rows, p.shape[1])
    g3 = gates.reshape(n_seq, seq_rows, gates.shape[1])
    st_in = (lambda b, c: (0, 0, 0, 0)) if shared_state else (lambda b, c: (b, 0, 0, 0))
    st_rows = 1 if shared_state else nb
    h3, c_out, n_out, m_out = pl.pallas_call(
        functools.partial(_mlstm_chunk_kernel, n_valid=n_valid, n_seq_step=nb, shared_state=shared_state),
        grid=(n_seq // nb, n_chunks),
        in_specs=[
            pl.BlockSpec((nb, L, A_HEADS * A_DK), lambda b, c: (b, row_block0 + c, 0)),
            pl.BlockSpec((nb, L, A_HEADS * A_DK), lambda b, c: (b, row_block0 + c, 1)),
            pl.BlockSpec((nb, L, A_HEADS * A_DV), lambda b, c: (b, row_block0 + c, 1)),
            pl.BlockSpec((nb, L, LANES), lambda b, c: (b, row_block0 + c, 0)),
            pl.BlockSpec((st_rows, A_HEADS, A_DK, A_DV), st_in),
            pl.BlockSpec((st_rows, A_HEADS, 1, A_DK), st_in),
            pl.BlockSpec((st_rows, A_HEADS, 1, LANES), st_in),
        ],
        out_specs=[
            pl.BlockSpec((nb, L, A_HEADS * A_DV), lambda b, c: (b, c, 0)),
            pl.BlockSpec((nb, A_HEADS, A_DK, A_DV), lambda b, c: (b, 0, 0, 0)),
            pl.BlockSpec((nb, A_HEADS, 1, A_DK), lambda b, c: (b, 0, 0, 0)),
            pl.BlockSpec((nb, A_HEADS, 1, LANES), lambda b, c: (b, 0, 0, 0)),
        ],
        out_shape=[
            jax.ShapeDtypeStruct((n_seq, n_chunks * L, D_MODEL), F32),
            jax.ShapeDtypeStruct((n_seq, A_HEADS, A_DK, A_DV), F32),
            jax.ShapeDtypeStruct((n_seq, A_HEADS, 1, A_DK), F32),
            jax.ShapeDtypeStruct((n_seq, A_HEADS, 1, LANES), F32),
        ],
        scratch_shapes=[pltpu.VMEM((A_HEADS, L, L), F32),
                        pltpu.VMEM((A_HEADS, L, A_DV), F32),
                        pltpu.VMEM((A_HEADS, L, L), BF16)],
        compiler_params=_params(2, VMEM_FFN_MIB if nb > 2 else VMEM_DEFAULT_MIB),
        name="mlstm_chunks",
    )(p3, p3, p3, g3, c0, n0, m0)
    return h3.reshape(n_seq * n_chunks * L, D_MODEL), c_out, n_out, m_out


def _mlstm_sample_kernel(q_ref, k_ref, v_ref, g_ref, c0_ref, n0_ref, m0_ref,
                         h_ref, c_ref, n_ref, m_ref):
    R = 2 * DEC_SEQ
    PAD = A_CHUNK - R
    gates = g_ref[...]
    r_col = lax.broadcasted_iota(jnp.int32, (R, 1), 0)
    is_a = r_col < DEC_SEQ
    row = lax.broadcasted_iota(jnp.int32, (R, LANES), 0)
    lane = lax.broadcasted_iota(jnp.int32, (R, LANES), 1)
    same = ((lane < DEC_SEQ) & (row < DEC_SEQ)) | ((lane >= DEC_SEQ) & (lane < R) & (row >= DEC_SEQ))
    causal = same & (lane <= row)
    eye = lane == row

    def to_row(x_col):
        return jnp.sum(jnp.where(eye, x_col, 0.0), axis=0, keepdims=True)

    for hd in range(A_HEADS):
        lf_col = gates[:, A_HEADS + hd:A_HEADS + hd + 1]
        ig_col = gates[:, hd:hd + 1]
        lf_row = to_row(lf_col)
        ig_row = to_row(ig_col)
        b_col = jnp.sum(jnp.where(causal, lf_row, 0.0), axis=1, keepdims=True)
        b_row = to_row(b_col)
        m_a = m0_ref[0, hd][:, 0:1]
        m_b = m0_ref[1, hd][:, 0:1]
        m_prev = jnp.where(is_a, m_a, m_b)
        c_a = c0_ref[0, hd]
        c_b = c0_ref[1, hd]
        n_a = n0_ref[0, hd]
        n_b = n0_ref[1, hd]

        q = q_ref[:, hd * A_DK:(hd + 1) * A_DK]
        k = k_ref[:, hd * A_DK:(hd + 1) * A_DK] * (A_DK ** -0.5)
        v = v_ref[:, hd * A_DV:(hd + 1) * A_DV]
        qb = q.astype(BF16)
        k_pad = jnp.concatenate([k, jnp.zeros((PAD, A_DK), F32)], axis=0).astype(BF16)
        v_pad = jnp.concatenate([v, jnp.zeros((PAD, A_DV), F32)], axis=0).astype(BF16)

        d_log = jnp.where(causal, b_col - b_row + ig_row, NEG_INF)
        inter_log = b_col + m_prev
        m_t = jnp.maximum(inter_log, jnp.max(d_log, axis=-1, keepdims=True))
        w_intra = jnp.exp(d_log - m_t)
        w_inter = jnp.exp(inter_log - m_t)
        s = _dot_nt(qb, k_pad) * w_intra
        q_c = jnp.where(is_a, _dot(qb, c_a.astype(BF16)), _dot(qb, c_b.astype(BF16)))
        num = _dot(s.astype(BF16), v_pad) + w_inter * q_c
        q_n = jnp.sum(q * jnp.where(is_a, n_a, n_b), axis=-1, keepdims=True)
        den = jnp.sum(s, axis=-1, keepdims=True) + w_inter * q_n
        den = jnp.maximum(jnp.abs(den), jnp.exp(-m_t))
        h_ref[:, hd * A_DV:(hd + 1) * A_DV] = num / den

        for idx, sel, m_x, c_x, n_x in ((0, is_a, m_a, c_a, n_a),
                                        (1, jnp.logical_not(is_a), m_b, c_b, n_b)):
            last = (idx + 1) * DEC_SEQ - 1
            b_last = b_col[last:last + 1, :]
            w_log = jnp.where(sel, b_last - b_col + ig_col, NEG_INF)
            m_new = jnp.maximum(b_last + m_x, jnp.max(w_log, axis=0, keepdims=True))
            w_state = jnp.exp(w_log - m_new)
            decay = jnp.exp(b_last + m_x - m_new)
            kw = k * w_state
            kw_pad = jnp.concatenate([kw, jnp.zeros((PAD, A_DK), F32)], axis=0).astype(BF16)
            c_ref[idx, hd] = decay * c_x + _dot_tn(kw_pad, v_pad)
            n_ref[idx, hd] = decay * n_x + jnp.sum(kw, axis=0, keepdims=True)
            m_ref[idx, hd] = jnp.broadcast_to(m_new, (1, LANES))


def _mlstm_sample(p, gates, c0, n0, m0):
    R = 2 * DEC_SEQ
    n_pairs = DEC_BATCH // 2
    st = lambda i: (i, 0, 0, 0)
    return pl.pallas_call(
        _mlstm_sample_kernel,
        grid=(n_pairs,),
        in_specs=[
            pl.BlockSpec((R, A_HEADS * A_DK), lambda i: (i, 0)),
            pl.BlockSpec((R, A_HEADS * A_DK), lambda i: (i, 1)),
            pl.BlockSpec((R, A_HEADS * A_DV), lambda i: (i, 1)),
            pl.BlockSpec((R, LANES), lambda i: (i, 0)),
            pl.BlockSpec((2, A_HEADS, A_DK, A_DV), st),
            pl.BlockSpec((2, A_HEADS, 1, A_DK), st),
            pl.BlockSpec((2, A_HEADS, 1, LANES), st),
        ],
        out_specs=[
            pl.BlockSpec((R, A_HEADS * A_DV), lambda i: (i, 0)),
            pl.BlockSpec((2, A_HEADS, A_DK, A_DV), st),
            pl.BlockSpec((2, A_HEADS, 1, A_DK), st),
            pl.BlockSpec((2, A_HEADS, 1, LANES), st),
        ],
        out_shape=[
            jax.ShapeDtypeStruct((S_ROWS, D_MODEL), F32),
            jax.ShapeDtypeStruct((DEC_BATCH, A_HEADS, A_DK, A_DV), F32),
            jax.ShapeDtypeStruct((DEC_BATCH, A_HEADS, 1, A_DK), F32),
            jax.ShapeDtypeStruct((DEC_BATCH, A_HEADS, 1, LANES), F32),
        ],
        compiler_params=_params(1),
        name="mlstm_sample",
    )(p, p, p, gates, c0, n0, m0)


def _mlstm_out_kernel(hm_ref, o_ref, hg_ref, w_ref, res_ref, out_ref):
    cols = []
    for hd in range(A_HEADS):
        x = hm_ref[:, hd * A_DV:(hd + 1) * A_DV]
        cols.append(x * lax.rsqrt(jnp.mean(x * x, axis=-1, keepdims=True) + EPS))
    hn = jnp.concatenate(cols, axis=1) * hg_ref[...]
    o = o_ref[...]
    pre = (hn * (1.0 / (1.0 + jnp.exp(-o)))).astype(BF16)
    out_ref[...] = res_ref[...] + _dot(pre, w_ref[...])


def _mlstm_out(hm, p, head_gain, w, res, tm):
    rows = hm.shape[0]
    return pl.pallas_call(
        _mlstm_out_kernel,
        grid=(rows // tm,),
        in_specs=[
            pl.BlockSpec((tm, D_MODEL), lambda i: (i, 0)),
            pl.BlockSpec((tm, D_MODEL), lambda i: (i, 2)),
            pl.BlockSpec((1, D_MODEL), lambda i: (0, 0)),
            _resident((D_MODEL, D_MODEL)),
            pl.BlockSpec((tm, D_MODEL), lambda i: (i, 0)),
        ],
        out_specs=pl.BlockSpec((tm, D_MODEL), lambda i: (i, 0)),
        out_shape=jax.ShapeDtypeStruct((rows, D_MODEL), F32),
        compiler_params=_params(1),
        name="mlstm_out",
    )(hm, p, head_gain, w, res)


def _normproj_kernel(h_ref, g_ref, w_ref, hg_ref, o_ref, *, n_norm):
    xn = _rms(h_ref[...], g_ref[...]).astype(BF16)
    y = _dot(xn, w_ref[...])
    if n_norm == y.shape[1]:
        o_ref[...] = _head_norm64(y, hg_ref[...]).astype(o_ref.dtype)
    else:
        o_ref[:, :n_norm] = _head_norm64(y[:, :n_norm], hg_ref[...])
        o_ref[:, n_norm:] = y[:, n_norm:]


def _normproj(h, gain, w, head_gain, tm, out_dtype=F32):
    rows = h.shape[0]
    n = w.shape[1]
    n_norm = head_gain.shape[1]
    return pl.pallas_call(
        functools.partial(_normproj_kernel, n_norm=n_norm),
        grid=(rows // tm,),
        in_specs=[
            pl.BlockSpec((tm, D_MODEL), lambda i: (i, 0)),
            pl.BlockSpec((1, D_MODEL), lambda i: (0, 0)),
            _resident((D_MODEL, n)),
            pl.BlockSpec((1, n_norm), lambda i: (0, 0)),
        ],
        out_specs=pl.BlockSpec((tm, n), lambda i: (i, 0)),
        out_shape=jax.ShapeDtypeStruct((rows, n), out_dtype),
        compiler_params=_params(1),
        name="normproj",
    )(h, gain, w, head_gain)


def _matres_kernel(x_ref, w_ref, res_ref, o_ref):
    o_ref[...] = res_ref[...] + _dot(x_ref[...].astype(BF16), w_ref[...])


def _matres(x, w, res, tm):
    rows = x.shape[0]
    return pl.pallas_call(
        _matres_kernel,
        grid=(rows // tm,),
        in_specs=[
            pl.BlockSpec((tm, D_MODEL), lambda i: (i, 0)),
            _resident((D_MODEL, D_MODEL)),
            pl.BlockSpec((tm, D_MODEL), lambda i: (i, 0)),
        ],
        out_specs=pl.BlockSpec((tm, D_MODEL), lambda i: (i, 0)),
        out_shape=jax.ShapeDtypeStruct((rows, D_MODEL), F32),
        compiler_params=_params(1),
        name="matres",
    )(x, w, res)


def _alibi_slope(head):
    return 2.0 ** (-8.0 * (head + 1) / B_HEADS)


PROMPT_SUB = WINDOW // 2
PROMPT_BAND = WINDOW + PROMPT_SUB
PROMPT_KEYS = 2 * LANES
HEAD_PAIRS = B_HEADS // 2
PAIRS_PER_GROUP = B_GROUP // 2


def _pair_rhs(x, c):
    col = x[:, c * LANES:(c + 1) * LANES]
    rol = pltpu.roll(col, B_DH, axis=1)
    lo = lax.broadcasted_iota(jnp.int32, (1, LANES), 1) < B_DH
    even = jnp.concatenate([jnp.where(lo, col, 0.0), jnp.where(lo, 0.0, rol)], axis=0)
    odd = jnp.concatenate([jnp.where(lo, rol, 0.0), jnp.where(lo, 0.0, col)], axis=0)
    return even.astype(BF16), odd.astype(BF16)


def _attn_prompt_kernel(q_ref, kvo_ref, kvp_ref, kvm_ref, sink_ref, o_ref, bias_ref, s_ref, p_ref):
    W = WINDOW
    SB = PROMPT_SUB
    NK = PROMPT_KEYS
    first = (pl.program_id(0) == 0) & (pl.program_id(1) == 0)
    j = pl.program_id(1)
    rows = PAIRS_PER_GROUP * SB
    lo = lax.broadcasted_iota(jnp.int32, (1, LANES), 1) < B_DH

    @pl.when(first)
    def _():
        c = lax.broadcasted_iota(jnp.int32, (SB, NK), 1)
        is_meta = c < N_META
        is_sink = c == NK - 1
        for sub in range(2):
            i = lax.broadcasted_iota(jnp.int32, (SB, NK), 0) + sub * SB
            pos = c - N_META + sub * SB
            rel = W + i - pos
            in_band = (c >= N_META) & (c < N_META + PROMPT_BAND) & (rel >= 0) & (rel < W)
            for variant in range(2):
                if variant == 0:
                    dist = jnp.where(is_meta, jnp.minimum(i + N_META - c, W), rel).astype(F32)
                    valid = is_meta | (in_band & (pos >= W))
                else:
                    dist = jnp.where(is_meta, W, rel).astype(F32)
                    valid = is_meta | in_band
                for hd in range(B_HEADS):
                    r0 = (hd // 2) * SB
                    c0 = (hd % 2) * NK
                    table = jnp.where(valid, -_alibi_slope(hd) * dist, NEG_INF)
                    bias_ref[variant, sub, r0:r0 + SB, c0:c0 + NK] = jnp.where(
                        is_sink, sink_ref[0:1, hd:hd + 1], table)

    variant = jnp.minimum(j, 1)
    kvm = kvm_ref[...]
    band = jnp.concatenate([kvp_ref[...], kvo_ref[...]], axis=0)
    pad = jnp.zeros((NK - N_META - PROMPT_BAND, 2 * B_KV), F32)
    ones_rhs = jnp.concatenate([jnp.broadcast_to(jnp.where(lo, 1.0, 0.0), (NK, LANES)),
                                jnp.broadcast_to(jnp.where(lo, 0.0, 1.0), (NK, LANES))], axis=0).astype(BF16)

    sub_rows = B_KV_HEADS * rows
    for sub in range(2):
        keys = jnp.concatenate([kvm, band[sub * SB:sub * SB + PROMPT_BAND], pad], axis=0)
        k_rhs = _pair_rhs(keys, 0) + _pair_rhs(keys, 1)
        v_rhs = _pair_rhs(keys, 2) + _pair_rhs(keys, 3)
        for g in range(B_KV_HEADS):
            p0 = g * PAIRS_PER_GROUP
            r0 = (sub * B_KV_HEADS + g) * rows
            qg = jnp.concatenate([q_ref[sub * SB:(sub + 1) * SB, (p0 + m) * LANES:(p0 + m + 1) * LANES]
                                  for m in range(PAIRS_PER_GROUP)], axis=0) * (B_DH ** -0.5)
            s_ref[r0:r0 + rows, :] = _dot_nt(qg.astype(BF16), k_rhs[g])

        s = s_ref[sub * sub_rows:(sub + 1) * sub_rows, :] + bias_ref[variant, sub]
        halves = []
        for half in range(2):
            sh = s[:, half * NK:(half + 1) * NK]
            halves.append(jnp.exp(sh - jnp.max(sh, axis=-1, keepdims=True)).astype(BF16))
        p_ref[sub * sub_rows:(sub + 1) * sub_rows, :] = jnp.concatenate(halves, axis=1)

        for g in range(B_KV_HEADS):
            p0 = g * PAIRS_PER_GROUP
            r0 = (sub * B_KV_HEADS + g) * rows
            od = _dot(p_ref[r0:r0 + rows, :], jnp.concatenate([v_rhs[g], ones_rhs], axis=1))
            o = (od[:, :LANES] / od[:, LANES:]).astype(o_ref.dtype)
            for m in range(PAIRS_PER_GROUP):
                o_ref[sub * SB:(sub + 1) * SB, (p0 + m) * LANES:(p0 + m + 1) * LANES] = o[m * SB:(m + 1) * SB, :]


def _attn_prompt(q, kv_p, kv_q, sinks_row):
    nb = SEQ // WINDOW
    blk = lambda b, j: b * nb + j
    return pl.pallas_call(
        _attn_prompt_kernel,
        grid=(BATCH, nb),
        in_specs=[
            pl.BlockSpec((WINDOW, D_MODEL), lambda b, j: (blk(b, j), 0)),
            pl.BlockSpec((WINDOW, 2 * B_KV), lambda b, j: (blk(b, j), 0)),
            pl.BlockSpec((WINDOW, 2 * B_KV), lambda b, j: (blk(b, jnp.maximum(j - 1, 0)), 0)),
            pl.BlockSpec((N_META, 2 * B_KV), lambda b, j: (S_ROWS // N_META, 0)),
            pl.BlockSpec((1, LANES), lambda b, j: (0, 0)),
        ],
        out_specs=pl.BlockSpec((WINDOW, D_MODEL), lambda b, j: (blk(b, j), 0)),
        out_shape=jax.ShapeDtypeStruct((P_ROWS, D_MODEL), BF16),
        scratch_shapes=[pltpu.VMEM((2, 2, HEAD_PAIRS * PROMPT_SUB, 2 * PROMPT_KEYS), F32),
                        pltpu.VMEM((2 * HEAD_PAIRS * PROMPT_SUB, 2 * PROMPT_KEYS), F32),
                        pltpu.VMEM((2 * HEAD_PAIRS * PROMPT_SUB, 2 * PROMPT_KEYS), BF16)],
        compiler_params=_params(2),
        name="attn_prompt",
    )(q, kv_p, kv_p, kv_q, sinks_row)


GROUP_SHIFT = B_GROUP.bit_length() - 1
assert 1 << GROUP_SHIFT == B_GROUP
SAMPLE_NEW = SUBLANES
SAMPLE_SMALL = LANES
SAMPLE_KEYS = WINDOW + SAMPLE_SMALL
SAMPLE_SEQ_PER_STEP = 8
SAMPLE_ROWS = DEC_SEQ * B_GROUP
KV_PAIRS = B_KV_HEADS // 2


def _split_rhs(col):
    lo = lax.broadcasted_iota(jnp.int32, (1, LANES), 1) < B_DH
    return jnp.concatenate([jnp.where(lo, col, 0.0), jnp.where(lo, 0.0, col)], axis=0)


def _split_cols(mat_t):
    top = lax.broadcasted_iota(jnp.int32, (LANES, 1), 0) < B_DH
    return jnp.concatenate([jnp.where(top, mat_t, 0.0), jnp.where(top, 0.0, mat_t)], axis=1)


def _attn_sample_kernel(q_ref, km_ref, vm_ref, kvn_ref, kt_ref, vt_ref, sink_ref,
                        o_ref, kt_out_ref, vt_out_ref, bias_ref, s_ref, p_ref):
    W = WINDOW
    NK = SAMPLE_KEYS
    R = SAMPLE_ROWS
    lo = lax.broadcasted_iota(jnp.int32, (1, LANES), 1) < B_DH
    lane = lax.broadcasted_iota(jnp.int32, (1, LANES), 1)

    @pl.when(pl.program_id(0) == 0)
    def _():
        row = lax.broadcasted_iota(jnp.int32, (R, 1), 0)
        r_in_group = jnp.bitwise_and(row, B_GROUP - 1)
        t = jnp.right_shift(lax.broadcasted_iota(jnp.int32, (R, NK), 0), GROUP_SHIFT)
        c = lax.broadcasted_iota(jnp.int32, (R, NK), 1)
        is_win = c < W
        is_meta = (c >= W) & (c < W + N_META)
        is_new = (c >= W + N_META) & (c < W + N_META + DEC_SEQ)
        t_new = c - (W + N_META)
        dist = jnp.where(is_win, W + t - c, jnp.where(is_meta, W, t - t_new)).astype(F32)
        valid = (is_win & (c > t)) | is_meta | (is_new & (t_new <= t))
        for pair in range(KV_PAIRS):
            for e in range(2):
                slope = jnp.zeros((R, 1), F32)
                sink = jnp.zeros((R, 1), F32)
                for r in range(B_GROUP):
                    hd = (2 * pair + e) * B_GROUP + r
                    slope = jnp.where(r_in_group == r, _alibi_slope(hd), slope)
                    sink = jnp.where(r_in_group == r, sink_ref[0:1, hd:hd + 1], sink)
                table = jnp.where(c == NK - 1, sink, jnp.where(valid, -slope * dist, NEG_INF))
                for sq in range(SAMPLE_SEQ_PER_STEP):
                    r0 = (sq * KV_PAIRS + pair) * R
                    bias_ref[r0:r0 + R, e * NK:(e + 1) * NK] = table

    sel_r = lax.broadcasted_iota(jnp.int32, (SAMPLE_SMALL, W), 0)
    sel_c = lax.broadcasted_iota(jnp.int32, (SAMPLE_SMALL, W), 1)
    place_new = ((sel_r >= N_META) & (sel_r < N_META + DEC_SEQ)
                 & (sel_c == sel_r - N_META + W - DEC_SEQ)).astype(F32)
    pad = jnp.zeros((SAMPLE_SMALL - N_META - SAMPLE_NEW, B_KV), F32)
    top = lax.broadcasted_iota(jnp.int32, (LANES, 1), 0) < B_DH
    ones_rhs = jnp.concatenate([jnp.broadcast_to(jnp.where(lo, 1.0, 0.0), (SAMPLE_SMALL, LANES)),
                                jnp.broadcast_to(jnp.where(lo, 0.0, 1.0), (SAMPLE_SMALL, LANES))], axis=0)
    ones_t = jnp.concatenate([jnp.broadcast_to(jnp.where(top, 1.0, 0.0), (LANES, W)),
                              jnp.broadcast_to(jnp.where(top, 0.0, 1.0), (LANES, W))], axis=1)

    v_small, v_win = [], []
    for sq in range(SAMPLE_SEQ_PER_STEP):
        kvn = kvn_ref[sq]
        k_small = jnp.concatenate([km_ref[sq], kvn[:, :B_KV], pad], axis=0)
        v_small_sq = jnp.concatenate([vm_ref[sq], kvn[:, B_KV:], pad], axis=0)
        kt = kt_ref[sq]
        vt = vt_ref[sq]
        for src, small, dst in ((kt, k_small, kt_out_ref), (vt, v_small_sq, vt_out_ref)):
            new_cols = lax.dot_general(small, place_new, (((0,), (0,)), ((), ())),
                                       precision=lax.Precision.HIGHEST, preferred_element_type=F32)
            dst[sq] = jnp.where(lane < W - DEC_SEQ, pltpu.roll(src, W - DEC_SEQ, axis=1), new_cols)
        for pair in range(KV_PAIRS):
            r0 = (sq * KV_PAIRS + pair) * R
            q = (q_ref[sq, pair] * (B_DH ** -0.5)).astype(BF16)
            s_win = _dot(q, _split_cols(kt[pair * LANES:(pair + 1) * LANES, :]).astype(BF16))
            s_small = _dot_nt(q, _split_rhs(k_small[:, pair * LANES:(pair + 1) * LANES]).astype(BF16))
            for e in range(2):
                s_ref[r0:r0 + R, e * NK:e * NK + W] = s_win[:, e * W:(e + 1) * W]
                s_ref[r0:r0 + R, e * NK + W:(e + 1) * NK] = s_small[:, e * SAMPLE_SMALL:(e + 1) * SAMPLE_SMALL]
            v_win.append(jnp.concatenate([_split_cols(vt[pair * LANES:(pair + 1) * LANES, :]), ones_t],
                                         axis=0).astype(BF16))
            v_small.append(jnp.concatenate([_split_rhs(v_small_sq[:, pair * LANES:(pair + 1) * LANES]),
                                            ones_rhs], axis=1).astype(BF16))

    s = s_ref[...] + bias_ref[...]
    halves = []
    for e in range(2):
        sh = s[:, e * NK:(e + 1) * NK]
        halves.append(jnp.exp(sh - jnp.max(sh, axis=-1, keepdims=True)).astype(BF16))
    p_ref[...] = jnp.concatenate(halves, axis=1)

    for sq in range(SAMPLE_SEQ_PER_STEP):
        for pair in range(KV_PAIRS):
            b = sq * KV_PAIRS + pair
            p = p_ref[b * R:(b + 1) * R, :]
            p_win = jnp.concatenate([p[:, 0:W], p[:, NK:NK + W]], axis=1)
            p_small = jnp.concatenate([p[:, W:NK], p[:, NK + W:2 * NK]], axis=1)
            od = _dot_nt(p_win, v_win[b]) + _dot(p_small, v_small[b])
            o_ref[sq, pair] = od[:, :LANES] / od[:, LANES:]


def _attn_sample(q4, k_meta, v_meta, kv_new, k_win_t, v_win_t, sinks_row):
    nb = SAMPLE_SEQ_PER_STEP
    R = SAMPLE_ROWS
    n_rows = nb * KV_PAIRS * R
    seq3 = lambda rows, cols: pl.BlockSpec((nb, rows, cols), lambda i: (i, 0, 0))
    qspec = pl.BlockSpec((nb, KV_PAIRS, R, LANES), lambda i: (i, 0, 0, 0))
    win_shape = jax.ShapeDtypeStruct((DEC_BATCH, B_KV, WINDOW), F32)
    return pl.pallas_call(
        _attn_sample_kernel,
        grid=(DEC_BATCH // nb,),
        in_specs=[qspec, seq3(N_META, B_KV), seq3(N_META, B_KV), seq3(SAMPLE_NEW, 2 * B_KV),
                  seq3(B_KV, WINDOW), seq3(B_KV, WINDOW), pl.BlockSpec((1, LANES), lambda i: (0, 0))],
        out_specs=[qspec, seq3(B_KV, WINDOW), seq3(B_KV, WINDOW)],
        out_shape=[jax.ShapeDtypeStruct((DEC_BATCH, KV_PAIRS, R, LANES), F32), win_shape, win_shape],
        scratch_shapes=[pltpu.VMEM((n_rows, 2 * SAMPLE_KEYS), F32),
                        pltpu.VMEM((n_rows, 2 * SAMPLE_KEYS), F32),
                        pltpu.VMEM((n_rows, 2 * SAMPLE_KEYS), BF16)],
        compiler_params=_params(1),
        name="attn_sample",
    )(q4, k_meta, v_meta, kv_new, k_win_t, v_win_t, sinks_row)


def kernel(x_prompt, x_sample, state_C, state_n, state_m, cache_k_meta, cache_v_meta, cache_k_win, cache_v_win, meta_tokens, ffn_norm, w_ffn_in, w_ffn_out, mix_norm, w_a_in, b_a_gate, a_head_norm, w_a_out, kv_norm, w_kv, k_norm, w_q, q_norm, sinks, w_b_out):
    assert x_prompt.shape == (BATCH, SEQ, D_MODEL) and x_sample.shape == (DEC_BATCH, DEC_SEQ, D_MODEL)
    assert w_a_in.shape[0] == 1 and w_q.shape[0] == 1 and ffn_norm.shape[0] == 2

    wa_in_t = jnp.swapaxes(w_a_in[0], 0, 1)
    ba_gate = jnp.pad(b_a_gate[0].astype(F32), (0, LANES - 2 * A_HEADS)).reshape(1, LANES)
    wa_out = w_a_out[0].astype(BF16)
    wkv = w_kv.astype(BF16)
    wq = w_q[0].astype(BF16)
    wb_out = w_b_out[0].astype(BF16)
    row = lambda x: x.astype(F32).reshape(1, -1)
    k_gain = jnp.tile(row(k_norm), (1, B_KV_HEADS))
    q_gain = jnp.tile(row(q_norm[0]), (1, B_HEADS))
    sinks_row = jnp.pad(sinks[0].astype(F32), (0, LANES - B_HEADS)).reshape(1, LANES)

    h_p = x_prompt.reshape(P_ROWS, D_MODEL)
    h_q = jnp.concatenate([x_sample.reshape(S_ROWS, D_MODEL), meta_tokens.astype(F32),
                           jnp.zeros((A_CHUNK - N_META, D_MODEL), F32)], axis=0)
    TM_P, TM_W = 1024, 512

    h_q, *wf = _ffn_cast(h_q, row(ffn_norm[0, 0]), w_ffn_in, w_ffn_out, 0, 0)
    h_p = _ffn(h_p, row(ffn_norm[0, 0]), *wf, TM_P)
    p_q, g_q, wa_in = _inproj(h_q, row(mix_norm[0]), wa_in_t, wa_in_t, ba_gate, Q_ROWS, emit_bf16=True)
    p_p, g_p = _inproj(h_p, row(mix_norm[0]), wa_in, wa_in_t, ba_gate, TM_P, tn=D_MODEL)

    zc = jnp.zeros((1, A_HEADS, A_DK, A_DV), F32)
    zn = jnp.zeros((1, A_HEADS, 1, A_DK), F32)
    zm = jnp.zeros((1, A_HEADS, 1, LANES), F32)
    hm_m, c_m, n_m, m_m = _mlstm_chunks(p_q, g_q, zc, zn, zm, 1, 1, META_BLOCK, N_META, True)
    hm_p, c_p, n_p, m_p = _mlstm_chunks(p_p, g_p, c_m, n_m, m_m, BATCH, SEQ // A_CHUNK, 0, A_CHUNK, True,
                                         n_seq_step=4)
    m0_s = jnp.broadcast_to(state_m[0].astype(F32)[:, :, None, None], (DEC_BATCH, A_HEADS, 1, LANES))
    hm_s, c_s, n_s, m_s = _mlstm_sample(p_q, g_q, state_C[0].astype(F32),
                                        state_n[0].astype(F32)[:, :, None, :], m0_s)
    hm_q = jnp.concatenate([hm_s, hm_m], axis=0)

    h_p = _mlstm_out(hm_p, p_p, row(a_head_norm[0]), wa_out, h_p, TM_W)
    h_q = _mlstm_out(hm_q, p_q, row(a_head_norm[0]), wa_out, h_q, A_CHUNK)
    h_q, *wf = _ffn_cast(h_q, row(ffn_norm[0, 1]), w_ffn_in, w_ffn_out, 0, 1)
    h_p = _ffn(h_p, row(ffn_norm[0, 1]), *wf, TM_P)

    kv_p = _normproj(h_p, row(kv_norm), wkv, k_gain, TM_P)
    kv_q = _normproj(h_q, row(kv_norm), wkv, k_gain, Q_ROWS)
    h_s = h_q[:S_ROWS]
    h_s, *wf = _ffn_cast(h_s, row(ffn_norm[1, 0]), w_ffn_in, w_ffn_out, 1, 0)
    h_p = _ffn(h_p, row(ffn_norm[1, 0]), *wf, TM_P)
    q_p = _normproj(h_p, row(mix_norm[1]), wq, q_gain, TM_W, BF16)
    q_s = _normproj(h_s, row(mix_norm[1]), wq, q_gain, TM_W, BF16)

    o_p = _attn_prompt(q_p, kv_p, kv_q, sinks_row)
    q4 = q_s.reshape(DEC_BATCH, DEC_SEQ, KV_PAIRS, 2, B_GROUP, B_DH).transpose(0, 2, 1, 4, 3, 5)
    q4 = q4.reshape(DEC_BATCH, KV_PAIRS, SAMPLE_ROWS, LANES)
    seq3 = lambda x: x.astype(F32).reshape(DEC_BATCH, -1, B_KV)
    win_t = lambda x: x.astype(F32).transpose(0, 2, 3, 1).reshape(DEC_BATCH, B_KV, WINDOW)
    kv_new = jnp.pad(kv_q[:S_ROWS].reshape(DEC_BATCH, DEC_SEQ, 2 * B_KV), ((0, 0), (0, SAMPLE_NEW - DEC_SEQ), (0, 0)))
    o4, k_win_t, v_win_t = _attn_sample(q4, seq3(cache_k_meta), seq3(cache_v_meta), kv_new,
                                        win_t(cache_k_win), win_t(cache_v_win), sinks_row)
    from_t = lambda x, like: x.reshape(DEC_BATCH, B_KV_HEADS, B_DH, WINDOW).transpose(0, 3, 1, 2).astype(like.dtype)
    k_win_s = from_t(k_win_t, cache_k_win)
    v_win_s = from_t(v_win_t, cache_v_win)
    o_s = o4.reshape(DEC_BATCH, KV_PAIRS, DEC_SEQ, B_GROUP, 2, B_DH).transpose(0, 2, 1, 4, 3, 5)
    o_s = o_s.reshape(S_ROWS, D_MODEL)

    h_p = _matres(o_p, wb_out, h_p, TM_W)
    h_s = _matres(o_s, wb_out, h_s, TM_W)
    h_s, *wf = _ffn_cast(h_s, row(ffn_norm[1, 1]), w_ffn_in, w_ffn_out, 1, 1)
    h_p = _ffn(h_p, row(ffn_norm[1, 1]), *wf, TM_P)

    kv4 = lambda x: x.reshape(x.shape[:-1] + (B_KV_HEADS, B_DH))
    meta_rows = kv_q[S_ROWS:S_ROWS + N_META]
    kv_p3 = kv_p.reshape(BATCH, SEQ, 2 * B_KV)
    st = lambda x, dt: x[None].astype(dt)
    return (
        h_p.reshape(BATCH, SEQ, D_MODEL),
        h_s.reshape(DEC_BATCH, DEC_SEQ, D_MODEL),
        st(c_p, state_C.dtype), st(n_p[:, :, 0, :], state_n.dtype), st(m_p[:, :, 0, 0], state_m.dtype),
        jnp.broadcast_to(kv4(meta_rows[:, :B_KV])[None], (BATCH, N_META, B_KV_HEADS, B_DH)),
        jnp.broadcast_to(kv4(meta_rows[:, B_KV:])[None], (BATCH, N_META, B_KV_HEADS, B_DH)),
        kv4(kv_p3[:, -WINDOW:, :B_KV]), kv4(kv_p3[:, -WINDOW:, B_KV:]),
        st(c_s, state_C.dtype), st(n_s[:, :, 0, :], state_n.dtype), st(m_s[:, :, 0, 0], state_m.dtype),
        k_win_s, v_win_s,
    )
```

```python
import functools

import jax
import jax.numpy as jnp
from jax import lax
from jax.experimental import pallas as pl
from jax.experimental.pallas import tpu as pltpu

D_MODEL = 2048
BATCH = 8
SEQ = 2048
DEC_BATCH = 128
DEC_SEQ = 4
PAST_LEN = 8192
N_META = 16
A_HEADS = 4
A_DV = D_MODEL // A_HEADS
A_DK = A_DV // 2
A_CHUNK = 128
A_GATE_CAP = 15.0
A_QKVO = 2 * A_HEADS * A_DK + 2 * A_HEADS * A_DV
B_HEADS = 32
B_DH = D_MODEL // B_HEADS
B_KV_HEADS = 4
B_GROUP = B_HEADS // B_KV_HEADS
B_KV = B_KV_HEADS * B_DH
WINDOW = 128
D_FF = ((8 * D_MODEL // 3 + 255) // 256) * 256
EPS = 1e-6

LANES = 128
SUBLANES = 8
VMEM_DEFAULT_MIB = 48
VMEM_FFN_MIB = 60

P_ROWS = BATCH * SEQ
S_ROWS = DEC_BATCH * DEC_SEQ
Q_ROWS = S_ROWS + A_CHUNK
META_BLOCK = S_ROWS // A_CHUNK

F32 = jnp.float32
BF16 = jnp.bfloat16
NEG_INF = float("-inf")


def _params(n_axes, vmem_mib=VMEM_DEFAULT_MIB):
    return pltpu.CompilerParams(dimension_semantics=("arbitrary",) * n_axes,
                                vmem_limit_bytes=vmem_mib * 1024 * 1024)


def _resident(shape):
    return pl.BlockSpec(shape, lambda i: (0, 0), pipeline_mode=pl.Buffered(1))


def _rms(x, g):
    return x * lax.rsqrt(jnp.mean(x * x, axis=-1, keepdims=True) + EPS) * g


def _dot(a, b):
    return jnp.dot(a, b, preferred_element_type=F32)


def _dot_nt(a, b):
    return lax.dot_general(a, b, (((1,), (1,)), ((), ())), preferred_element_type=F32)


def _dot_tn(a, b):
    return lax.dot_general(a, b, (((0,), (0,)), ((), ())), preferred_element_type=F32)


def _log_sigmoid(x):
    return -(jnp.maximum(-x, 0.0) + jnp.log1p(jnp.exp(-jnp.abs(x))))


def _head_norm64(y, gain):
    lo = lax.broadcasted_iota(jnp.int32, (1, LANES), 1) < B_DH
    cols = []
    for c in range(y.shape[1] // LANES):
        x = y[:, c * LANES:(c + 1) * LANES]
        xx = x * x
        s_lo = jnp.sum(jnp.where(lo, xx, 0.0), axis=-1, keepdims=True)
        s_hi = jnp.sum(jnp.where(lo, 0.0, xx), axis=-1, keepdims=True)
        scale = jnp.where(lo, lax.rsqrt(s_lo / B_DH + EPS), lax.rsqrt(s_hi / B_DH + EPS))
        cols.append(x * scale * gain[:, c * LANES:(c + 1) * LANES])
    return jnp.concatenate(cols, axis=1)


FFN_TF = 512
FFN_CAST_TF = 256


def _ffn_kernel(h_ref, g_ref, wg_ref, wu_ref, wo_ref, o_ref, *rest, emit_bf16):
    j = pl.program_id(1)
    xn_ref = rest[-1]

    def half_ffn(xn, wg, wu, wo):
        g = _dot(xn, wg)
        u = _dot(xn, wu)
        a = (g / (1.0 + jnp.exp(-g))) * (0.5 * u)
        return _dot(a.astype(BF16), wo)

    def normed(h):
        xn = _rms(h, g_ref[0:1, :]).astype(BF16)
        xn_ref[...] = xn
        return xn

    if emit_bf16:
        @pl.when(j == 0)
        def _():
            h = h_ref[...]
            normed(h)
            o_ref[...] = h

        wgb_ref, wub_ref, wob_ref = rest[:-1]
        wg = wg_ref[...].astype(BF16)
        wu = wu_ref[...].astype(BF16)
        wo = wo_ref[...].astype(BF16)
        wgb_ref[...] = wg
        wub_ref[...] = wu
        wob_ref[...] = wo
        o_ref[...] += half_ffn(xn_ref[...], wg, wu, wo)
    else:
        @pl.when(j == 0)
        def _():
            h = h_ref[...]
            o_ref[...] = h + half_ffn(normed(h), wg_ref[...], wu_ref[...], wo_ref[...])

        @pl.when(j > 0)
        def _():
            o_ref[...] += half_ffn(xn_ref[...], wg_ref[...], wu_ref[...], wo_ref[...])


def _ffn(h, gain, wg, wu, wo, tm):
    rows = h.shape[0]
    tf = FFN_TF
    n_ff = D_FF // tf
    gain = jnp.broadcast_to(gain, (SUBLANES, D_MODEL))
    return pl.pallas_call(
        functools.partial(_ffn_kernel, emit_bf16=False),
        grid=(rows // tm, n_ff),
        in_specs=[
            pl.BlockSpec((tm, D_MODEL), lambda i, j: (i, 0)),
            pl.BlockSpec((SUBLANES, D_MODEL), lambda i, j: (0, 0)),
            pl.BlockSpec((D_MODEL, tf), lambda i, j: (0, j)),
            pl.BlockSpec((D_MODEL, tf), lambda i, j: (0, j)),
            pl.BlockSpec((tf, D_MODEL), lambda i, j: (j, 0)),
        ],
        out_specs=pl.BlockSpec((tm, D_MODEL), lambda i, j: (i, 0)),
        out_shape=jax.ShapeDtypeStruct((rows, D_MODEL), F32),
        scratch_shapes=[pltpu.VMEM((tm, D_MODEL), BF16)],
        compiler_params=_params(2, VMEM_FFN_MIB),
        name="ffn",
    )(h, gain, wg, wu, wo)


def _ffn_cast(h, gain, w_in, w_out, layer, which):
    rows = h.shape[0]
    tf = FFN_CAST_TF
    n_ff = D_FF // tf
    return pl.pallas_call(
        functools.partial(_ffn_kernel, emit_bf16=True),
        grid=(1, n_ff),
        in_specs=[
            pl.BlockSpec((rows, D_MODEL), lambda i, j: (0, 0)),
            pl.BlockSpec((1, D_MODEL), lambda i, j: (0, 0)),
            pl.BlockSpec((None, None, D_MODEL, tf), lambda i, j: (layer, which, 0, j)),
            pl.BlockSpec((None, None, D_MODEL, tf), lambda i, j: (layer, which, 0, j + n_ff)),
            pl.BlockSpec((None, None, tf, D_MODEL), lambda i, j: (layer, which, j, 0)),
        ],
        out_specs=[
            pl.BlockSpec((rows, D_MODEL), lambda i, j: (0, 0)),
            pl.BlockSpec((D_MODEL, tf), lambda i, j: (0, j)),
            pl.BlockSpec((D_MODEL, tf), lambda i, j: (0, j)),
            pl.BlockSpec((tf, D_MODEL), lambda i, j: (j, 0)),
        ],
        out_shape=[
            jax.ShapeDtypeStruct((rows, D_MODEL), F32),
            jax.ShapeDtypeStruct((D_MODEL, D_FF), BF16),
            jax.ShapeDtypeStruct((D_MODEL, D_FF), BF16),
            jax.ShapeDtypeStruct((D_FF, D_MODEL), BF16),
        ],
        scratch_shapes=[pltpu.VMEM((rows, D_MODEL), BF16)],
        compiler_params=_params(2),
        name="ffn_cast",
    )(h, gain, w_in, w_in, w_out)


GATE_ROWS = 2 * A_HEADS


def _inproj_kernel(h_ref, g_ref, w_ref, wgate_ref, bgate_ref, p_ref, gates_ref, *rest, emit_bf16):
    j = pl.program_id(1)
    xn_ref = rest[-1]

    @pl.when(j == 0)
    def _():
        xn = _rms(h_ref[...], g_ref[...]).astype(BF16)
        xn_ref[...] = xn
        wgate = jnp.concatenate([wgate_ref[...], jnp.zeros((LANES - GATE_ROWS, D_MODEL), F32)], axis=0)
        pre = _dot_nt(xn, wgate.astype(BF16)) + bgate_ref[...]
        capped = A_GATE_CAP * jnp.tanh(pre / A_GATE_CAP)
        lane = lax.broadcasted_iota(jnp.int32, (1, LANES), 1)
        gates_ref[...] = jnp.where(lane < A_HEADS, capped, _log_sigmoid(capped))

    if emit_bf16:
        w = w_ref[...].astype(BF16)
        rest[0][...] = w
    else:
        w = w_ref[...]
    p_ref[...] = _dot_nt(xn_ref[...], w)


def _inproj(h, gain, w_t, w_gate_t, bgate, tm, emit_bf16=False, tn=1024):
    rows = h.shape[0]
    assert not emit_bf16 or rows == tm
    out_specs = [
        pl.BlockSpec((tm, tn), lambda i, j: (i, j)),
        pl.BlockSpec((tm, LANES), lambda i, j: (i, 0)),
    ]
    out_shape = [jax.ShapeDtypeStruct((rows, A_QKVO), F32),
                 jax.ShapeDtypeStruct((rows, LANES), F32)]
    if emit_bf16:
        out_specs.append(pl.BlockSpec((tn, D_MODEL), lambda i, j: (j, 0)))
        out_shape.append(jax.ShapeDtypeStruct((A_QKVO, D_MODEL), BF16))
    return pl.pallas_call(
        functools.partial(_inproj_kernel, emit_bf16=emit_bf16),
        grid=(rows // tm, A_QKVO // tn),
        in_specs=[
            pl.BlockSpec((tm, D_MODEL), lambda i, j: (i, 0)),
            pl.BlockSpec((1, D_MODEL), lambda i, j: (0, 0)),
            pl.BlockSpec((tn, D_MODEL), lambda i, j: (j, 0)),
            pl.BlockSpec((GATE_ROWS, D_MODEL), lambda i, j: (A_QKVO // GATE_ROWS, 0)),
            pl.BlockSpec((1, LANES), lambda i, j: (0, 0)),
        ],
        out_specs=out_specs,
        out_shape=out_shape,
        scratch_shapes=[pltpu.VMEM((tm, D_MODEL), BF16)],
        compiler_params=_params(2, VMEM_FFN_MIB if tn > 1024 else VMEM_DEFAULT_MIB),
        name="mlstm_inproj",
    )(h, gain, w_t, w_gate_t, bgate)


def _mlstm_chunk_kernel(q_ref, k_ref, v_ref, g_ref, c0_ref, n0_ref, m0_ref,
                        h_ref, c_ref, n_ref, m_ref, qk_ref, qc_ref, sb_ref, *, n_valid, n_seq_step, shared_state):
    L = A_CHUNK
    heads = range(A_HEADS)

    @pl.when(pl.program_id(1) == 0)
    def _():
        for sq in range(n_seq_step):
            s0 = 0 if shared_state else sq
            c_ref[sq] = c0_ref[s0]
            n_ref[sq] = n0_ref[s0]
            m_ref[sq] = m0_ref[s0]

    row = lax.broadcasted_iota(jnp.int32, (L, L), 0)
    col = lax.broadcasted_iota(jnp.int32, (L, L), 1)
    causal = col <= row
    eye = col == row
    masked = n_valid < L
    if masked:
        row_ok = lax.broadcasted_iota(jnp.int32, (L, 1), 0) < n_valid

    def to_row(x_col):
        return jnp.sum(jnp.where(eye, x_col, 0.0), axis=0, keepdims=True)

    for sq in range(n_seq_step):
        for hd in heads:
            qb = q_ref[sq, :, hd * A_DK:(hd + 1) * A_DK].astype(BF16)
            kb = (k_ref[sq, :, hd * A_DK:(hd + 1) * A_DK] * (A_DK ** -0.5)).astype(BF16)
            qk_ref[hd] = _dot_nt(qb, kb)
            qc_ref[hd] = _dot(qb, c_ref[sq, hd].astype(BF16))

        gates = g_ref[sq]
        gates_lf = jnp.where(row_ok, gates, 0.0) if masked else gates
        csum = jnp.dot(causal.astype(F32), gates_lf, precision=lax.Precision.HIGHEST,
                       preferred_element_type=F32)

        w_intra, w_inter, floor, w_state, decay, m_new = [], [], [], [], [], []
        for hd in heads:
            b_col = csum[:, A_HEADS + hd:A_HEADS + hd + 1]
            ig_col = gates[:, hd:hd + 1]
            if masked:
                ig_col = jnp.where(row_ok, ig_col, NEG_INF)
            b_row = to_row(b_col)
            ig_row = to_row(ig_col)
            m_prev = m_ref[sq, hd][:, 0:1]
            d_log = jnp.where(causal, b_col - b_row + ig_row, NEG_INF)
            inter_log = b_col + m_prev
            m_t = jnp.maximum(inter_log, jnp.max(d_log, axis=-1, keepdims=True))
            w_intra.append(jnp.exp(d_log - m_t))
            w_inter.append(jnp.exp(inter_log - m_t))
            floor.append(jnp.exp(-m_t))
            b_last = b_col[L - 1:L, :]
            w_log = b_last - b_col + ig_col
            m_new.append(jnp.maximum(b_last + m_prev, jnp.max(w_log, axis=0, keepdims=True)))
            w_state.append(jnp.exp(w_log - m_new[hd]))
            decay.append(jnp.exp(b_last + m_prev - m_new[hd]))

        den = []
        for hd in heads:
            s = qk_ref[hd] * w_intra[hd]
            sb_ref[hd] = s.astype(BF16)
            q = q_ref[sq, :, hd * A_DK:(hd + 1) * A_DK]
            d = jnp.sum(s, axis=-1, keepdims=True) + w_inter[hd] * jnp.sum(q * n_ref[sq, hd], axis=-1, keepdims=True)
            den.append(jnp.maximum(jnp.abs(d), floor[hd]))

        for hd in heads:
            k = k_ref[sq, :, hd * A_DK:(hd + 1) * A_DK] * (A_DK ** -0.5)
            vb = v_ref[sq, :, hd * A_DV:(hd + 1) * A_DV].astype(BF16)
            num = _dot(sb_ref[hd], vb) + w_inter[hd] * qc_ref[hd]
            h_ref[sq, :, hd * A_DV:(hd + 1) * A_DV] = num / den[hd]
            kw = k * w_state[hd]
            c_ref[sq, hd] = decay[hd] * c_ref[sq, hd] + _dot_tn(kw.astype(BF16), vb)
            n_ref[sq, hd] = decay[hd] * n_ref[sq, hd] + jnp.sum(kw, axis=0, keepdims=True)
            m_ref[sq, hd] = jnp.broadcast_to(m_new[hd], (1, LANES))


def _mlstm_chunks(p, gates, c0, n0, m0, n_seq, n_chunks, row_block0, n_valid, shared_state, n_seq_step=1):
    L = A_CHUNK
    nb = n_seq_step
    if row_block0:
        assert n_seq == 1 and n_chunks == 1
    seq_rows = p.shape[0] // n_seq if not row_block0 else p.shape[0]
    p3 = p.reshape(n_seq, seq_rows, p.shape[1])
    g3 = gates.reshape(n_seq, seq_rows, gates.shape[1])
    st_in = (lambda b, c: (0, 0, 0, 0)) if shared_state else (lambda b, c: (b, 0, 0, 0))
    st_rows = 1 if shared_state else nb
    h3, c_out, n_out, m_out = pl.pallas_call(
        functools.partial(_mlstm_chunk_kernel, n_valid=n_valid, n_seq_step=nb, shared_state=shared_state),
        grid=(n_seq // nb, n_chunks),
        in_specs=[
            pl.BlockSpec((nb, L, A_HEADS * A_DK), lambda b, c: (b, row_block0 + c, 0)),
            pl.BlockSpec((nb, L, A_HEADS * A_DK), lambda b, c: (b, row_block0 + c, 1)),
            pl.BlockSpec((nb, L, A_HEADS * A_DV), lambda b, c: (b, row_block0 + c, 1)),
            pl.BlockSpec((nb, L, LANES), lambda b, c: (b, row_block0 + c, 0)),
            pl.BlockSpec((st_rows, A_HEADS, A_DK, A_DV), st_in),
            pl.BlockSpec((st_rows, A_HEADS, 1, A_DK), st_in),
            pl.BlockSpec((st_rows, A_HEADS, 1, LANES), st_in),
        ],
        out_specs=[
            pl.BlockSpec((nb, L, A_HEADS * A_DV), lambda b, c: (b, c, 0)),
            pl.BlockSpec((nb, A_HEADS, A_DK, A_DV), lambda b, c: (b, 0, 0, 0)),
            pl.BlockSpec((nb, A_HEADS, 1, A_DK), lambda b, c: (b, 0, 0, 0)),
            pl.BlockSpec((nb, A_HEADS, 1, LANES), lambda b, c: (b, 0, 0, 0)),
        ],
        out_shape=[
            jax.ShapeDtypeStruct((n_seq, n_chunks * L, D_MODEL), F32),
            jax.ShapeDtypeStruct((n_seq, A_HEADS, A_DK, A_DV), F32),
            jax.ShapeDtypeStruct((n_seq, A_HEADS, 1, A_DK), F32),
            jax.ShapeDtypeStruct((n_seq, A_HEADS, 1, LANES), F32),
        ],
        scratch_shapes=[pltpu.VMEM((A_HEADS, L, L), F32),
                        pltpu.VMEM((A_HEADS, L, A_DV), F32),
                        pltpu.VMEM((A_HEADS, L, L), BF16)],
        compiler_params=_params(2, VMEM_FFN_MIB if nb > 2 else VMEM_DEFAULT_MIB),
        name="mlstm_chunks",
    )(p3, p3, p3, g3, c0, n0, m0)
    return h3.reshape(n_seq * n_chunks * L, D_MODEL), c_out, n_out, m_out


def _mlstm_sample_kernel(q_ref, k_ref, v_ref, g_ref, c0_ref, n0_ref, m0_ref,
                         h_ref, c_ref, n_ref, m_ref):
    R = 2 * DEC_SEQ
    PAD = A_CHUNK - R
    gates = g_ref[...]
    r_col = lax.broadcasted_iota(jnp.int32, (R, 1), 0)
    is_a = r_col < DEC_SEQ
    row = lax.broadcasted_iota(jnp.int32, (R, LANES), 0)
    lane = lax.broadcasted_iota(jnp.int32, (R, LANES), 1)
    same = ((lane < DEC_SEQ) & (row < DEC_SEQ)) | ((lane >= DEC_SEQ) & (lane < R) & (row >= DEC_SEQ))
    causal = same & (lane <= row)
    eye = lane == row

    def to_row(x_col):
        return jnp.sum(jnp.where(eye, x_col, 0.0), axis=0, keepdims=True)

    for hd in range(A_HEADS):
        lf_col = gates[:, A_HEADS + hd:A_HEADS + hd + 1]
        ig_col = gates[:, hd:hd + 1]
        lf_row = to_row(lf_col)
        ig_row = to_row(ig_col)
        b_col = jnp.sum(jnp.where(causal, lf_row, 0.0), axis=1, keepdims=True)
        b_row = to_row(b_col)
        m_a = m0_ref[0, hd][:, 0:1]
        m_b = m0_ref[1, hd][:, 0:1]
        m_prev = jnp.where(is_a, m_a, m_b)
        c_a = c0_ref[0, hd]
        c_b = c0_ref[1, hd]
        n_a = n0_ref[0, hd]
        n_b = n0_ref[1, hd]

        q = q_ref[:, hd * A_DK:(hd + 1) * A_DK]
        k = k_ref[:, hd * A_DK:(hd + 1) * A_DK] * (A_DK ** -0.5)
        v = v_ref[:, hd * A_DV:(hd + 1) * A_DV]
        qb = q.astype(BF16)
        k_pad = jnp.concatenate([k, jnp.zeros((PAD, A_DK), F32)], axis=0).astype(BF16)
        v_pad = jnp.concatenate([v, jnp.zeros((PAD, A_DV), F32)], axis=0).astype(BF16)

        d_log = jnp.where(causal, b_col - b_row + ig_row, NEG_INF)
        inter_log = b_col + m_prev
        m_t = jnp.maximum(inter_log, jnp.max(d_log, axis=-1, keepdims=True))
        w_intra = jnp.exp(d_log - m_t)
        w_inter = jnp.exp(inter_log - m_t)
        s = _dot_nt(qb, k_pad) * w_intra
        q_c = jnp.where(is_a, _dot(qb, c_a.astype(BF16)), _dot(qb, c_b.astype(BF16)))
        num = _dot(s.astype(BF16), v_pad) + w_inter * q_c
        q_n = jnp.sum(q * jnp.where(is_a, n_a, n_b), axis=-1, keepdims=True)
        den = jnp.sum(s, axis=-1, keepdims=True) + w_inter * q_n
        den = jnp.maximum(jnp.abs(den), jnp.exp(-m_t))
        h_ref[:, hd * A_DV:(hd + 1) * A_DV] = num / den

        for idx, sel, m_x, c_x, n_x in ((0, is_a, m_a, c_a, n_a),
                                        (1, jnp.logical_not(is_a), m_b, c_b, n_b)):
            last = (idx + 1) * DEC_SEQ - 1
            b_last = b_col[last:last + 1, :]
            w_log = jnp.where(sel, b_last - b_col + ig_col, NEG_INF)
            m_new = jnp.maximum(b_last + m_x, jnp.max(w_log, axis=0, keepdims=True))
            w_state = jnp.exp(w_log - m_new)
            decay = jnp.exp(b_last + m_x - m_new)
            kw = k * w_state
            kw_pad = jnp.concatenate([kw, jnp.zeros((PAD, A_DK), F32)], axis=0).astype(BF16)
            c_ref[idx, hd] = decay * c_x + _dot_tn(kw_pad, v_pad)
            n_ref[idx, hd] = decay * n_x + jnp.sum(kw, axis=0, keepdims=True)
            m_ref[idx, hd] = jnp.broadcast_to(m_new, (1, LANES))


def _mlstm_sample(p, gates, c0, n0, m0):
    R = 2 * DEC_SEQ
    n_pairs = DEC_BATCH // 2
    st = lambda i: (i, 0, 0, 0)
    return pl.pallas_call(
        _mlstm_sample_kernel,
        grid=(n_pairs,),
        in_specs=[
            pl.BlockSpec((R, A_HEADS * A_DK), lambda i: (i, 0)),
            pl.BlockSpec((R, A_HEADS * A_DK), lambda i: (i, 1)),
            pl.BlockSpec((R, A_HEADS * A_DV), lambda i: (i, 1)),
            pl.BlockSpec((R, LANES), lambda i: (i, 0)),
            pl.BlockSpec((2, A_HEADS, A_DK, A_DV), st),
            pl.BlockSpec((2, A_HEADS, 1, A_DK), st),
            pl.BlockSpec((2, A_HEADS, 1, LANES), st),
        ],
        out_specs=[
            pl.BlockSpec((R, A_HEADS * A_DV), lambda i: (i, 0)),
            pl.BlockSpec((2, A_HEADS, A_DK, A_DV), st),
            pl.BlockSpec((2, A_HEADS, 1, A_DK), st),
            pl.BlockSpec((2, A_HEADS, 1, LANES), st),
        ],
        out_shape=[
            jax.ShapeDtypeStruct((S_ROWS, D_MODEL), F32),
            jax.ShapeDtypeStruct((DEC_BATCH, A_HEADS, A_DK, A_DV), F32),
            jax.ShapeDtypeStruct((DEC_BATCH, A_HEADS, 1, A_DK), F32),
            jax.ShapeDtypeStruct((DEC_BATCH, A_HEADS, 1, LANES), F32),
        ],
        compiler_params=_params(1),
        name="mlstm_sample",
    )(p, p, p, gates, c0, n0, m0)


def _mlstm_out_kernel(hm_ref, o_ref, hg_ref, w_ref, res_ref, out_ref):
    cols = []
    for hd in range(A_HEADS):
        x = hm_ref[:, hd * A_DV:(hd + 1) * A_DV]
        cols.append(x * lax.rsqrt(jnp.mean(x * x, axis=-1, keepdims=True) + EPS))
    hn = jnp.concatenate(cols, axis=1) * hg_ref[...]
    o = o_ref[...]
    pre = (hn * (1.0 / (1.0 + jnp.exp(-o)))).astype(BF16)
    out_ref[...] = res_ref[...] + _dot(pre, w_ref[...])


def _mlstm_out(hm, p, head_gain, w, res, tm):
    rows = hm.shape[0]
    return pl.pallas_call(
        _mlstm_out_kernel,
        grid=(rows // tm,),
        in_specs=[
            pl.BlockSpec((tm, D_MODEL), lambda i: (i, 0)),
            pl.BlockSpec((tm, D_MODEL), lambda i: (i, 2)),
            pl.BlockSpec((1, D_MODEL), lambda i: (0, 0)),
            _resident((D_MODEL, D_MODEL)),
            pl.BlockSpec((tm, D_MODEL), lambda i: (i, 0)),
        ],
        out_specs=pl.BlockSpec((tm, D_MODEL), lambda i: (i, 0)),
        out_shape=jax.ShapeDtypeStruct((rows, D_MODEL), F32),
        compiler_params=_params(1),
        name="mlstm_out",
    )(hm, p, head_gain, w, res)


def _normproj_kernel(h_ref, g_ref, w_ref, hg_ref, o_ref, *, n_norm):
    xn = _rms(h_ref[...], g_ref[...]).astype(BF16)
    y = _dot(xn, w_ref[...])
    if n_norm == y.shape[1]:
        o_ref[...] = _head_norm64(y, hg_ref[...]).astype(o_ref.dtype)
    else:
        o_ref[:, :n_norm] = _head_norm64(y[:, :n_norm], hg_ref[...])
        o_ref[:, n_norm:] = y[:, n_norm:]


def _normproj(h, gain, w, head_gain, tm, out_dtype=F32):
    rows = h.shape[0]
    n = w.shape[1]
    n_norm = head_gain.shape[1]
    return pl.pallas_call(
        functools.partial(_normproj_kernel, n_norm=n_norm),
        grid=(rows // tm,),
        in_specs=[
            pl.BlockSpec((tm, D_MODEL), lambda i: (i, 0)),
            pl.BlockSpec((1, D_MODEL), lambda i: (0, 0)),
            _resident((D_MODEL, n)),
            pl.BlockSpec((1, n_norm), lambda i: (0, 0)),
        ],
        out_specs=pl.BlockSpec((tm, n), lambda i: (i, 0)),
        out_shape=jax.ShapeDtypeStruct((rows, n), out_dtype),
        compiler_params=_params(1),
        name="normproj",
    )(h, gain, w, head_gain)


def _matres_kernel(x_ref, w_ref, res_ref, o_ref):
    o_ref[...] = res_ref[...] + _dot(x_ref[...].astype(BF16), w_ref[...])


def _matres(x, w, res, tm):
    rows = x.shape[0]
    return pl.pallas_call(
        _matres_kernel,
        grid=(rows // tm,),
        in_specs=[
            pl.BlockSpec((tm, D_MODEL), lambda i: (i, 0)),
            _resident((D_MODEL, D_MODEL)),
            pl.BlockSpec((tm, D_MODEL), lambda i: (i, 0)),
        ],
        out_specs=pl.BlockSpec((tm, D_MODEL), lambda i: (i, 0)),
        out_shape=jax.ShapeDtypeStruct((rows, D_MODEL), F32),
        compiler_params=_params(1),
        name="matres",
    )(x, w, res)


def _alibi_slope(head):
    return 2.0 ** (-8.0 * (head + 1) / B_HEADS)


PROMPT_SUB = WINDOW // 2
PROMPT_BAND = WINDOW + PROMPT_SUB
PROMPT_KEYS = 2 * LANES
HEAD_PAIRS = B_HEADS // 2
PAIRS_PER_GROUP = B_GROUP // 2


def _pair_rhs(x, c):
    col = x[:, c * LANES:(c + 1) * LANES]
    rol = pltpu.roll(col, B_DH, axis=1)
    lo = lax.broadcasted_iota(jnp.int32, (1, LANES), 1) < B_DH
    even = jnp.concatenate([jnp.where(lo, col, 0.0), jnp.where(lo, 0.0, rol)], axis=0)
    odd = jnp.concatenate([jnp.where(lo, rol, 0.0), jnp.where(lo, 0.0, col)], axis=0)
    return even.astype(BF16), odd.astype(BF16)


def _attn_prompt_kernel(q_ref, kvo_ref, kvp_ref, kvm_ref, sink_ref, o_ref, bias_ref, s_ref, p_ref):
    W = WINDOW
    SB = PROMPT_SUB
    NK = PROMPT_KEYS
    first = (pl.program_id(0) == 0) & (pl.program_id(1) == 0)
    j = pl.program_id(1)
    rows = PAIRS_PER_GROUP * SB
    lo = lax.broadcasted_iota(jnp.int32, (1, LANES), 1) < B_DH

    @pl.when(first)
    def _():
        c = lax.broadcasted_iota(jnp.int32, (SB, NK), 1)
        is_meta = c < N_META
        is_sink = c == NK - 1
        for sub in range(2):
            i = lax.broadcasted_iota(jnp.int32, (SB, NK), 0) + sub * SB
            pos = c - N_META + sub * SB
            rel = W + i - pos
            in_band = (c >= N_META) & (c < N_META + PROMPT_BAND) & (rel >= 0) & (rel < W)
            for variant in range(2):
                if variant == 0:
                    dist = jnp.where(is_meta, jnp.minimum(i + N_META - c, W), rel).astype(F32)
                    valid = is_meta | (in_band & (pos >= W))
                else:
                    dist = jnp.where(is_meta, W, rel).astype(F32)
                    valid = is_meta | in_band
                for hd in range(B_HEADS):
                    r0 = (hd // 2) * SB
                    c0 = (hd % 2) * NK
                    table = jnp.where(valid, -_alibi_slope(hd) * dist, NEG_INF)
                    bias_ref[variant, sub, r0:r0 + SB, c0:c0 + NK] = jnp.where(
                        is_sink, sink_ref[0:1, hd:hd + 1], table)

    variant = jnp.minimum(j, 1)
    kvm = kvm_ref[...]
    band = jnp.concatenate([kvp_ref[...], kvo_ref[...]], axis=0)
    pad = jnp.zeros((NK - N_META - PROMPT_BAND, 2 * B_KV), F32)
    ones_rhs = jnp.concatenate([jnp.broadcast_to(jnp.where(lo, 1.0, 0.0), (NK, LANES)),
                                jnp.broadcast_to(jnp.where(lo, 0.0, 1.0), (NK, LANES))], axis=0).astype(BF16)

    sub_rows = B_KV_HEADS * rows
    for sub in range(2):
        keys = jnp.concatenate([kvm, band[sub * SB:sub * SB + PROMPT_BAND], pad], axis=0)
        k_rhs = _pair_rhs(keys, 0) + _pair_rhs(keys, 1)
        v_rhs = _pair_rhs(keys, 2) + _pair_rhs(keys, 3)
        for g in range(B_KV_HEADS):
            p0 = g * PAIRS_PER_GROUP
            r0 = (sub * B_KV_HEADS + g) * rows
            qg = jnp.concatenate([q_ref[sub * SB:(sub + 1) * SB, (p0 + m) * LANES:(p0 + m + 1) * LANES]
                                  for m in range(PAIRS_PER_GROUP)], axis=0) * (B_DH ** -0.5)
            s_ref[r0:r0 + rows, :] = _dot_nt(qg.astype(BF16), k_rhs[g])

        s = s_ref[sub * sub_rows:(sub + 1) * sub_rows, :] + bias_ref[variant, sub]
        halves = []
        for half in range(2):
            sh = s[:, half * NK:(half + 1) * NK]
            halves.append(jnp.exp(sh - jnp.max(sh, axis=-1, keepdims=True)).astype(BF16))
        p_ref[sub * sub_rows:(sub + 1) * sub_rows, :] = jnp.concatenate(halves, axis=1)

        for g in range(B_KV_HEADS):
            p0 = g * PAIRS_PER_GROUP
            r0 = (sub * B_KV_HEADS + g) * rows
            od = _dot(p_ref[r0:r0 + rows, :], jnp.concatenate([v_rhs[g], ones_rhs], axis=1))
            o = (od[:, :LANES] / od[:, LANES:]).astype(o_ref.dtype)
            for m in range(PAIRS_PER_GROUP):
                o_ref[sub * SB:(sub + 1) * SB, (p0 + m) * LANES:(p0 + m + 1) * LANES] = o[m * SB:(m + 1) * SB, :]


def _attn_prompt(q, kv_p, kv_q, sinks_row):
    nb = SEQ // WINDOW
    blk = lambda b, j: b * nb + j
    return pl.pallas_call(
        _attn_prompt_kernel,
        grid=(BATCH, nb),
        in_specs=[
            pl.BlockSpec((WINDOW, D_MODEL), lambda b, j: (blk(b, j), 0)),
            pl.BlockSpec((WINDOW, 2 * B_KV), lambda b, j: (blk(b, j), 0)),
            pl.BlockSpec((WINDOW, 2 * B_KV), lambda b, j: (blk(b, jnp.maximum(j - 1, 0)), 0)),
            pl.BlockSpec((N_META, 2 * B_KV), lambda b, j: (S_ROWS // N_META, 0)),
            pl.BlockSpec((1, LANES), lambda b, j: (0, 0)),
        ],
        out_specs=pl.BlockSpec((WINDOW, D_MODEL), lambda b, j: (blk(b, j), 0)),
        out_shape=jax.ShapeDtypeStruct((P_ROWS, D_MODEL), BF16),
        scratch_shapes=[pltpu.VMEM((2, 2, HEAD_PAIRS * PROMPT_SUB, 2 * PROMPT_KEYS), F32),
                        pltpu.VMEM((2 * HEAD_PAIRS * PROMPT_SUB, 2 * PROMPT_KEYS), F32),
                        pltpu.VMEM((2 * HEAD_PAIRS * PROMPT_SUB, 2 * PROMPT_KEYS), BF16)],
        compiler_params=_params(2),
        name="attn_prompt",
    )(q, kv_p, kv_p, kv_q, sinks_row)


GROUP_SHIFT = B_GROUP.bit_length() - 1
assert 1 << GROUP_SHIFT == B_GROUP
SAMPLE_NEW = SUBLANES
SAMPLE_SMALL = LANES
SAMPLE_KEYS = WINDOW + SAMPLE_SMALL
SAMPLE_SEQ_PER_STEP = 8
SAMPLE_ROWS = DEC_SEQ * B_GROUP
KV_PAIRS = B_KV_HEADS // 2


def _split_rhs(col):
    lo = lax.broadcasted_iota(jnp.int32, (1, LANES), 1) < B_DH
    return jnp.concatenate([jnp.where(lo, col, 0.0), jnp.where(lo, 0.0, col)], axis=0)


def _split_cols(mat_t):
    top = lax.broadcasted_iota(jnp.int32, (LANES, 1), 0) < B_DH
    return jnp.concatenate([jnp.where(top, mat_t, 0.0), jnp.where(top, 0.0, mat_t)], axis=1)


def _attn_sample_kernel(q_ref, km_ref, vm_ref, kvn_ref, kt_ref, vt_ref, sink_ref,
                        o_ref, kt_out_ref, vt_out_ref, bias_ref, s_ref, p_ref):
    W = WINDOW
    NK = SAMPLE_KEYS
    R = SAMPLE_ROWS
    lo = lax.broadcasted_iota(jnp.int32, (1, LANES), 1) < B_DH
    lane = lax.broadcasted_iota(jnp.int32, (1, LANES), 1)

    @pl.when(pl.program_id(0) == 0)
    def _():
        row = lax.broadcasted_iota(jnp.int32, (R, 1), 0)
        r_in_group = jnp.bitwise_and(row, B_GROUP - 1)
        t = jnp.right_shift(lax.broadcasted_iota(jnp.int32, (R, NK), 0), GROUP_SHIFT)
        c = lax.broadcasted_iota(jnp.int32, (R, NK), 1)
        is_win = c < W
        is_meta = (c >= W) & (c < W + N_META)
        is_new = (c >= W + N_META) & (c < W + N_META + DEC_SEQ)
        t_new = c - (W + N_META)
        dist = jnp.where(is_win, W + t - c, jnp.where(is_meta, W, t - t_new)).astype(F32)
        valid = (is_win & (c > t)) | is_meta | (is_new & (t_new <= t))
        for pair in range(KV_PAIRS):
            for e in range(2):
                slope = jnp.zeros((R, 1), F32)
                sink = jnp.zeros((R, 1), F32)
                for r in range(B_GROUP):
                    hd = (2 * pair + e) * B_GROUP + r
                    slope = jnp.where(r_in_group == r, _alibi_slope(hd), slope)
                    sink = jnp.where(r_in_group == r, sink_ref[0:1, hd:hd + 1], sink)
                table = jnp.where(c == NK - 1, sink, jnp.where(valid, -slope * dist, NEG_INF))
                for sq in range(SAMPLE_SEQ_PER_STEP):
                    r0 = (sq * KV_PAIRS + pair) * R
                    bias_ref[r0:r0 + R, e * NK:(e + 1) * NK] = table

    sel_r = lax.broadcasted_iota(jnp.int32, (SAMPLE_SMALL, W), 0)
    sel_c = lax.broadcasted_iota(jnp.int32, (SAMPLE_SMALL, W), 1)
    place_new = ((sel_r >= N_META) & (sel_r < N_META + DEC_SEQ)
                 & (sel_c == sel_r - N_META + W - DEC_SEQ)).astype(F32)
    pad = jnp.zeros((SAMPLE_SMALL - N_META - SAMPLE_NEW, B_KV), F32)
    top = lax.broadcasted_iota(jnp.int32, (LANES, 1), 0) < B_DH
    ones_rhs = jnp.concatenate([jnp.broadcast_to(jnp.where(lo, 1.0, 0.0), (SAMPLE_SMALL, LANES)),
                                jnp.broadcast_to(jnp.where(lo, 0.0, 1.0), (SAMPLE_SMALL, LANES))], axis=0)
    ones_t = jnp.concatenate([jnp.broadcast_to(jnp.where(top, 1.0, 0.0), (LANES, W)),
                              jnp.broadcast_to(jnp.where(top, 0.0, 1.0), (LANES, W))], axis=1)

    v_small, v_win = [], []
    for sq in range(SAMPLE_SEQ_PER_STEP):
        kvn = kvn_ref[sq]
        k_small = jnp.concatenate([km_ref[sq], kvn[:, :B_KV], pad], axis=0)
        v_small_sq = jnp.concatenate([vm_ref[sq], kvn[:, B_KV:], pad], axis=0)
        kt = kt_ref[sq]
        vt = vt_ref[sq]
        for src, small, dst in ((kt, k_small, kt_out_ref), (vt, v_small_sq, vt_out_ref)):
            new_cols = lax.dot_general(small, place_new, (((0,), (0,)), ((), ())),
                                       precision=lax.Precision.HIGHEST, preferred_element_type=F32)
            dst[sq] = jnp.where(lane < W - DEC_SEQ, pltpu.roll(src, W - DEC_SEQ, axis=1), new_cols)
        for pair in range(KV_PAIRS):
            r0 = (sq * KV_PAIRS + pair) * R
            q = (q_ref[sq, pair] * (B_DH ** -0.5)).astype(BF16)
            s_win = _dot(q, _split_cols(kt[pair * LANES:(pair + 1) * LANES, :]).astype(BF16))
            s_small = _dot_nt(q, _split_rhs(k_small[:, pair * LANES:(pair + 1) * LANES]).astype(BF16))
            for e in range(2):
                s_ref[r0:r0 + R, e * NK:e * NK + W] = s_win[:, e * W:(e + 1) * W]
                s_ref[r0:r0 + R, e * NK + W:(e + 1) * NK] = s_small[:, e * SAMPLE_SMALL:(e + 1) * SAMPLE_SMALL]
            v_win.append(jnp.concatenate([_split_cols(vt[pair * LANES:(pair + 1) * LANES, :]), ones_t],
                                         axis=0).astype(BF16))
            v_small.append(jnp.concatenate([_split_rhs(v_small_sq[:, pair * LANES:(pair + 1) * LANES]),
                                            ones_rhs], axis=1).astype(BF16))

    s = s_ref[...] + bias_ref[...]
    halves = []
    for e in range(2):
        sh = s[:, e * NK:(e + 1) * NK]
        halves.append(jnp.exp(sh - jnp.max(sh, axis=-1, keepdims=True)).astype(BF16))
    p_ref[...] = jnp.concatenate(halves, axis=1)

    for sq in range(SAMPLE_SEQ_PER_STEP):
        for pair in range(KV_PAIRS):
            b = sq * KV_PAIRS + pair
            p = p_ref[b * R:(b + 1) * R, :]
            p_win = jnp.concatenate([p[:, 0:W], p[:, NK:NK + W]], axis=1)
            p_small = jnp.concatenate([p[:, W:NK], p[:, NK + W:2 * NK]], axis=1)
            od = _dot_nt(p_win, v_win[b]) + _dot(p_small, v_small[b])
            o_ref[sq, pair] = od[:, :LANES] / od[:, LANES:]


def _attn_sample(q4, k_meta, v_meta, kv_new, k_win_t, v_win_t, sinks_row):
    nb = SAMPLE_SEQ_PER_STEP
    R = SAMPLE_ROWS
    n_rows = nb * KV_PAIRS * R
    seq3 = lambda rows, cols: pl.BlockSpec((nb, rows, cols), lambda i: (i, 0, 0))
    qspec = pl.BlockSpec((nb, KV_PAIRS, R, LANES), lambda i: (i, 0, 0, 0))
    win_shape = jax.ShapeDtypeStruct((DEC_BATCH, B_KV, WINDOW), F32)
    return pl.pallas_call(
        _attn_sample_kernel,
        grid=(DEC_BATCH // nb,),
        in_specs=[qspec, seq3(N_META, B_KV), seq3(N_META, B_KV), seq3(SAMPLE_NEW, 2 * B_KV),
                  seq3(B_KV, WINDOW), seq3(B_KV, WINDOW), pl.BlockSpec((1, LANES), lambda i: (0, 0))],
        out_specs=[qspec, seq3(B_KV, WINDOW), seq3(B_KV, WINDOW)],
        out_shape=[jax.ShapeDtypeStruct((DEC_BATCH, KV_PAIRS, R, LANES), F32), win_shape, win_shape],
        scratch_shapes=[pltpu.VMEM((n_rows, 2 * SAMPLE_KEYS), F32),
                        pltpu.VMEM((n_rows, 2 * SAMPLE_KEYS), F32),
                        pltpu.VMEM((n_rows, 2 * SAMPLE_KEYS), BF16)],
        compiler_params=_params(1),
        name="attn_sample",
    )(q4, k_meta, v_meta, kv_new, k_win_t, v_win_t, sinks_row)


def kernel(x_prompt, x_sample, state_C, state_n, state_m, cache_k_meta, cache_v_meta, cache_k_win, cache_v_win, meta_tokens, ffn_norm, w_ffn_in, w_ffn_out, mix_norm, w_a_in, b_a_gate, a_head_norm, w_a_out, kv_norm, w_kv, k_norm, w_q, q_norm, sinks, w_b_out):
    assert x_prompt.shape == (BATCH, SEQ, D_MODEL) and x_sample.shape == (DEC_BATCH, DEC_SEQ, D_MODEL)
    assert w_a_in.shape[0] == 1 and w_q.shape[0] == 1 and ffn_norm.shape[0] == 2

    wa_in_t = jnp.swapaxes(w_a_in[0], 0, 1)
    ba_gate = jnp.pad(b_a_gate[0].astype(F32), (0, LANES - 2 * A_HEADS)).reshape(1, LANES)
    wa_out = w_a_out[0].astype(BF16)
    wkv = w_kv.astype(BF16)
    wq = w_q[0].astype(BF16)
    wb_out = w_b_out[0].astype(BF16)
    row = lambda x: x.astype(F32).reshape(1, -1)
    k_gain = jnp.tile(row(k_norm), (1, B_KV_HEADS))
    q_gain = jnp.tile(row(q_norm[0]), (1, B_HEADS))
    sinks_row = jnp.pad(sinks[0].astype(F32), (0, LANES - B_HEADS)).reshape(1, LANES)

    h_p = x_prompt.reshape(P_ROWS, D_MODEL)
    h_q = jnp.concatenate([x_sample.reshape(S_ROWS, D_MODEL), meta_tokens.astype(F32),
                           jnp.zeros((A_CHUNK - N_META, D_MODEL), F32)], axis=0)
    TM_P, TM_W = 1024, 512

    h_q, *wf = _ffn_cast(h_q, row(ffn_norm[0, 0]), w_ffn_in, w_ffn_out, 0, 0)
    h_p = _ffn(h_p, row(ffn_norm[0, 0]), *wf, TM_P)
    p_q, g_q, wa_in = _inproj(h_q, row(mix_norm[0]), wa_in_t, wa_in_t, ba_gate, Q_ROWS, emit_bf16=True)
    p_p, g_p = _inproj(h_p, row(mix_norm[0]), wa_in, wa_in_t, ba_gate, TM_P, tn=D_MODEL)

    zc = jnp.zeros((1, A_HEADS, A_DK, A_DV), F32)
    zn = jnp.zeros((1, A_HEADS, 1, A_DK), F32)
    zm = jnp.zeros((1, A_HEADS, 1, LANES), F32)
    hm_m, c_m, n_m, m_m = _mlstm_chunks(p_q, g_q, zc, zn, zm, 1, 1, META_BLOCK, N_META, True)
    hm_p, c_p, n_p, m_p = _mlstm_chunks(p_p, g_p, c_m, n_m, m_m, BATCH, SEQ // A_CHUNK, 0, A_CHUNK, True,
                                         n_seq_step=4)
    m0_s = jnp.broadcast_to(state_m[0].astype(F32)[:, :, None, None], (DEC_BATCH, A_HEADS, 1, LANES))
    hm_s, c_s, n_s, m_s = _mlstm_sample(p_q, g_q, state_C[0].astype(F32),
                                        state_n[0].astype(F32)[:, :, None, :], m0_s)
    hm_q = jnp.concatenate([hm_s, hm_m], axis=0)

    h_p = _mlstm_out(hm_p, p_p, row(a_head_norm[0]), wa_out, h_p, TM_W)
    h_q = _mlstm_out(hm_q, p_q, row(a_head_norm[0]), wa_out, h_q, A_CHUNK)
    h_q, *wf = _ffn_cast(h_q, row(ffn_norm[0, 1]), w_ffn_in, w_ffn_out, 0, 1)
    h_p = _ffn(h_p, row(ffn_norm[0, 1]), *wf, TM_P)

    kv_p = _normproj(h_p, row(kv_norm), wkv, k_gain, TM_P)
    kv_q = _normproj(h_q, row(kv_norm), wkv, k_gain, Q_ROWS)
    h_s = h_q[:S_ROWS]
    h_s, *wf = _ffn_cast(h_s, row(ffn_norm[1, 0]), w_ffn_in, w_ffn_out, 1, 0)
    h_p = _ffn(h_p, row(ffn_norm[1, 0]), *wf, TM_P)
    q_p = _normproj(h_p, row(mix_norm[1]), wq, q_gain, TM_W, BF16)
    q_s = _normproj(h_s, row(mix_norm[1]), wq, q_gain, TM_W, BF16)

    o_p = _attn_prompt(q_p, kv_p, kv_q, sinks_row)
    q4 = q_s.reshape(DEC_BATCH, DEC_SEQ, KV_PAIRS, 2, B_GROUP, B_DH).transpose(0, 2, 1, 4, 3, 5)
    q4 = q4.reshape(DEC_BATCH, KV_PAIRS, SAMPLE_ROWS, LANES)
    seq3 = lambda x: x.astype(F32).reshape(DEC_BATCH, -1, B_KV)
    win_t = lambda x: x.astype(F32).transpose(0, 2, 3, 1).reshape(DEC_BATCH, B_KV, WINDOW)
    kv_new = jnp.pad(kv_q[:S_ROWS].reshape(DEC_BATCH, DEC_SEQ, 2 * B_KV), ((0, 0), (0, SAMPLE_NEW - DEC_SEQ), (0, 0)))
    o4, k_win_t, v_win_t = _attn_sample(q4, seq3(cache_k_meta), seq3(cache_v_meta), kv_new,
                                        win_t(cache_k_win), win_t(cache_v_win), sinks_row)
    from_t = lambda x, like: x.reshape(DEC_BATCH, B_KV_HEADS, B_DH, WINDOW).transpose(0, 3, 1, 2).astype(like.dtype)
    k_win_s = from_t(k_win_t, cache_k_win)
    v_win_s = from_t(v_win_t, cache_v_win)
    o_s = o4.reshape(DEC_BATCH, KV_PAIRS, DEC_SEQ, B_GROUP, 2, B_DH).transpose(0, 2, 1, 4, 3, 5)
    o_s = o_s.reshape(S_ROWS, D_MODEL)

    h_p = _matres(o_p, wb_out, h_p, TM_W)
    h_s = _matres(o_s, wb_out, h_s, TM_W)
    h_s, *wf = _ffn_cast(h_s, row(ffn_norm[1, 1]), w_ffn_in, w_ffn_out, 1, 1)
    h_p = _ffn(h_p, row(ffn_norm[1, 1]), *wf, TM_P)

    kv4 = lambda x: x.reshape(x.shape[:-1] + (B_KV_HEADS, B_DH))
    meta_rows = kv_q[S_ROWS:S_ROWS + N_META]
    kv_p3 = kv_p.reshape(BATCH, SEQ, 2 * B_KV)
    st = lambda x, dt: x[None].astype(dt)
    return (
        h_p.reshape(BATCH, SEQ, D_MODEL),
        h_s.reshape(DEC_BATCH, DEC_SEQ, D_MODEL),
        st(c_p, state_C.dtype), st(n_p[:, :, 0, :], state_n.dtype), st(m_p[:, :, 0, 0], state_m.dtype),
        jnp.broadcast_to(kv4(meta_rows[:, :B_KV])[None], (BATCH, N_META, B_KV_HEADS, B_DH)),
        jnp.broadcast_to(kv4(meta_rows[:, B_KV:])[None], (BATCH, N_META, B_KV_HEADS, B_DH)),
        kv4(kv_p3[:, -WINDOW:, :B_KV]), kv4(kv_p3[:, -WINDOW:, B_KV:]),
        st(c_s, state_C.dtype), st(n_s[:, :, 0, :], state_n.dtype), st(m_s[:, :, 0, 0], state_m.dtype),
        k_win_s, v_win_s,
    )
```
